```python
import math
import jax, jax.numpy as jnp
from jax import lax
import numpy as np

D_MODEL = 1024
BATCH = 8
SEQ = 2048
DEPTH = 1
DEC_BATCH = 128
DEC_SEQ = 1
PAST_LEN = 8192
PAGE_SIZE = 128

GDN_HEADS = 8
GDN_DK = 128
GDN_DV = 128
GDN_CONV = 4
GDN_CHUNK = 64
GDN_QK = GDN_HEADS * GDN_DK
GDN_V = GDN_HEADS * GDN_DV
GDN_CONV_CH = 2 * GDN_QK + GDN_V
SWA_Q_HEADS = 16
SWA_KV_HEADS = 4
SWA_GROUP = SWA_Q_HEADS // SWA_KV_HEADS
SWA_HD = 64
SWA_Q = SWA_Q_HEADS * SWA_HD
SWA_KV = SWA_KV_HEADS * SWA_HD
WINDOW = 128
D_FF = 2816
FFN_CONV = 3
IN_WIDTHS = (GDN_CONV_CH, GDN_V, GDN_HEADS, GDN_HEADS, SWA_Q, SWA_KV, SWA_KV, D_MODEL, D_MODEL)
IN_WIDTH = GDN_CONV_CH + GDN_V + 2 * GDN_HEADS + SWA_Q + 2 * SWA_KV + 2 * D_MODEL
EPS = 1e-6

kernel_name = "hybrid_gdn_swa_convffn_adaln_step"


def _rmsnorm(x, w):
    xf = x.astype(jnp.float32)
    y = xf * lax.rsqrt(jnp.mean(xf * xf, axis=-1, keepdims=True) + EPS)
    return (y * w.astype(jnp.float32)).astype(x.dtype)


def _l2norm(x):
    return x * lax.rsqrt(jnp.sum(x * x, axis=-1, keepdims=True) + EPS)


def _adaln(c, w_mod, b_mod):
    m = jax.nn.silu(c) @ w_mod + b_mod
    return jnp.split(m[:, None, :], 6, axis=-1)


def _split_in(z):
    parts, o = [], 0
    for w_ in IN_WIDTHS:
        parts.append(z[..., o:o + w_])
        o += w_
    return parts


def _causal_dwconv(x_ext, w, n_out):
    width = w.shape[0]
    return sum(w[j] * x_ext[:, j:j + n_out] for j in range(width))


def _gdn_features(qkv_conv, beta_raw, a_raw, a_log, dt_bias):
    f32 = jnp.float32
    B_, L = qkv_conv.shape[:2]
    act = jax.nn.silu(qkv_conv).astype(f32)
    q = act[..., :GDN_QK].reshape(B_, L, GDN_HEADS, GDN_DK)
    k = act[..., GDN_QK:2 * GDN_QK].reshape(B_, L, GDN_HEADS, GDN_DK)
    v = act[..., 2 * GDN_QK:].reshape(B_, L, GDN_HEADS, GDN_DV)
    q = _l2norm(q) * (GDN_DK ** -0.5)
    k = _l2norm(k)
    beta = jax.nn.sigmoid(beta_raw.astype(f32))
    g = -jnp.exp(a_log.astype(f32)) * jax.nn.softplus(a_raw.astype(f32) + dt_bias.astype(f32))
    return q, k, v, beta, g


def _gdn_chunked(q, k, v, beta, g):
    f32 = jnp.float32
    B_, L = q.shape[:2]
    C = GDN_CHUNK
    N = L // C

    def blk(t):
        return jnp.moveaxis(t.reshape((B_, N, C) + t.shape[2:]), 3, 1)

    q, k, v, beta, g = blk(q), blk(k), blk(v), blk(beta), blk(g)
    decay = jnp.cumsum(g, axis=-1)
    idx = jnp.arange(C)
    tril = idx[:, None] >= idx[None, :]
    strict = idx[:, None] > idx[None, :]
    diff = decay[..., :, None] - decay[..., None, :]
    gam = jnp.where(tril, jnp.exp(jnp.where(tril, diff, 0.0)), 0.0)
    kb = k * beta[..., None]
    m = jnp.where(strict, jnp.einsum('bhnik,bhnjk->bhnij', kb, k) * gam, 0.0)
    eye = jnp.eye(C, dtype=f32)
    t_inv = lax.linalg.triangular_solve(eye + m, jnp.broadcast_to(eye, m.shape),
                                        left_side=True, lower=True, unit_diagonal=True)
    u = jnp.einsum('bhnij,bhnjv->bhniv', t_inv, v * beta[..., None])
    w = jnp.einsum('bhnij,bhnjk->bhnik', t_inv, kb * jnp.exp(decay)[..., None])
    a_intra = jnp.where(tril, jnp.einsum('bhnik,bhnjk->bhnij', q, k) * gam, 0.0)
    q_dec = q * jnp.exp(decay)[..., None]
    k_dec = k * jnp.exp(decay[..., -1:] - decay)[..., None]
    last = jnp.exp(decay[..., -1])
    xs = tuple(jnp.moveaxis(t, 2, 0) for t in (u, w, a_intra, q_dec, k_dec, last))

    def step(S, inp):
        u_n, w_n, a_n, qd_n, kd_n, last_n = inp
        v_new = u_n - jnp.einsum('bhck,bhkv->bhcv', w_n, S)
        o = jnp.einsum('bhck,bhkv->bhcv', qd_n, S) + jnp.einsum('bhij,bhjv->bhiv', a_n, v_new)
        S = S * last_n[..., None, None] + jnp.einsum('bhck,bhcv->bhkv', kd_n, v_new)
        return S, o

    S0 = jnp.zeros((B_, GDN_HEADS, GDN_DK, GDN_DV), f32)
    S, o = lax.scan(step, S0, xs)
    o = jnp.transpose(o, (1, 0, 3, 2, 4)).reshape(B_, L, GDN_HEADS, GDN_DV)
    return o, S


def _gdn_recurrent(q, k, v, beta, g, S0):
    xs = tuple(jnp.moveaxis(t, 1, 0) for t in (q, k, v, beta, g))

    def step(S, inp):
        q_t, k_t, v_t, b_t, g_t = inp
        S = S * jnp.exp(g_t)[..., None, None]
        delta = (v_t - jnp.einsum('bhk,bhkv->bhv', k_t, S)) * b_t[..., None]
        S = S + jnp.einsum('bhk,bhv->bhkv', k_t, delta)
        return S, jnp.einsum('bhk,bhkv->bhv', q_t, S)

    S, o = lax.scan(step, S0.astype(jnp.float32), xs)
    return jnp.moveaxis(o, 0, 1), S


def _gdn_out(o, gate, w):
    B_, L = o.shape[:2]
    on = o * lax.rsqrt(jnp.mean(o * o, axis=-1, keepdims=True) + EPS) * w.astype(jnp.float32)
    gf = jax.nn.silu(gate.astype(jnp.float32)).reshape(B_, L, GDN_HEADS, GDN_DV)
    return (on * gf).reshape(B_, L, GDN_V).astype(gate.dtype)


def _sink_attention(qg, kb, vb, mask, sinks):
    s = jnp.einsum('...qhgd,...khd->...hgqk', qg, kb).astype(jnp.float32) * (SWA_HD ** -0.5)
    s = jnp.where(mask, s, -jnp.inf)
    sink = jnp.broadcast_to(sinks.astype(jnp.float32).reshape(SWA_KV_HEADS, SWA_GROUP, 1, 1),
                            s.shape[:-1] + (1,))
    p = jax.nn.softmax(jnp.concatenate([s, sink], axis=-1), axis=-1)[..., :-1]
    return jnp.einsum('...hgqk,...khd->...qhgd', p.astype(vb.dtype), vb)


def _swa_banded(qg, k, v, sinks):
    B_, L = qg.shape[:2]
    nb = L // WINDOW
    qb = qg.reshape(B_, nb, WINDOW, SWA_KV_HEADS, SWA_GROUP, SWA_HD)

    def band(t):
        tp = jnp.concatenate([jnp.zeros_like(t[:, :WINDOW]), t], axis=1)
        tp = tp.reshape(B_, nb + 1, WINDOW, SWA_KV_HEADS, SWA_HD)
        return jnp.concatenate([tp[:, :-1], tp[:, 1:]], axis=2)

    blocks = jnp.arange(nb)[:, None]
    qabs = blocks * WINDOW + jnp.arange(WINDOW)[None, :]
    kabs = (blocks - 1) * WINDOW + jnp.arange(2 * WINDOW)[None, :]
    d = qabs[:, :, None] - kabs[:, None, :]
    mask = (d >= 0) & (d < WINDOW) & (kabs[:, None, :] >= 0)
    o = _sink_attention(qb, band(k), band(v), mask[None, :, None, None], sinks)
    return o.reshape(B_, L, SWA_Q)


def _layer(x, c, lp, st):
    B_, L, _ = x.shape
    sh1, sc1, gt1, sh2, sc2, gt2 = _adaln(c, lp['w_mod'], lp['b_mod'])
    h = _rmsnorm(x, lp['norm1_w']) * (1 + sc1) + sh1
    qkv, gdn_gate, beta_raw, a_raw, sq, sk, sv, ga, gb = _split_in(h @ lp['w_in'])

    if st is None:
        conv_prev = jnp.zeros((B_, GDN_CONV - 1, GDN_CONV_CH), qkv.dtype)
    else:
        conv_prev = st['gdn_conv'].astype(qkv.dtype)
    qkv_ext = jnp.concatenate([conv_prev, qkv], axis=1)
    new_gdn_conv = qkv_ext[:, -(GDN_CONV - 1):]
    q, k, v, beta, g = _gdn_features(_causal_dwconv(qkv_ext, lp['gdn_conv_w'], L),
                                     beta_raw, a_raw, lp['gdn_a_log'], lp['gdn_dt_bias'])
    if st is None:
        o_a, S = _gdn_chunked(q, k, v, beta, g)
    else:
        o_a, S = _gdn_recurrent(q, k, v, beta, g, st['gdn_S'])
    y_a = _gdn_out(o_a, gdn_gate, lp['gdn_onorm_w']) @ lp['w_gdn_out']

    qg = sq.reshape(B_, L, SWA_KV_HEADS, SWA_GROUP, SWA_HD)
    kk = sk.reshape(B_, L, SWA_KV_HEADS, SWA_HD)
    vv = sv.reshape(B_, L, SWA_KV_HEADS, SWA_HD)
    if st is None:
        o_b = _swa_banded(qg, kk, vv, lp['swa_sinks'])
        new_k, new_v = kk[:, -WINDOW:], vv[:, -WINDOW:]
    else:
        kc = jnp.concatenate([st['k'].astype(kk.dtype), kk], axis=1)
        vc = jnp.concatenate([st['v'].astype(vv.dtype), vv], axis=1)
        qpos = WINDOW + jnp.arange(L)
        kpos = jnp.arange(WINDOW + L)
        d = qpos[:, None] - kpos[None, :]
        mask = (d >= 0) & (d < WINDOW)
        o_b = _sink_attention(qg, kc, vc, mask[None, None, None], lp['swa_sinks']).reshape(B_, L, SWA_Q)
        new_k, new_v = kc[:, -WINDOW:], vc[:, -WINDOW:]
    y_b = o_b @ lp['w_swa_out']

    mix = (jax.nn.sigmoid(ga) * y_a + jax.nn.sigmoid(gb) * y_b) @ lp['w_o']
    x = x + gt1 * mix

    h2 = _rmsnorm(x, lp['norm2_w']) * (1 + sc2) + sh2
    gate = h2 @ lp['w_ffn_gate']
    up = h2 @ lp['w_ffn_up']
    if st is None:
        ffn_prev = jnp.zeros((B_, FFN_CONV - 1, D_FF), gate.dtype)
    else:
        ffn_prev = st['ffn_conv'].astype(gate.dtype)
    gate_ext = jnp.concatenate([ffn_prev, gate], axis=1)
    new_ffn_conv = gate_ext[:, -(FFN_CONV - 1):]
    gc = _causal_dwconv(gate_ext, lp['ffn_conv_w'], L) + lp['ffn_conv_b']
    x = x + gt2 * ((jax.nn.silu(gc) * up) @ lp['w_ffn_down'])
    return x, (S.astype(x.dtype), new_gdn_conv, new_k, new_v, new_ffn_conv)


def setup_inputs(seed: int = 0) -> dict:
    key = jax.random.key(seed)
    ks = jax.random.split(key, 32)
    f32 = jnp.float32
    D = D_MODEL

    def nrm(k, shape, s=1.0):
        return jax.random.normal(k, shape, f32) * s

    dt = jnp.exp(jax.random.uniform(ks[12], (DEPTH, GDN_HEADS), f32, math.log(1e-3), math.log(1e-1)))
    return {
        'x_prompt': nrm(ks[0], (BATCH, SEQ, D)),
        'x_sample': nrm(ks[1], (DEC_BATCH, DEC_SEQ, D)),
        'c_prompt': nrm(ks[2], (BATCH, D)),
        'c_sample': nrm(ks[3], (DEC_BATCH, D)),
        'state_gdn_S': nrm(ks[4], (DEPTH, DEC_BATCH, GDN_HEADS, GDN_DK, GDN_DV), 0.1),
        'state_gdn_conv': nrm(ks[5], (DEPTH, DEC_BATCH, GDN_CONV - 1, GDN_CONV_CH)),
        'cache_swa_k': nrm(ks[6], (DEPTH, DEC_BATCH, WINDOW, SWA_KV_HEADS, SWA_HD)),
        'cache_swa_v': nrm(ks[7], (DEPTH, DEC_BATCH, WINDOW, SWA_KV_HEADS, SWA_HD)),
        'state_ffn_conv': nrm(ks[8], (DEPTH, DEC_BATCH, FFN_CONV - 1, D_FF)),
        'w_mod': nrm(ks[9], (DEPTH, D, 6 * D), 0.5 * D ** -0.5),
        'b_mod': nrm(ks[10], (DEPTH, 6 * D), 0.02),
        'norm1_w': 1.0 + nrm(ks[11], (DEPTH, D), 0.05),
        'norm2_w': 1.0 + nrm(ks[13], (DEPTH, D), 0.05),
        'w_in': nrm(ks[14], (DEPTH, D, IN_WIDTH), D ** -0.5),
        'gdn_conv_w': nrm(ks[15], (DEPTH, GDN_CONV, GDN_CONV_CH), GDN_CONV ** -0.5),
        'gdn_a_log': jnp.log(jax.random.uniform(ks[16], (DEPTH, GDN_HEADS), f32, 1.0, 16.0)),
        'gdn_dt_bias': dt + jnp.log(-jnp.expm1(-dt)),
        'gdn_onorm_w': 1.0 + nrm(ks[17], (DEPTH, GDN_DV), 0.05),
        'w_gdn_out': nrm(ks[18], (DEPTH, GDN_V, D), GDN_V ** -0.5),
        'swa_sinks': nrm(ks[19], (DEPTH, SWA_Q_HEADS)),
        'w_swa_out': nrm(ks[20], (DEPTH, SWA_Q, D), SWA_Q ** -0.5),
        'w_o': nrm(ks[21], (DEPTH, D, D), D ** -0.5),
        'w_ffn_gate': nrm(ks[22], (DEPTH, D, D_FF), D ** -0.5),
        'w_ffn_up': nrm(ks[23], (DEPTH, D, D_FF), D ** -0.5),
        'ffn_conv_w': nrm(ks[24], (DEPTH, FFN_CONV, D_FF), FFN_CONV ** -0.5),
        'ffn_conv_b': nrm(ks[25], (DEPTH, D_FF), 0.02),
        'w_ffn_down': nrm(ks[26], (DEPTH, D_FF, D), D_FF ** -0.5),
        'final_norm_w': 1.0 + nrm(ks[27], (D,), 0.05),
    }


def reference(x_prompt, x_sample, c_prompt, c_sample, state_gdn_S, state_gdn_conv, cache_swa_k,
              cache_swa_v, state_ffn_conv, w_mod, b_mod, norm1_w, norm2_w, w_in, gdn_conv_w,
              gdn_a_log, gdn_dt_bias, gdn_onorm_w, w_gdn_out, swa_sinks, w_swa_out, w_o,
              w_ffn_gate, w_ffn_up, ffn_conv_w, ffn_conv_b, w_ffn_down, final_norm_w):
    xp, xs = x_prompt, x_sample
    new_p = [[] for _ in range(5)]
    new_s = [[] for _ in range(5)]
    for l in range(DEPTH):
        lp = dict(w_mod=w_mod[l], b_mod=b_mod[l], norm1_w=norm1_w[l], norm2_w=norm2_w[l],
                  w_in=w_in[l], gdn_conv_w=gdn_conv_w[l], gdn_a_log=gdn_a_log[l],
                  gdn_dt_bias=gdn_dt_bias[l], gdn_onorm_w=gdn_onorm_w[l], w_gdn_out=w_gdn_out[l],
                  swa_sinks=swa_sinks[l], w_swa_out=w_swa_out[l], w_o=w_o[l],
                  w_ffn_gate=w_ffn_gate[l], w_ffn_up=w_ffn_up[l], ffn_conv_w=ffn_conv_w[l],
                  ffn_conv_b=ffn_conv_b[l], w_ffn_down=w_ffn_down[l])
        st = dict(gdn_S=state_gdn_S[l], gdn_conv=state_gdn_conv[l], k=cache_swa_k[l],
                  v=cache_swa_v[l], ffn_conv=state_ffn_conv[l])
        xp, sp = _layer(xp, c_prompt, lp, None)
        xs, ss = _layer(xs, c_sample, lp, st)
        for i in range(5):
            new_p[i].append(sp[i])
            new_s[i].append(ss[i])
    y_prompt = _rmsnorm(xp, final_norm_w)
    y_sample = _rmsnorm(xs, final_norm_w)
    gS_p, gc_p, k_p, v_p, f_p = [jnp.stack(a) for a in new_p]
    gS_s, gc_s, k_s, v_s, f_s = [jnp.stack(a) for a in new_s]
    return (y_prompt, y_sample, gS_p, gS_s, gc_p, gc_s, k_p, k_s, v_p, v_s, f_p, f_s)
```

```python
import functools

import jax
import jax.numpy as jnp
from jax import lax
from jax.experimental import pallas as pl
from jax.experimental.pallas import tpu as pltpu

F32 = jnp.float32
BF16 = jnp.bfloat16

D_MODEL = 1024
GDN_HEADS = 8
GDN_DK = 128
GDN_DV = 128
GDN_QK = GDN_HEADS * GDN_DK
GDN_V = GDN_HEADS * GDN_DV
GDN_CONV = 4
GDN_CONV_CH = 2 * GDN_QK + GDN_V
GDN_SECTIONS = GDN_CONV_CH // 128
SWA_Q_HEADS = 16
SWA_KV_HEADS = 4
SWA_GROUP = SWA_Q_HEADS // SWA_KV_HEADS
SWA_HD = 64
SWA_Q = SWA_Q_HEADS * SWA_HD
SWA_KV = SWA_KV_HEADS * SWA_HD
WINDOW = 128
D_FF = 2816
FFN_CONV = 3
EPS = 1e-6

LANES = 128
SUBLANES = 8
VMEM_LIMIT = 56 * 1024 * 1024

CHUNK = 128
FFN_COLS = 256


def _mm(a, b):
    return jnp.dot(a.astype(BF16), b.astype(BF16), preferred_element_type=F32)


def _mm_nt(a, b):
    return lax.dot_general(a.astype(BF16), b.astype(BF16), (((1,), (1,)), ((), ())),
                           preferred_element_type=F32)


def _silu(x):
    return x * jax.nn.sigmoid(x)


def _softplus(x):
    return jnp.maximum(x, 0.0) + jnp.log1p(jnp.exp(-jnp.abs(x)))


def _rms(x, w):
    return x * lax.rsqrt(jnp.mean(x * x, axis=-1, keepdims=True) + EPS) * w


def _const_spec(shape):
    n = len(shape)
    return pl.BlockSpec(shape, lambda *_: (0,) * n, pipeline_mode=pl.Buffered(1))


def _params(sem):
    return pltpu.CompilerParams(dimension_semantics=sem, vmem_limit_bytes=VMEM_LIMIT)


def _mod_body(c_ref, w_ref, b_ref, o_ref):
    o_ref[...] = _mm(_silu(c_ref[...]), w_ref[...]) + b_ref[...]


def _modulation(c_all, w_mod, b_mod):
    rows = c_all.shape[0]
    n_out = w_mod.shape[1]
    tn = D_MODEL
    return pl.pallas_call(
        _mod_body,
        grid=(n_out // tn,),
        in_specs=[pl.BlockSpec((rows, D_MODEL), lambda j: (0, 0)),
                  pl.BlockSpec((D_MODEL, tn), lambda j: (0, j)),
                  pl.BlockSpec((1, tn), lambda j: (0, j))],
        out_specs=pl.BlockSpec((rows, tn), lambda j: (0, j)),
        out_shape=jax.ShapeDtypeStruct((rows, n_out), F32),
        compiler_params=_params(("arbitrary",)),
        name="modulation",
    )(c_all, w_mod, b_mod)


def _inproj_body(x_ref, sh_ref, sc_ref, nw_ref, wqkv_ref, wgg_ref, wba_ref, wsq_ref, wskv_ref, wgab_ref,
                 qkv_ref, gg_ref, ba_ref, sq_ref, skv_ref, gab_ref):
    h = _rms(x_ref[...], nw_ref[...]) * (1.0 + sc_ref[...]) + sh_ref[...]
    hb = h.astype(BF16)

    def proj(w_ref, lo, width):
        return jnp.dot(hb, w_ref[:, lo:lo + width], preferred_element_type=F32)

    step = 512
    for c in range(GDN_CONV_CH // step):
        z = proj(wqkv_ref, c * step, step)
        for k in range(step // LANES):
            qkv_ref[c * (step // LANES) + k] = z[:, k * LANES:(k + 1) * LANES]
    for c in range(GDN_V // step):
        z = proj(wgg_ref, c * step, step)
        for k in range(step // LANES):
            gg_ref[c * (step // LANES) + k] = z[:, k * LANES:(k + 1) * LANES]
    ba_ref[...] = proj(wba_ref, 0, LANES)
    for c in range(SWA_Q // step):
        sq_ref[:, c * step:(c + 1) * step] = proj(wsq_ref, c * step, step)
    skv_ref[...] = proj(wskv_ref, 0, 2 * SWA_KV)
    for c in range(2 * D_MODEL // step):
        gab_ref[:, c * step:(c + 1) * step] = proj(wgab_ref, c * step, step)


def _inproj(x, sh, sc, nw, ws, tm):
    b_, l_, _ = x.shape
    r_ = sh.shape[1]
    rt = 1 if r_ == 1 else tm
    mod_map = (lambda b, t: (b, 0, 0)) if r_ == 1 else (lambda b, t: (b, t, 0))
    row_map = lambda b, t: (b, t, 0)
    head_map = lambda b, t: (b, 0, t, 0)
    wqkv, wgg, wba, wsq, wskv, wgab = ws
    out_shape = (
        jax.ShapeDtypeStruct((b_, GDN_SECTIONS, l_, LANES), F32),
        jax.ShapeDtypeStruct((b_, GDN_HEADS, l_, LANES), F32),
        jax.ShapeDtypeStruct((b_, l_, LANES), F32),
        jax.ShapeDtypeStruct((b_, l_, SWA_Q), F32),
        jax.ShapeDtypeStruct((b_, l_, 2 * SWA_KV), F32),
        jax.ShapeDtypeStruct((b_, l_, 2 * D_MODEL), F32),
    )
    out_specs = (
        pl.BlockSpec((None, GDN_SECTIONS, tm, LANES), head_map),
        pl.BlockSpec((None, GDN_HEADS, tm, LANES), head_map),
        pl.BlockSpec((None, tm, LANES), row_map),
        pl.BlockSpec((None, tm, SWA_Q), row_map),
        pl.BlockSpec((None, tm, 2 * SWA_KV), row_map),
        pl.BlockSpec((None, tm, 2 * D_MODEL), row_map),
    )
    in_specs = [
        pl.BlockSpec((None, tm, D_MODEL), row_map),
        pl.BlockSpec((None, rt, D_MODEL), mod_map),
        pl.BlockSpec((None, rt, D_MODEL), mod_map),
        _const_spec(nw.shape),
    ] + [_const_spec(w.shape) for w in ws]
    return pl.pallas_call(
        _inproj_body,
        grid=(b_, l_ // tm),
        in_specs=in_specs,
        out_specs=out_specs,
        out_shape=out_shape,
        compiler_params=_params(("arbitrary", "arbitrary")),
        name="inproj",
    )(x, sh, sc, nw, *ws)


def _delta_gates(ba, alog_row, dtb_row):
    beta_all = jax.nn.sigmoid(ba)
    g_all = -jnp.exp(alog_row) * _softplus(ba + dtb_row)
    return beta_all, g_all


def _lane_column(x, lane_idx, lane):
    return jnp.sum(jnp.where(lane_idx == lane, x, 0.0), axis=1, keepdims=True)


def _unit_lower_inverse(m, row, col):
    n = m.shape[0]
    eye = row == col
    t = jnp.where(eye, 1.0, 0.0) - jnp.where((row == col + 1) & ((row & 1) == 1), m, 0.0)
    half = 2
    while half < n:
        full = 2 * half
        same_block = (row & -full) == (col & -full)
        off = same_block & ((row & half) != 0) & ((col & half) == 0)
        o = jnp.where(off, m, 0.0)
        t = t - _mm(t, _mm(o, t))
        half = full
    return t


def _gdn_chunk(q, k, v, beta_col, g_col, s_prev, row, col):
    c = q.shape[0]
    eye = row == col
    tril = row >= col
    g_cb = jnp.broadcast_to(g_col, (c, c))
    g_row = jnp.sum(jnp.where(eye, g_cb, 0.0), axis=0, keepdims=True)
    g_rb = jnp.broadcast_to(g_row, (c, c))
    dec_col = jnp.sum(jnp.where(tril, g_rb, 0.0), axis=1, keepdims=True)
    dec_row = jnp.sum(jnp.where(row <= col, g_cb, 0.0), axis=0, keepdims=True)
    gam = jnp.where(tril, jnp.exp(jnp.where(tril, dec_col - dec_row, 0.0)), 0.0)
    e_col = jnp.exp(dec_col)
    dec_last = dec_col[c - 1:c, :]
    kb = k * beta_col
    gram = _mm_nt(jnp.concatenate([kb, q], axis=0), k)
    m = jnp.where(row > col, gram[:c] * gam, 0.0)
    a_intra = jnp.where(tril, gram[c:] * gam, 0.0)
    t_inv = _unit_lower_inverse(m, row, col)
    uw = _mm(t_inv, jnp.concatenate([v * beta_col, kb * e_col], axis=1))
    u, w = uw[:, :GDN_DV], uw[:, GDN_DV:]
    ws_qs = _mm(jnp.concatenate([w, q * e_col], axis=0), s_prev)
    v_new = u - ws_qs[:c]
    o = ws_qs[c:] + _mm(a_intra, v_new)
    k_dec = k * jnp.exp(dec_last - dec_col)
    s_new = s_prev * jnp.exp(dec_last) + _mm(k_dec.T, v_new)
    return o, s_new


def _gated_out_norm(o, gate, onw):
    on = o * lax.rsqrt(jnp.mean(o * o, axis=-1, keepdims=True) + EPS) * onw
    return on * _silu(gate)


def _gdn_prompt_body(hb, lt, q_ref, k_ref, v_ref, cwq_ref, cwk_ref, cwv_ref, ba_ref, alog_ref, dtb_ref,
                     gate_ref, onw_ref, og_ref, s_ref, xe_ref):
    hg = pl.program_id(1)
    t = pl.program_id(2)

    @pl.when(t == 0)
    def _():
        s_ref[...] = jnp.zeros_like(s_ref)
        xe_ref[:, 0:SUBLANES, :] = jnp.zeros((3 * hb, SUBLANES, LANES), F32)

    beta_all, g_all = _delta_gates(ba_ref[...], alog_ref[...], dtb_ref[...])
    lane_idx = lax.broadcasted_iota(jnp.int32, (lt, LANES), 1)
    row = lax.broadcasted_iota(jnp.int32, (CHUNK, CHUNK), 0)
    col = lax.broadcasted_iota(jnp.int32, (CHUNK, CHUNK), 1)
    onw = onw_ref[...]
    for j in range(hb):
        head = hg * hb + j
        beta_col = _lane_column(beta_all, lane_idx, head)
        g_col = _lane_column(g_all, lane_idx, head + GDN_HEADS)
        feats = []
        for s, (x_ref, cw_ref) in enumerate(((q_ref, cwq_ref), (k_ref, cwk_ref), (v_ref, cwv_ref))):
            idx = s * hb + j
            xe_ref[idx, SUBLANES:SUBLANES + lt, :] = x_ref[j]
            w = cw_ref[j]
            y = w[0:1] * xe_ref[idx, SUBLANES - 3:SUBLANES - 3 + lt, :]
            for tap in range(1, GDN_CONV):
                lo = SUBLANES - 3 + tap
                y = y + w[tap:tap + 1] * xe_ref[idx, lo:lo + lt, :]
            xe_ref[idx, 0:SUBLANES, :] = xe_ref[idx, lt:lt + SUBLANES, :]
            feats.append(_silu(y))
        q, k, v = feats
        q = q * lax.rsqrt(jnp.sum(q * q, axis=-1, keepdims=True) + EPS) * (GDN_DK ** -0.5)
        k = k * lax.rsqrt(jnp.sum(k * k, axis=-1, keepdims=True) + EPS)
        for c in range(lt // CHUNK):
            rows = slice(c * CHUNK, (c + 1) * CHUNK)
            o, s_new = _gdn_chunk(q[rows], k[rows], v[rows], beta_col[rows], g_col[rows], s_ref[j], row, col)
            s_ref[j] = s_new
            og = _gated_out_norm(o, gate_ref[j, rows, :], onw)
            og_ref[rows, j * GDN_DV:(j + 1) * GDN_DV] = og.astype(og_ref.dtype)


def _gdn_prompt(qkv4, gg, ba, cw, alog_row, dtb_row, onw, hb, lt):
    b_, _, l_, _ = qkv4.shape
    ng = GDN_HEADS // hb
    sec = lambda s: pl.BlockSpec((None, hb, lt, LANES), lambda b, g, t, s=s: (b, s * ng + g, t, 0))
    cws = lambda s: pl.BlockSpec((hb, GDN_CONV, LANES), lambda b, g, t, s=s: (s * ng + g, 0, 0))
    return pl.pallas_call(
        functools.partial(_gdn_prompt_body, hb, lt),
        grid=(b_, ng, l_ // lt),
        in_specs=[sec(0), sec(1), sec(2), cws(0), cws(1), cws(2),
                  pl.BlockSpec((None, lt, LANES), lambda b, g, t: (b, t, 0)),
                  _const_spec(alog_row.shape), _const_spec(dtb_row.shape),
                  pl.BlockSpec((None, hb, lt, LANES), lambda b, g, t: (b, g, t, 0)),
                  _const_spec(onw.shape)],
        out_specs=(pl.BlockSpec((None, lt, hb * GDN_DV), lambda b, g, t: (b, t, g)),
                   pl.BlockSpec((None, hb, GDN_DK, GDN_DV), lambda b, g, t: (b, g, 0, 0))),
        out_shape=(jax.ShapeDtypeStruct((b_, l_, GDN_V), BF16),
                   jax.ShapeDtypeStruct((b_, GDN_HEADS, GDN_DK, GDN_DV), F32)),
        scratch_shapes=[pltpu.VMEM((3 * hb, lt + SUBLANES, LANES), F32)],
        compiler_params=_params(("arbitrary", "arbitrary", "arbitrary")),
        name="gdn_prompt",
    )(qkv4, qkv4, qkv4, cw, cw, cw, ba, alog_row, dtb_row, gg, onw)


def _gdn_step_body(bb, x_ref, st_ref, cw_ref, ba_ref, alog_ref, dtb_ref, gate_ref, onw_ref, s0_ref,
                   og_ref, sn_ref, q_s, k_s, v_s, b_s, e_s, o_s):
    beta_all, g_all = _delta_gates(ba_ref[...], alog_ref[...], dtb_ref[...])
    lane_idx = lax.broadcasted_iota(jnp.int32, (bb, LANES), 1)
    for h in range(GDN_HEADS):
        feats = []
        for s in range(3):
            idx = s * GDN_HEADS + h
            w = cw_ref[idx]
            y = w[0:1] * st_ref[0, idx]
            for tap in range(1, GDN_CONV - 1):
                y = y + w[tap:tap + 1] * st_ref[tap, idx]
            y = y + w[GDN_CONV - 1:GDN_CONV] * x_ref[idx]
            feats.append(_silu(y))
        q, k, v = feats
        q_s[h] = q * lax.rsqrt(jnp.sum(q * q, axis=-1, keepdims=True) + EPS) * (GDN_DK ** -0.5)
        k_s[h] = k * lax.rsqrt(jnp.sum(k * k, axis=-1, keepdims=True) + EPS)
        v_s[h] = v
        b_s[h] = jnp.broadcast_to(_lane_column(beta_all, lane_idx, h), (bb, LANES))
        e_s[h] = jnp.broadcast_to(jnp.exp(_lane_column(g_all, lane_idx, h + GDN_HEADS)), (bb, LANES))

    eye = (lax.broadcasted_iota(jnp.int32, (GDN_DK, GDN_DK), 0)
           == lax.broadcasted_iota(jnp.int32, (GDN_DK, GDN_DK), 1))

    def to_col(r):
        return jnp.sum(jnp.where(eye, jnp.broadcast_to(r, (GDN_DK, GDN_DK)), 0.0), axis=1, keepdims=True)

    def seq_body(i, carry):
        for h in range(GDN_HEADS):
            one = pl.ds(i, 1)
            k_col = to_col(k_s[h, one, :])
            q_col = to_col(q_s[h, one, :])
            s1 = s0_ref[i, h] * e_s[h, one, :]
            ks = jnp.sum(s1 * k_col, axis=0, keepdims=True)
            delta = (v_s[h, one, :] - ks) * b_s[h, one, :]
            s2 = s1 + k_col * delta
            sn_ref[i, h] = s2
            o_s[h, one, :] = jnp.sum(s2 * q_col, axis=0, keepdims=True)
        return carry

    lax.fori_loop(0, bb, seq_body, 0)
    onw = onw_ref[...]
    for h in range(GDN_HEADS):
        og = _gated_out_norm(o_s[h], gate_ref[h], onw)
        og_ref[:, h * GDN_DV:(h + 1) * GDN_DV] = og.astype(og_ref.dtype)


def _gdn_step(qkv4, st4, gg, ba, cw, alog_row, dtb_row, onw, s0, bb):
    n_ = ba.shape[0]
    vec = pltpu.VMEM((GDN_HEADS, bb, LANES), F32)
    return pl.pallas_call(
        functools.partial(_gdn_step_body, bb),
        grid=(n_ // bb,),
        in_specs=[pl.BlockSpec((GDN_SECTIONS, bb, LANES), lambda i: (0, i, 0)),
                  pl.BlockSpec((GDN_CONV - 1, GDN_SECTIONS, bb, LANES), lambda i: (0, 0, i, 0)),
                  _const_spec(cw.shape),
                  pl.BlockSpec((bb, LANES), lambda i: (i, 0)),
                  _const_spec(alog_row.shape), _const_spec(dtb_row.shape),
                  pl.BlockSpec((GDN_HEADS, bb, LANES), lambda i: (0, i, 0)),
                  _const_spec(onw.shape),
                  pl.BlockSpec((bb, GDN_HEADS, GDN_DK, GDN_DV), lambda i: (i, 0, 0, 0))],
        out_specs=(pl.BlockSpec((bb, GDN_V), lambda i: (i, 0)),
                   pl.BlockSpec((bb, GDN_HEADS, GDN_DK, GDN_DV), lambda i: (i, 0, 0, 0))),
        out_shape=(jax.ShapeDtypeStruct((n_, GDN_V), BF16),
                   jax.ShapeDtypeStruct(s0.shape, F32)),
        scratch_shapes=[vec, vec, vec, vec, vec, vec],
        compiler_params=_params(("arbitrary",)),
        name="gdn_step",
    )(qkv4, st4, cw, ba, alog_row, dtb_row, gg, onw, s0)


def _swa_prompt_body(sinks_ref, q_ref, kvp_ref, kvc_ref, o_ref):
    n = pl.program_id(1)
    w = WINDOW
    q = q_ref[...]
    kvp = kvp_ref[...]
    kvc = kvc_ref[...]
    i = lax.broadcasted_iota(jnp.int32, (SWA_GROUP * w, 2 * w), 0) & (w - 1)
    c = lax.broadcasted_iota(jnp.int32, (SWA_GROUP * w, 2 * w), 1)
    valid = (c > i) & (c <= i + w) & ((c >= w) | (n > 0))
    outs = []
    for g in range(SWA_KV_HEADS):
        ks = slice(g * SWA_HD, (g + 1) * SWA_HD)
        vs = slice(SWA_KV + g * SWA_HD, SWA_KV + (g + 1) * SWA_HD)
        kk = jnp.concatenate([kvp[:, ks], kvc[:, ks]], axis=0)
        vv = jnp.concatenate([kvp[:, vs], kvc[:, vs]], axis=0)
        heads = [g * SWA_GROUP + j for j in range(SWA_GROUP)]
        qs = jnp.concatenate([q[:, h * SWA_HD:(h + 1) * SWA_HD] for h in heads], axis=0)
        s = _mm_nt(qs, kk) * (SWA_HD ** -0.5)
        s = jnp.where(valid, s, -jnp.inf)
        sink = jnp.concatenate([jnp.full((w, 1), sinks_ref[h], F32) for h in heads], axis=0)
        m = jnp.maximum(jnp.max(s, axis=1, keepdims=True), sink)
        e = jnp.exp(s - m)
        den = jnp.sum(e, axis=1, keepdims=True) + jnp.exp(sink - m)
        o = _mm(e / den, vv)
        outs.extend(o[j * w:(j + 1) * w] for j in range(SWA_GROUP))
    o_ref[...] = jnp.concatenate(outs, axis=1).astype(o_ref.dtype)


def _swa_prompt(sq, skv, sinks):
    b_, l_, _ = sq.shape
    nb = l_ // WINDOW
    return pl.pallas_call(
        _swa_prompt_body,
        grid=(b_, nb),
        in_specs=[pl.BlockSpec(memory_space=pltpu.SMEM),
                  pl.BlockSpec((None, WINDOW, SWA_Q), lambda b, n: (b, n, 0)),
                  pl.BlockSpec((None, WINDOW, 2 * SWA_KV), lambda b, n: (b, jnp.maximum(n - 1, 0), 0)),
                  pl.BlockSpec((None, WINDOW, 2 * SWA_KV), lambda b, n: (b, n, 0))],
        out_specs=pl.BlockSpec((None, WINDOW, SWA_Q), lambda b, n: (b, n, 0)),
        out_shape=jax.ShapeDtypeStruct((b_, l_, SWA_Q), BF16),
        compiler_params=_params(("arbitrary", "arbitrary")),
        name="swa_prompt",
    )(sinks, sq, skv, skv)


def _swa_step_body(bb, q_ref, kvn_ref, ck_ref, cv_ref, sink_ref, o_ref, nk_ref, nv_ref):
    w = WINDOW
    row = lax.broadcasted_iota(jnp.int32, (SWA_Q_HEADS, SWA_KV), 0)
    lane = lax.broadcasted_iota(jnp.int32, (SWA_Q_HEADS, SWA_KV), 1)
    own = (lane // SWA_HD) == (row // SWA_GROUP)
    key = lax.broadcasted_iota(jnp.int32, (SWA_Q_HEADS, w), 1)
    sink = sink_ref[...]
    scale = SWA_HD ** -0.5
    for i in range(bb):
        qb = q_ref[i]
        q_bd = jnp.where(own, jnp.concatenate([qb] * SWA_KV_HEADS, axis=1), 0.0)
        kc = ck_ref[i]
        vc = cv_ref[i]
        kn = kvn_ref[i:i + 1, 0:SWA_KV]
        vn = kvn_ref[i:i + 1, SWA_KV:2 * SWA_KV]
        s_c = jnp.where(key >= 1, _mm_nt(q_bd, kc) * scale, -jnp.inf)
        s_n = jnp.sum(q_bd * kn, axis=1, keepdims=True) * scale
        m = jnp.maximum(jnp.maximum(jnp.max(s_c, axis=1, keepdims=True), s_n), sink)
        e_c = jnp.exp(s_c - m)
        e_n = jnp.exp(s_n - m)
        den = jnp.sum(e_c, axis=1, keepdims=True) + e_n + jnp.exp(sink - m)
        pv = jnp.where(own, _mm(e_c / den, vc) + (e_n / den) * vn, 0.0)
        o = pv[:, 0:SWA_HD]
        for g in range(1, SWA_KV_HEADS):
            o = o + pv[:, g * SWA_HD:(g + 1) * SWA_HD]
        o_ref[i] = o
        nk_ref[i, 0:w - 1, :] = ck_ref[i, 1:w, :]
        nk_ref[i, w - 1:w, :] = kn
        nv_ref[i, 0:w - 1, :] = cv_ref[i, 1:w, :]
        nv_ref[i, w - 1:w, :] = vn


def _swa_step(q3, kvn, ck, cv, sink_col, bb):
    n_ = q3.shape[0]
    cache = pl.BlockSpec((bb, WINDOW, SWA_KV), lambda i: (i, 0, 0))
    return pl.pallas_call(
        functools.partial(_swa_step_body, bb),
        grid=(n_ // bb,),
        in_specs=[pl.BlockSpec((bb, SWA_Q_HEADS, SWA_HD), lambda i: (i, 0, 0)),
                  pl.BlockSpec((bb, 2 * SWA_KV), lambda i: (i, 0)),
                  cache, cache,
                  _const_spec(sink_col.shape)],
        out_specs=(pl.BlockSpec((bb, SWA_Q_HEADS, SWA_HD), lambda i: (i, 0, 0)), cache, cache),
        out_shape=(jax.ShapeDtypeStruct(q3.shape, F32),
                   jax.ShapeDtypeStruct(ck.shape, F32),
                   jax.ShapeDtypeStruct(cv.shape, F32)),
        compiler_params=_params(("arbitrary",)),
        name="swa_step",
    )(q3, kvn, ck, cv, sink_col)


def _dense_body(stateful, tm, og_ref, ob_ref, gab_ref, x_ref, gt1_ref, sh2_ref, sc2_ref, gt2_ref,
                n2w_ref, fnw_ref, wa_ref, wb_ref, wo_ref, wg_ref, wu_ref, cw_ref, cb_ref, wd_ref, *rest):
    if stateful:
        st_ref, y_ref, gout_ref, act_ref = rest
    else:
        y_ref, gout_ref, act_ref, gbuf_ref, carry_ref = rest

        @pl.when(pl.program_id(1) == 0)
        def _():
            carry_ref[...] = jnp.zeros_like(carry_ref)

    y_a = jnp.dot(og_ref[...], wa_ref[...], preferred_element_type=F32)
    y_b = jnp.dot(ob_ref[...], wb_ref[...], preferred_element_type=F32)
    merged = (jax.nn.sigmoid(gab_ref[:, 0:D_MODEL]) * y_a
              + jax.nn.sigmoid(gab_ref[:, D_MODEL:2 * D_MODEL]) * y_b)
    x1 = x_ref[...] + gt1_ref[...] * _mm(merged, wo_ref[...])
    h2 = (_rms(x1, n2w_ref[...]) * (1.0 + sc2_ref[...]) + sh2_ref[...]).astype(BF16)

    for c in range(D_FF // FFN_COLS):
        cols = slice(c * FFN_COLS, (c + 1) * FFN_COLS)
        gate = jnp.dot(h2, wg_ref[:, cols], preferred_element_type=F32)
        up = jnp.dot(h2, wu_ref[:, cols], preferred_element_type=F32)
        if stateful:
            g2 = st_ref[0, :, cols]
            g1 = st_ref[1, :, cols]
            gout_ref[:, cols] = gate
        else:
            gbuf_ref[0:SUBLANES, :] = carry_ref[:, cols]
            gbuf_ref[SUBLANES:SUBLANES + tm, :] = gate
            g2 = gbuf_ref[SUBLANES - 2:SUBLANES - 2 + tm, :]
            g1 = gbuf_ref[SUBLANES - 1:SUBLANES - 1 + tm, :]
            carry_ref[:, cols] = gbuf_ref[tm:tm + SUBLANES, :]
        gc = (cw_ref[0:1, cols] * g2 + cw_ref[1:2, cols] * g1 + cw_ref[2:3, cols] * gate) + cb_ref[:, cols]
        act_ref[:, cols] = (_silu(gc) * up).astype(BF16)
    if not stateful:
        gout_ref[...] = carry_ref[...]

    x2 = x1 + gt2_ref[...] * jnp.dot(act_ref[...], wd_ref[...], preferred_element_type=F32)
    y_ref[...] = _rms(x2, fnw_ref[...])


def _dense(og, ob, gab, x, mods, vecs, ws, st, tm):
    b_, l_, _ = x.shape
    r_ = mods[0].shape[1]
    rt = 1 if r_ == 1 else tm
    mod_map = (lambda b, t: (b, 0, 0)) if r_ == 1 else (lambda b, t: (b, t, 0))
    row_map = lambda b, t: (b, t, 0)
    stateful = st is not None
    in_specs = ([pl.BlockSpec((None, tm, D_MODEL), row_map),
                 pl.BlockSpec((None, tm, D_MODEL), row_map),
                 pl.BlockSpec((None, tm, 2 * D_MODEL), row_map),
                 pl.BlockSpec((None, tm, D_MODEL), row_map)]
                + [pl.BlockSpec((None, rt, D_MODEL), mod_map)] * 4
                + [_const_spec(a.shape) for a in vecs[:2]]
                + [_const_spec(ws[0].shape), _const_spec(ws[1].shape), _const_spec(ws[2].shape),
                   _const_spec(ws[3].shape), _const_spec(ws[4].shape),
                   _const_spec(vecs[2].shape), _const_spec(vecs[3].shape), _const_spec(ws[5].shape)])
    args = [og, ob, gab, x, *mods, vecs[0], vecs[1], ws[0], ws[1], ws[2], ws[3], ws[4], vecs[2], vecs[3], ws[5]]
    scratch = [pltpu.VMEM((tm, D_FF), BF16)]
    if stateful:
        in_specs.append(pl.BlockSpec((FFN_CONV - 1, None, tm, D_FF), lambda b, t: (0, b, t, 0)))
        args.append(st)
        gout_shape = jax.ShapeDtypeStruct((b_, l_, D_FF), F32)
        gout_spec = pl.BlockSpec((None, tm, D_FF), row_map)
    else:
        scratch += [pltpu.VMEM((tm + SUBLANES, FFN_COLS), F32), pltpu.VMEM((SUBLANES, D_FF), F32)]
        gout_shape = jax.ShapeDtypeStruct((b_, SUBLANES, D_FF), F32)
        gout_spec = pl.BlockSpec((None, SUBLANES, D_FF), lambda b, t: (b, 0, 0))
    return pl.pallas_call(
        functools.partial(_dense_body, stateful, tm),
        grid=(b_, l_ // tm),
        in_specs=in_specs,
        out_specs=(pl.BlockSpec((None, tm, D_MODEL), row_map), gout_spec),
        out_shape=(jax.ShapeDtypeStruct((b_, l_, D_MODEL), F32), gout_shape),
        scratch_shapes=scratch,
        compiler_params=_params(("arbitrary", "arbitrary")),
        name="dense_step" if stateful else "dense_prompt",
    )(*args)


def _lane_row(values, offset):
    return jnp.zeros((1, LANES), F32).at[0, offset:offset + values.shape[0]].set(values)


def kernel(x_prompt, x_sample, c_prompt, c_sample, state_gdn_S, state_gdn_conv, cache_swa_k, cache_swa_v,
           state_ffn_conv, w_mod, b_mod, norm1_w, norm2_w, w_in, gdn_conv_w, gdn_a_log, gdn_dt_bias,
           gdn_onorm_w, w_gdn_out, swa_sinks, w_swa_out, w_o, w_ffn_gate, w_ffn_up, ffn_conv_w, ffn_conv_b,
           w_ffn_down, final_norm_w):
    assert w_mod.shape[0] == 1, "single-layer trunk"
    nb, seq, _ = x_prompt.shape
    ns = x_sample.shape[0]
    assert x_sample.shape[1] == 1

    w_in0 = w_in[0].astype(BF16)
    o0 = GDN_CONV_CH
    o1 = o0 + GDN_V
    o2 = o1 + 2 * GDN_HEADS
    o3 = o2 + SWA_Q
    o4 = o3 + 2 * SWA_KV
    w_ba = jnp.zeros((D_MODEL, LANES), BF16).at[:, :2 * GDN_HEADS].set(w_in0[:, o1:o2])
    in_ws = (w_in0[:, :o0], w_in0[:, o0:o1], w_ba, w_in0[:, o2:o3], w_in0[:, o3:o4], w_in0[:, o4:])
    dense_ws = (w_gdn_out[0].astype(BF16), w_swa_out[0].astype(BF16), w_o[0].astype(BF16),
                w_ffn_gate[0].astype(BF16), w_ffn_up[0].astype(BF16), w_ffn_down[0].astype(BF16))
    dense_vecs = (norm2_w, final_norm_w[None, :], ffn_conv_w[0], ffn_conv_b)
    cw = jnp.transpose(gdn_conv_w[0].reshape(GDN_CONV, GDN_SECTIONS, LANES), (1, 0, 2))
    alog_row = _lane_row(gdn_a_log[0], GDN_HEADS)
    dtb_row = _lane_row(gdn_dt_bias[0], GDN_HEADS)

    mod = _modulation(jnp.concatenate([c_prompt, c_sample], axis=0), w_mod[0].astype(BF16), b_mod)
    mod_p = [mod[:nb, i * D_MODEL:(i + 1) * D_MODEL][:, None, :] for i in range(6)]
    mod_s = [mod[nb:, i * D_MODEL:(i + 1) * D_MODEL][None, :, :] for i in range(6)]

    qkv4, gg, ba, sq, skv, gab = _inproj(x_prompt, mod_p[0], mod_p[1], norm1_w, in_ws, tm=256)
    og, gdn_s_p = _gdn_prompt(qkv4, gg, ba, cw, alog_row, dtb_row, gdn_onorm_w, hb=2, lt=256)
    ob = _swa_prompt(sq, skv, swa_sinks[0])
    y_p, gate_tail = _dense(og, ob, gab, x_prompt, (mod_p[2], mod_p[3], mod_p[4], mod_p[5]),
                            dense_vecs, dense_ws, None, tm=256)
    gdn_conv_p = jnp.transpose(qkv4[:, :, seq - (GDN_CONV - 1):, :], (0, 2, 1, 3)).reshape(
        nb, GDN_CONV - 1, GDN_CONV_CH)
    k_p = skv[:, seq - WINDOW:, :SWA_KV].reshape(nb, WINDOW, SWA_KV_HEADS, SWA_HD)
    v_p = skv[:, seq - WINDOW:, SWA_KV:].reshape(nb, WINDOW, SWA_KV_HEADS, SWA_HD)
    ffn_conv_p = gate_tail[:, SUBLANES - (FFN_CONV - 1):, :]

    xs = x_sample.reshape(1, ns, D_MODEL)
    qkv4s, ggs, bas, sqs, skvs, gabs = _inproj(xs, mod_s[0], mod_s[1], norm1_w, in_ws, tm=ns)
    st4 = jnp.transpose(state_gdn_conv[0].reshape(ns, GDN_CONV - 1, GDN_SECTIONS, LANES), (1, 2, 0, 3))
    og_s, gdn_s_s = _gdn_step(qkv4s[0], st4, ggs[0], bas[0], cw, alog_row, dtb_row, gdn_onorm_w,
                              state_gdn_S[0], bb=8)
    o3, k_s, v_s = _swa_step(sqs[0].reshape(ns, SWA_Q_HEADS, SWA_HD), skvs[0],
                             cache_swa_k[0].reshape(ns, WINDOW, SWA_KV),
                             cache_swa_v[0].reshape(ns, WINDOW, SWA_KV),
                             swa_sinks[0][:, None], bb=8)
    ob_s = o3.reshape(1, ns, SWA_Q).astype(BF16)
    st_ffn = jnp.transpose(state_ffn_conv[0], (1, 0, 2))[:, None]
    y_s, gate_new = _dense(og_s[None], ob_s, gabs, xs, (mod_s[2], mod_s[3], mod_s[4], mod_s[5]),
                           dense_vecs, dense_ws, st_ffn, tm=ns)
    qkv_new = jnp.transpose(qkv4s[0], (1, 0, 2)).reshape(ns, 1, GDN_CONV_CH)
    gdn_conv_s = jnp.concatenate([state_gdn_conv[0][:, 1:], qkv_new], axis=1)
    ffn_conv_s = jnp.concatenate([state_ffn_conv[0][:, 1:], gate_new[0][:, None, :]], axis=1)

    return (y_p, y_s.reshape(ns, 1, D_MODEL),
            gdn_s_p[None], gdn_s_s[None],
            gdn_conv_p[None], gdn_conv_s[None],
            k_p[None], k_s.reshape(ns, WINDOW, SWA_KV_HEADS, SWA_HD)[None],
            v_p[None], v_s.reshape(ns, WINDOW, SWA_KV_HEADS, SWA_HD)[None],
            ffn_conv_p[None], ffn_conv_s[None])
```

```python
import functools

import jax
import jax.numpy as jnp
from jax import lax
from jax.experimental import pallas as pl
from jax.experimental.pallas import tpu as pltpu

F32 = jnp.float32
BF16 = jnp.bfloat16

D_MODEL = 1024
GDN_HEADS = 8
GDN_DK = 128
GDN_DV = 128
GDN_QK = GDN_HEADS * GDN_DK
GDN_V = GDN_HEADS * GDN_DV
GDN_CONV = 4
GDN_CONV_CH = 2 * GDN_QK + GDN_V
GDN_SECTIONS = GDN_CONV_CH // 128
SWA_Q_HEADS = 16
SWA_KV_HEADS = 4
SWA_GROUP = SWA_Q_HEADS // SWA_KV_HEADS
SWA_HD = 64
SWA_Q = SWA_Q_HEADS * SWA_HD
SWA_KV = SWA_KV_HEADS * SWA_HD
WINDOW = 128
D_FF = 2816
FFN_CONV = 3
EPS = 1e-6

LANES = 128
SUBLANES = 8
VMEM_LIMIT = 56 * 1024 * 1024

CHUNK = 128
FFN_COLS = 256


def _mm(a, b):
    return jnp.dot(a.astype(BF16), b.astype(BF16), preferred_element_type=F32)


def _mm_nt(a, b):
    return lax.dot_general(a.astype(BF16), b.astype(BF16), (((1,), (1,)), ((), ())),
                           preferred_element_type=F32)


def _silu(x):
    return x * jax.nn.sigmoid(x)


def _softplus(x):
    return jnp.maximum(x, 0.0) + jnp.log1p(jnp.exp(-jnp.abs(x)))


def _rms(x, w):
    return x * lax.rsqrt(jnp.mean(x * x, axis=-1, keepdims=True) + EPS) * w


def _const_spec(shape):
    n = len(shape)
    return pl.BlockSpec(shape, lambda *_: (0,) * n, pipeline_mode=pl.Buffered(1))


def _params(sem):
    return pltpu.CompilerParams(dimension_semantics=sem, vmem_limit_bytes=VMEM_LIMIT)


def _mod_body(c_ref, w_ref, b_ref, o_ref):
    o_ref[...] = _mm(_silu(c_ref[...]), w_ref[...]) + b_ref[...]


def _modulation(c_all, w_mod, b_mod):
    rows = c_all.shape[0]
    n_out = w_mod.shape[1]
    tn = D_MODEL
    return pl.pallas_call(
        _mod_body,
        grid=(n_out // tn,),
        in_specs=[pl.BlockSpec((rows, D_MODEL), lambda j: (0, 0)),
                  pl.BlockSpec((D_MODEL, tn), lambda j: (0, j)),
                  pl.BlockSpec((1, tn), lambda j: (0, j))],
        out_specs=pl.BlockSpec((rows, tn), lambda j: (0, j)),
        out_shape=jax.ShapeDtypeStruct((rows, n_out), F32),
        compiler_params=_params(("arbitrary",)),
        name="modulation",
    )(c_all, w_mod, b_mod)


def _inproj_body(x_ref, sh_ref, sc_ref, nw_ref, wqkv_ref, wgg_ref, wba_ref, wsq_ref, wskv_ref, wgab_ref,
                 qkv_ref, gg_ref, ba_ref, sq_ref, skv_ref, gab_ref):
    h = _rms(x_ref[...], nw_ref[...]) * (1.0 + sc_ref[...]) + sh_ref[...]
    hb = h.astype(BF16)

    def proj(w_ref, lo, width):
        return jnp.dot(hb, w_ref[:, lo:lo + width], preferred_element_type=F32)

    step = 512
    for c in range(GDN_CONV_CH // step):
        z = proj(wqkv_ref, c * step, step)
        for k in range(step // LANES):
            qkv_ref[c * (step // LANES) + k] = z[:, k * LANES:(k + 1) * LANES]
    for c in range(GDN_V // step):
        z = proj(wgg_ref, c * step, step)
        for k in range(step // LANES):
            gg_ref[c * (step // LANES) + k] = z[:, k * LANES:(k + 1) * LANES]
    ba_ref[...] = proj(wba_ref, 0, LANES)
    for c in range(SWA_Q // step):
        sq_ref[:, c * step:(c + 1) * step] = proj(wsq_ref, c * step, step)
    skv_ref[...] = proj(wskv_ref, 0, 2 * SWA_KV)
    for c in range(2 * D_MODEL // step):
        gab_ref[:, c * step:(c + 1) * step] = proj(wgab_ref, c * step, step)


def _inproj(x, sh, sc, nw, ws, tm):
    b_, l_, _ = x.shape
    r_ = sh.shape[1]
    rt = 1 if r_ == 1 else tm
    mod_map = (lambda b, t: (b, 0, 0)) if r_ == 1 else (lambda b, t: (b, t, 0))
    row_map = lambda b, t: (b, t, 0)
    head_map = lambda b, t: (b, 0, t, 0)
    wqkv, wgg, wba, wsq, wskv, wgab = ws
    out_shape = (
        jax.ShapeDtypeStruct((b_, GDN_SECTIONS, l_, LANES), F32),
        jax.ShapeDtypeStruct((b_, GDN_HEADS, l_, LANES), F32),
        jax.ShapeDtypeStruct((b_, l_, LANES), F32),
        jax.ShapeDtypeStruct((b_, l_, SWA_Q), F32),
        jax.ShapeDtypeStruct((b_, l_, 2 * SWA_KV), F32),
        jax.ShapeDtypeStruct((b_, l_, 2 * D_MODEL), F32),
    )
    out_specs = (
        pl.BlockSpec((None, GDN_SECTIONS, tm, LANES), head_map),
        pl.BlockSpec((None, GDN_HEADS, tm, LANES), head_map),
        pl.BlockSpec((None, tm, LANES), row_map),
        pl.BlockSpec((None, tm, SWA_Q), row_map),
        pl.BlockSpec((None, tm, 2 * SWA_KV), row_map),
        pl.BlockSpec((None, tm, 2 * D_MODEL), row_map),
    )
    in_specs = [
        pl.BlockSpec((None, tm, D_MODEL), row_map),
        pl.BlockSpec((None, rt, D_MODEL), mod_map),
        pl.BlockSpec((None, rt, D_MODEL), mod_map),
        _const_spec(nw.shape),
    ] + [_const_spec(w.shape) for w in ws]
    return pl.pallas_call(
        _inproj_body,
        grid=(b_, l_ // tm),
        in_specs=in_specs,
        out_specs=out_specs,
        out_shape=out_shape,
        compiler_params=_params(("arbitrary", "arbitrary")),
        name="inproj",
    )(x, sh, sc, nw, *ws)


def _delta_gates(ba, alog_row, dtb_row):
    beta_all = jax.nn.sigmoid(ba)
    g_all = -jnp.exp(alog_row) * _softplus(ba + dtb_row)
    return beta_all, g_all


def _lane_column(x, lane_idx, lane):
    return jnp.sum(jnp.where(lane_idx == lane, x, 0.0), axis=1, keepdims=True)


def _unit_lower_inverses(ms, row, col):
    n = ms[0].shape[0]
    eye = jnp.where(row == col, 1.0, 0.0)
    pair = (row == col + 1) & ((row & 1) == 1)
    ts = [eye - jnp.where(pair, m, 0.0) for m in ms]
    half = 2
    while half < n:
        full = 2 * half
        off = ((row & -full) == (col & -full)) & ((row & half) != 0) & ((col & half) == 0)
        tb = [t.astype(BF16) for t in ts]
        xs = [jnp.dot(jnp.where(off, m, 0.0).astype(BF16), t, preferred_element_type=F32) for m, t in zip(ms, tb)]
        ys = [jnp.dot(t, x.astype(BF16), preferred_element_type=F32) for t, x in zip(tb, xs)]
        ts = [t - y for t, y in zip(ts, ys)]
        half = full
    return ts


def _decay_terms(g_col, row, col):
    c = g_col.shape[0]
    tril = row >= col
    g_cb = jnp.broadcast_to(g_col, (c, c))
    g_row = jnp.sum(jnp.where(row == col, g_cb, 0.0), axis=0, keepdims=True)
    g_rb = jnp.broadcast_to(g_row, (c, c))
    dec_col = jnp.sum(jnp.where(tril, g_rb, 0.0), axis=1, keepdims=True)
    dec_row = jnp.sum(jnp.where(row <= col, g_cb, 0.0), axis=0, keepdims=True)
    gam = jnp.where(tril, jnp.exp(jnp.where(tril, dec_col - dec_row, 0.0)), 0.0)
    return dec_col, gam


def _gated_out_norm(o, gate, onw):
    on = o * lax.rsqrt(jnp.mean(o * o, axis=-1, keepdims=True) + EPS) * onw
    return on * _silu(gate)


def _gdn_prompt_body(hb, lt, q_ref, k_ref, v_ref, cwq_ref, cwk_ref, cwv_ref, ba_ref, alog_ref, dtb_ref,
                     gate_ref, onw_ref, og_ref, s_ref, xe_ref):
    hg = pl.program_id(1)
    t = pl.program_id(2)

    @pl.when(t == 0)
    def _():
        s_ref[...] = jnp.zeros_like(s_ref)
        xe_ref[:, 0:SUBLANES, :] = jnp.zeros((3 * hb, SUBLANES, LANES), F32)

    beta_all, g_all = _delta_gates(ba_ref[...], alog_ref[...], dtb_ref[...])
    lane_idx = lax.broadcasted_iota(jnp.int32, (lt, LANES), 1)
    row = lax.broadcasted_iota(jnp.int32, (CHUNK, CHUNK), 0)
    col = lax.broadcasted_iota(jnp.int32, (CHUNK, CHUNK), 1)
    tril = row >= col
    onw = onw_ref[...]
    heads = range(hb)
    chunks = range(lt // CHUNK)

    qs, ks, vs, betas, gs = [], [], [], [], []
    for j in heads:
        head = hg * hb + j
        betas.append(_lane_column(beta_all, lane_idx, head))
        gs.append(_lane_column(g_all, lane_idx, head + GDN_HEADS))
        feats = []
        for s, (x_ref, cw_ref) in enumerate(((q_ref, cwq_ref), (k_ref, cwk_ref), (v_ref, cwv_ref))):
            idx = s * hb + j
            xe_ref[idx, SUBLANES:SUBLANES + lt, :] = x_ref[j]
            w = cw_ref[j]
            y = w[0:1] * xe_ref[idx, SUBLANES - 3:SUBLANES - 3 + lt, :]
            for tap in range(1, GDN_CONV):
                lo = SUBLANES - 3 + tap
                y = y + w[tap:tap + 1] * xe_ref[idx, lo:lo + lt, :]
            xe_ref[idx, 0:SUBLANES, :] = xe_ref[idx, lt:lt + SUBLANES, :]
            feats.append(_silu(y))
        q, k, v = feats
        qs.append(q * lax.rsqrt(jnp.sum(q * q, axis=-1, keepdims=True) + EPS) * (GDN_DK ** -0.5))
        ks.append(k * lax.rsqrt(jnp.sum(k * k, axis=-1, keepdims=True) + EPS))
        vs.append(v)

    blocks = [(c, j) for c in chunks for j in heads]
    pre = {}
    for c, j in blocks:
        rows = slice(c * CHUNK, (c + 1) * CHUNK)
        q, k, v, beta_col = qs[j][rows], ks[j][rows], vs[j][rows], betas[j][rows]
        dec_col, gam = _decay_terms(gs[j][rows], row, col)
        dec_last = dec_col[CHUNK - 1:CHUNK, :]
        e_col = jnp.exp(dec_col)
        kb = k * beta_col
        pre[c, j] = dict(q=q, k=k, gam=gam, kb=kb, qe=q * e_col, e_last=jnp.exp(dec_last),
                         kd=k * jnp.exp(dec_last - dec_col),
                         rhs=jnp.concatenate([v * beta_col, kb * e_col], axis=1).astype(BF16))
    grams = [_mm_nt(jnp.concatenate([pre[b]["kb"], pre[b]["q"]], axis=0), pre[b]["k"]) for b in blocks]
    ms = [jnp.where(row > col, g[:CHUNK] * pre[b]["gam"], 0.0) for g, b in zip(grams, blocks)]
    a_intra = {b: jnp.where(tril, g[CHUNK:] * pre[b]["gam"], 0.0) for g, b in zip(grams, blocks)}
    t_inv = _unit_lower_inverses(ms, row, col)
    uw = {b: jnp.dot(t.astype(BF16), pre[b]["rhs"], preferred_element_type=F32) for t, b in zip(t_inv, blocks)}

    for c in chunks:
        rows = slice(c * CHUNK, (c + 1) * CHUNK)
        s_prev = [s_ref[j] for j in heads]
        ws_qs = [_mm(jnp.concatenate([uw[c, j][:, GDN_DV:], pre[c, j]["qe"]], axis=0), s_prev[j]) for j in heads]
        v_new = [uw[c, j][:, :GDN_DV] - ws_qs[j][:CHUNK] for j in heads]
        o = [ws_qs[j][CHUNK:] + _mm(a_intra[c, j], v_new[j]) for j in heads]
        s_new = [s_prev[j] * pre[c, j]["e_last"] + _mm(pre[c, j]["kd"].T, v_new[j]) for j in heads]
        for j in heads:
            s_ref[j] = s_new[j]
            og = _gated_out_norm(o[j], gate_ref[j, rows, :], onw)
            og_ref[rows, j * GDN_DV:(j + 1) * GDN_DV] = og.astype(og_ref.dtype)


def _gdn_prompt(qkv4, gg, ba, cw, alog_row, dtb_row, onw, hb, lt):
    b_, _, l_, _ = qkv4.shape
    ng = GDN_HEADS // hb
    sec = lambda s: pl.BlockSpec((None, hb, lt, LANES), lambda b, g, t, s=s: (b, s * ng + g, t, 0))
    cws = lambda s: pl.BlockSpec((hb, GDN_CONV, LANES), lambda b, g, t, s=s: (s * ng + g, 0, 0))
    return pl.pallas_call(
        functools.partial(_gdn_prompt_body, hb, lt),
        grid=(b_, ng, l_ // lt),
        in_specs=[sec(0), sec(1), sec(2), cws(0), cws(1), cws(2),
                  pl.BlockSpec((None, lt, LANES), lambda b, g, t: (b, t, 0)),
                  _const_spec(alog_row.shape), _const_spec(dtb_row.shape),
                  pl.BlockSpec((None, hb, lt, LANES), lambda b, g, t: (b, g, t, 0)),
                  _const_spec(onw.shape)],
        out_specs=(pl.BlockSpec((None, lt, hb * GDN_DV), lambda b, g, t: (b, t, g)),
                   pl.BlockSpec((None, hb, GDN_DK, GDN_DV), lambda b, g, t: (b, g, 0, 0))),
        out_shape=(jax.ShapeDtypeStruct((b_, l_, GDN_V), BF16),
                   jax.ShapeDtypeStruct((b_, GDN_HEADS, GDN_DK, GDN_DV), F32)),
        scratch_shapes=[pltpu.VMEM((3 * hb, lt + SUBLANES, LANES), F32)],
        compiler_params=_params(("arbitrary", "arbitrary", "arbitrary")),
        name="gdn_prompt",
    )(qkv4, qkv4, qkv4, cw, cw, cw, ba, alog_row, dtb_row, gg, onw)


def _gdn_step_body(bb, x_ref, st_ref, cw_ref, ba_ref, alog_ref, dtb_ref, gate_ref, onw_ref, s0_ref,
                   og_ref, sn_ref, q_s, k_s, v_s, b_s, e_s, o_s):
    beta_all, g_all = _delta_gates(ba_ref[...], alog_ref[...], dtb_ref[...])
    lane_idx = lax.broadcasted_iota(jnp.int32, (bb, LANES), 1)
    for h in range(GDN_HEADS):
        feats = []
        for s in range(3):
            idx = s * GDN_HEADS + h
            w = cw_ref[idx]
            y = w[0:1] * st_ref[0, idx]
            for tap in range(1, GDN_CONV - 1):
                y = y + w[tap:tap + 1] * st_ref[tap, idx]
            y = y + w[GDN_CONV - 1:GDN_CONV] * x_ref[idx]
            feats.append(_silu(y))
        q, k, v = feats
        q_s[h] = q * lax.rsqrt(jnp.sum(q * q, axis=-1, keepdims=True) + EPS) * (GDN_DK ** -0.5)
        k_s[h] = k * lax.rsqrt(jnp.sum(k * k, axis=-1, keepdims=True) + EPS)
        v_s[h] = v
        b_s[h] = jnp.broadcast_to(_lane_column(beta_all, lane_idx, h), (bb, LANES))
        e_s[h] = jnp.broadcast_to(jnp.exp(_lane_column(g_all, lane_idx, h + GDN_HEADS)), (bb, LANES))

    eye = (lax.broadcasted_iota(jnp.int32, (GDN_DK, GDN_DK), 0)
           == lax.broadcasted_iota(jnp.int32, (GDN_DK, GDN_DK), 1))

    def to_col(r):
        return jnp.sum(jnp.where(eye, jnp.broadcast_to(r, (GDN_DK, GDN_DK)), 0.0), axis=1, keepdims=True)

    def seq_body(i, carry):
        for h in range(GDN_HEADS):
            one = pl.ds(i, 1)
            k_col = to_col(k_s[h, one, :])
            q_col = to_col(q_s[h, one, :])
            s1 = s0_ref[i, h] * e_s[h, one, :]
            ks = jnp.sum(s1 * k_col, axis=0, keepdims=True)
            delta = (v_s[h, one, :] - ks) * b_s[h, one, :]
            s2 = s1 + k_col * delta
            sn_ref[i, h] = s2
            o_s[h, one, :] = jnp.sum(s2 * q_col, axis=0, keepdims=True)
        return carry

    lax.fori_loop(0, bb, seq_body, 0)
    onw = onw_ref[...]
    for h in range(GDN_HEADS):
        og = _gated_out_norm(o_s[h], gate_ref[h], onw)
        og_ref[:, h * GDN_DV:(h + 1) * GDN_DV] = og.astype(og_ref.dtype)


def _gdn_step(qkv4, st4, gg, ba, cw, alog_row, dtb_row, onw, s0, bb):
    n_ = ba.shape[0]
    vec = pltpu.VMEM((GDN_HEADS, bb, LANES), F32)
    return pl.pallas_call(
        functools.partial(_gdn_step_body, bb),
        grid=(n_ // bb,),
        in_specs=[pl.BlockSpec((GDN_SECTIONS, bb, LANES), lambda i: (0, i, 0)),
                  pl.BlockSpec((GDN_CONV - 1, GDN_SECTIONS, bb, LANES), lambda i: (0, 0, i, 0)),
                  _const_spec(cw.shape),
                  pl.BlockSpec((bb, LANES), lambda i: (i, 0)),
                  _const_spec(alog_row.shape), _const_spec(dtb_row.shape),
                  pl.BlockSpec((GDN_HEADS, bb, LANES), lambda i: (0, i, 0)),
                  _const_spec(onw.shape),
                  pl.BlockSpec((bb, GDN_HEADS, GDN_DK, GDN_DV), lambda i: (i, 0, 0, 0))],
        out_specs=(pl.BlockSpec((bb, GDN_V), lambda i: (i, 0)),
                   pl.BlockSpec((bb, GDN_HEADS, GDN_DK, GDN_DV), lambda i: (i, 0, 0, 0))),
        out_shape=(jax.ShapeDtypeStruct((n_, GDN_V), BF16),
                   jax.ShapeDtypeStruct(s0.shape, F32)),
        scratch_shapes=[vec, vec, vec, vec, vec, vec],
        compiler_params=_params(("arbitrary",)),
        name="gdn_step",
    )(qkv4, st4, cw, ba, alog_row, dtb_row, gg, onw, s0)


def _swa_prompt_body(sinks_ref, q_ref, kvp_ref, kvc_ref, o_ref):
    n = pl.program_id(1)
    w = WINDOW
    q = q_ref[...]
    kvp = kvp_ref[...]
    kvc = kvc_ref[...]
    i = lax.broadcasted_iota(jnp.int32, (SWA_GROUP * w, 2 * w), 0) & (w - 1)
    c = lax.broadcasted_iota(jnp.int32, (SWA_GROUP * w, 2 * w), 1)
    valid = (c > i) & (c <= i + w) & ((c >= w) | (n > 0))
    outs = []
    for g in range(SWA_KV_HEADS):
        ks = slice(g * SWA_HD, (g + 1) * SWA_HD)
        vs = slice(SWA_KV + g * SWA_HD, SWA_KV + (g + 1) * SWA_HD)
        kk = jnp.concatenate([kvp[:, ks], kvc[:, ks]], axis=0)
        vv = jnp.concatenate([kvp[:, vs], kvc[:, vs]], axis=0)
        heads = [g * SWA_GROUP + j for j in range(SWA_GROUP)]
        qs = jnp.concatenate([q[:, h * SWA_HD:(h + 1) * SWA_HD] for h in heads], axis=0)
        s = _mm_nt(qs, kk) * (SWA_HD ** -0.5)
        s = jnp.where(valid, s, -jnp.inf)
        sink = jnp.concatenate([jnp.full((w, 1), sinks_ref[h], F32) for h in heads], axis=0)
        m = jnp.maximum(jnp.max(s, axis=1, keepdims=True), sink)
        e = jnp.exp(s - m)
        den = jnp.sum(e, axis=1, keepdims=True) + jnp.exp(sink - m)
        o = _mm(e / den, vv)
        outs.extend(o[j * w:(j + 1) * w] for j in range(SWA_GROUP))
    o_ref[...] = jnp.concatenate(outs, axis=1).astype(o_ref.dtype)


def _swa_prompt(sq, skv, sinks):
    b_, l_, _ = sq.shape
    nb = l_ // WINDOW
    return pl.pallas_call(
        _swa_prompt_body,
        grid=(b_, nb),
        in_specs=[pl.BlockSpec(memory_space=pltpu.SMEM),
                  pl.BlockSpec((None, WINDOW, SWA_Q), lambda b, n: (b, n, 0)),
                  pl.BlockSpec((None, WINDOW, 2 * SWA_KV), lambda b, n: (b, jnp.maximum(n - 1, 0), 0)),
                  pl.BlockSpec((None, WINDOW, 2 * SWA_KV), lambda b, n: (b, n, 0))],
        out_specs=pl.BlockSpec((None, WINDOW, SWA_Q), lambda b, n: (b, n, 0)),
        out_shape=jax.ShapeDtypeStruct((b_, l_, SWA_Q), BF16),
        compiler_params=_params(("arbitrary", "arbitrary")),
        name="swa_prompt",
    )(sinks, sq, skv, skv)


def _swa_step_body(bb, q_ref, kvn_ref, ck_ref, cv_ref, sink_ref, o_ref, nk_ref, nv_ref):
    w = WINDOW
    row = lax.broadcasted_iota(jnp.int32, (SWA_Q_HEADS, SWA_KV), 0)
    lane = lax.broadcasted_iota(jnp.int32, (SWA_Q_HEADS, SWA_KV), 1)
    own = (lane // SWA_HD) == (row // SWA_GROUP)
    key = lax.broadcasted_iota(jnp.int32, (SWA_Q_HEADS, w), 1)
    sink = sink_ref[...]
    scale = SWA_HD ** -0.5
    for i in range(bb):
        qb = q_ref[i]
        q_bd = jnp.where(own, jnp.concatenate([qb] * SWA_KV_HEADS, axis=1), 0.0)
        kc = ck_ref[i]
        vc = cv_ref[i]
        kn = kvn_ref[i:i + 1, 0:SWA_KV]
        vn = kvn_ref[i:i + 1, SWA_KV:2 * SWA_KV]
        s_c = jnp.where(key >= 1, _mm_nt(q_bd, kc) * scale, -jnp.inf)
        s_n = jnp.sum(q_bd * kn, axis=1, keepdims=True) * scale
        m = jnp.maximum(jnp.maximum(jnp.max(s_c, axis=1, keepdims=True), s_n), sink)
        e_c = jnp.exp(s_c - m)
        e_n = jnp.exp(s_n - m)
        den = jnp.sum(e_c, axis=1, keepdims=True) + e_n + jnp.exp(sink - m)
        pv = jnp.where(own, _mm(e_c / den, vc) + (e_n / den) * vn, 0.0)
        o = pv[:, 0:SWA_HD]
        for g in range(1, SWA_KV_HEADS):
            o = o + pv[:, g * SWA_HD:(g + 1) * SWA_HD]
        o_ref[i] = o
        nk_ref[i, 0:w - 1, :] = ck_ref[i, 1:w, :]
        nk_ref[i, w - 1:w, :] = kn
        nv_ref[i, 0:w - 1, :] = cv_ref[i, 1:w, :]
        nv_ref[i, w - 1:w, :] = vn


def _swa_step(q3, kvn, ck, cv, sink_col, bb):
    n_ = q3.shape[0]
    cache = pl.BlockSpec((bb, WINDOW, SWA_KV), lambda i: (i, 0, 0))
    return pl.pallas_call(
        functools.partial(_swa_step_body, bb),
        grid=(n_ // bb,),
        in_specs=[pl.BlockSpec((bb, SWA_Q_HEADS, SWA_HD), lambda i: (i, 0, 0)),
                  pl.BlockSpec((bb, 2 * SWA_KV), lambda i: (i, 0)),
                  cache, cache,
                  _const_spec(sink_col.shape)],
        out_specs=(pl.BlockSpec((bb, SWA_Q_HEADS, SWA_HD), lambda i: (i, 0, 0)), cache, cache),
        out_shape=(jax.ShapeDtypeStruct(q3.shape, F32),
                   jax.ShapeDtypeStruct(ck.shape, F32),
                   jax.ShapeDtypeStruct(cv.shape, F32)),
        compiler_params=_params(("arbitrary",)),
        name="swa_step",
    )(q3, kvn, ck, cv, sink_col)


def _dense_body(stateful, tm, og_ref, ob_ref, gab_ref, x_ref, gt1_ref, sh2_ref, sc2_ref, gt2_ref,
                n2w_ref, fnw_ref, wa_ref, wb_ref, wo_ref, wg_ref, wu_ref, cw_ref, cb_ref, wd_ref, *rest):
    if stateful:
        st_ref, y_ref, gout_ref, act_ref = rest
    else:
        y_ref, gout_ref, act_ref, gbuf_ref, carry_ref = rest

        @pl.when(pl.program_id(1) == 0)
        def _():
            carry_ref[...] = jnp.zeros_like(carry_ref)

    y_a = jnp.dot(og_ref[...], wa_ref[...], preferred_element_type=F32)
    y_b = jnp.dot(ob_ref[...], wb_ref[...], preferred_element_type=F32)
    merged = (jax.nn.sigmoid(gab_ref[:, 0:D_MODEL]) * y_a
              + jax.nn.sigmoid(gab_ref[:, D_MODEL:2 * D_MODEL]) * y_b)
    x1 = x_ref[...] + gt1_ref[...] * _mm(merged, wo_ref[...])
    h2 = (_rms(x1, n2w_ref[...]) * (1.0 + sc2_ref[...]) + sh2_ref[...]).astype(BF16)

    for c in range(D_FF // FFN_COLS):
        cols = slice(c * FFN_COLS, (c + 1) * FFN_COLS)
        gate = jnp.dot(h2, wg_ref[:, cols], preferred_element_type=F32)
        up = jnp.dot(h2, wu_ref[:, cols], preferred_element_type=F32)
        if stateful:
            g2 = st_ref[0, :, cols]
            g1 = st_ref[1, :, cols]
            gout_ref[:, cols] = gate
        else:
            gbuf_ref[0:SUBLANES, :] = carry_ref[:, cols]
            gbuf_ref[SUBLANES:SUBLANES + tm, :] = gate
            g2 = gbuf_ref[SUBLANES - 2:SUBLANES - 2 + tm, :]
            g1 = gbuf_ref[SUBLANES - 1:SUBLANES - 1 + tm, :]
            carry_ref[:, cols] = gbuf_ref[tm:tm + SUBLANES, :]
        gc = (cw_ref[0:1, cols] * g2 + cw_ref[1:2, cols] * g1 + cw_ref[2:3, cols] * gate) + cb_ref[:, cols]
        act_ref[:, cols] = (_silu(gc) * up).astype(BF16)
    if not stateful:
        gout_ref[...] = carry_ref[...]

    x2 = x1 + gt2_ref[...] * jnp.dot(act_ref[...], wd_ref[...], preferred_element_type=F32)
    y_ref[...] = _rms(x2, fnw_ref[...])


def _dense(og, ob, gab, x, mods, vecs, ws, st, tm):
    b_, l_, _ = x.shape
    r_ = mods[0].shape[1]
    rt = 1 if r_ == 1 else tm
    mod_map = (lambda b, t: (b, 0, 0)) if r_ == 1 else (lambda b, t: (b, t, 0))
    row_map = lambda b, t: (b, t, 0)
    stateful = st is not None
    in_specs = ([pl.BlockSpec((None, tm, D_MODEL), row_map),
                 pl.BlockSpec((None, tm, D_MODEL), row_map),
                 pl.BlockSpec((None, tm, 2 * D_MODEL), row_map),
                 pl.BlockSpec((None, tm, D_MODEL), row_map)]
                + [pl.BlockSpec((None, rt, D_MODEL), mod_map)] * 4
                + [_const_spec(a.shape) for a in vecs[:2]]
                + [_const_spec(ws[0].shape), _const_spec(ws[1].shape), _const_spec(ws[2].shape),
                   _const_spec(ws[3].shape), _const_spec(ws[4].shape),
                   _const_spec(vecs[2].shape), _const_spec(vecs[3].shape), _const_spec(ws[5].shape)])
    args = [og, ob, gab, x, *mods, vecs[0], vecs[1], ws[0], ws[1], ws[2], ws[3], ws[4], vecs[2], vecs[3], ws[5]]
    scratch = [pltpu.VMEM((tm, D_FF), BF16)]
    if stateful:
        in_specs.append(pl.BlockSpec((FFN_CONV - 1, None, tm, D_FF), lambda b, t: (0, b, t, 0)))
        args.append(st)
        gout_shape = jax.ShapeDtypeStruct((b_, l_, D_FF), F32)
        gout_spec = pl.BlockSpec((None, tm, D_FF), row_map)
    else:
        scratch += [pltpu.VMEM((tm + SUBLANES, FFN_COLS), F32), pltpu.VMEM((SUBLANES, D_FF), F32)]
        gout_shape = jax.ShapeDtypeStruct((b_, SUBLANES, D_FF), F32)
        gout_spec = pl.BlockSpec((None, SUBLANES, D_FF), lambda b, t: (b, 0, 0))
    return pl.pallas_call(
        functools.partial(_dense_body, stateful, tm),
        grid=(b_, l_ // tm),
        in_specs=in_specs,
        out_specs=(pl.BlockSpec((None, tm, D_MODEL), row_map), gout_spec),
        out_shape=(jax.ShapeDtypeStruct((b_, l_, D_MODEL), F32), gout_shape),
        scratch_shapes=scratch,
        compiler_params=_params(("arbitrary", "arbitrary")),
        name="dense_step" if stateful else "dense_prompt",
    )(*args)


def _lane_row(values, offset):
    return jnp.zeros((1, LANES), F32).at[0, offset:offset + values.shape[0]].set(values)


def kernel(x_prompt, x_sample, c_prompt, c_sample, state_gdn_S, state_gdn_conv, cache_swa_k, cache_swa_v,
           state_ffn_conv, w_mod, b_mod, norm1_w, norm2_w, w_in, gdn_conv_w, gdn_a_log, gdn_dt_bias,
           gdn_onorm_w, w_gdn_out, swa_sinks, w_swa_out, w_o, w_ffn_gate, w_ffn_up, ffn_conv_w, ffn_conv_b,
           w_ffn_down, final_norm_w):
    assert w_mod.shape[0] == 1, "single-layer trunk"
    nb, seq, _ = x_prompt.shape
    ns = x_sample.shape[0]
    assert x_sample.shape[1] == 1

    w_in0 = w_in[0].astype(BF16)
    o0 = GDN_CONV_CH
    o1 = o0 + GDN_V
    o2 = o1 + 2 * GDN_HEADS
    o3 = o2 + SWA_Q
    o4 = o3 + 2 * SWA_KV
    w_ba = jnp.zeros((D_MODEL, LANES), BF16).at[:, :2 * GDN_HEADS].set(w_in0[:, o1:o2])
    in_ws = (w_in0[:, :o0], w_in0[:, o0:o1], w_ba, w_in0[:, o2:o3], w_in0[:, o3:o4], w_in0[:, o4:])
    dense_ws = (w_gdn_out[0].astype(BF16), w_swa_out[0].astype(BF16), w_o[0].astype(BF16),
                w_ffn_gate[0].astype(BF16), w_ffn_up[0].astype(BF16), w_ffn_down[0].astype(BF16))
    dense_vecs = (norm2_w, final_norm_w[None, :], ffn_conv_w[0], ffn_conv_b)
    cw = jnp.transpose(gdn_conv_w[0].reshape(GDN_CONV, GDN_SECTIONS, LANES), (1, 0, 2))
    alog_row = _lane_row(gdn_a_log[0], GDN_HEADS)
    dtb_row = _lane_row(gdn_dt_bias[0], GDN_HEADS)

    mod = _modulation(jnp.concatenate([c_prompt, c_sample], axis=0), w_mod[0].astype(BF16), b_mod)
    mod_p = [mod[:nb, i * D_MODEL:(i + 1) * D_MODEL][:, None, :] for i in range(6)]
    mod_s = [mod[nb:, i * D_MODEL:(i + 1) * D_MODEL][None, :, :] for i in range(6)]

    qkv4, gg, ba, sq, skv, gab = _inproj(x_prompt, mod_p[0], mod_p[1], norm1_w, in_ws, tm=256)
    og, gdn_s_p = _gdn_prompt(qkv4, gg, ba, cw, alog_row, dtb_row, gdn_onorm_w, hb=8, lt=128)
    ob = _swa_prompt(sq, skv, swa_sinks[0])
    y_p, gate_tail = _dense(og, ob, gab, x_prompt, (mod_p[2], mod_p[3], mod_p[4], mod_p[5]),
                            dense_vecs, dense_ws, None, tm=256)
    gdn_conv_p = jnp.transpose(qkv4[:, :, seq - (GDN_CONV - 1):, :], (0, 2, 1, 3)).reshape(
        nb, GDN_CONV - 1, GDN_CONV_CH)
    k_p = skv[:, seq - WINDOW:, :SWA_KV].reshape(nb, WINDOW, SWA_KV_HEADS, SWA_HD)
    v_p = skv[:, seq - WINDOW:, SWA_KV:].reshape(nb, WINDOW, SWA_KV_HEADS, SWA_HD)
    ffn_conv_p = gate_tail[:, SUBLANES - (FFN_CONV - 1):, :]

    xs = x_sample.reshape(1, ns, D_MODEL)
    qkv4s, ggs, bas, sqs, skvs, gabs = _inproj(xs, mod_s[0], mod_s[1], norm1_w, in_ws, tm=ns)
    st4 = jnp.transpose(state_gdn_conv[0].reshape(ns, GDN_CONV - 1, GDN_SECTIONS, LANES), (1, 2, 0, 3))
    og_s, gdn_s_s = _gdn_step(qkv4s[0], st4, ggs[0], bas[0], cw, alog_row, dtb_row, gdn_onorm_w,
                              state_gdn_S[0], bb=8)
    o3, k_s, v_s = _swa_step(sqs[0].reshape(ns, SWA_Q_HEADS, SWA_HD), skvs[0],
                             cache_swa_k[0].reshape(ns, WINDOW, SWA_KV),
                             cache_swa_v[0].reshape(ns, WINDOW, SWA_KV),
                             swa_sinks[0][:, None], bb=8)
    ob_s = o3.reshape(1, ns, SWA_Q).astype(BF16)
    st_ffn = jnp.transpose(state_ffn_conv[0], (1, 0, 2))[:, None]
    y_s, gate_new = _dense(og_s[None], ob_s, gabs, xs, (mod_s[2], mod_s[3], mod_s[4], mod_s[5]),
                           dense_vecs, dense_ws, st_ffn, tm=ns)
    qkv_new = jnp.transpose(qkv4s[0], (1, 0, 2)).reshape(ns, 1, GDN_CONV_CH)
    gdn_conv_s = jnp.concatenate([state_gdn_conv[0][:, 1:], qkv_new], axis=1)
    ffn_conv_s = jnp.concatenate([state_ffn_conv[0][:, 1:], gate_new[0][:, None, :]], axis=1)

    return (y_p, y_s.reshape(ns, 1, D_MODEL),
            gdn_s_p[None], gdn_s_s[None],
            gdn_conv_p[None], gdn_conv_s[None],
            k_p[None], k_s.reshape(ns, WINDOW, SWA_KV_HEADS, SWA_HD)[None],
            v_p[None], v_s.reshape(ns, WINDOW, SWA_KV_HEADS, SWA_HD)[None],
            ffn_conv_p[None], ffn_conv_s[None])
```

```python
import functools

import jax
import jax.numpy as jnp
from jax import lax
from jax.experimental import pallas as pl
from jax.experimental.pallas import tpu as pltpu

F32 = jnp.float32
BF16 = jnp.bfloat16

D_MODEL = 1024
GDN_HEADS = 8
GDN_DK = 128
GDN_DV = 128
GDN_QK = GDN_HEADS * GDN_DK
GDN_V = GDN_HEADS * GDN_DV
GDN_CONV = 4
GDN_CONV_CH = 2 * GDN_QK + GDN_V
GDN_SECTIONS = GDN_CONV_CH // 128
SWA_Q_HEADS = 16
SWA_KV_HEADS = 4
SWA_GROUP = SWA_Q_HEADS // SWA_KV_HEADS
SWA_HD = 64
SWA_Q = SWA_Q_HEADS * SWA_HD
SWA_KV = SWA_KV_HEADS * SWA_HD
WINDOW = 128
D_FF = 2816
FFN_CONV = 3
EPS = 1e-6

LANES = 128
SUBLANES = 8
VMEM_LIMIT = 56 * 1024 * 1024

CHUNK = 128
FFN_COLS = 256


def _mm(a, b):
    return jnp.dot(a.astype(BF16), b.astype(BF16), preferred_element_type=F32)


def _mm_nt(a, b):
    return lax.dot_general(a.astype(BF16), b.astype(BF16), (((1,), (1,)), ((), ())),
                           preferred_element_type=F32)


def _silu(x):
    return x * jax.nn.sigmoid(x)


def _softplus(x):
    return jnp.maximum(x, 0.0) + jnp.log1p(jnp.exp(-jnp.abs(x)))


def _rms(x, w):
    return x * lax.rsqrt(jnp.mean(x * x, axis=-1, keepdims=True) + EPS) * w


def _const_spec(shape):
    n = len(shape)
    return pl.BlockSpec(shape, lambda *_: (0,) * n, pipeline_mode=pl.Buffered(1))


def _params(sem):
    return pltpu.CompilerParams(dimension_semantics=sem, vmem_limit_bytes=VMEM_LIMIT)


def _mod_body(c_ref, w_ref, b_ref, o_ref):
    o_ref[...] = _mm(_silu(c_ref[...]), w_ref[...]) + b_ref[...]


def _modulation(c_all, w_mod, b_mod):
    rows = c_all.shape[0]
    n_out = w_mod.shape[1]
    tn = D_MODEL
    return pl.pallas_call(
        _mod_body,
        grid=(n_out // tn,),
        in_specs=[pl.BlockSpec((rows, D_MODEL), lambda j: (0, 0)),
                  pl.BlockSpec((D_MODEL, tn), lambda j: (0, j)),
                  pl.BlockSpec((1, tn), lambda j: (0, j))],
        out_specs=pl.BlockSpec((rows, tn), lambda j: (0, j)),
        out_shape=jax.ShapeDtypeStruct((rows, n_out), F32),
        compiler_params=_params(("arbitrary",)),
        name="modulation",
    )(c_all, w_mod, b_mod)


def _inproj_body(x_ref, sh_ref, sc_ref, nw_ref, wqkv_ref, wgg_ref, wba_ref, wsq_ref, wskv_ref, wgab_ref,
                 qkv_ref, gg_ref, ba_ref, sq_ref, skv_ref, gab_ref):
    h = _rms(x_ref[...], nw_ref[...]) * (1.0 + sc_ref[...]) + sh_ref[...]
    hb = h.astype(BF16)

    def proj(w_ref, lo, width):
        return jnp.dot(hb, w_ref[:, lo:lo + width], preferred_element_type=F32)

    step = 512
    for c in range(GDN_CONV_CH // step):
        z = proj(wqkv_ref, c * step, step)
        for k in range(step // LANES):
            qkv_ref[c * (step // LANES) + k] = z[:, k * LANES:(k + 1) * LANES]
    for c in range(GDN_V // step):
        z = proj(wgg_ref, c * step, step)
        for k in range(step // LANES):
            gg_ref[c * (step // LANES) + k] = z[:, k * LANES:(k + 1) * LANES]
    ba_ref[...] = proj(wba_ref, 0, LANES)
    for c in range(SWA_Q // step):
        sq_ref[:, c * step:(c + 1) * step] = proj(wsq_ref, c * step, step)
    skv_ref[...] = proj(wskv_ref, 0, 2 * SWA_KV)
    for c in range(2 * D_MODEL // step):
        gab_ref[:, c * step:(c + 1) * step] = proj(wgab_ref, c * step, step)


def _inproj(x, sh, sc, nw, ws, tm):
    b_, l_, _ = x.shape
    r_ = sh.shape[1]
    rt = 1 if r_ == 1 else tm
    mod_map = (lambda b, t: (b, 0, 0)) if r_ == 1 else (lambda b, t: (b, t, 0))
    row_map = lambda b, t: (b, t, 0)
    head_map = lambda b, t: (b, 0, t, 0)
    wqkv, wgg, wba, wsq, wskv, wgab = ws
    out_shape = (
        jax.ShapeDtypeStruct((b_, GDN_SECTIONS, l_, LANES), F32),
        jax.ShapeDtypeStruct((b_, GDN_HEADS, l_, LANES), F32),
        jax.ShapeDtypeStruct((b_, l_, LANES), F32),
        jax.ShapeDtypeStruct((b_, l_, SWA_Q), F32),
        jax.ShapeDtypeStruct((b_, l_, 2 * SWA_KV), F32),
        jax.ShapeDtypeStruct((b_, l_, 2 * D_MODEL), F32),
    )
    out_specs = (
        pl.BlockSpec((None, GDN_SECTIONS, tm, LANES), head_map),
        pl.BlockSpec((None, GDN_HEADS, tm, LANES), head_map),
        pl.BlockSpec((None, tm, LANES), row_map),
        pl.BlockSpec((None, tm, SWA_Q), row_map),
        pl.BlockSpec((None, tm, 2 * SWA_KV), row_map),
        pl.BlockSpec((None, tm, 2 * D_MODEL), row_map),
    )
    in_specs = [
        pl.BlockSpec((None, tm, D_MODEL), row_map),
        pl.BlockSpec((None, rt, D_MODEL), mod_map),
        pl.BlockSpec((None, rt, D_MODEL), mod_map),
        _const_spec(nw.shape),
    ] + [_const_spec(w.shape) for w in ws]
    return pl.pallas_call(
        _inproj_body,
        grid=(b_, l_ // tm),
        in_specs=in_specs,
        out_specs=out_specs,
        out_shape=out_shape,
        compiler_params=_params(("arbitrary", "arbitrary")),
        name="inproj",
    )(x, sh, sc, nw, *ws)


def _delta_gates(ba, alog_row, dtb_row):
    beta_all = jax.nn.sigmoid(ba)
    g_all = -jnp.exp(alog_row) * _softplus(ba + dtb_row)
    return beta_all, g_all


def _lane_column(x, lane_idx, lane):
    return jnp.sum(jnp.where(lane_idx == lane, x, 0.0), axis=1, keepdims=True)


def _unit_lower_inverses(ms, row, col):
    n = ms[0].shape[0]
    eye = jnp.where(row == col, 1.0, 0.0)
    pair = (row == col + 1) & ((row & 1) == 1)
    ts = [eye - jnp.where(pair, m, 0.0) for m in ms]
    half = 2
    while half < n:
        full = 2 * half
        off = ((row & -full) == (col & -full)) & ((row & half) != 0) & ((col & half) == 0)
        tb = [t.astype(BF16) for t in ts]
        xs = [jnp.dot(jnp.where(off, m, 0.0).astype(BF16), t, preferred_element_type=F32) for m, t in zip(ms, tb)]
        ys = [jnp.dot(t, x.astype(BF16), preferred_element_type=F32) for t, x in zip(tb, xs)]
        ts = [t - y for t, y in zip(ts, ys)]
        half = full
    return ts


def _decay_terms(g_col, row, col):
    c = g_col.shape[0]
    tril = row >= col
    g_cb = jnp.broadcast_to(g_col, (c, c))
    g_row = jnp.sum(jnp.where(row == col, g_cb, 0.0), axis=0, keepdims=True)
    g_rb = jnp.broadcast_to(g_row, (c, c))
    dec_col = jnp.sum(jnp.where(tril, g_rb, 0.0), axis=1, keepdims=True)
    dec_row = jnp.sum(jnp.where(row <= col, g_cb, 0.0), axis=0, keepdims=True)
    gam = jnp.where(tril, jnp.exp(jnp.where(tril, dec_col - dec_row, 0.0)), 0.0)
    return dec_col, gam


def _gated_out_norm(o, gate, onw):
    on = o * lax.rsqrt(jnp.mean(o * o, axis=-1, keepdims=True) + EPS) * onw
    return on * _silu(gate)


def _gdn_prompt_body(hb, lt, q_ref, k_ref, v_ref, cwq_ref, cwk_ref, cwv_ref, ba_ref, alog_ref, dtb_ref,
                     gate_ref, onw_ref, og_ref, s_ref, xe_ref):
    hg = pl.program_id(1)
    t = pl.program_id(2)

    @pl.when(t == 0)
    def _():
        s_ref[...] = jnp.zeros_like(s_ref)
        xe_ref[:, 0:SUBLANES, :] = jnp.zeros((3 * hb, SUBLANES, LANES), F32)

    beta_all, g_all = _delta_gates(ba_ref[...], alog_ref[...], dtb_ref[...])
    lane_idx = lax.broadcasted_iota(jnp.int32, (lt, LANES), 1)
    row = lax.broadcasted_iota(jnp.int32, (CHUNK, CHUNK), 0)
    col = lax.broadcasted_iota(jnp.int32, (CHUNK, CHUNK), 1)
    tril = row >= col
    onw = onw_ref[...]
    heads = range(hb)
    chunks = range(lt // CHUNK)

    qs, ks, vs, betas, gs = [], [], [], [], []
    for j in heads:
        head = hg * hb + j
        betas.append(_lane_column(beta_all, lane_idx, head))
        gs.append(_lane_column(g_all, lane_idx, head + GDN_HEADS))
        feats = []
        for s, (x_ref, cw_ref) in enumerate(((q_ref, cwq_ref), (k_ref, cwk_ref), (v_ref, cwv_ref))):
            idx = s * hb + j
            xe_ref[idx, SUBLANES:SUBLANES + lt, :] = x_ref[j]
            w = cw_ref[j]
            y = w[0:1] * xe_ref[idx, SUBLANES - 3:SUBLANES - 3 + lt, :]
            for tap in range(1, GDN_CONV):
                lo = SUBLANES - 3 + tap
                y = y + w[tap:tap + 1] * xe_ref[idx, lo:lo + lt, :]
            xe_ref[idx, 0:SUBLANES, :] = xe_ref[idx, lt:lt + SUBLANES, :]
            feats.append(_silu(y))
        q, k, v = feats
        qs.append(q * lax.rsqrt(jnp.sum(q * q, axis=-1, keepdims=True) + EPS) * (GDN_DK ** -0.5))
        ks.append(k * lax.rsqrt(jnp.sum(k * k, axis=-1, keepdims=True) + EPS))
        vs.append(v)

    blocks = [(c, j) for c in chunks for j in heads]
    pre = {}
    for c, j in blocks:
        rows = slice(c * CHUNK, (c + 1) * CHUNK)
        q, k, v, beta_col = qs[j][rows], ks[j][rows], vs[j][rows], betas[j][rows]
        dec_col, gam = _decay_terms(gs[j][rows], row, col)
        dec_last = dec_col[CHUNK - 1:CHUNK, :]
        e_col = jnp.exp(dec_col)
        kb = k * beta_col
        pre[c, j] = dict(q=q, k=k, gam=gam, kb=kb, qe=q * e_col, e_last=jnp.exp(dec_last),
                         kd=k * jnp.exp(dec_last - dec_col),
                         rhs=jnp.concatenate([v * beta_col, kb * e_col], axis=1).astype(BF16))
    grams = [_mm_nt(jnp.concatenate([pre[b]["kb"], pre[b]["q"]], axis=0), pre[b]["k"]) for b in blocks]
    ms = [jnp.where(row > col, g[:CHUNK] * pre[b]["gam"], 0.0) for g, b in zip(grams, blocks)]
    a_intra = {b: jnp.where(tril, g[CHUNK:] * pre[b]["gam"], 0.0) for g, b in zip(grams, blocks)}
    t_inv = _unit_lower_inverses(ms, row, col)
    uw = {b: jnp.dot(t.astype(BF16), pre[b]["rhs"], preferred_element_type=F32) for t, b in zip(t_inv, blocks)}

    for c in chunks:
        rows = slice(c * CHUNK, (c + 1) * CHUNK)
        s_prev = [s_ref[j] for j in heads]
        ws_qs = [_mm(jnp.concatenate([uw[c, j][:, GDN_DV:], pre[c, j]["qe"]], axis=0), s_prev[j]) for j in heads]
        v_new = [uw[c, j][:, :GDN_DV] - ws_qs[j][:CHUNK] for j in heads]
        o = [ws_qs[j][CHUNK:] + _mm(a_intra[c, j], v_new[j]) for j in heads]
        s_new = [s_prev[j] * pre[c, j]["e_last"] + _mm(pre[c, j]["kd"].T, v_new[j]) for j in heads]
        for j in heads:
            s_ref[j] = s_new[j]
            og = _gated_out_norm(o[j], gate_ref[j, rows, :], onw)
            og_ref[rows, j * GDN_DV:(j + 1) * GDN_DV] = og.astype(og_ref.dtype)


def _gdn_prompt(qkv4, gg, ba, cw, alog_row, dtb_row, onw, hb, lt):
    b_, _, l_, _ = qkv4.shape
    ng = GDN_HEADS // hb
    sec = lambda s: pl.BlockSpec((None, hb, lt, LANES), lambda b, g, t, s=s: (b, s * ng + g, t, 0))
    cws = lambda s: pl.BlockSpec((hb, GDN_CONV, LANES), lambda b, g, t, s=s: (s * ng + g, 0, 0))
    return pl.pallas_call(
        functools.partial(_gdn_prompt_body, hb, lt),
        grid=(b_, ng, l_ // lt),
        in_specs=[sec(0), sec(1), sec(2), cws(0), cws(1), cws(2),
                  pl.BlockSpec((None, lt, LANES), lambda b, g, t: (b, t, 0)),
                  _const_spec(alog_row.shape), _const_spec(dtb_row.shape),
                  pl.BlockSpec((None, hb, lt, LANES), lambda b, g, t: (b, g, t, 0)),
                  _const_spec(onw.shape)],
        out_specs=(pl.BlockSpec((None, lt, hb * GDN_DV), lambda b, g, t: (b, t, g)),
                   pl.BlockSpec((None, hb, GDN_DK, GDN_DV), lambda b, g, t: (b, g, 0, 0))),
        out_shape=(jax.ShapeDtypeStruct((b_, l_, GDN_V), BF16),
                   jax.ShapeDtypeStruct((b_, GDN_HEADS, GDN_DK, GDN_DV), F32)),
        scratch_shapes=[pltpu.VMEM((3 * hb, lt + SUBLANES, LANES), F32)],
        compiler_params=_params(("arbitrary", "arbitrary", "arbitrary")),
        name="gdn_prompt",
    )(qkv4, qkv4, qkv4, cw, cw, cw, ba, alog_row, dtb_row, gg, onw)


def _gdn_step_body(bb, x_ref, st_ref, cw_ref, ba_ref, alog_ref, dtb_ref, gate_ref, onw_ref, s0_ref,
                   og_ref, sn_ref, q_s, k_s, v_s, b_s, e_s, o_s):
    beta_all, g_all = _delta_gates(ba_ref[...], alog_ref[...], dtb_ref[...])
    lane_idx = lax.broadcasted_iota(jnp.int32, (bb, LANES), 1)
    for h in range(GDN_HEADS):
        feats = []
        for s in range(3):
            idx = s * GDN_HEADS + h
            w = cw_ref[idx]
            y = w[0:1] * st_ref[0, idx]
            for tap in range(1, GDN_CONV - 1):
                y = y + w[tap:tap + 1] * st_ref[tap, idx]
            y = y + w[GDN_CONV - 1:GDN_CONV] * x_ref[idx]
            feats.append(_silu(y))
        q, k, v = feats
        q_s[h] = q * lax.rsqrt(jnp.sum(q * q, axis=-1, keepdims=True) + EPS) * (GDN_DK ** -0.5)
        k_s[h] = k * lax.rsqrt(jnp.sum(k * k, axis=-1, keepdims=True) + EPS)
        v_s[h] = v
        b_s[h] = jnp.broadcast_to(_lane_column(beta_all, lane_idx, h), (bb, LANES))
        e_s[h] = jnp.broadcast_to(jnp.exp(_lane_column(g_all, lane_idx, h + GDN_HEADS)), (bb, LANES))

    eye = (lax.broadcasted_iota(jnp.int32, (GDN_DK, GDN_DK), 0)
           == lax.broadcasted_iota(jnp.int32, (GDN_DK, GDN_DK), 1))

    def to_col(r):
        return jnp.sum(jnp.where(eye, jnp.broadcast_to(r, (GDN_DK, GDN_DK)), 0.0), axis=1, keepdims=True)

    def seq_body(i, carry):
        for h in range(GDN_HEADS):
            one = pl.ds(i, 1)
            k_col = to_col(k_s[h, one, :])
            q_col = to_col(q_s[h, one, :])
            s1 = s0_ref[i, h] * e_s[h, one, :]
            ks = jnp.sum(s1 * k_col, axis=0, keepdims=True)
            delta = (v_s[h, one, :] - ks) * b_s[h, one, :]
            s2 = s1 + k_col * delta
            sn_ref[i, h] = s2
            o_s[h, one, :] = jnp.sum(s2 * q_col, axis=0, keepdims=True)
        return carry

    lax.fori_loop(0, bb, seq_body, 0)
    onw = onw_ref[...]
    for h in range(GDN_HEADS):
        og = _gated_out_norm(o_s[h], gate_ref[h], onw)
        og_ref[:, h * GDN_DV:(h + 1) * GDN_DV] = og.astype(og_ref.dtype)


def _gdn_step(qkv4, st4, gg, ba, cw, alog_row, dtb_row, onw, s0, bb):
    n_ = ba.shape[0]
    vec = pltpu.VMEM((GDN_HEADS, bb, LANES), F32)
    return pl.pallas_call(
        functools.partial(_gdn_step_body, bb),
        grid=(n_ // bb,),
        in_specs=[pl.BlockSpec((GDN_SECTIONS, bb, LANES), lambda i: (0, i, 0)),
                  pl.BlockSpec((GDN_CONV - 1, GDN_SECTIONS, bb, LANES), lambda i: (0, 0, i, 0)),
                  _const_spec(cw.shape),
                  pl.BlockSpec((bb, LANES), lambda i: (i, 0)),
                  _const_spec(alog_row.shape), _const_spec(dtb_row.shape),
                  pl.BlockSpec((GDN_HEADS, bb, LANES), lambda i: (0, i, 0)),
                  _const_spec(onw.shape),
                  pl.BlockSpec((bb, GDN_HEADS, GDN_DK, GDN_DV), lambda i: (i, 0, 0, 0))],
        out_specs=(pl.BlockSpec((bb, GDN_V), lambda i: (i, 0)),
                   pl.BlockSpec((bb, GDN_HEADS, GDN_DK, GDN_DV), lambda i: (i, 0, 0, 0))),
        out_shape=(jax.ShapeDtypeStruct((n_, GDN_V), BF16),
                   jax.ShapeDtypeStruct(s0.shape, F32)),
        scratch_shapes=[vec, vec, vec, vec, vec, vec],
        compiler_params=_params(("arbitrary",)),
        name="gdn_step",
    )(qkv4, st4, cw, ba, alog_row, dtb_row, gg, onw, s0)


def _swa_prompt_body(sinks_ref, q_ref, kvp_ref, kvc_ref, o_ref):
    n = pl.program_id(1)
    w = WINDOW
    tiles = SWA_KV // LANES
    pairs = 2
    lo_lane = lax.broadcasted_iota(jnp.int32, (w, LANES), 1) < SWA_HD
    lo_row = lax.broadcasted_iota(jnp.int32, (LANES, w), 0) < SWA_HD
    c = lax.broadcasted_iota(jnp.int32, (2 * w, pairs * w), 0)
    i = lax.broadcasted_iota(jnp.int32, (2 * w, pairs * w), 1) & (w - 1)
    valid = (c > i) & (c <= i + w) & ((c >= w) | (n > 0))
    k_nat, k_rot, vt_nat, vt_rot = [], [], [], []
    for t in range(tiles):
        kcols = slice(t * LANES, (t + 1) * LANES)
        vcols = slice(SWA_KV + t * LANES, SWA_KV + (t + 1) * LANES)
        kx = jnp.concatenate([kvp_ref[:, kcols], kvc_ref[:, kcols]], axis=0)
        vt = jnp.concatenate([kvp_ref[:, vcols], kvc_ref[:, vcols]], axis=0).T
        k_nat.append(kx.astype(BF16))
        k_rot.append(pltpu.roll(kx, SWA_HD, axis=1).astype(BF16))
        vt_nat.append(vt.astype(BF16))
        vt_rot.append(jnp.concatenate([vt[SWA_HD:], vt[:SWA_HD]], axis=0).astype(BF16))
    items = [(g, p) for g in range(SWA_KV_HEADS) for p in range(2)]
    scale = SWA_HD ** -0.5
    qm, kz, vzt, sink = {}, {}, {}, {}
    for g, p in items:
        keep = lo_lane if p == 0 else jnp.logical_not(lo_lane)
        q_tiles = [q_ref[:, (2 * g + r) * LANES:(2 * g + r + 1) * LANES] for r in range(pairs)]
        qm[g, p] = jnp.concatenate([jnp.where(keep, x * scale, 0.0) for x in q_tiles], axis=0).astype(BF16)
        natural = p == g % 2
        kz[g, p] = (k_nat if natural else k_rot)[g // 2]
        vzt[g, p] = (vt_nat if natural else vt_rot)[g // 2]
        sink[g, p] = jnp.concatenate([jnp.full((1, w), sinks_ref[SWA_GROUP * g + 2 * r + p], F32)
                                      for r in range(pairs)], axis=1)
    st = {b: jnp.where(valid, lax.dot_general(kz[b], qm[b], (((1,), (1,)), ((), ())),
                                              preferred_element_type=F32), -jnp.inf) for b in items}
    m = {b: jnp.maximum(jnp.max(st[b], axis=0, keepdims=True), sink[b]) for b in items}
    et = {b: jnp.exp(st[b] - m[b]) for b in items}
    den = {b: jnp.sum(et[b], axis=0, keepdims=True) + jnp.exp(sink[b] - m[b]) for b in items}
    ot = {b: jnp.dot(vzt[b], et[b].astype(BF16), preferred_element_type=F32) / den[b] for b in items}
    for g in range(SWA_KV_HEADS):
        for r in range(pairs):
            cols = slice(r * w, (r + 1) * w)
            tile_t = jnp.where(lo_row, ot[g, 0][:, cols], ot[g, 1][:, cols])
            o_ref[:, (2 * g + r) * LANES:(2 * g + r + 1) * LANES] = tile_t.T.astype(o_ref.dtype)


def _swa_prompt(sq, skv, sinks):
    b_, l_, _ = sq.shape
    nb = l_ // WINDOW
    return pl.pallas_call(
        _swa_prompt_body,
        grid=(b_, nb),
        in_specs=[pl.BlockSpec(memory_space=pltpu.SMEM),
                  pl.BlockSpec((None, WINDOW, SWA_Q), lambda b, n: (b, n, 0)),
                  pl.BlockSpec((None, WINDOW, 2 * SWA_KV), lambda b, n: (b, jnp.maximum(n - 1, 0), 0)),
                  pl.BlockSpec((None, WINDOW, 2 * SWA_KV), lambda b, n: (b, n, 0))],
        out_specs=pl.BlockSpec((None, WINDOW, SWA_Q), lambda b, n: (b, n, 0)),
        out_shape=jax.ShapeDtypeStruct((b_, l_, SWA_Q), BF16),
        compiler_params=_params(("arbitrary", "arbitrary")),
        name="swa_prompt",
    )(sinks, sq, skv, skv)


def _swa_step_body(bb, q_ref, kvn_ref, ck_ref, cv_ref, sink_ref, o_ref, nk_ref, nv_ref):
    w = WINDOW
    row = lax.broadcasted_iota(jnp.int32, (SWA_Q_HEADS, SWA_KV), 0)
    lane = lax.broadcasted_iota(jnp.int32, (SWA_Q_HEADS, SWA_KV), 1)
    own = (lane // SWA_HD) == (row // SWA_GROUP)
    key = lax.broadcasted_iota(jnp.int32, (SWA_Q_HEADS, w), 1)
    sink = sink_ref[...]
    scale = SWA_HD ** -0.5
    for i in range(bb):
        qb = q_ref[i]
        q_bd = jnp.where(own, jnp.concatenate([qb] * SWA_KV_HEADS, axis=1), 0.0)
        kc = ck_ref[i]
        vc = cv_ref[i]
        kn = kvn_ref[i:i + 1, 0:SWA_KV]
        vn = kvn_ref[i:i + 1, SWA_KV:2 * SWA_KV]
        s_c = jnp.where(key >= 1, _mm_nt(q_bd, kc) * scale, -jnp.inf)
        s_n = jnp.sum(q_bd * kn, axis=1, keepdims=True) * scale
        m = jnp.maximum(jnp.maximum(jnp.max(s_c, axis=1, keepdims=True), s_n), sink)
        e_c = jnp.exp(s_c - m)
        e_n = jnp.exp(s_n - m)
        den = jnp.sum(e_c, axis=1, keepdims=True) + e_n + jnp.exp(sink - m)
        pv = jnp.where(own, _mm(e_c / den, vc) + (e_n / den) * vn, 0.0)
        o = pv[:, 0:SWA_HD]
        for g in range(1, SWA_KV_HEADS):
            o = o + pv[:, g * SWA_HD:(g + 1) * SWA_HD]
        o_ref[i] = o
        nk_ref[i, 0:w - 1, :] = ck_ref[i, 1:w, :]
        nk_ref[i, w - 1:w, :] = kn
        nv_ref[i, 0:w - 1, :] = cv_ref[i, 1:w, :]
        nv_ref[i, w - 1:w, :] = vn


def _swa_step(q3, kvn, ck, cv, sink_col, bb):
    n_ = q3.shape[0]
    cache = pl.BlockSpec((bb, WINDOW, SWA_KV), lambda i: (i, 0, 0))
    return pl.pallas_call(
        functools.partial(_swa_step_body, bb),
        grid=(n_ // bb,),
        in_specs=[pl.BlockSpec((bb, SWA_Q_HEADS, SWA_HD), lambda i: (i, 0, 0)),
                  pl.BlockSpec((bb, 2 * SWA_KV), lambda i: (i, 0)),
                  cache, cache,
                  _const_spec(sink_col.shape)],
        out_specs=(pl.BlockSpec((bb, SWA_Q_HEADS, SWA_HD), lambda i: (i, 0, 0)), cache, cache),
        out_shape=(jax.ShapeDtypeStruct(q3.shape, F32),
                   jax.ShapeDtypeStruct(ck.shape, F32),
                   jax.ShapeDtypeStruct(cv.shape, F32)),
        compiler_params=_params(("arbitrary",)),
        name="swa_step",
    )(q3, kvn, ck, cv, sink_col)


def _dense_body(stateful, tm, og_ref, ob_ref, gab_ref, x_ref, gt1_ref, sh2_ref, sc2_ref, gt2_ref,
                n2w_ref, fnw_ref, wa_ref, wb_ref, wo_ref, wg_ref, wu_ref, cw_ref, cb_ref, wd_ref, *rest):
    if stateful:
        st_ref, y_ref, gout_ref, act_ref = rest
    else:
        y_ref, gout_ref, act_ref, gbuf_ref, carry_ref = rest

        @pl.when(pl.program_id(1) == 0)
        def _():
            carry_ref[...] = jnp.zeros_like(carry_ref)

    y_a = jnp.dot(og_ref[...], wa_ref[...], preferred_element_type=F32)
    y_b = jnp.dot(ob_ref[...], wb_ref[...], preferred_element_type=F32)
    merged = (jax.nn.sigmoid(gab_ref[:, 0:D_MODEL]) * y_a
              + jax.nn.sigmoid(gab_ref[:, D_MODEL:2 * D_MODEL]) * y_b)
    x1 = x_ref[...] + gt1_ref[...] * _mm(merged, wo_ref[...])
    h2 = (_rms(x1, n2w_ref[...]) * (1.0 + sc2_ref[...]) + sh2_ref[...]).astype(BF16)

    for c in range(D_FF // FFN_COLS):
        cols = slice(c * FFN_COLS, (c + 1) * FFN_COLS)
        gate = jnp.dot(h2, wg_ref[:, cols], preferred_element_type=F32)
        up = jnp.dot(h2, wu_ref[:, cols], preferred_element_type=F32)
        if stateful:
            g2 = st_ref[0, :, cols]
            g1 = st_ref[1, :, cols]
            gout_ref[:, cols] = gate
        else:
            gbuf_ref[0:SUBLANES, :] = carry_ref[:, cols]
            gbuf_ref[SUBLANES:SUBLANES + tm, :] = gate
            g2 = gbuf_ref[SUBLANES - 2:SUBLANES - 2 + tm, :]
            g1 = gbuf_ref[SUBLANES - 1:SUBLANES - 1 + tm, :]
            carry_ref[:, cols] = gbuf_ref[tm:tm + SUBLANES, :]
        gc = (cw_ref[0:1, cols] * g2 + cw_ref[1:2, cols] * g1 + cw_ref[2:3, cols] * gate) + cb_ref[:, cols]
        act_ref[:, cols] = (_silu(gc) * up).astype(BF16)
    if not stateful:
        gout_ref[...] = carry_ref[...]

    x2 = x1 + gt2_ref[...] * jnp.dot(act_ref[...], wd_ref[...], preferred_element_type=F32)
    y_ref[...] = _rms(x2, fnw_ref[...])


def _dense(og, ob, gab, x, mods, vecs, ws, st, tm):
    b_, l_, _ = x.shape
    r_ = mods[0].shape[1]
    rt = 1 if r_ == 1 else tm
    mod_map = (lambda b, t: (b, 0, 0)) if r_ == 1 else (lambda b, t: (b, t, 0))
    row_map = lambda b, t: (b, t, 0)
    stateful = st is not None
    in_specs = ([pl.BlockSpec((None, tm, D_MODEL), row_map),
                 pl.BlockSpec((None, tm, D_MODEL), row_map),
                 pl.BlockSpec((None, tm, 2 * D_MODEL), row_map),
                 pl.BlockSpec((None, tm, D_MODEL), row_map)]
                + [pl.BlockSpec((None, rt, D_MODEL), mod_map)] * 4
                + [_const_spec(a.shape) for a in vecs[:2]]
                + [_const_spec(ws[0].shape), _const_spec(ws[1].shape), _const_spec(ws[2].shape),
                   _const_spec(ws[3].shape), _const_spec(ws[4].shape),
                   _const_spec(vecs[2].shape), _const_spec(vecs[3].shape), _const_spec(ws[5].shape)])
    args = [og, ob, gab, x, *mods, vecs[0], vecs[1], ws[0], ws[1], ws[2], ws[3], ws[4], vecs[2], vecs[3], ws[5]]
    scratch = [pltpu.VMEM((tm, D_FF), BF16)]
    if stateful:
        in_specs.append(pl.BlockSpec((FFN_CONV - 1, None, tm, D_FF), lambda b, t: (0, b, t, 0)))
        args.append(st)
        gout_shape = jax.ShapeDtypeStruct((b_, l_, D_FF), F32)
        gout_spec = pl.BlockSpec((None, tm, D_FF), row_map)
    else:
        scratch += [pltpu.VMEM((tm + SUBLANES, FFN_COLS), F32), pltpu.VMEM((SUBLANES, D_FF), F32)]
        gout_shape = jax.ShapeDtypeStruct((b_, SUBLANES, D_FF), F32)
        gout_spec = pl.BlockSpec((None, SUBLANES, D_FF), lambda b, t: (b, 0, 0))
    return pl.pallas_call(
        functools.partial(_dense_body, stateful, tm),
        grid=(b_, l_ // tm),
        in_specs=in_specs,
        out_specs=(pl.BlockSpec((None, tm, D_MODEL), row_map), gout_spec),
        out_shape=(jax.ShapeDtypeStruct((b_, l_, D_MODEL), F32), gout_shape),
        scratch_shapes=scratch,
        compiler_params=_params(("arbitrary", "arbitrary")),
        name="dense_step" if stateful else "dense_prompt",
    )(*args)


def _lane_row(values, offset):
    return jnp.zeros((1, LANES), F32).at[0, offset:offset + values.shape[0]].set(values)


def kernel(x_prompt, x_sample, c_prompt, c_sample, state_gdn_S, state_gdn_conv, cache_swa_k, cache_swa_v,
           state_ffn_conv, w_mod, b_mod, norm1_w, norm2_w, w_in, gdn_conv_w, gdn_a_log, gdn_dt_bias,
           gdn_onorm_w, w_gdn_out, swa_sinks, w_swa_out, w_o, w_ffn_gate, w_ffn_up, ffn_conv_w, ffn_conv_b,
           w_ffn_down, final_norm_w):
    assert w_mod.shape[0] == 1, "single-layer trunk"
    nb, seq, _ = x_prompt.shape
    ns = x_sample.shape[0]
    assert x_sample.shape[1] == 1

    w_in0 = w_in[0].astype(BF16)
    o0 = GDN_CONV_CH
    o1 = o0 + GDN_V
    o2 = o1 + 2 * GDN_HEADS
    o3 = o2 + SWA_Q
    o4 = o3 + 2 * SWA_KV
    w_ba = jnp.zeros((D_MODEL, LANES), BF16).at[:, :2 * GDN_HEADS].set(w_in0[:, o1:o2])
    in_ws = (w_in0[:, :o0], w_in0[:, o0:o1], w_ba, w_in0[:, o2:o3], w_in0[:, o3:o4], w_in0[:, o4:])
    dense_ws = (w_gdn_out[0].astype(BF16), w_swa_out[0].astype(BF16), w_o[0].astype(BF16),
                w_ffn_gate[0].astype(BF16), w_ffn_up[0].astype(BF16), w_ffn_down[0].astype(BF16))
    dense_vecs = (norm2_w, final_norm_w[None, :], ffn_conv_w[0], ffn_conv_b)
    cw = jnp.transpose(gdn_conv_w[0].reshape(GDN_CONV, GDN_SECTIONS, LANES), (1, 0, 2))
    alog_row = _lane_row(gdn_a_log[0], GDN_HEADS)
    dtb_row = _lane_row(gdn_dt_bias[0], GDN_HEADS)

    mod = _modulation(jnp.concatenate([c_prompt, c_sample], axis=0), w_mod[0].astype(BF16), b_mod)
    mod_p = [mod[:nb, i * D_MODEL:(i + 1) * D_MODEL][:, None, :] for i in range(6)]
    mod_s = [mod[nb:, i * D_MODEL:(i + 1) * D_MODEL][None, :, :] for i in range(6)]

    qkv4, gg, ba, sq, skv, gab = _inproj(x_prompt, mod_p[0], mod_p[1], norm1_w, in_ws, tm=256)
    og, gdn_s_p = _gdn_prompt(qkv4, gg, ba, cw, alog_row, dtb_row, gdn_onorm_w, hb=8, lt=128)
    ob = _swa_prompt(sq, skv, swa_sinks[0])
    y_p, gate_tail = _dense(og, ob, gab, x_prompt, (mod_p[2], mod_p[3], mod_p[4], mod_p[5]),
                            dense_vecs, dense_ws, None, tm=256)
    gdn_conv_p = jnp.transpose(qkv4[:, :, seq - (GDN_CONV - 1):, :], (0, 2, 1, 3)).reshape(
        nb, GDN_CONV - 1, GDN_CONV_CH)
    k_p = skv[:, seq - WINDOW:, :SWA_KV].reshape(nb, WINDOW, SWA_KV_HEADS, SWA_HD)
    v_p = skv[:, seq - WINDOW:, SWA_KV:].reshape(nb, WINDOW, SWA_KV_HEADS, SWA_HD)
    ffn_conv_p = gate_tail[:, SUBLANES - (FFN_CONV - 1):, :]

    xs = x_sample.reshape(1, ns, D_MODEL)
    qkv4s, ggs, bas, sqs, skvs, gabs = _inproj(xs, mod_s[0], mod_s[1], norm1_w, in_ws, tm=ns)
    st4 = jnp.transpose(state_gdn_conv[0].reshape(ns, GDN_CONV - 1, GDN_SECTIONS, LANES), (1, 2, 0, 3))
    og_s, gdn_s_s = _gdn_step(qkv4s[0], st4, ggs[0], bas[0], cw, alog_row, dtb_row, gdn_onorm_w,
                              state_gdn_S[0], bb=8)
    o3, k_s, v_s = _swa_step(sqs[0].reshape(ns, SWA_Q_HEADS, SWA_HD), skvs[0],
                             cache_swa_k[0].reshape(ns, WINDOW, SWA_KV),
                             cache_swa_v[0].reshape(ns, WINDOW, SWA_KV),
                             swa_sinks[0][:, None], bb=8)
    ob_s = o3.reshape(1, ns, SWA_Q).astype(BF16)
    st_ffn = jnp.transpose(state_ffn_conv[0], (1, 0, 2))[:, None]
    y_s, gate_new = _dense(og_s[None], ob_s, gabs, xs, (mod_s[2], mod_s[3], mod_s[4], mod_s[5]),
                           dense_vecs, dense_ws, st_ffn, tm=ns)
    qkv_new = jnp.transpose(qkv4s[0], (1, 0, 2)).reshape(ns, 1, GDN_CONV_CH)
    gdn_conv_s = jnp.concatenate([state_gdn_conv[0][:, 1:], qkv_new], axis=1)
    ffn_conv_s = jnp.concatenate([state_ffn_conv[0][:, 1:], gate_new[0][:, None, :]], axis=1)

    return (y_p, y_s.reshape(ns, 1, D_MODEL),
            gdn_s_p[None], gdn_s_s[None],
            gdn_conv_p[None], gdn_conv_s[None],
            k_p[None], k_s.reshape(ns, WINDOW, SWA_KV_HEADS, SWA_HD)[None],
            v_p[None], v_s.reshape(ns, WINDOW, SWA_KV_HEADS, SWA_HD)[None],
            ffn_conv_p[None], ffn_conv_s[None])
```

```python
import functools

import numpy as np
import jax
import jax.numpy as jnp
from jax import lax
from jax.experimental import pallas as pl
from jax.experimental.pallas import tpu as pltpu

F32 = jnp.float32
BF16 = jnp.bfloat16

D_MODEL = 1024
GDN_HEADS = 8
GDN_DK = 128
GDN_DV = 128
GDN_QK = GDN_HEADS * GDN_DK
GDN_V = GDN_HEADS * GDN_DV
GDN_CONV = 4
GDN_CONV_CH = 2 * GDN_QK + GDN_V
GDN_SECTIONS = GDN_CONV_CH // 128
SWA_Q_HEADS = 16
SWA_KV_HEADS = 4
SWA_GROUP = SWA_Q_HEADS // SWA_KV_HEADS
SWA_HD = 64
SWA_Q = SWA_Q_HEADS * SWA_HD
SWA_KV = SWA_KV_HEADS * SWA_HD
WINDOW = 128
D_FF = 2816
FFN_CONV = 3
EPS = 1e-6

LANES = 128
SUBLANES = 8
VMEM_LIMIT = 56 * 1024 * 1024

CHUNK = 128
FFN_COLS = 256


def _mm(a, b):
    return jnp.dot(a.astype(BF16), b.astype(BF16), preferred_element_type=F32)


def _mm_nt(a, b):
    return lax.dot_general(a.astype(BF16), b.astype(BF16), (((1,), (1,)), ((), ())),
                           preferred_element_type=F32)


def _silu(x):
    return x * jax.nn.sigmoid(x)


def _softplus(x):
    return jnp.maximum(x, 0.0) + jnp.log1p(jnp.exp(-jnp.abs(x)))


def _rms(x, w):
    return x * lax.rsqrt(jnp.mean(x * x, axis=-1, keepdims=True) + EPS) * w


def _const_spec(shape):
    n = len(shape)
    return pl.BlockSpec(shape, lambda *_: (0,) * n, pipeline_mode=pl.Buffered(1))


def _params(sem):
    return pltpu.CompilerParams(dimension_semantics=sem, vmem_limit_bytes=VMEM_LIMIT)


def _mod_body(c_ref, w_ref, b_ref, o_ref):
    o_ref[...] = _mm(_silu(c_ref[...]), w_ref[...]) + b_ref[...]


def _modulation(c_all, w_mod, b_mod):
    rows = c_all.shape[0]
    n_out = w_mod.shape[1]
    tn = D_MODEL
    return pl.pallas_call(
        _mod_body,
        grid=(n_out // tn,),
        in_specs=[pl.BlockSpec((rows, D_MODEL), lambda j: (0, 0)),
                  pl.BlockSpec((D_MODEL, tn), lambda j: (0, j)),
                  pl.BlockSpec((1, tn), lambda j: (0, j))],
        out_specs=pl.BlockSpec((rows, tn), lambda j: (0, j)),
        out_shape=jax.ShapeDtypeStruct((rows, n_out), F32),
        compiler_params=_params(("arbitrary",)),
        name="modulation",
    )(c_all, w_mod, b_mod)


def _l2norm(x):
    return x * lax.rsqrt(jnp.sum(x * x, axis=-1, keepdims=True) + EPS)


def _inproj_body(seq_rows, tm, x_ref, sh_ref, sc_ref, nw_ref, wqkv_ref, wgg_ref, wba_ref, wsq_ref, wskv_ref,
                 wgab_ref, *rest):
    if seq_rows:
        cw_ref, qkv_ref, gg_ref, ba_ref, sq_ref, skv_ref, gab_ref, tail_ref, xe_ref = rest

        @pl.when(pl.program_id(1) == 0)
        def _():
            xe_ref[:, 0:SUBLANES, :] = jnp.zeros((GDN_SECTIONS, SUBLANES, LANES), F32)
    else:
        qkv_ref, gg_ref, ba_ref, sq_ref, skv_ref, gab_ref = rest

    h = _rms(x_ref[...], nw_ref[...]) * (1.0 + sc_ref[...]) + sh_ref[...]
    hb = h.astype(BF16)

    def proj(w_ref, lo, width):
        return jnp.dot(hb, w_ref[:, lo:lo + width], preferred_element_type=F32)

    step = 512
    per = step // LANES
    for c in range(GDN_CONV_CH // step):
        z = proj(wqkv_ref, c * step, step)
        for k in range(per):
            s = c * per + k
            zs = z[:, k * LANES:(k + 1) * LANES]
            if not seq_rows:
                qkv_ref[s] = zs
                continue
            xe_ref[s, SUBLANES:SUBLANES + tm, :] = zs
            w = cw_ref[s]
            y = w[0:1] * xe_ref[s, SUBLANES - 3:SUBLANES - 3 + tm, :]
            for tap in range(1, GDN_CONV):
                lo = SUBLANES - 3 + tap
                y = y + w[tap:tap + 1] * xe_ref[s, lo:lo + tm, :]
            xe_ref[s, 0:SUBLANES, :] = xe_ref[s, tm:tm + SUBLANES, :]
            f = _silu(y)
            if s < GDN_HEADS:
                f = _l2norm(f) * (GDN_DK ** -0.5)
            elif s < 2 * GDN_HEADS:
                f = _l2norm(f)
            qkv_ref[s] = f
    if seq_rows:
        tail_ref[...] = xe_ref[:, 0:SUBLANES, :]
    for c in range(GDN_V // step):
        z = proj(wgg_ref, c * step, step)
        for k in range(per):
            zs = z[:, k * LANES:(k + 1) * LANES]
            gg_ref[c * per + k] = _silu(zs) if seq_rows else zs
    ba_ref[...] = proj(wba_ref, 0, LANES)
    for c in range(SWA_Q // step):
        sq_ref[:, c * step:(c + 1) * step] = proj(wsq_ref, c * step, step)
    skv_ref[...] = proj(wskv_ref, 0, 2 * SWA_KV)
    for c in range(2 * D_MODEL // step):
        gab_ref[:, c * step:(c + 1) * step] = proj(wgab_ref, c * step, step)


def _inproj(x, sh, sc, nw, ws, cw, tm):
    b_, l_, _ = x.shape
    r_ = sh.shape[1]
    rt = 1 if r_ == 1 else tm
    mod_map = (lambda b, t: (b, 0, 0)) if r_ == 1 else (lambda b, t: (b, t, 0))
    row_map = lambda b, t: (b, t, 0)
    head_map = lambda b, t: (b, 0, t, 0)
    seq_rows = cw is not None
    out_shape = (
        jax.ShapeDtypeStruct((b_, GDN_SECTIONS, l_, LANES), F32),
        jax.ShapeDtypeStruct((b_, GDN_HEADS, l_, LANES), F32),
        jax.ShapeDtypeStruct((b_, l_, LANES), F32),
        jax.ShapeDtypeStruct((b_, l_, SWA_Q), F32),
        jax.ShapeDtypeStruct((b_, l_, 2 * SWA_KV), F32),
        jax.ShapeDtypeStruct((b_, l_, 2 * D_MODEL), F32),
    )
    out_specs = (
        pl.BlockSpec((None, GDN_SECTIONS, tm, LANES), head_map),
        pl.BlockSpec((None, GDN_HEADS, tm, LANES), head_map),
        pl.BlockSpec((None, tm, LANES), row_map),
        pl.BlockSpec((None, tm, SWA_Q), row_map),
        pl.BlockSpec((None, tm, 2 * SWA_KV), row_map),
        pl.BlockSpec((None, tm, 2 * D_MODEL), row_map),
    )
    in_specs = [
        pl.BlockSpec((None, tm, D_MODEL), row_map),
        pl.BlockSpec((None, rt, D_MODEL), mod_map),
        pl.BlockSpec((None, rt, D_MODEL), mod_map),
        _const_spec(nw.shape),
    ] + [_const_spec(w.shape) for w in ws]
    args = [x, sh, sc, nw, *ws]
    scratch = []
    if seq_rows:
        in_specs.append(_const_spec(cw.shape))
        args.append(cw)
        out_shape += (jax.ShapeDtypeStruct((b_, GDN_SECTIONS, SUBLANES, LANES), F32),)
        out_specs += (pl.BlockSpec((None, GDN_SECTIONS, SUBLANES, LANES), lambda b, t: (b, 0, 0, 0)),)
        scratch.append(pltpu.VMEM((GDN_SECTIONS, tm + SUBLANES, LANES), F32))
    return pl.pallas_call(
        functools.partial(_inproj_body, seq_rows, tm),
        grid=(b_, l_ // tm),
        in_specs=in_specs,
        out_specs=out_specs,
        out_shape=out_shape,
        scratch_shapes=scratch,
        compiler_params=_params(("arbitrary", "arbitrary")),
        name="inproj_seq" if seq_rows else "inproj_rows",
    )(*args)


def _delta_gates(ba, alog_row, dtb_row):
    beta_all = jax.nn.sigmoid(ba)
    g_all = -jnp.exp(alog_row) * _softplus(ba + dtb_row)
    return beta_all, g_all


def _lane_column(x, lane_idx, lane):
    return jnp.sum(jnp.where(lane_idx == lane, x, 0.0), axis=1, keepdims=True)


def _level_masks():
    r = np.arange(CHUNK)[:, None]
    c = np.arange(CHUNK)[None, :]
    masks = [(r == c + 1) & (r % 2 == 1)]
    half = 2
    while half < CHUNK:
        full = 2 * half
        masks.append((r // full == c // full) & (r % full >= half) & (c % full < half))
        half = full
    return jnp.asarray(np.stack(masks), dtype=BF16)


def _unit_lower_inverses(ms, masks_ref, eye):
    ts = [eye - m * masks_ref[0] for m in ms]
    for lvl in range(1, masks_ref.shape[0]):
        off = masks_ref[lvl]
        xs = [jnp.dot(m * off, t, preferred_element_type=F32).astype(BF16) for m, t in zip(ms, ts)]
        ys = [jnp.dot(t, x, preferred_element_type=F32).astype(BF16) for t, x in zip(ts, xs)]
        ts = [t - y for t, y in zip(ts, ys)]
    return ts


def _cumsum_rows(g, ltri):
    hi = g.astype(BF16)
    r1 = g - hi.astype(F32)
    mid = r1.astype(BF16)
    lo = (r1 - mid.astype(F32)).astype(BF16)
    return (jnp.dot(ltri, hi, preferred_element_type=F32) + jnp.dot(ltri, mid, preferred_element_type=F32)
            + jnp.dot(ltri, lo, preferred_element_type=F32))


def _gated_out_norm(o, gate_act, onw):
    on = o * lax.rsqrt(jnp.mean(o * o, axis=-1, keepdims=True) + EPS) * onw
    return on * gate_act


def _gdn_prompt_body(lt, q_ref, k_ref, v_ref, ba_ref, alog_ref, dtb_ref, gate_ref, onw_ref, masks_ref,
                     og_ref, s_ref):
    @pl.when(pl.program_id(1) == 0)
    def _():
        s_ref[...] = jnp.zeros_like(s_ref)

    beta_all, g_all = _delta_gates(ba_ref[...], alog_ref[...], dtb_ref[...])
    lane_idx = lax.broadcasted_iota(jnp.int32, (CHUNK, LANES), 1)
    row = lax.broadcasted_iota(jnp.int32, (CHUNK, CHUNK), 0)
    col = lax.broadcasted_iota(jnp.int32, (CHUNK, CHUNK), 1)
    tril = row >= col
    strict = row > col
    ltri = jnp.where(tril, 1.0, 0.0).astype(BF16)
    eye = jnp.where(row == col, 1.0, 0.0).astype(BF16)
    onw = onw_ref[...]
    heads = range(GDN_HEADS)
    chunks = range(lt // CHUNK)

    blocks = [(c, j) for c in chunks for j in heads]
    pre = {}
    for c in chunks:
        rows = slice(c * CHUNK, (c + 1) * CHUNK)
        dec = _cumsum_rows(g_all[rows], ltri)
        dec_t = dec.T
        for j in heads:
            q, k, v = q_ref[j, rows, :], k_ref[j, rows, :], v_ref[j, rows, :]
            beta_col = _lane_column(beta_all[rows], lane_idx, j)
            dec_col = _lane_column(dec, lane_idx, GDN_HEADS + j)
            dec_row = dec_t[GDN_HEADS + j:GDN_HEADS + j + 1, :]
            dec_last = dec_row[:, CHUNK - 1:CHUNK]
            gam = jnp.exp(jnp.minimum(dec_col - dec_row, 0.0))
            e_col = jnp.exp(dec_col)
            kb = k * beta_col
            pre[c, j] = dict(q=q, k=k, gam=gam, kb=kb, qe=q * e_col, e_last=jnp.exp(dec_last),
                             kd=k * jnp.exp(dec_last - dec_col),
                             rhs=jnp.concatenate([v * beta_col, kb * e_col], axis=1).astype(BF16))
    grams = [_mm_nt(jnp.concatenate([pre[b]["kb"], pre[b]["q"]], axis=0), pre[b]["k"]) for b in blocks]
    ms = [jnp.where(strict, g[:CHUNK] * pre[b]["gam"], 0.0).astype(BF16) for g, b in zip(grams, blocks)]
    a_intra = {b: jnp.where(tril, g[CHUNK:] * pre[b]["gam"], 0.0) for g, b in zip(grams, blocks)}
    t_inv = _unit_lower_inverses(ms, masks_ref, eye)
    uw = {b: jnp.dot(t, pre[b]["rhs"], preferred_element_type=F32) for t, b in zip(t_inv, blocks)}

    for c in chunks:
        rows = slice(c * CHUNK, (c + 1) * CHUNK)
        s_prev = [s_ref[j] for j in heads]
        ws_qs = [_mm(jnp.concatenate([uw[c, j][:, GDN_DV:], pre[c, j]["qe"]], axis=0), s_prev[j]) for j in heads]
        v_new = [uw[c, j][:, :GDN_DV] - ws_qs[j][:CHUNK] for j in heads]
        o = [ws_qs[j][CHUNK:] + _mm(a_intra[c, j], v_new[j]) for j in heads]
        s_new = [s_prev[j] * pre[c, j]["e_last"] + _mm(pre[c, j]["kd"].T, v_new[j]) for j in heads]
        for j in heads:
            s_ref[j] = s_new[j]
            og = _gated_out_norm(o[j], gate_ref[j, rows, :], onw)
            og_ref[rows, j * GDN_DV:(j + 1) * GDN_DV] = og.astype(og_ref.dtype)


def _gdn_prompt(qkvf, gact, ba, alog_row, dtb_row, onw, masks, lt):
    b_, _, l_, _ = qkvf.shape
    sec = lambda s: pl.BlockSpec((None, GDN_HEADS, lt, LANES), lambda b, t, s=s: (b, s, t, 0))
    return pl.pallas_call(
        functools.partial(_gdn_prompt_body, lt),
        grid=(b_, l_ // lt),
        in_specs=[sec(0), sec(1), sec(2),
                  pl.BlockSpec((None, lt, LANES), lambda b, t: (b, t, 0)),
                  _const_spec(alog_row.shape), _const_spec(dtb_row.shape),
                  pl.BlockSpec((None, GDN_HEADS, lt, LANES), lambda b, t: (b, 0, t, 0)),
                  _const_spec(onw.shape), _const_spec(masks.shape)],
        out_specs=(pl.BlockSpec((None, lt, GDN_V), lambda b, t: (b, t, 0)),
                   pl.BlockSpec((None, GDN_HEADS, GDN_DK, GDN_DV), lambda b, t: (b, 0, 0, 0))),
        out_shape=(jax.ShapeDtypeStruct((b_, l_, GDN_V), BF16),
                   jax.ShapeDtypeStruct((b_, GDN_HEADS, GDN_DK, GDN_DV), F32)),
        compiler_params=_params(("arbitrary", "arbitrary")),
        name="gdn_prompt",
    )(qkvf, qkvf, qkvf, ba, alog_row, dtb_row, gact, onw, masks)


def _gdn_step_body(bb, x_ref, st_ref, cw_ref, ba_ref, alog_ref, dtb_ref, gate_ref, onw_ref, s0_ref,
                   og_ref, sn_ref, q_s, k_s, v_s, b_s, e_s, o_s):
    beta_all, g_all = _delta_gates(ba_ref[...], alog_ref[...], dtb_ref[...])
    lane_idx = lax.broadcasted_iota(jnp.int32, (bb, LANES), 1)
    for h in range(GDN_HEADS):
        feats = []
        for s in range(3):
            idx = s * GDN_HEADS + h
            w = cw_ref[idx]
            y = w[0:1] * st_ref[0, idx]
            for tap in range(1, GDN_CONV - 1):
                y = y + w[tap:tap + 1] * st_ref[tap, idx]
            y = y + w[GDN_CONV - 1:GDN_CONV] * x_ref[idx]
            feats.append(_silu(y))
        q, k, v = feats
        q_s[h] = q * lax.rsqrt(jnp.sum(q * q, axis=-1, keepdims=True) + EPS) * (GDN_DK ** -0.5)
        k_s[h] = k * lax.rsqrt(jnp.sum(k * k, axis=-1, keepdims=True) + EPS)
        v_s[h] = v
        b_s[h] = jnp.broadcast_to(_lane_column(beta_all, lane_idx, h), (bb, LANES))
        e_s[h] = jnp.broadcast_to(jnp.exp(_lane_column(g_all, lane_idx, h + GDN_HEADS)), (bb, LANES))

    eye = (lax.broadcasted_iota(jnp.int32, (GDN_DK, GDN_DK), 0)
           == lax.broadcasted_iota(jnp.int32, (GDN_DK, GDN_DK), 1))

    def to_col(r):
        return jnp.sum(jnp.where(eye, jnp.broadcast_to(r, (GDN_DK, GDN_DK)), 0.0), axis=1, keepdims=True)

    def seq_body(i, carry):
        for h in range(GDN_HEADS):
            one = pl.ds(i, 1)
            k_col = to_col(k_s[h, one, :])
            q_col = to_col(q_s[h, one, :])
            s1 = s0_ref[i, h] * e_s[h, one, :]
            ks = jnp.sum(s1 * k_col, axis=0, keepdims=True)
            delta = (v_s[h, one, :] - ks) * b_s[h, one, :]
            s2 = s1 + k_col * delta
            sn_ref[i, h] = s2
            o_s[h, one, :] = jnp.sum(s2 * q_col, axis=0, keepdims=True)
        return carry

    lax.fori_loop(0, bb, seq_body, 0)
    onw = onw_ref[...]
    for h in range(GDN_HEADS):
        og = _gated_out_norm(o_s[h], _silu(gate_ref[h]), onw)
        og_ref[:, h * GDN_DV:(h + 1) * GDN_DV] = og.astype(og_ref.dtype)


def _gdn_step(qkv4, st4, gg, ba, cw, alog_row, dtb_row, onw, s0, bb):
    n_ = ba.shape[0]
    vec = pltpu.VMEM((GDN_HEADS, bb, LANES), F32)
    return pl.pallas_call(
        functools.partial(_gdn_step_body, bb),
        grid=(n_ // bb,),
        in_specs=[pl.BlockSpec((GDN_SECTIONS, bb, LANES), lambda i: (0, i, 0)),
                  pl.BlockSpec((GDN_CONV - 1, GDN_SECTIONS, bb, LANES), lambda i: (0, 0, i, 0)),
                  _const_spec(cw.shape),
                  pl.BlockSpec((bb, LANES), lambda i: (i, 0)),
                  _const_spec(alog_row.shape), _const_spec(dtb_row.shape),
                  pl.BlockSpec((GDN_HEADS, bb, LANES), lambda i: (0, i, 0)),
                  _const_spec(onw.shape),
                  pl.BlockSpec((bb, GDN_HEADS, GDN_DK, GDN_DV), lambda i: (i, 0, 0, 0))],
        out_specs=(pl.BlockSpec((bb, GDN_V), lambda i: (i, 0)),
                   pl.BlockSpec((bb, GDN_HEADS, GDN_DK, GDN_DV), lambda i: (i, 0, 0, 0))),
        out_shape=(jax.ShapeDtypeStruct((n_, GDN_V), BF16),
                   jax.ShapeDtypeStruct(s0.shape, F32)),
        scratch_shapes=[vec, vec, vec, vec, vec, vec],
        compiler_params=_params(("arbitrary",)),
        name="gdn_step",
    )(qkv4, st4, cw, ba, alog_row, dtb_row, gg, onw, s0)


def _swa_prompt_body(sinks_ref, q_ref, kvp_ref, kvc_ref, o_ref):
    n = pl.program_id(1)
    w = WINDOW
    tiles = SWA_KV // LANES
    pairs = 2
    lo_lane = lax.broadcasted_iota(jnp.int32, (w, LANES), 1) < SWA_HD
    lo_row = lax.broadcasted_iota(jnp.int32, (LANES, w), 0) < SWA_HD
    c = lax.broadcasted_iota(jnp.int32, (2 * w, pairs * w), 0)
    i = lax.broadcasted_iota(jnp.int32, (2 * w, pairs * w), 1) & (w - 1)
    valid = (c > i) & (c <= i + w) & ((c >= w) | (n > 0))
    k_nat, k_rot, vt_nat, vt_rot = [], [], [], []
    for t in range(tiles):
        kcols = slice(t * LANES, (t + 1) * LANES)
        vcols = slice(SWA_KV + t * LANES, SWA_KV + (t + 1) * LANES)
        kx = jnp.concatenate([kvp_ref[:, kcols], kvc_ref[:, kcols]], axis=0)
        vt = jnp.concatenate([kvp_ref[:, vcols], kvc_ref[:, vcols]], axis=0).T
        k_nat.append(kx.astype(BF16))
        k_rot.append(pltpu.roll(kx, SWA_HD, axis=1).astype(BF16))
        vt_nat.append(vt.astype(BF16))
        vt_rot.append(jnp.concatenate([vt[SWA_HD:], vt[:SWA_HD]], axis=0).astype(BF16))
    items = [(g, p) for g in range(SWA_KV_HEADS) for p in range(2)]
    scale = SWA_HD ** -0.5
    qm, kz, vzt, sink = {}, {}, {}, {}
    for g, p in items:
        keep = lo_lane if p == 0 else jnp.logical_not(lo_lane)
        q_tiles = [q_ref[:, (2 * g + r) * LANES:(2 * g + r + 1) * LANES] for r in range(pairs)]
        qm[g, p] = jnp.concatenate([jnp.where(keep, x * scale, 0.0) for x in q_tiles], axis=0).astype(BF16)
        natural = p == g % 2
        kz[g, p] = (k_nat if natural else k_rot)[g // 2]
        vzt[g, p] = (vt_nat if natural else vt_rot)[g // 2]
        sink[g, p] = jnp.concatenate([jnp.full((1, w), sinks_ref[SWA_GROUP * g + 2 * r + p], F32)
                                      for r in range(pairs)], axis=1)
    st = {b: jnp.where(valid, lax.dot_general(kz[b], qm[b], (((1,), (1,)), ((), ())),
                                              preferred_element_type=F32), -jnp.inf) for b in items}
    m = {b: jnp.maximum(jnp.max(st[b], axis=0, keepdims=True), sink[b]) for b in items}
    et = {b: jnp.exp(st[b] - m[b]) for b in items}
    den = {b: jnp.sum(et[b], axis=0, keepdims=True) + jnp.exp(sink[b] - m[b]) for b in items}
    ot = {b: jnp.dot(vzt[b], et[b].astype(BF16), preferred_element_type=F32) / den[b] for b in items}
    for g in range(SWA_KV_HEADS):
        for r in range(pairs):
            cols = slice(r * w, (r + 1) * w)
            tile_t = jnp.where(lo_row, ot[g, 0][:, cols], ot[g, 1][:, cols])
            o_ref[:, (2 * g + r) * LANES:(2 * g + r + 1) * LANES] = tile_t.T.astype(o_ref.dtype)


def _swa_prompt(sq, skv, sinks):
    b_, l_, _ = sq.shape
    nb = l_ // WINDOW
    return pl.pallas_call(
        _swa_prompt_body,
        grid=(b_, nb),
        in_specs=[pl.BlockSpec(memory_space=pltpu.SMEM),
                  pl.BlockSpec((None, WINDOW, SWA_Q), lambda b, n: (b, n, 0)),
                  pl.BlockSpec((None, WINDOW, 2 * SWA_KV), lambda b, n: (b, jnp.maximum(n - 1, 0), 0)),
                  pl.BlockSpec((None, WINDOW, 2 * SWA_KV), lambda b, n: (b, n, 0))],
        out_specs=pl.BlockSpec((None, WINDOW, SWA_Q), lambda b, n: (b, n, 0)),
        out_shape=jax.ShapeDtypeStruct((b_, l_, SWA_Q), BF16),
        compiler_params=_params(("arbitrary", "arbitrary")),
        name="swa_prompt",
    )(sinks, sq, skv, skv)


def _swa_step_body(bb, q_ref, kvn_ref, ck_ref, cv_ref, sink_ref, o_ref, nk_ref, nv_ref):
    w = WINDOW
    row = lax.broadcasted_iota(jnp.int32, (SWA_Q_HEADS, SWA_KV), 0)
    lane = lax.broadcasted_iota(jnp.int32, (SWA_Q_HEADS, SWA_KV), 1)
    own = (lane // SWA_HD) == (row // SWA_GROUP)
    key = lax.broadcasted_iota(jnp.int32, (SWA_Q_HEADS, w), 1)
    sink = sink_ref[...]
    scale = SWA_HD ** -0.5
    for i in range(bb):
        qb = q_ref[i]
        q_bd = jnp.where(own, jnp.concatenate([qb] * SWA_KV_HEADS, axis=1), 0.0)
        kc = ck_ref[i]
        vc = cv_ref[i]
        kn = kvn_ref[i:i + 1, 0:SWA_KV]
        vn = kvn_ref[i:i + 1, SWA_KV:2 * SWA_KV]
        s_c = jnp.where(key >= 1, _mm_nt(q_bd, kc) * scale, -jnp.inf)
        s_n = jnp.sum(q_bd * kn, axis=1, keepdims=True) * scale
        m = jnp.maximum(jnp.maximum(jnp.max(s_c, axis=1, keepdims=True), s_n), sink)
        e_c = jnp.exp(s_c - m)
        e_n = jnp.exp(s_n - m)
        den = jnp.sum(e_c, axis=1, keepdims=True) + e_n + jnp.exp(sink - m)
        pv = jnp.where(own, _mm(e_c / den, vc) + (e_n / den) * vn, 0.0)
        o = pv[:, 0:SWA_HD]
        for g in range(1, SWA_KV_HEADS):
            o = o + pv[:, g * SWA_HD:(g + 1) * SWA_HD]
        o_ref[i] = o
        nk_ref[i, 0:w - 1, :] = ck_ref[i, 1:w, :]
        nk_ref[i, w - 1:w, :] = kn
        nv_ref[i, 0:w - 1, :] = cv_ref[i, 1:w, :]
        nv_ref[i, w - 1:w, :] = vn


def _swa_step(q3, kvn, ck, cv, sink_col, bb):
    n_ = q3.shape[0]
    cache = pl.BlockSpec((bb, WINDOW, SWA_KV), lambda i: (i, 0, 0))
    return pl.pallas_call(
        functools.partial(_swa_step_body, bb),
        grid=(n_ // bb,),
        in_specs=[pl.BlockSpec((bb, SWA_Q_HEADS, SWA_HD), lambda i: (i, 0, 0)),
                  pl.BlockSpec((bb, 2 * SWA_KV), lambda i: (i, 0)),
                  cache, cache,
                  _const_spec(sink_col.shape)],
        out_specs=(pl.BlockSpec((bb, SWA_Q_HEADS, SWA_HD), lambda i: (i, 0, 0)), cache, cache),
        out_shape=(jax.ShapeDtypeStruct(q3.shape, F32),
                   jax.ShapeDtypeStruct(ck.shape, F32),
                   jax.ShapeDtypeStruct(cv.shape, F32)),
        compiler_params=_params(("arbitrary",)),
        name="swa_step",
    )(q3, kvn, ck, cv, sink_col)


def _dense_body(stateful, tm, og_ref, ob_ref, gab_ref, x_ref, gt1_ref, sh2_ref, sc2_ref, gt2_ref,
                n2w_ref, fnw_ref, wa_ref, wb_ref, wo_ref, wg_ref, wu_ref, cw_ref, cb_ref, wd_ref, *rest):
    if stateful:
        st_ref, y_ref, gout_ref, act_ref = rest
    else:
        y_ref, gout_ref, act_ref, gbuf_ref, carry_ref = rest

        @pl.when(pl.program_id(1) == 0)
        def _():
            carry_ref[...] = jnp.zeros_like(carry_ref)

    y_a = jnp.dot(og_ref[...], wa_ref[...], preferred_element_type=F32)
    y_b = jnp.dot(ob_ref[...], wb_ref[...], preferred_element_type=F32)
    merged = (jax.nn.sigmoid(gab_ref[:, 0:D_MODEL]) * y_a
              + jax.nn.sigmoid(gab_ref[:, D_MODEL:2 * D_MODEL]) * y_b)
    x1 = x_ref[...] + gt1_ref[...] * _mm(merged, wo_ref[...])
    h2 = (_rms(x1, n2w_ref[...]) * (1.0 + sc2_ref[...]) + sh2_ref[...]).astype(BF16)

    for c in range(D_FF // FFN_COLS):
        cols = slice(c * FFN_COLS, (c + 1) * FFN_COLS)
        gate = jnp.dot(h2, wg_ref[:, cols], preferred_element_type=F32)
        up = jnp.dot(h2, wu_ref[:, cols], preferred_element_type=F32)
        if stateful:
            g2 = st_ref[0, :, cols]
            g1 = st_ref[1, :, cols]
            gout_ref[:, cols] = gate
        else:
            gbuf_ref[0:SUBLANES, :] = carry_ref[:, cols]
            gbuf_ref[SUBLANES:SUBLANES + tm, :] = gate
            g2 = gbuf_ref[SUBLANES - 2:SUBLANES - 2 + tm, :]
            g1 = gbuf_ref[SUBLANES - 1:SUBLANES - 1 + tm, :]
            carry_ref[:, cols] = gbuf_ref[tm:tm + SUBLANES, :]
        gc = (cw_ref[0:1, cols] * g2 + cw_ref[1:2, cols] * g1 + cw_ref[2:3, cols] * gate) + cb_ref[:, cols]
        act_ref[:, cols] = (_silu(gc) * up).astype(BF16)
    if not stateful:
        gout_ref[...] = carry_ref[...]

    x2 = x1 + gt2_ref[...] * jnp.dot(act_ref[...], wd_ref[...], preferred_element_type=F32)
    y_ref[...] = _rms(x2, fnw_ref[...])


def _dense(og, ob, gab, x, mods, vecs, ws, st, tm):
    b_, l_, _ = x.shape
    r_ = mods[0].shape[1]
    rt = 1 if r_ == 1 else tm
    mod_map = (lambda b, t: (b, 0, 0)) if r_ == 1 else (lambda b, t: (b, t, 0))
    row_map = lambda b, t: (b, t, 0)
    stateful = st is not None
    in_specs = ([pl.BlockSpec((None, tm, D_MODEL), row_map),
                 pl.BlockSpec((None, tm, D_MODEL), row_map),
                 pl.BlockSpec((None, tm, 2 * D_MODEL), row_map),
                 pl.BlockSpec((None, tm, D_MODEL), row_map)]
                + [pl.BlockSpec((None, rt, D_MODEL), mod_map)] * 4
                + [_const_spec(a.shape) for a in vecs[:2]]
                + [_const_spec(ws[0].shape), _const_spec(ws[1].shape), _const_spec(ws[2].shape),
                   _const_spec(ws[3].shape), _const_spec(ws[4].shape),
                   _const_spec(vecs[2].shape), _const_spec(vecs[3].shape), _const_spec(ws[5].shape)])
    args = [og, ob, gab, x, *mods, vecs[0], vecs[1], ws[0], ws[1], ws[2], ws[3], ws[4], vecs[2], vecs[3], ws[5]]
    scratch = [pltpu.VMEM((tm, D_FF), BF16)]
    if stateful:
        in_specs.append(pl.BlockSpec((FFN_CONV - 1, None, tm, D_FF), lambda b, t: (0, b, t, 0)))
        args.append(st)
        gout_shape = jax.ShapeDtypeStruct((b_, l_, D_FF), F32)
        gout_spec = pl.BlockSpec((None, tm, D_FF), row_map)
    else:
        scratch += [pltpu.VMEM((tm + SUBLANES, FFN_COLS), F32), pltpu.VMEM((SUBLANES, D_FF), F32)]
        gout_shape = jax.ShapeDtypeStruct((b_, SUBLANES, D_FF), F32)
        gout_spec = pl.BlockSpec((None, SUBLANES, D_FF), lambda b, t: (b, 0, 0))
    return pl.pallas_call(
        functools.partial(_dense_body, stateful, tm),
        grid=(b_, l_ // tm),
        in_specs=in_specs,
        out_specs=(pl.BlockSpec((None, tm, D_MODEL), row_map), gout_spec),
        out_shape=(jax.ShapeDtypeStruct((b_, l_, D_MODEL), F32), gout_shape),
        scratch_shapes=scratch,
        compiler_params=_params(("arbitrary", "arbitrary")),
        name="dense_step" if stateful else "dense_prompt",
    )(*args)


def _lane_row(values, offset):
    return jnp.zeros((1, LANES), F32).at[0, offset:offset + values.shape[0]].set(values)


def kernel(x_prompt, x_sample, c_prompt, c_sample, state_gdn_S, state_gdn_conv, cache_swa_k, cache_swa_v,
           state_ffn_conv, w_mod, b_mod, norm1_w, norm2_w, w_in, gdn_conv_w, gdn_a_log, gdn_dt_bias,
           gdn_onorm_w, w_gdn_out, swa_sinks, w_swa_out, w_o, w_ffn_gate, w_ffn_up, ffn_conv_w, ffn_conv_b,
           w_ffn_down, final_norm_w):
    assert w_mod.shape[0] == 1, "single-layer trunk"
    nb, seq, _ = x_prompt.shape
    ns = x_sample.shape[0]
    assert x_sample.shape[1] == 1

    w_in0 = w_in[0]
    o0 = GDN_CONV_CH
    o1 = o0 + GDN_V
    o2 = o1 + 2 * GDN_HEADS
    o3 = o2 + SWA_Q
    o4 = o3 + 2 * SWA_KV
    w_ba = jnp.zeros((D_MODEL, LANES), BF16).at[:, :2 * GDN_HEADS].set(w_in0[:, o1:o2].astype(BF16))
    in_ws = (w_in0[:, :o0].astype(BF16), w_in0[:, o0:o1].astype(BF16), w_ba, w_in0[:, o2:o3].astype(BF16),
             w_in0[:, o3:o4].astype(BF16), w_in0[:, o4:].astype(BF16))
    dense_ws = (w_gdn_out[0].astype(BF16), w_swa_out[0].astype(BF16), w_o[0].astype(BF16),
                w_ffn_gate[0].astype(BF16), w_ffn_up[0].astype(BF16), w_ffn_down[0].astype(BF16))
    dense_vecs = (norm2_w, final_norm_w[None, :], ffn_conv_w[0], ffn_conv_b)
    cw = jnp.transpose(gdn_conv_w[0].reshape(GDN_CONV, GDN_SECTIONS, LANES), (1, 0, 2))
    alog_row = _lane_row(gdn_a_log[0], GDN_HEADS)
    dtb_row = _lane_row(gdn_dt_bias[0], GDN_HEADS)

    mod = _modulation(jnp.concatenate([c_prompt, c_sample], axis=0), w_mod[0].astype(BF16), b_mod)
    mod_p = [mod[:nb, i * D_MODEL:(i + 1) * D_MODEL][:, None, :] for i in range(6)]
    mod_s = [mod[nb:, i * D_MODEL:(i + 1) * D_MODEL][None, :, :] for i in range(6)]

    qkvf, gact, ba, sq, skv, gab, qkv_tail = _inproj(x_prompt, mod_p[0], mod_p[1], norm1_w, in_ws, cw, tm=256)
    og, gdn_s_p = _gdn_prompt(qkvf, gact, ba, alog_row, dtb_row, gdn_onorm_w, _level_masks(), lt=CHUNK)
    ob = _swa_prompt(sq, skv, swa_sinks[0])
    y_p, gate_tail = _dense(og, ob, gab, x_prompt, (mod_p[2], mod_p[3], mod_p[4], mod_p[5]),
                            dense_vecs, dense_ws, None, tm=256)
    gdn_conv_p = jnp.transpose(qkv_tail[:, :, SUBLANES - (GDN_CONV - 1):, :], (0, 2, 1, 3)).reshape(
        nb, GDN_CONV - 1, GDN_CONV_CH)
    k_p = skv[:, seq - WINDOW:, :SWA_KV].reshape(nb, WINDOW, SWA_KV_HEADS, SWA_HD)
    v_p = skv[:, seq - WINDOW:, SWA_KV:].reshape(nb, WINDOW, SWA_KV_HEADS, SWA_HD)
    ffn_conv_p = gate_tail[:, SUBLANES - (FFN_CONV - 1):, :]

    xs = x_sample.reshape(1, ns, D_MODEL)
    qkv4s, ggs, bas, sqs, skvs, gabs = _inproj(xs, mod_s[0], mod_s[1], norm1_w, in_ws, None, tm=ns)
    st4 = jnp.transpose(state_gdn_conv[0].reshape(ns, GDN_CONV - 1, GDN_SECTIONS, LANES), (1, 2, 0, 3))
    og_s, gdn_s_s = _gdn_step(qkv4s[0], st4, ggs[0], bas[0], cw, alog_row, dtb_row, gdn_onorm_w,
                              state_gdn_S[0], bb=8)
    o3, k_s, v_s = _swa_step(sqs[0].reshape(ns, SWA_Q_HEADS, SWA_HD), skvs[0],
                             cache_swa_k[0].reshape(ns, WINDOW, SWA_KV),
                             cache_swa_v[0].reshape(ns, WINDOW, SWA_KV),
                             swa_sinks[0][:, None], bb=8)
    ob_s = o3.reshape(1, ns, SWA_Q).astype(BF16)
    st_ffn = jnp.transpose(state_ffn_conv[0], (1, 0, 2))[:, None]
    y_s, gate_new = _dense(og_s[None], ob_s, gabs, xs, (mod_s[2], mod_s[3], mod_s[4], mod_s[5]),
                           dense_vecs, dense_ws, st_ffn, tm=ns)
    qkv_new = jnp.transpose(qkv4s[0], (1, 0, 2)).reshape(ns, 1, GDN_CONV_CH)
    gdn_conv_s = jnp.concatenate([state_gdn_conv[0][:, 1:], qkv_new], axis=1)
    ffn_conv_s = jnp.concatenate([state_ffn_conv[0][:, 1:], gate_new[0][:, None, :]], axis=1)

    return (y_p, y_s.reshape(ns, 1, D_MODEL),
            gdn_s_p[None], gdn_s_s[None],
            gdn_conv_p[None], gdn_conv_s[None],
            k_p[None], k_s.reshape(ns, WINDOW, SWA_KV_HEADS, SWA_HD)[None],
            v_p[None], v_s.reshape(ns, WINDOW, SWA_KV_HEADS, SWA_HD)[None],
            ffn_conv_p[None], ffn_conv_s[None])
```

```python
import functools

import numpy as np
import jax
import jax.numpy as jnp
from jax import lax
from jax.experimental import pallas as pl
from jax.experimental.pallas import tpu as pltpu

F32 = jnp.float32
BF16 = jnp.bfloat16

D_MODEL = 1024
GDN_HEADS = 8
GDN_DK = 128
GDN_DV = 128
GDN_QK = GDN_HEADS * GDN_DK
GDN_V = GDN_HEADS * GDN_DV
GDN_CONV = 4
GDN_CONV_CH = 2 * GDN_QK + GDN_V
GDN_SECTIONS = GDN_CONV_CH // 128
SWA_Q_HEADS = 16
SWA_KV_HEADS = 4
SWA_GROUP = SWA_Q_HEADS // SWA_KV_HEADS
SWA_HD = 64
SWA_Q = SWA_Q_HEADS * SWA_HD
SWA_KV = SWA_KV_HEADS * SWA_HD
WINDOW = 128
D_FF = 2816
FFN_CONV = 3
EPS = 1e-6

LANES = 128
SUBLANES = 8
VMEM_LIMIT = 56 * 1024 * 1024

COL_QKV = 0
COL_GATE = COL_QKV + GDN_CONV_CH
COL_SQ = COL_GATE + GDN_V
COL_SKV = COL_SQ + SWA_Q
COL_GAB = COL_SKV + 2 * SWA_KV
COL_BA = COL_GAB + 2 * D_MODEL
IN_COLS = COL_BA + LANES

CHUNK = 128
FFN_COLS = 256


def _mm(a, b):
    return jnp.dot(a.astype(BF16), b.astype(BF16), preferred_element_type=F32)


def _mm_nt(a, b):
    return lax.dot_general(a.astype(BF16), b.astype(BF16), (((1,), (1,)), ((), ())),
                           preferred_element_type=F32)


def _silu(x):
    return x * jax.nn.sigmoid(x)


def _softplus(x):
    return jnp.maximum(x, 0.0) + jnp.log1p(jnp.exp(-jnp.abs(x)))


def _rms(x, w):
    return x * lax.rsqrt(jnp.mean(x * x, axis=-1, keepdims=True) + EPS) * w


def _const_spec(shape):
    n = len(shape)
    return pl.BlockSpec(shape, lambda *_: (0,) * n, pipeline_mode=pl.Buffered(1))


def _params(sem):
    return pltpu.CompilerParams(dimension_semantics=sem, vmem_limit_bytes=VMEM_LIMIT)


def _mod_body(c_ref, w_ref, b_ref, o_ref):
    o_ref[...] = _mm(_silu(c_ref[...]), w_ref[...]) + b_ref[...]


def _modulation(c_all, w_mod, b_mod):
    rows = c_all.shape[0]
    n_out = w_mod.shape[1]
    tn = D_MODEL
    return pl.pallas_call(
        _mod_body,
        grid=(n_out // tn,),
        in_specs=[pl.BlockSpec((rows, D_MODEL), lambda j: (0, 0)),
                  pl.BlockSpec((D_MODEL, tn), lambda j: (0, j)),
                  pl.BlockSpec((1, tn), lambda j: (0, j))],
        out_specs=pl.BlockSpec((rows, tn), lambda j: (0, j)),
        out_shape=jax.ShapeDtypeStruct((rows, n_out), F32),
        compiler_params=_params(("arbitrary",)),
        name="modulation",
    )(c_all, w_mod, b_mod)


def _l2norm(x):
    return x * lax.rsqrt(jnp.sum(x * x, axis=-1, keepdims=True) + EPS)


def _inproj_body(seq_rows, tm, x_ref, sh_ref, sc_ref, nw_ref, w_ref, *rest):
    if seq_rows:
        cw_ref, qkv_ref, gg_ref, ba_ref, sq_ref, skv_ref, gab_ref, tail_ref, xe_ref = rest

        @pl.when(pl.program_id(1) == 0)
        def _():
            xe_ref[:, 0:SUBLANES, :] = jnp.zeros((GDN_SECTIONS, SUBLANES, LANES), F32)
    else:
        qkv_ref, gg_ref, ba_ref, sq_ref, skv_ref, gab_ref = rest

    h = _rms(x_ref[...], nw_ref[...]) * (1.0 + sc_ref[...]) + sh_ref[...]
    hb = h.astype(BF16)

    def proj(lo, width):
        return jnp.dot(hb, w_ref[:, lo:lo + width], preferred_element_type=F32)

    step = 512
    per = step // LANES
    for c in range(GDN_CONV_CH // step):
        z = proj(COL_QKV + c * step, step)
        for k in range(per):
            s = c * per + k
            zs = z[:, k * LANES:(k + 1) * LANES]
            if not seq_rows:
                qkv_ref[s] = zs
                continue
            xe_ref[s, SUBLANES:SUBLANES + tm, :] = zs
            w = cw_ref[s]
            y = w[0:1] * xe_ref[s, SUBLANES - 3:SUBLANES - 3 + tm, :]
            for tap in range(1, GDN_CONV):
                lo = SUBLANES - 3 + tap
                y = y + w[tap:tap + 1] * xe_ref[s, lo:lo + tm, :]
            xe_ref[s, 0:SUBLANES, :] = xe_ref[s, tm:tm + SUBLANES, :]
            f = _silu(y)
            if s < GDN_HEADS:
                f = _l2norm(f) * (GDN_DK ** -0.5)
            elif s < 2 * GDN_HEADS:
                f = _l2norm(f)
            qkv_ref[s] = f
    if seq_rows:
        tail_ref[...] = xe_ref[:, 0:SUBLANES, :]
    for c in range(GDN_V // step):
        z = proj(COL_GATE + c * step, step)
        for k in range(per):
            zs = z[:, k * LANES:(k + 1) * LANES]
            gg_ref[c * per + k] = _silu(zs) if seq_rows else zs
    ba_ref[...] = proj(COL_BA, LANES)
    for c in range(SWA_Q // step):
        sq_ref[:, c * step:(c + 1) * step] = proj(COL_SQ + c * step, step)
    skv_ref[...] = proj(COL_SKV, 2 * SWA_KV)
    for c in range(2 * D_MODEL // step):
        gab_ref[:, c * step:(c + 1) * step] = proj(COL_GAB + c * step, step)


def _inproj(x, sh, sc, nw, w_all, cw, tm):
    b_, l_, _ = x.shape
    r_ = sh.shape[1]
    rt = 1 if r_ == 1 else tm
    mod_map = (lambda b, t: (b, 0, 0)) if r_ == 1 else (lambda b, t: (b, t, 0))
    row_map = lambda b, t: (b, t, 0)
    head_map = lambda b, t: (b, 0, t, 0)
    seq_rows = cw is not None
    out_shape = (
        jax.ShapeDtypeStruct((b_, GDN_SECTIONS, l_, LANES), F32),
        jax.ShapeDtypeStruct((b_, GDN_HEADS, l_, LANES), F32),
        jax.ShapeDtypeStruct((b_, l_, LANES), F32),
        jax.ShapeDtypeStruct((b_, l_, SWA_Q), F32),
        jax.ShapeDtypeStruct((b_, l_, 2 * SWA_KV), F32),
        jax.ShapeDtypeStruct((b_, l_, 2 * D_MODEL), F32),
    )
    out_specs = (
        pl.BlockSpec((None, GDN_SECTIONS, tm, LANES), head_map),
        pl.BlockSpec((None, GDN_HEADS, tm, LANES), head_map),
        pl.BlockSpec((None, tm, LANES), row_map),
        pl.BlockSpec((None, tm, SWA_Q), row_map),
        pl.BlockSpec((None, tm, 2 * SWA_KV), row_map),
        pl.BlockSpec((None, tm, 2 * D_MODEL), row_map),
    )
    in_specs = [
        pl.BlockSpec((None, tm, D_MODEL), row_map),
        pl.BlockSpec((None, rt, D_MODEL), mod_map),
        pl.BlockSpec((None, rt, D_MODEL), mod_map),
        _const_spec(nw.shape),
        _const_spec(w_all.shape),
    ]
    args = [x, sh, sc, nw, w_all]
    scratch = []
    if seq_rows:
        in_specs.append(_const_spec(cw.shape))
        args.append(cw)
        out_shape += (jax.ShapeDtypeStruct((b_, GDN_SECTIONS, SUBLANES, LANES), F32),)
        out_specs += (pl.BlockSpec((None, GDN_SECTIONS, SUBLANES, LANES), lambda b, t: (b, 0, 0, 0)),)
        scratch.append(pltpu.VMEM((GDN_SECTIONS, tm + SUBLANES, LANES), F32))
    return pl.pallas_call(
        functools.partial(_inproj_body, seq_rows, tm),
        grid=(b_, l_ // tm),
        in_specs=in_specs,
        out_specs=out_specs,
        out_shape=out_shape,
        scratch_shapes=scratch,
        compiler_params=_params(("arbitrary", "arbitrary")),
        name="inproj_seq" if seq_rows else "inproj_rows",
    )(*args)


def _delta_gates(ba, alog_row, dtb_row):
    beta_all = jax.nn.sigmoid(ba)
    g_all = -jnp.exp(alog_row) * _softplus(ba + dtb_row)
    return beta_all, g_all


def _lane_column(x, lane_idx, lane):
    return jnp.sum(jnp.where(lane_idx == lane, x, 0.0), axis=1, keepdims=True)


def _level_masks():
    r = np.arange(CHUNK)[:, None]
    c = np.arange(CHUNK)[None, :]
    masks = [(r == c + 1) & (r % 2 == 1)]
    half = 2
    while half < CHUNK:
        full = 2 * half
        masks.append((r // full == c // full) & (r % full >= half) & (c % full < half))
        half = full
    return jnp.asarray(np.stack(masks), dtype=BF16)


def _unit_lower_inverses(ms, masks_ref, eye):
    ts = [eye - m * masks_ref[0] for m in ms]
    for lvl in range(1, masks_ref.shape[0]):
        off = masks_ref[lvl]
        xs = [jnp.dot(m * off, t, preferred_element_type=F32).astype(BF16) for m, t in zip(ms, ts)]
        ys = [jnp.dot(t, x, preferred_element_type=F32).astype(BF16) for t, x in zip(ts, xs)]
        ts = [t - y for t, y in zip(ts, ys)]
    return ts


def _cumsum_rows(g, ltri):
    hi = g.astype(BF16)
    r1 = g - hi.astype(F32)
    mid = r1.astype(BF16)
    lo = (r1 - mid.astype(F32)).astype(BF16)
    return (jnp.dot(ltri, hi, preferred_element_type=F32) + jnp.dot(ltri, mid, preferred_element_type=F32)
            + jnp.dot(ltri, lo, preferred_element_type=F32))


def _gated_out_norm(o, gate_act, onw):
    on = o * lax.rsqrt(jnp.mean(o * o, axis=-1, keepdims=True) + EPS) * onw
    return on * gate_act


def _gdn_prompt_body(lt, q_ref, k_ref, v_ref, ba_ref, alog_ref, dtb_ref, gate_ref, onw_ref, masks_ref,
                     og_ref, s_ref):
    @pl.when(pl.program_id(1) == 0)
    def _():
        s_ref[...] = jnp.zeros_like(s_ref)

    beta_all, g_all = _delta_gates(ba_ref[...], alog_ref[...], dtb_ref[...])
    lane_idx = lax.broadcasted_iota(jnp.int32, (CHUNK, LANES), 1)
    row = lax.broadcasted_iota(jnp.int32, (CHUNK, CHUNK), 0)
    col = lax.broadcasted_iota(jnp.int32, (CHUNK, CHUNK), 1)
    tril = row >= col
    strict = row > col
    ltri = jnp.where(tril, 1.0, 0.0).astype(BF16)
    eye = jnp.where(row == col, 1.0, 0.0).astype(BF16)
    onw = onw_ref[...]
    heads = range(GDN_HEADS)
    chunks = range(lt // CHUNK)

    blocks = [(c, j) for c in chunks for j in heads]
    pre = {}
    for c in chunks:
        rows = slice(c * CHUNK, (c + 1) * CHUNK)
        dec = _cumsum_rows(g_all[rows], ltri)
        dec_t = dec.T
        for j in heads:
            q, k, v = q_ref[j, rows, :], k_ref[j, rows, :], v_ref[j, rows, :]
            beta_col = _lane_column(beta_all[rows], lane_idx, j)
            dec_col = _lane_column(dec, lane_idx, GDN_HEADS + j)
            dec_row = dec_t[GDN_HEADS + j:GDN_HEADS + j + 1, :]
            dec_last = dec_row[:, CHUNK - 1:CHUNK]
            gam = jnp.exp(jnp.minimum(dec_col - dec_row, 0.0))
            e_col = jnp.exp(dec_col)
            kb = k * beta_col
            pre[c, j] = dict(q=q, k=k, gam=gam, kb=kb, qe=q * e_col, e_last=jnp.exp(dec_last),
                             kd=k * jnp.exp(dec_last - dec_col),
                             rhs=jnp.concatenate([v * beta_col, kb * e_col], axis=1).astype(BF16))
    grams = [_mm_nt(jnp.concatenate([pre[b]["kb"], pre[b]["q"]], axis=0), pre[b]["k"]) for b in blocks]
    ms = [jnp.where(strict, g[:CHUNK] * pre[b]["gam"], 0.0).astype(BF16) for g, b in zip(grams, blocks)]
    a_intra = {b: jnp.where(tril, g[CHUNK:] * pre[b]["gam"], 0.0) for g, b in zip(grams, blocks)}
    t_inv = _unit_lower_inverses(ms, masks_ref, eye)
    uw = {b: jnp.dot(t, pre[b]["rhs"], preferred_element_type=F32) for t, b in zip(t_inv, blocks)}

    for c in chunks:
        rows = slice(c * CHUNK, (c + 1) * CHUNK)
        s_prev = [s_ref[j] for j in heads]
        ws_qs = [_mm(jnp.concatenate([uw[c, j][:, GDN_DV:], pre[c, j]["qe"]], axis=0), s_prev[j]) for j in heads]
        v_new = [uw[c, j][:, :GDN_DV] - ws_qs[j][:CHUNK] for j in heads]
        o = [ws_qs[j][CHUNK:] + _mm(a_intra[c, j], v_new[j]) for j in heads]
        s_new = [s_prev[j] * pre[c, j]["e_last"] + _mm(pre[c, j]["kd"].T, v_new[j]) for j in heads]
        for j in heads:
            s_ref[j] = s_new[j]
            og = _gated_out_norm(o[j], gate_ref[j, rows, :], onw)
            og_ref[rows, j * GDN_DV:(j + 1) * GDN_DV] = og.astype(og_ref.dtype)


def _gdn_prompt(qkvf, gact, ba, alog_row, dtb_row, onw, masks, lt):
    b_, _, l_, _ = qkvf.shape
    sec = lambda s: pl.BlockSpec((None, GDN_HEADS, lt, LANES), lambda b, t, s=s: (b, s, t, 0))
    return pl.pallas_call(
        functools.partial(_gdn_prompt_body, lt),
        grid=(b_, l_ // lt),
        in_specs=[sec(0), sec(1), sec(2),
                  pl.BlockSpec((None, lt, LANES), lambda b, t: (b, t, 0)),
                  _const_spec(alog_row.shape), _const_spec(dtb_row.shape),
                  pl.BlockSpec((None, GDN_HEADS, lt, LANES), lambda b, t: (b, 0, t, 0)),
                  _const_spec(onw.shape), _const_spec(masks.shape)],
        out_specs=(pl.BlockSpec((None, lt, GDN_V), lambda b, t: (b, t, 0)),
                   pl.BlockSpec((None, GDN_HEADS, GDN_DK, GDN_DV), lambda b, t: (b, 0, 0, 0))),
        out_shape=(jax.ShapeDtypeStruct((b_, l_, GDN_V), BF16),
                   jax.ShapeDtypeStruct((b_, GDN_HEADS, GDN_DK, GDN_DV), F32)),
        compiler_params=_params(("arbitrary", "arbitrary")),
        name="gdn_prompt",
    )(qkvf, qkvf, qkvf, ba, alog_row, dtb_row, gact, onw, masks)


def _gdn_step_body(bb, x_ref, st_ref, cw_ref, ba_ref, alog_ref, dtb_ref, gate_ref, onw_ref, s0_ref,
                   og_ref, sn_ref, q_s, k_s, v_s, b_s, e_s, o_s):
    beta_all, g_all = _delta_gates(ba_ref[...], alog_ref[...], dtb_ref[...])
    lane_idx = lax.broadcasted_iota(jnp.int32, (bb, LANES), 1)
    for h in range(GDN_HEADS):
        feats = []
        for s in range(3):
            idx = s * GDN_HEADS + h
            w = cw_ref[idx]
            y = w[0:1] * st_ref[0, idx]
            for tap in range(1, GDN_CONV - 1):
                y = y + w[tap:tap + 1] * st_ref[tap, idx]
            y = y + w[GDN_CONV - 1:GDN_CONV] * x_ref[idx]
            feats.append(_silu(y))
        q, k, v = feats
        q_s[h] = q * lax.rsqrt(jnp.sum(q * q, axis=-1, keepdims=True) + EPS) * (GDN_DK ** -0.5)
        k_s[h] = k * lax.rsqrt(jnp.sum(k * k, axis=-1, keepdims=True) + EPS)
        v_s[h] = v
        b_s[h] = jnp.broadcast_to(_lane_column(beta_all, lane_idx, h), (bb, LANES))
        e_s[h] = jnp.broadcast_to(jnp.exp(_lane_column(g_all, lane_idx, h + GDN_HEADS)), (bb, LANES))

    eye = (lax.broadcasted_iota(jnp.int32, (GDN_DK, GDN_DK), 0)
           == lax.broadcasted_iota(jnp.int32, (GDN_DK, GDN_DK), 1))

    def to_col(r):
        return jnp.sum(jnp.where(eye, jnp.broadcast_to(r, (GDN_DK, GDN_DK)), 0.0), axis=1, keepdims=True)

    def seq_body(i, carry):
        for h in range(GDN_HEADS):
            one = pl.ds(i, 1)
            k_col = to_col(k_s[h, one, :])
            q_col = to_col(q_s[h, one, :])
            s1 = s0_ref[i, h] * e_s[h, one, :]
            ks = jnp.sum(s1 * k_col, axis=0, keepdims=True)
            delta = (v_s[h, one, :] - ks) * b_s[h, one, :]
            s2 = s1 + k_col * delta
            sn_ref[i, h] = s2
            o_s[h, one, :] = jnp.sum(s2 * q_col, axis=0, keepdims=True)
        return carry

    lax.fori_loop(0, bb, seq_body, 0)
    onw = onw_ref[...]
    for h in range(GDN_HEADS):
        og = _gated_out_norm(o_s[h], _silu(gate_ref[h]), onw)
        og_ref[:, h * GDN_DV:(h + 1) * GDN_DV] = og.astype(og_ref.dtype)


def _gdn_step(qkv4, st4, gg, ba, cw, alog_row, dtb_row, onw, s0, bb):
    n_ = ba.shape[0]
    vec = pltpu.VMEM((GDN_HEADS, bb, LANES), F32)
    return pl.pallas_call(
        functools.partial(_gdn_step_body, bb),
        grid=(n_ // bb,),
        in_specs=[pl.BlockSpec((GDN_SECTIONS, bb, LANES), lambda i: (0, i, 0)),
                  pl.BlockSpec((GDN_CONV - 1, GDN_SECTIONS, bb, LANES), lambda i: (0, 0, i, 0)),
                  _const_spec(cw.shape),
                  pl.BlockSpec((bb, LANES), lambda i: (i, 0)),
                  _const_spec(alog_row.shape), _const_spec(dtb_row.shape),
                  pl.BlockSpec((GDN_HEADS, bb, LANES), lambda i: (0, i, 0)),
                  _const_spec(onw.shape),
                  pl.BlockSpec((bb, GDN_HEADS, GDN_DK, GDN_DV), lambda i: (i, 0, 0, 0))],
        out_specs=(pl.BlockSpec((bb, GDN_V), lambda i: (i, 0)),
                   pl.BlockSpec((bb, GDN_HEADS, GDN_DK, GDN_DV), lambda i: (i, 0, 0, 0))),
        out_shape=(jax.ShapeDtypeStruct((n_, GDN_V), BF16),
                   jax.ShapeDtypeStruct(s0.shape, F32)),
        scratch_shapes=[vec, vec, vec, vec, vec, vec],
        compiler_params=_params(("arbitrary",)),
        name="gdn_step",
    )(qkv4, st4, cw, ba, alog_row, dtb_row, gg, onw, s0)


def _swa_prompt_body(nq, sinks_ref, q_ref, kvp_ref, kvc_ref, o_ref):
    n = pl.program_id(1)
    w = WINDOW
    tiles = SWA_KV // LANES
    pairs = 2
    lo_lane = lax.broadcasted_iota(jnp.int32, (w, LANES), 1) < SWA_HD
    lo_row = lax.broadcasted_iota(jnp.int32, (LANES, w), 0) < SWA_HD
    c = lax.broadcasted_iota(jnp.int32, (2 * w, pairs * w), 0)
    i = lax.broadcasted_iota(jnp.int32, (2 * w, pairs * w), 1) & (w - 1)
    banded = (c > i) & (c <= i + w)
    banded_first = banded & ((c >= w) | (n > 0))
    k_blk, vt_blk = [], []
    for j in range(nq + 1):
        src, rows = (kvp_ref, slice(0, w)) if j == 0 else (kvc_ref, slice((j - 1) * w, j * w))
        k_tiles, vt_tiles = [], []
        for t in range(tiles):
            kx = src[rows, t * LANES:(t + 1) * LANES]
            vt = src[rows, SWA_KV + t * LANES:SWA_KV + (t + 1) * LANES].T
            k_tiles.append((kx.astype(BF16), pltpu.roll(kx, SWA_HD, axis=1).astype(BF16)))
            vt_tiles.append((vt.astype(BF16),
                             jnp.concatenate([vt[SWA_HD:], vt[:SWA_HD]], axis=0).astype(BF16)))
        k_blk.append(k_tiles)
        vt_blk.append(vt_tiles)
    items = [(qb, g, p) for qb in range(nq) for g in range(SWA_KV_HEADS) for p in range(2)]
    scale = SWA_HD ** -0.5
    qm, kz, vzt, sink, valid = {}, {}, {}, {}, {}
    for qb, g, p in items:
        keep = lo_lane if p == 0 else jnp.logical_not(lo_lane)
        q_tiles = [q_ref[qb * w:(qb + 1) * w, (2 * g + r) * LANES:(2 * g + r + 1) * LANES] for r in range(pairs)]
        qm[qb, g, p] = jnp.concatenate([jnp.where(keep, x * scale, 0.0) for x in q_tiles], axis=0).astype(BF16)
        variant = 0 if p == g % 2 else 1
        kz[qb, g, p] = jnp.concatenate([k_blk[qb + d][g // 2][variant] for d in range(2)], axis=0)
        vzt[qb, g, p] = jnp.concatenate([vt_blk[qb + d][g // 2][variant] for d in range(2)], axis=1)
        sink[qb, g, p] = jnp.concatenate([jnp.full((1, w), sinks_ref[SWA_GROUP * g + 2 * r + p], F32)
                                          for r in range(pairs)], axis=1)
        valid[qb, g, p] = banded_first if qb == 0 else banded
    st = {b: jnp.where(valid[b], lax.dot_general(kz[b], qm[b], (((1,), (1,)), ((), ())),
                                                 preferred_element_type=F32), -jnp.inf) for b in items}
    m = {b: jnp.maximum(jnp.max(st[b], axis=0, keepdims=True), sink[b]) for b in items}
    et = {b: jnp.exp(st[b] - m[b]) for b in items}
    den = {b: jnp.sum(et[b], axis=0, keepdims=True) + jnp.exp(sink[b] - m[b]) for b in items}
    ot = {b: jnp.dot(vzt[b], et[b].astype(BF16), preferred_element_type=F32) / den[b] for b in items}
    for qb in range(nq):
        for g in range(SWA_KV_HEADS):
            for r in range(pairs):
                cols = slice(r * w, (r + 1) * w)
                tile_t = jnp.where(lo_row, ot[qb, g, 0][:, cols], ot[qb, g, 1][:, cols])
                o_ref[qb * w:(qb + 1) * w, (2 * g + r) * LANES:(2 * g + r + 1) * LANES] = (
                    tile_t.T.astype(o_ref.dtype))


def _swa_prompt(sq, skv, sinks, nq):
    b_, l_, _ = sq.shape
    rows = nq * WINDOW
    return pl.pallas_call(
        functools.partial(_swa_prompt_body, nq),
        grid=(b_, l_ // rows),
        in_specs=[pl.BlockSpec(memory_space=pltpu.SMEM),
                  pl.BlockSpec((None, rows, SWA_Q), lambda b, n: (b, n, 0)),
                  pl.BlockSpec((None, WINDOW, 2 * SWA_KV), lambda b, n: (b, jnp.maximum(n * nq - 1, 0), 0)),
                  pl.BlockSpec((None, rows, 2 * SWA_KV), lambda b, n: (b, n, 0))],
        out_specs=pl.BlockSpec((None, rows, SWA_Q), lambda b, n: (b, n, 0)),
        out_shape=jax.ShapeDtypeStruct((b_, l_, SWA_Q), BF16),
        compiler_params=_params(("arbitrary", "arbitrary")),
        name="swa_prompt",
    )(sinks, sq, skv, skv)


def _swa_step_body(bb, q_ref, kvn_ref, ck_ref, cv_ref, sink_ref, o_ref, nk_ref, nv_ref):
    w = WINDOW
    row = lax.broadcasted_iota(jnp.int32, (SWA_Q_HEADS, SWA_KV), 0)
    lane = lax.broadcasted_iota(jnp.int32, (SWA_Q_HEADS, SWA_KV), 1)
    own = (lane // SWA_HD) == (row // SWA_GROUP)
    key = lax.broadcasted_iota(jnp.int32, (SWA_Q_HEADS, w), 1)
    sink = sink_ref[...]
    scale = SWA_HD ** -0.5
    seqs = range(bb)
    q_bd = [jnp.where(own, jnp.concatenate([q_ref[i]] * SWA_KV_HEADS, axis=1), 0.0) for i in seqs]
    kn = [kvn_ref[i:i + 1, 0:SWA_KV] for i in seqs]
    vn = [kvn_ref[i:i + 1, SWA_KV:2 * SWA_KV] for i in seqs]
    s_c = [jnp.where(key >= 1, _mm_nt(q_bd[i], ck_ref[i]) * scale, -jnp.inf) for i in seqs]
    s_n = [jnp.sum(q_bd[i] * kn[i], axis=1, keepdims=True) * scale for i in seqs]
    m = [jnp.maximum(jnp.maximum(jnp.max(s_c[i], axis=1, keepdims=True), s_n[i]), sink) for i in seqs]
    e_c = [jnp.exp(s_c[i] - m[i]) for i in seqs]
    e_n = [jnp.exp(s_n[i] - m[i]) for i in seqs]
    den = [jnp.sum(e_c[i], axis=1, keepdims=True) + e_n[i] + jnp.exp(sink - m[i]) for i in seqs]
    pv = [jnp.where(own, _mm(e_c[i] / den[i], cv_ref[i]) + (e_n[i] / den[i]) * vn[i], 0.0) for i in seqs]
    for i in seqs:
        o = pv[i][:, 0:SWA_HD]
        for g in range(1, SWA_KV_HEADS):
            o = o + pv[i][:, g * SWA_HD:(g + 1) * SWA_HD]
        o_ref[i] = o
        nk_ref[i, 0:w - 1, :] = ck_ref[i, 1:w, :]
        nk_ref[i, w - 1:w, :] = kn[i]
        nv_ref[i, 0:w - 1, :] = cv_ref[i, 1:w, :]
        nv_ref[i, w - 1:w, :] = vn[i]


def _swa_step(q3, kvn, ck, cv, sink_col, bb):
    n_ = q3.shape[0]
    cache = pl.BlockSpec((bb, WINDOW, SWA_KV), lambda i: (i, 0, 0))
    return pl.pallas_call(
        functools.partial(_swa_step_body, bb),
        grid=(n_ // bb,),
        in_specs=[pl.BlockSpec((bb, SWA_Q_HEADS, SWA_HD), lambda i: (i, 0, 0)),
                  pl.BlockSpec((bb, 2 * SWA_KV), lambda i: (i, 0)),
                  cache, cache,
                  _const_spec(sink_col.shape)],
        out_specs=(pl.BlockSpec((bb, SWA_Q_HEADS, SWA_HD), lambda i: (i, 0, 0)), cache, cache),
        out_shape=(jax.ShapeDtypeStruct(q3.shape, F32),
                   jax.ShapeDtypeStruct(ck.shape, F32),
                   jax.ShapeDtypeStruct(cv.shape, F32)),
        compiler_params=_params(("arbitrary",)),
        name="swa_step",
    )(q3, kvn, ck, cv, sink_col)


def _dense_body(stateful, tm, og_ref, ob_ref, gab_ref, x_ref, gt1_ref, sh2_ref, sc2_ref, gt2_ref,
                n2w_ref, fnw_ref, wa_ref, wb_ref, wo_ref, wg_ref, wu_ref, cw_ref, cb_ref, wd_ref, *rest):
    if stateful:
        st_ref, y_ref, gout_ref, act_ref = rest
    else:
        y_ref, gout_ref, act_ref, gbuf_ref, carry_ref = rest

        @pl.when(pl.program_id(1) == 0)
        def _():
            carry_ref[...] = jnp.zeros_like(carry_ref)

    y_a = jnp.dot(og_ref[...], wa_ref[...], preferred_element_type=F32)
    y_b = jnp.dot(ob_ref[...], wb_ref[...], preferred_element_type=F32)
    merged = (jax.nn.sigmoid(gab_ref[:, 0:D_MODEL]) * y_a
              + jax.nn.sigmoid(gab_ref[:, D_MODEL:2 * D_MODEL]) * y_b)
    x1 = x_ref[...] + gt1_ref[...] * _mm(merged, wo_ref[...])
    h2 = (_rms(x1, n2w_ref[...]) * (1.0 + sc2_ref[...]) + sh2_ref[...]).astype(BF16)

    for c in range(D_FF // FFN_COLS):
        cols = slice(c * FFN_COLS, (c + 1) * FFN_COLS)
        gate = jnp.dot(h2, wg_ref[:, cols], preferred_element_type=F32)
        up = jnp.dot(h2, wu_ref[:, cols], preferred_element_type=F32)
        if stateful:
            g2 = st_ref[0, :, cols]
            g1 = st_ref[1, :, cols]
            gout_ref[:, cols] = gate
        else:
            gbuf_ref[0:SUBLANES, :] = carry_ref[:, cols]
            gbuf_ref[SUBLANES:SUBLANES + tm, :] = gate
            g2 = gbuf_ref[SUBLANES - 2:SUBLANES - 2 + tm, :]
            g1 = gbuf_ref[SUBLANES - 1:SUBLANES - 1 + tm, :]
            carry_ref[:, cols] = gbuf_ref[tm:tm + SUBLANES, :]
        gc = (cw_ref[0:1, cols] * g2 + cw_ref[1:2, cols] * g1 + cw_ref[2:3, cols] * gate) + cb_ref[:, cols]
        act_ref[:, cols] = (_silu(gc) * up).astype(BF16)
    if not stateful:
        gout_ref[...] = carry_ref[...]

    x2 = x1 + gt2_ref[...] * jnp.dot(act_ref[...], wd_ref[...], preferred_element_type=F32)
    y_ref[...] = _rms(x2, fnw_ref[...])


def _dense(og, ob, gab, x, mods, vecs, ws, st, tm):
    b_, l_, _ = x.shape
    r_ = mods[0].shape[1]
    rt = 1 if r_ == 1 else tm
    mod_map = (lambda b, t: (b, 0, 0)) if r_ == 1 else (lambda b, t: (b, t, 0))
    row_map = lambda b, t: (b, t, 0)
    stateful = st is not None
    in_specs = ([pl.BlockSpec((None, tm, D_MODEL), row_map),
                 pl.BlockSpec((None, tm, D_MODEL), row_map),
                 pl.BlockSpec((None, tm, 2 * D_MODEL), row_map),
                 pl.BlockSpec((None, tm, D_MODEL), row_map)]
                + [pl.BlockSpec((None, rt, D_MODEL), mod_map)] * 4
                + [_const_spec(a.shape) for a in vecs[:2]]
                + [_const_spec(ws[0].shape), _const_spec(ws[1].shape), _const_spec(ws[2].shape),
                   _const_spec(ws[3].shape), _const_spec(ws[4].shape),
                   _const_spec(vecs[2].shape), _const_spec(vecs[3].shape), _const_spec(ws[5].shape)])
    args = [og, ob, gab, x, *mods, vecs[0], vecs[1], ws[0], ws[1], ws[2], ws[3], ws[4], vecs[2], vecs[3], ws[5]]
    scratch = [pltpu.VMEM((tm, D_FF), BF16)]
    if stateful:
        in_specs.append(pl.BlockSpec((FFN_CONV - 1, None, tm, D_FF), lambda b, t: (0, b, t, 0)))
        args.append(st)
        gout_shape = jax.ShapeDtypeStruct((b_, l_, D_FF), F32)
        gout_spec = pl.BlockSpec((None, tm, D_FF), row_map)
    else:
        scratch += [pltpu.VMEM((tm + SUBLANES, FFN_COLS), F32), pltpu.VMEM((SUBLANES, D_FF), F32)]
        gout_shape = jax.ShapeDtypeStruct((b_, SUBLANES, D_FF), F32)
        gout_spec = pl.BlockSpec((None, SUBLANES, D_FF), lambda b, t: (b, 0, 0))
    return pl.pallas_call(
        functools.partial(_dense_body, stateful, tm),
        grid=(b_, l_ // tm),
        in_specs=in_specs,
        out_specs=(pl.BlockSpec((None, tm, D_MODEL), row_map), gout_spec),
        out_shape=(jax.ShapeDtypeStruct((b_, l_, D_MODEL), F32), gout_shape),
        scratch_shapes=scratch,
        compiler_params=_params(("arbitrary", "arbitrary")),
        name="dense_step" if stateful else "dense_prompt",
    )(*args)


def _lane_row(values, offset):
    return jnp.zeros((1, LANES), F32).at[0, offset:offset + values.shape[0]].set(values)


def kernel(x_prompt, x_sample, c_prompt, c_sample, state_gdn_S, state_gdn_conv, cache_swa_k, cache_swa_v,
           state_ffn_conv, w_mod, b_mod, norm1_w, norm2_w, w_in, gdn_conv_w, gdn_a_log, gdn_dt_bias,
           gdn_onorm_w, w_gdn_out, swa_sinks, w_swa_out, w_o, w_ffn_gate, w_ffn_up, ffn_conv_w, ffn_conv_b,
           w_ffn_down, final_norm_w):
    assert w_mod.shape[0] == 1, "single-layer trunk"
    nb, seq, _ = x_prompt.shape
    ns = x_sample.shape[0]
    assert x_sample.shape[1] == 1

    n_ba = 2 * GDN_HEADS
    split = GDN_CONV_CH + GDN_V
    in_ws = jnp.concatenate([w_in[0][:, :split], w_in[0][:, split + n_ba:], w_in[0][:, split:split + n_ba],
                             jnp.zeros((D_MODEL, LANES - n_ba), F32)], axis=1).astype(BF16)
    assert in_ws.shape == (D_MODEL, IN_COLS)
    dense_ws = (w_gdn_out[0].astype(BF16), w_swa_out[0].astype(BF16), w_o[0].astype(BF16),
                w_ffn_gate[0].astype(BF16), w_ffn_up[0].astype(BF16), w_ffn_down[0].astype(BF16))
    dense_vecs = (norm2_w, final_norm_w[None, :], ffn_conv_w[0], ffn_conv_b)
    cw = jnp.transpose(gdn_conv_w[0].reshape(GDN_CONV, GDN_SECTIONS, LANES), (1, 0, 2))
    alog_row = _lane_row(gdn_a_log[0], GDN_HEADS)
    dtb_row = _lane_row(gdn_dt_bias[0], GDN_HEADS)

    mod = _modulation(jnp.concatenate([c_prompt, c_sample], axis=0), w_mod[0].astype(BF16), b_mod)
    mod_p = [mod[:nb, i * D_MODEL:(i + 1) * D_MODEL][:, None, :] for i in range(6)]
    mod_s = [mod[nb:, i * D_MODEL:(i + 1) * D_MODEL][None, :, :] for i in range(6)]

    qkvf, gact, ba, sq, skv, gab, qkv_tail = _inproj(x_prompt, mod_p[0], mod_p[1], norm1_w, in_ws, cw, tm=256)
    og, gdn_s_p = _gdn_prompt(qkvf, gact, ba, alog_row, dtb_row, gdn_onorm_w, _level_masks(), lt=2 * CHUNK)
    ob = _swa_prompt(sq, skv, swa_sinks[0], nq=2)
    y_p, gate_tail = _dense(og, ob, gab, x_prompt, (mod_p[2], mod_p[3], mod_p[4], mod_p[5]),
                            dense_vecs, dense_ws, None, tm=256)
    gdn_conv_p = jnp.transpose(qkv_tail[:, :, SUBLANES - (GDN_CONV - 1):, :], (0, 2, 1, 3)).reshape(
        nb, GDN_CONV - 1, GDN_CONV_CH)
    k_p = skv[:, seq - WINDOW:, :SWA_KV].reshape(nb, WINDOW, SWA_KV_HEADS, SWA_HD)
    v_p = skv[:, seq - WINDOW:, SWA_KV:].reshape(nb, WINDOW, SWA_KV_HEADS, SWA_HD)
    ffn_conv_p = gate_tail[:, SUBLANES - (FFN_CONV - 1):, :]

    xs = x_sample.reshape(1, ns, D_MODEL)
    qkv4s, ggs, bas, sqs, skvs, gabs = _inproj(xs, mod_s[0], mod_s[1], norm1_w, in_ws, None, tm=ns)
    st4 = jnp.transpose(state_gdn_conv[0].reshape(ns, GDN_CONV - 1, GDN_SECTIONS, LANES), (1, 2, 0, 3))
    og_s, gdn_s_s = _gdn_step(qkv4s[0], st4, ggs[0], bas[0], cw, alog_row, dtb_row, gdn_onorm_w,
                              state_gdn_S[0], bb=8)
    o3, k_s, v_s = _swa_step(sqs[0].reshape(ns, SWA_Q_HEADS, SWA_HD), skvs[0],
                             cache_swa_k[0].reshape(ns, WINDOW, SWA_KV),
                             cache_swa_v[0].reshape(ns, WINDOW, SWA_KV),
                             swa_sinks[0][:, None], bb=8)
    ob_s = o3.reshape(1, ns, SWA_Q).astype(BF16)
    st_ffn = jnp.transpose(state_ffn_conv[0], (1, 0, 2))[:, None]
    y_s, gate_new = _dense(og_s[None], ob_s, gabs, xs, (mod_s[2], mod_s[3], mod_s[4], mod_s[5]),
                           dense_vecs, dense_ws, st_ffn, tm=ns)
    qkv_new = jnp.transpose(qkv4s[0], (1, 0, 2)).reshape(ns, 1, GDN_CONV_CH)
    gdn_conv_s = jnp.concatenate([state_gdn_conv[0][:, 1:], qkv_new], axis=1)
    ffn_conv_s = jnp.concatenate([state_ffn_conv[0][:, 1:], gate_new[0][:, None, :]], axis=1)

    return (y_p, y_s.reshape(ns, 1, D_MODEL),
            gdn_s_p[None], gdn_s_s[None],
            gdn_conv_p[None], gdn_conv_s[None],
            k_p[None], k_s.reshape(ns, WINDOW, SWA_KV_HEADS, SWA_HD)[None],
            v_p[None], v_s.reshape(ns, WINDOW, SWA_KV_HEADS, SWA_HD)[None],
            ffn_conv_p[None], ffn_conv_s[None])
```

```python
import functools

import numpy as np
import jax
import jax.numpy as jnp
from jax import lax
from jax.experimental import pallas as pl
from jax.experimental.pallas import tpu as pltpu

F32 = jnp.float32
BF16 = jnp.bfloat16

D_MODEL = 1024
GDN_HEADS = 8
GDN_DK = 128
GDN_DV = 128
GDN_QK = GDN_HEADS * GDN_DK
GDN_V = GDN_HEADS * GDN_DV
GDN_CONV = 4
GDN_CONV_CH = 2 * GDN_QK + GDN_V
GDN_SECTIONS = GDN_CONV_CH // 128
SWA_Q_HEADS = 16
SWA_KV_HEADS = 4
SWA_GROUP = SWA_Q_HEADS // SWA_KV_HEADS
SWA_HD = 64
SWA_Q = SWA_Q_HEADS * SWA_HD
SWA_KV = SWA_KV_HEADS * SWA_HD
WINDOW = 128
D_FF = 2816
FFN_CONV = 3
EPS = 1e-6

LANES = 128
SUBLANES = 8
VMEM_LIMIT = 56 * 1024 * 1024

COL_QKV = 0
COL_GATE = COL_QKV + GDN_CONV_CH
COL_SQ = COL_GATE + GDN_V
COL_SKV = COL_SQ + SWA_Q
COL_GAB = COL_SKV + 2 * SWA_KV
COL_BA = COL_GAB + 2 * D_MODEL
IN_COLS = COL_BA + LANES

CHUNK = 128
FFN_COLS = 256


def _mm(a, b):
    return jnp.dot(a.astype(BF16), b.astype(BF16), preferred_element_type=F32)


def _mm_nt(a, b):
    return lax.dot_general(a.astype(BF16), b.astype(BF16), (((1,), (1,)), ((), ())),
                           preferred_element_type=F32)


def _silu(x):
    return x * jax.nn.sigmoid(x)


def _softplus(x):
    return jnp.maximum(x, 0.0) + jnp.log1p(jnp.exp(-jnp.abs(x)))


def _rms(x, w):
    return x * lax.rsqrt(jnp.mean(x * x, axis=-1, keepdims=True) + EPS) * w


def _const_spec(shape):
    n = len(shape)
    return pl.BlockSpec(shape, lambda *_: (0,) * n, pipeline_mode=pl.Buffered(1))


def _params(sem):
    return pltpu.CompilerParams(dimension_semantics=sem, vmem_limit_bytes=VMEM_LIMIT)


def _mod_body(c_ref, w_ref, b_ref, o_ref):
    o_ref[...] = _mm(_silu(c_ref[...]), w_ref[...]) + b_ref[...]


def _modulation(c_all, w_mod, b_mod):
    rows = c_all.shape[0]
    n_out = w_mod.shape[1]
    tn = D_MODEL
    return pl.pallas_call(
        _mod_body,
        grid=(n_out // tn,),
        in_specs=[pl.BlockSpec((rows, D_MODEL), lambda j: (0, 0)),
                  pl.BlockSpec((D_MODEL, tn), lambda j: (0, j)),
                  pl.BlockSpec((1, tn), lambda j: (0, j))],
        out_specs=pl.BlockSpec((rows, tn), lambda j: (0, j)),
        out_shape=jax.ShapeDtypeStruct((rows, n_out), F32),
        compiler_params=_params(("arbitrary",)),
        name="modulation",
    )(c_all, w_mod, b_mod)


def _in_weight_body(w_ref, o_ref):
    split = GDN_CONV_CH + GDN_V
    n_ba = 2 * GDN_HEADS
    rest = D_MODEL * 2 + SWA_Q + 2 * SWA_KV
    o_ref[:, 0:split] = w_ref[:, 0:split].astype(BF16)
    o_ref[:, split:split + rest] = w_ref[:, split + n_ba:split + n_ba + rest].astype(BF16)
    ba = jnp.concatenate([w_ref[:, split:split + n_ba], jnp.zeros((w_ref.shape[0], LANES - n_ba), F32)], axis=1)
    o_ref[:, COL_BA:COL_BA + LANES] = ba.astype(BF16)


def _in_weight(w_in, tr):
    rows, cols = w_in.shape
    return pl.pallas_call(
        _in_weight_body,
        grid=(rows // tr,),
        in_specs=[pl.BlockSpec((tr, cols), lambda i: (i, 0))],
        out_specs=pl.BlockSpec((tr, IN_COLS), lambda i: (i, 0)),
        out_shape=jax.ShapeDtypeStruct((rows, IN_COLS), BF16),
        compiler_params=_params(("arbitrary",)),
        name="in_weight",
    )(w_in)


def _l2norm(x):
    return x * lax.rsqrt(jnp.sum(x * x, axis=-1, keepdims=True) + EPS)


def _inproj_body(seq_rows, tm, x_ref, sh_ref, sc_ref, nw_ref, w_ref, *rest):
    if seq_rows:
        cw_ref, qkv_ref, gg_ref, ba_ref, sq_ref, skv_ref, gab_ref, tail_ref, xe_ref = rest

        @pl.when(pl.program_id(1) == 0)
        def _():
            xe_ref[:, 0:SUBLANES, :] = jnp.zeros((GDN_SECTIONS, SUBLANES, LANES), F32)
    else:
        qkv_ref, gg_ref, ba_ref, sq_ref, skv_ref, gab_ref = rest

    h = _rms(x_ref[...], nw_ref[...]) * (1.0 + sc_ref[...]) + sh_ref[...]
    hb = h.astype(BF16)

    def proj(lo, width):
        return jnp.dot(hb, w_ref[:, lo:lo + width], preferred_element_type=F32)

    step = 512
    per = step // LANES
    for c in range(GDN_CONV_CH // step):
        z = proj(COL_QKV + c * step, step)
        for k in range(per):
            s = c * per + k
            zs = z[:, k * LANES:(k + 1) * LANES]
            if not seq_rows:
                qkv_ref[s] = zs
                continue
            xe_ref[s, SUBLANES:SUBLANES + tm, :] = zs
            w = cw_ref[s]
            y = w[0:1] * xe_ref[s, SUBLANES - 3:SUBLANES - 3 + tm, :]
            for tap in range(1, GDN_CONV):
                lo = SUBLANES - 3 + tap
                y = y + w[tap:tap + 1] * xe_ref[s, lo:lo + tm, :]
            xe_ref[s, 0:SUBLANES, :] = xe_ref[s, tm:tm + SUBLANES, :]
            f = _silu(y)
            if s < GDN_HEADS:
                f = _l2norm(f) * (GDN_DK ** -0.5)
            elif s < 2 * GDN_HEADS:
                f = _l2norm(f)
            qkv_ref[s] = f
    if seq_rows:
        tail_ref[...] = xe_ref[:, 0:SUBLANES, :]
    for c in range(GDN_V // step):
        z = proj(COL_GATE + c * step, step)
        for k in range(per):
            zs = z[:, k * LANES:(k + 1) * LANES]
            gg_ref[c * per + k] = _silu(zs) if seq_rows else zs
    ba_ref[...] = proj(COL_BA, LANES)
    for c in range(SWA_Q // step):
        sq_ref[:, c * step:(c + 1) * step] = proj(COL_SQ + c * step, step)
    skv_ref[...] = proj(COL_SKV, 2 * SWA_KV)
    for c in range(2 * D_MODEL // step):
        gab_ref[:, c * step:(c + 1) * step] = proj(COL_GAB + c * step, step)


def _inproj(x, sh, sc, nw, w_all, cw, tm):
    b_, l_, _ = x.shape
    r_ = sh.shape[1]
    rt = 1 if r_ == 1 else tm
    mod_map = (lambda b, t: (b, 0, 0)) if r_ == 1 else (lambda b, t: (b, t, 0))
    row_map = lambda b, t: (b, t, 0)
    head_map = lambda b, t: (b, 0, t, 0)
    seq_rows = cw is not None
    out_shape = (
        jax.ShapeDtypeStruct((b_, GDN_SECTIONS, l_, LANES), F32),
        jax.ShapeDtypeStruct((b_, GDN_HEADS, l_, LANES), F32),
        jax.ShapeDtypeStruct((b_, l_, LANES), F32),
        jax.ShapeDtypeStruct((b_, l_, SWA_Q), F32),
        jax.ShapeDtypeStruct((b_, l_, 2 * SWA_KV), F32),
        jax.ShapeDtypeStruct((b_, l_, 2 * D_MODEL), F32),
    )
    out_specs = (
        pl.BlockSpec((None, GDN_SECTIONS, tm, LANES), head_map),
        pl.BlockSpec((None, GDN_HEADS, tm, LANES), head_map),
        pl.BlockSpec((None, tm, LANES), row_map),
        pl.BlockSpec((None, tm, SWA_Q), row_map),
        pl.BlockSpec((None, tm, 2 * SWA_KV), row_map),
        pl.BlockSpec((None, tm, 2 * D_MODEL), row_map),
    )
    in_specs = [
        pl.BlockSpec((None, tm, D_MODEL), row_map),
        pl.BlockSpec((None, rt, D_MODEL), mod_map),
        pl.BlockSpec((None, rt, D_MODEL), mod_map),
        _const_spec(nw.shape),
        _const_spec(w_all.shape),
    ]
    args = [x, sh, sc, nw, w_all]
    scratch = []
    if seq_rows:
        in_specs.append(_const_spec(cw.shape))
        args.append(cw)
        out_shape += (jax.ShapeDtypeStruct((b_, GDN_SECTIONS, SUBLANES, LANES), F32),)
        out_specs += (pl.BlockSpec((None, GDN_SECTIONS, SUBLANES, LANES), lambda b, t: (b, 0, 0, 0)),)
        scratch.append(pltpu.VMEM((GDN_SECTIONS, tm + SUBLANES, LANES), F32))
    return pl.pallas_call(
        functools.partial(_inproj_body, seq_rows, tm),
        grid=(b_, l_ // tm),
        in_specs=in_specs,
        out_specs=out_specs,
        out_shape=out_shape,
        scratch_shapes=scratch,
        compiler_params=_params(("arbitrary", "arbitrary")),
        name="inproj_seq" if seq_rows else "inproj_rows",
    )(*args)


def _delta_gates(ba, alog_row, dtb_row):
    beta_all = jax.nn.sigmoid(ba)
    g_all = -jnp.exp(alog_row) * _softplus(ba + dtb_row)
    return beta_all, g_all


def _lane_column(x, lane_idx, lane):
    return jnp.sum(jnp.where(lane_idx == lane, x, 0.0), axis=1, keepdims=True)


def _level_masks():
    r = np.arange(CHUNK)[:, None]
    c = np.arange(CHUNK)[None, :]
    masks = [(r == c + 1) & (r % 2 == 1)]
    half = 2
    while half < CHUNK:
        full = 2 * half
        masks.append((r // full == c // full) & (r % full >= half) & (c % full < half))
        half = full
    return jnp.asarray(np.stack(masks), dtype=BF16)


def _unit_lower_inverses(ms, masks_ref, eye):
    ts = [eye - m * masks_ref[0] for m in ms]
    for lvl in range(1, masks_ref.shape[0]):
        off = masks_ref[lvl]
        xs = [jnp.dot(m * off, t, preferred_element_type=F32).astype(BF16) for m, t in zip(ms, ts)]
        ys = [jnp.dot(t, x, preferred_element_type=F32).astype(BF16) for t, x in zip(ts, xs)]
        ts = [t - y for t, y in zip(ts, ys)]
    return ts


def _cumsum_rows(g, ltri):
    hi = g.astype(BF16)
    r1 = g - hi.astype(F32)
    mid = r1.astype(BF16)
    lo = (r1 - mid.astype(F32)).astype(BF16)
    return (jnp.dot(ltri, hi, preferred_element_type=F32) + jnp.dot(ltri, mid, preferred_element_type=F32)
            + jnp.dot(ltri, lo, preferred_element_type=F32))


def _gated_out_norm(o, gate_act, onw):
    on = o * lax.rsqrt(jnp.mean(o * o, axis=-1, keepdims=True) + EPS) * onw
    return on * gate_act


def _gdn_prompt_body(lt, q_ref, k_ref, v_ref, ba_ref, alog_ref, dtb_ref, gate_ref, onw_ref, masks_ref,
                     og_ref, s_ref):
    @pl.when(pl.program_id(1) == 0)
    def _():
        s_ref[...] = jnp.zeros_like(s_ref)

    beta_all, g_all = _delta_gates(ba_ref[...], alog_ref[...], dtb_ref[...])
    lane_idx = lax.broadcasted_iota(jnp.int32, (CHUNK, LANES), 1)
    row = lax.broadcasted_iota(jnp.int32, (CHUNK, CHUNK), 0)
    col = lax.broadcasted_iota(jnp.int32, (CHUNK, CHUNK), 1)
    tril = row >= col
    strict = row > col
    ltri = jnp.where(tril, 1.0, 0.0).astype(BF16)
    eye = jnp.where(row == col, 1.0, 0.0).astype(BF16)
    onw = onw_ref[...]
    heads = range(GDN_HEADS)
    chunks = range(lt // CHUNK)

    blocks = [(c, j) for c in chunks for j in heads]
    pre = {}
    for c in chunks:
        rows = slice(c * CHUNK, (c + 1) * CHUNK)
        dec = _cumsum_rows(g_all[rows], ltri)
        dec_t = dec.T
        for j in heads:
            q, k, v = q_ref[j, rows, :], k_ref[j, rows, :], v_ref[j, rows, :]
            beta_col = _lane_column(beta_all[rows], lane_idx, j)
            dec_col = _lane_column(dec, lane_idx, GDN_HEADS + j)
            dec_row = dec_t[GDN_HEADS + j:GDN_HEADS + j + 1, :]
            dec_last = dec_row[:, CHUNK - 1:CHUNK]
            gam = jnp.exp(jnp.minimum(dec_col - dec_row, 0.0))
            e_col = jnp.exp(dec_col)
            kb = k * beta_col
            pre[c, j] = dict(q=q, k=k, gam=gam, kb=kb, qe=q * e_col, e_last=jnp.exp(dec_last),
                             kd=k * jnp.exp(dec_last - dec_col),
                             rhs=jnp.concatenate([v * beta_col, kb * e_col], axis=1).astype(BF16))
    grams = [_mm_nt(jnp.concatenate([pre[b]["kb"], pre[b]["q"]], axis=0), pre[b]["k"]) for b in blocks]
    ms = [jnp.where(strict, g[:CHUNK] * pre[b]["gam"], 0.0).astype(BF16) for g, b in zip(grams, blocks)]
    a_intra = {b: jnp.where(tril, g[CHUNK:] * pre[b]["gam"], 0.0) for g, b in zip(grams, blocks)}
    t_inv = _unit_lower_inverses(ms, masks_ref, eye)
    uw = {b: jnp.dot(t, pre[b]["rhs"], preferred_element_type=F32) for t, b in zip(t_inv, blocks)}

    for c in chunks:
        rows = slice(c * CHUNK, (c + 1) * CHUNK)
        s_prev = [s_ref[j] for j in heads]
        ws_qs = [_mm(jnp.concatenate([uw[c, j][:, GDN_DV:], pre[c, j]["qe"]], axis=0), s_prev[j]) for j in heads]
        v_new = [uw[c, j][:, :GDN_DV] - ws_qs[j][:CHUNK] for j in heads]
        o = [ws_qs[j][CHUNK:] + _mm(a_intra[c, j], v_new[j]) for j in heads]
        s_new = [s_prev[j] * pre[c, j]["e_last"] + _mm(pre[c, j]["kd"].T, v_new[j]) for j in heads]
        for j in heads:
            s_ref[j] = s_new[j]
            og = _gated_out_norm(o[j], gate_ref[j, rows, :], onw)
            og_ref[rows, j * GDN_DV:(j + 1) * GDN_DV] = og.astype(og_ref.dtype)


def _gdn_prompt(qkvf, gact, ba, alog_row, dtb_row, onw, masks, lt):
    b_, _, l_, _ = qkvf.shape
    sec = lambda s: pl.BlockSpec((None, GDN_HEADS, lt, LANES), lambda b, t, s=s: (b, s, t, 0))
    return pl.pallas_call(
        functools.partial(_gdn_prompt_body, lt),
        grid=(b_, l_ // lt),
        in_specs=[sec(0), sec(1), sec(2),
                  pl.BlockSpec((None, lt, LANES), lambda b, t: (b, t, 0)),
                  _const_spec(alog_row.shape), _const_spec(dtb_row.shape),
                  pl.BlockSpec((None, GDN_HEADS, lt, LANES), lambda b, t: (b, 0, t, 0)),
                  _const_spec(onw.shape), _const_spec(masks.shape)],
        out_specs=(pl.BlockSpec((None, lt, GDN_V), lambda b, t: (b, t, 0)),
                   pl.BlockSpec((None, GDN_HEADS, GDN_DK, GDN_DV), lambda b, t: (b, 0, 0, 0))),
        out_shape=(jax.ShapeDtypeStruct((b_, l_, GDN_V), BF16),
                   jax.ShapeDtypeStruct((b_, GDN_HEADS, GDN_DK, GDN_DV), F32)),
        compiler_params=_params(("arbitrary", "arbitrary")),
        name="gdn_prompt",
    )(qkvf, qkvf, qkvf, ba, alog_row, dtb_row, gact, onw, masks)


def _gdn_step_body(bb, x_ref, st_ref, cw_ref, ba_ref, alog_ref, dtb_ref, gate_ref, onw_ref, s0_ref,
                   og_ref, sn_ref, q_s, k_s, v_s, b_s, e_s, o_s):
    beta_all, g_all = _delta_gates(ba_ref[...], alog_ref[...], dtb_ref[...])
    lane_idx = lax.broadcasted_iota(jnp.int32, (bb, LANES), 1)
    for h in range(GDN_HEADS):
        feats = []
        for s in range(3):
            idx = s * GDN_HEADS + h
            w = cw_ref[idx]
            y = w[0:1] * st_ref[0, idx]
            for tap in range(1, GDN_CONV - 1):
                y = y + w[tap:tap + 1] * st_ref[tap, idx]
            y = y + w[GDN_CONV - 1:GDN_CONV] * x_ref[idx]
            feats.append(_silu(y))
        q, k, v = feats
        q_s[h] = q * lax.rsqrt(jnp.sum(q * q, axis=-1, keepdims=True) + EPS) * (GDN_DK ** -0.5)
        k_s[h] = k * lax.rsqrt(jnp.sum(k * k, axis=-1, keepdims=True) + EPS)
        v_s[h] = v
        b_s[h] = jnp.broadcast_to(_lane_column(beta_all, lane_idx, h), (bb, LANES))
        e_s[h] = jnp.broadcast_to(jnp.exp(_lane_column(g_all, lane_idx, h + GDN_HEADS)), (bb, LANES))

    eye = (lax.broadcasted_iota(jnp.int32, (GDN_DK, GDN_DK), 0)
           == lax.broadcasted_iota(jnp.int32, (GDN_DK, GDN_DK), 1))

    def to_col(r):
        return jnp.sum(jnp.where(eye, jnp.broadcast_to(r, (GDN_DK, GDN_DK)), 0.0), axis=1, keepdims=True)

    def seq_body(i, carry):
        for h in range(GDN_HEADS):
            one = pl.ds(i, 1)
            k_col = to_col(k_s[h, one, :])
            q_col = to_col(q_s[h, one, :])
            s1 = s0_ref[i, h] * e_s[h, one, :]
            ks = jnp.sum(s1 * k_col, axis=0, keepdims=True)
            delta = (v_s[h, one, :] - ks) * b_s[h, one, :]
            s2 = s1 + k_col * delta
            sn_ref[i, h] = s2
            o_s[h, one, :] = jnp.sum(s2 * q_col, axis=0, keepdims=True)
        return carry

    lax.fori_loop(0, bb, seq_body, 0)
    onw = onw_ref[...]
    for h in range(GDN_HEADS):
        og = _gated_out_norm(o_s[h], _silu(gate_ref[h]), onw)
        og_ref[:, h * GDN_DV:(h + 1) * GDN_DV] = og.astype(og_ref.dtype)


def _gdn_step(qkv4, st4, gg, ba, cw, alog_row, dtb_row, onw, s0, bb):
    n_ = ba.shape[0]
    vec = pltpu.VMEM((GDN_HEADS, bb, LANES), F32)
    return pl.pallas_call(
        functools.partial(_gdn_step_body, bb),
        grid=(n_ // bb,),
        in_specs=[pl.BlockSpec((GDN_SECTIONS, bb, LANES), lambda i: (0, i, 0)),
                  pl.BlockSpec((GDN_CONV - 1, GDN_SECTIONS, bb, LANES), lambda i: (0, 0, i, 0)),
                  _const_spec(cw.shape),
                  pl.BlockSpec((bb, LANES), lambda i: (i, 0)),
                  _const_spec(alog_row.shape), _const_spec(dtb_row.shape),
                  pl.BlockSpec((GDN_HEADS, bb, LANES), lambda i: (0, i, 0)),
                  _const_spec(onw.shape),
                  pl.BlockSpec((bb, GDN_HEADS, GDN_DK, GDN_DV), lambda i: (i, 0, 0, 0))],
        out_specs=(pl.BlockSpec((bb, GDN_V), lambda i: (i, 0)),
                   pl.BlockSpec((bb, GDN_HEADS, GDN_DK, GDN_DV), lambda i: (i, 0, 0, 0))),
        out_shape=(jax.ShapeDtypeStruct((n_, GDN_V), BF16),
                   jax.ShapeDtypeStruct(s0.shape, F32)),
        scratch_shapes=[vec, vec, vec, vec, vec, vec],
        compiler_params=_params(("arbitrary",)),
        name="gdn_step",
    )(qkv4, st4, cw, ba, alog_row, dtb_row, gg, onw, s0)


def _swa_prompt_body(nq, sinks_ref, q_ref, kvp_ref, kvc_ref, o_ref):
    n = pl.program_id(1)
    w = WINDOW
    tiles = SWA_KV // LANES
    pairs = 2
    lo_lane = lax.broadcasted_iota(jnp.int32, (w, LANES), 1) < SWA_HD
    lo_row = lax.broadcasted_iota(jnp.int32, (LANES, w), 0) < SWA_HD
    c = lax.broadcasted_iota(jnp.int32, (2 * w, pairs * w), 0)
    i = lax.broadcasted_iota(jnp.int32, (2 * w, pairs * w), 1) & (w - 1)
    banded = (c > i) & (c <= i + w)
    banded_first = banded & ((c >= w) | (n > 0))
    k_blk, vt_blk = [], []
    for j in range(nq + 1):
        src, rows = (kvp_ref, slice(0, w)) if j == 0 else (kvc_ref, slice((j - 1) * w, j * w))
        k_tiles, vt_tiles = [], []
        for t in range(tiles):
            kx = src[rows, t * LANES:(t + 1) * LANES]
            vt = src[rows, SWA_KV + t * LANES:SWA_KV + (t + 1) * LANES].T
            k_tiles.append((kx.astype(BF16), pltpu.roll(kx, SWA_HD, axis=1).astype(BF16)))
            vt_tiles.append((vt.astype(BF16),
                             jnp.concatenate([vt[SWA_HD:], vt[:SWA_HD]], axis=0).astype(BF16)))
        k_blk.append(k_tiles)
        vt_blk.append(vt_tiles)
    items = [(qb, g, p) for qb in range(nq) for g in range(SWA_KV_HEADS) for p in range(2)]
    scale = SWA_HD ** -0.5
    qm, kz, vzt, sink, valid = {}, {}, {}, {}, {}
    for qb, g, p in items:
        keep = lo_lane if p == 0 else jnp.logical_not(lo_lane)
        q_tiles = [q_ref[qb * w:(qb + 1) * w, (2 * g + r) * LANES:(2 * g + r + 1) * LANES] for r in range(pairs)]
        qm[qb, g, p] = jnp.concatenate([jnp.where(keep, x * scale, 0.0) for x in q_tiles], axis=0).astype(BF16)
        variant = 0 if p == g % 2 else 1
        kz[qb, g, p] = jnp.concatenate([k_blk[qb + d][g // 2][variant] for d in range(2)], axis=0)
        vzt[qb, g, p] = jnp.concatenate([vt_blk[qb + d][g // 2][variant] for d in range(2)], axis=1)
        sink[qb, g, p] = jnp.concatenate([jnp.full((1, w), sinks_ref[SWA_GROUP * g + 2 * r + p], F32)
                                          for r in range(pairs)], axis=1)
        valid[qb, g, p] = banded_first if qb == 0 else banded
    st = {b: jnp.where(valid[b], lax.dot_general(kz[b], qm[b], (((1,), (1,)), ((), ())),
                                                 preferred_element_type=F32), -jnp.inf) for b in items}
    m = {b: jnp.maximum(jnp.max(st[b], axis=0, keepdims=True), sink[b]) for b in items}
    et = {b: jnp.exp(st[b] - m[b]) for b in items}
    den = {b: jnp.sum(et[b], axis=0, keepdims=True) + jnp.exp(sink[b] - m[b]) for b in items}
    ot = {b: jnp.dot(vzt[b], et[b].astype(BF16), preferred_element_type=F32) / den[b] for b in items}
    for qb in range(nq):
        for g in range(SWA_KV_HEADS):
            for r in range(pairs):
                cols = slice(r * w, (r + 1) * w)
                tile_t = jnp.where(lo_row, ot[qb, g, 0][:, cols], ot[qb, g, 1][:, cols])
                o_ref[qb * w:(qb + 1) * w, (2 * g + r) * LANES:(2 * g + r + 1) * LANES] = (
                    tile_t.T.astype(o_ref.dtype))


def _swa_prompt(sq, skv, sinks, nq):
    b_, l_, _ = sq.shape
    rows = nq * WINDOW
    return pl.pallas_call(
        functools.partial(_swa_prompt_body, nq),
        grid=(b_, l_ // rows),
        in_specs=[pl.BlockSpec(memory_space=pltpu.SMEM),
                  pl.BlockSpec((None, rows, SWA_Q), lambda b, n: (b, n, 0)),
                  pl.BlockSpec((None, WINDOW, 2 * SWA_KV), lambda b, n: (b, jnp.maximum(n * nq - 1, 0), 0)),
                  pl.BlockSpec((None, rows, 2 * SWA_KV), lambda b, n: (b, n, 0))],
        out_specs=pl.BlockSpec((None, rows, SWA_Q), lambda b, n: (b, n, 0)),
        out_shape=jax.ShapeDtypeStruct((b_, l_, SWA_Q), BF16),
        compiler_params=_params(("arbitrary", "arbitrary")),
        name="swa_prompt",
    )(sinks, sq, skv, skv)


def _swa_step_body(bb, q_ref, kvn_ref, ck_ref, cv_ref, sink_ref, o_ref, nk_ref, nv_ref):
    w = WINDOW
    row = lax.broadcasted_iota(jnp.int32, (SWA_Q_HEADS, SWA_KV), 0)
    lane = lax.broadcasted_iota(jnp.int32, (SWA_Q_HEADS, SWA_KV), 1)
    own = (lane // SWA_HD) == (row // SWA_GROUP)
    key = lax.broadcasted_iota(jnp.int32, (SWA_Q_HEADS, w), 1)
    sink = sink_ref[...]
    scale = SWA_HD ** -0.5
    seqs = range(bb)
    q_bd = [jnp.where(own, jnp.concatenate([q_ref[i]] * SWA_KV_HEADS, axis=1), 0.0) for i in seqs]
    kn = [kvn_ref[i:i + 1, 0:SWA_KV] for i in seqs]
    vn = [kvn_ref[i:i + 1, SWA_KV:2 * SWA_KV] for i in seqs]
    s_c = [jnp.where(key >= 1, _mm_nt(q_bd[i], ck_ref[i]) * scale, -jnp.inf) for i in seqs]
    s_n = [jnp.sum(q_bd[i] * kn[i], axis=1, keepdims=True) * scale for i in seqs]
    m = [jnp.maximum(jnp.maximum(jnp.max(s_c[i], axis=1, keepdims=True), s_n[i]), sink) for i in seqs]
    e_c = [jnp.exp(s_c[i] - m[i]) for i in seqs]
    e_n = [jnp.exp(s_n[i] - m[i]) for i in seqs]
    den = [jnp.sum(e_c[i], axis=1, keepdims=True) + e_n[i] + jnp.exp(sink - m[i]) for i in seqs]
    pv = [jnp.where(own, _mm(e_c[i] / den[i], cv_ref[i]) + (e_n[i] / den[i]) * vn[i], 0.0) for i in seqs]
    for i in seqs:
        o = pv[i][:, 0:SWA_HD]
        for g in range(1, SWA_KV_HEADS):
            o = o + pv[i][:, g * SWA_HD:(g + 1) * SWA_HD]
        o_ref[i] = o
        nk_ref[i, 0:w - 1, :] = ck_ref[i, 1:w, :]
        nk_ref[i, w - 1:w, :] = kn[i]
        nv_ref[i, 0:w - 1, :] = cv_ref[i, 1:w, :]
        nv_ref[i, w - 1:w, :] = vn[i]


def _swa_step(q3, kvn, ck, cv, sink_col, bb):
    n_ = q3.shape[0]
    cache = pl.BlockSpec((bb, WINDOW, SWA_KV), lambda i: (i, 0, 0))
    return pl.pallas_call(
        functools.partial(_swa_step_body, bb),
        grid=(n_ // bb,),
        in_specs=[pl.BlockSpec((bb, SWA_Q_HEADS, SWA_HD), lambda i: (i, 0, 0)),
                  pl.BlockSpec((bb, 2 * SWA_KV), lambda i: (i, 0)),
                  cache, cache,
                  _const_spec(sink_col.shape)],
        out_specs=(pl.BlockSpec((bb, SWA_Q_HEADS, SWA_HD), lambda i: (i, 0, 0)), cache, cache),
        out_shape=(jax.ShapeDtypeStruct(q3.shape, F32),
                   jax.ShapeDtypeStruct(ck.shape, F32),
                   jax.ShapeDtypeStruct(cv.shape, F32)),
        compiler_params=_params(("arbitrary",)),
        name="swa_step",
    )(q3, kvn, ck, cv, sink_col)


def _dense_body(stateful, tm, og_ref, ob_ref, gab_ref, x_ref, gt1_ref, sh2_ref, sc2_ref, gt2_ref,
                n2w_ref, fnw_ref, wa_ref, wb_ref, wo_ref, wg_ref, wu_ref, cw_ref, cb_ref, wd_ref, *rest):
    if stateful:
        st_ref, y_ref, gout_ref, act_ref = rest
    else:
        y_ref, gout_ref, act_ref, gbuf_ref, carry_ref = rest

        @pl.when(pl.program_id(1) == 0)
        def _():
            carry_ref[...] = jnp.zeros_like(carry_ref)

    y_a = jnp.dot(og_ref[...], wa_ref[...], preferred_element_type=F32)
    y_b = jnp.dot(ob_ref[...], wb_ref[...], preferred_element_type=F32)
    merged = (jax.nn.sigmoid(gab_ref[:, 0:D_MODEL]) * y_a
              + jax.nn.sigmoid(gab_ref[:, D_MODEL:2 * D_MODEL]) * y_b)
    x1 = x_ref[...] + gt1_ref[...] * _mm(merged, wo_ref[...])
    h2 = (_rms(x1, n2w_ref[...]) * (1.0 + sc2_ref[...]) + sh2_ref[...]).astype(BF16)

    for c in range(D_FF // FFN_COLS):
        cols = slice(c * FFN_COLS, (c + 1) * FFN_COLS)
        gate = jnp.dot(h2, wg_ref[:, cols], preferred_element_type=F32)
        up = jnp.dot(h2, wu_ref[:, cols], preferred_element_type=F32)
        if stateful:
            g2 = st_ref[0, :, cols]
            g1 = st_ref[1, :, cols]
            gout_ref[:, cols] = gate
        else:
            gbuf_ref[0:SUBLANES, :] = carry_ref[:, cols]
            gbuf_ref[SUBLANES:SUBLANES + tm, :] = gate
            g2 = gbuf_ref[SUBLANES - 2:SUBLANES - 2 + tm, :]
            g1 = gbuf_ref[SUBLANES - 1:SUBLANES - 1 + tm, :]
            carry_ref[:, cols] = gbuf_ref[tm:tm + SUBLANES, :]
        gc = (cw_ref[0:1, cols] * g2 + cw_ref[1:2, cols] * g1 + cw_ref[2:3, cols] * gate) + cb_ref[:, cols]
        act_ref[:, cols] = (_silu(gc) * up).astype(BF16)
    if not stateful:
        gout_ref[...] = carry_ref[...]

    x2 = x1 + gt2_ref[...] * jnp.dot(act_ref[...], wd_ref[...], preferred_element_type=F32)
    y_ref[...] = _rms(x2, fnw_ref[...])


def _dense(og, ob, gab, x, mods, vecs, ws, st, tm):
    b_, l_, _ = x.shape
    r_ = mods[0].shape[1]
    rt = 1 if r_ == 1 else tm
    mod_map = (lambda b, t: (b, 0, 0)) if r_ == 1 else (lambda b, t: (b, t, 0))
    row_map = lambda b, t: (b, t, 0)
    stateful = st is not None
    in_specs = ([pl.BlockSpec((None, tm, D_MODEL), row_map),
                 pl.BlockSpec((None, tm, D_MODEL), row_map),
                 pl.BlockSpec((None, tm, 2 * D_MODEL), row_map),
                 pl.BlockSpec((None, tm, D_MODEL), row_map)]
                + [pl.BlockSpec((None, rt, D_MODEL), mod_map)] * 4
                + [_const_spec(a.shape) for a in vecs[:2]]
                + [_const_spec(ws[0].shape), _const_spec(ws[1].shape), _const_spec(ws[2].shape),
                   _const_spec(ws[3].shape), _const_spec(ws[4].shape),
                   _const_spec(vecs[2].shape), _const_spec(vecs[3].shape), _const_spec(ws[5].shape)])
    args = [og, ob, gab, x, *mods, vecs[0], vecs[1], ws[0], ws[1], ws[2], ws[3], ws[4], vecs[2], vecs[3], ws[5]]
    scratch = [pltpu.VMEM((tm, D_FF), BF16)]
    if stateful:
        in_specs.append(pl.BlockSpec((FFN_CONV - 1, None, tm, D_FF), lambda b, t: (0, b, t, 0)))
        args.append(st)
        gout_shape = jax.ShapeDtypeStruct((b_, l_, D_FF), F32)
        gout_spec = pl.BlockSpec((None, tm, D_FF), row_map)
    else:
        scratch += [pltpu.VMEM((tm + SUBLANES, FFN_COLS), F32), pltpu.VMEM((SUBLANES, D_FF), F32)]
        gout_shape = jax.ShapeDtypeStruct((b_, SUBLANES, D_FF), F32)
        gout_spec = pl.BlockSpec((None, SUBLANES, D_FF), lambda b, t: (b, 0, 0))
    return pl.pallas_call(
        functools.partial(_dense_body, stateful, tm),
        grid=(b_, l_ // tm),
        in_specs=in_specs,
        out_specs=(pl.BlockSpec((None, tm, D_MODEL), row_map), gout_spec),
        out_shape=(jax.ShapeDtypeStruct((b_, l_, D_MODEL), F32), gout_shape),
        scratch_shapes=scratch,
        compiler_params=_params(("arbitrary", "arbitrary")),
        name="dense_step" if stateful else "dense_prompt",
    )(*args)


def _lane_row(values, offset):
    return jnp.zeros((1, LANES), F32).at[0, offset:offset + values.shape[0]].set(values)


def kernel(x_prompt, x_sample, c_prompt, c_sample, state_gdn_S, state_gdn_conv, cache_swa_k, cache_swa_v,
           state_ffn_conv, w_mod, b_mod, norm1_w, norm2_w, w_in, gdn_conv_w, gdn_a_log, gdn_dt_bias,
           gdn_onorm_w, w_gdn_out, swa_sinks, w_swa_out, w_o, w_ffn_gate, w_ffn_up, ffn_conv_w, ffn_conv_b,
           w_ffn_down, final_norm_w):
    assert w_mod.shape[0] == 1, "single-layer trunk"
    nb, seq, _ = x_prompt.shape
    ns = x_sample.shape[0]
    assert x_sample.shape[1] == 1

    in_ws = _in_weight(w_in[0], tr=256)
    dense_ws = (w_gdn_out[0].astype(BF16), w_swa_out[0].astype(BF16), w_o[0].astype(BF16),
                w_ffn_gate[0].astype(BF16), w_ffn_up[0].astype(BF16), w_ffn_down[0].astype(BF16))
    dense_vecs = (norm2_w, final_norm_w[None, :], ffn_conv_w[0], ffn_conv_b)
    cw = jnp.transpose(gdn_conv_w[0].reshape(GDN_CONV, GDN_SECTIONS, LANES), (1, 0, 2))
    alog_row = _lane_row(gdn_a_log[0], GDN_HEADS)
    dtb_row = _lane_row(gdn_dt_bias[0], GDN_HEADS)

    mod = _modulation(jnp.concatenate([c_prompt, c_sample], axis=0), w_mod[0], b_mod)
    mod_p = [mod[:nb, i * D_MODEL:(i + 1) * D_MODEL][:, None, :] for i in range(6)]
    mod_s = [mod[nb:, i * D_MODEL:(i + 1) * D_MODEL][None, :, :] for i in range(6)]

    qkvf, gact, ba, sq, skv, gab, qkv_tail = _inproj(x_prompt, mod_p[0], mod_p[1], norm1_w, in_ws, cw, tm=256)
    og, gdn_s_p = _gdn_prompt(qkvf, gact, ba, alog_row, dtb_row, gdn_onorm_w, _level_masks(), lt=2 * CHUNK)
    ob = _swa_prompt(sq, skv, swa_sinks[0], nq=2)
    y_p, gate_tail = _dense(og, ob, gab, x_prompt, (mod_p[2], mod_p[3], mod_p[4], mod_p[5]),
                            dense_vecs, dense_ws, None, tm=256)
    gdn_conv_p = jnp.transpose(qkv_tail[:, :, SUBLANES - (GDN_CONV - 1):, :], (0, 2, 1, 3)).reshape(
        nb, GDN_CONV - 1, GDN_CONV_CH)
    k_p = skv[:, seq - WINDOW:, :SWA_KV].reshape(nb, WINDOW, SWA_KV_HEADS, SWA_HD)
    v_p = skv[:, seq - WINDOW:, SWA_KV:].reshape(nb, WINDOW, SWA_KV_HEADS, SWA_HD)
    ffn_conv_p = gate_tail[:, SUBLANES - (FFN_CONV - 1):, :]

    xs = x_sample.reshape(1, ns, D_MODEL)
    qkv4s, ggs, bas, sqs, skvs, gabs = _inproj(xs, mod_s[0], mod_s[1], norm1_w, in_ws, None, tm=ns)
    st4 = jnp.transpose(state_gdn_conv[0].reshape(ns, GDN_CONV - 1, GDN_SECTIONS, LANES), (1, 2, 0, 3))
    og_s, gdn_s_s = _gdn_step(qkv4s[0], st4, ggs[0], bas[0], cw, alog_row, dtb_row, gdn_onorm_w,
                              state_gdn_S[0], bb=8)
    o3, k_s, v_s = _swa_step(sqs[0].reshape(ns, SWA_Q_HEADS, SWA_HD), skvs[0],
                             cache_swa_k[0].reshape(ns, WINDOW, SWA_KV),
                             cache_swa_v[0].reshape(ns, WINDOW, SWA_KV),
                             swa_sinks[0][:, None], bb=8)
    ob_s = o3.reshape(1, ns, SWA_Q).astype(BF16)
    st_ffn = jnp.transpose(state_ffn_conv[0], (1, 0, 2))[:, None]
    y_s, gate_new = _dense(og_s[None], ob_s, gabs, xs, (mod_s[2], mod_s[3], mod_s[4], mod_s[5]),
                           dense_vecs, dense_ws, st_ffn, tm=ns)
    qkv_new = jnp.transpose(qkv4s[0], (1, 0, 2)).reshape(ns, 1, GDN_CONV_CH)
    gdn_conv_s = jnp.concatenate([state_gdn_conv[0][:, 1:], qkv_new], axis=1)
    ffn_conv_s = jnp.concatenate([state_ffn_conv[0][:, 1:], gate_new[0][:, None, :]], axis=1)

    return (y_p, y_s.reshape(ns, 1, D_MODEL),
            gdn_s_p[None], gdn_s_s[None],
            gdn_conv_p[None], gdn_conv_s[None],
            k_p[None], k_s.reshape(ns, WINDOW, SWA_KV_HEADS, SWA_HD)[None],
            v_p[None], v_s.reshape(ns, WINDOW, SWA_KV_HEADS, SWA_HD)[None],
            ffn_conv_p[None], ffn_conv_s[None])
```

```python
import functools

import numpy as np
import jax
import jax.numpy as jnp
from jax import lax
from jax.experimental import pallas as pl
from jax.experimental.pallas import tpu as pltpu

F32 = jnp.float32
BF16 = jnp.bfloat16

D_MODEL = 1024
GDN_HEADS = 8
GDN_DK = 128
GDN_DV = 128
GDN_QK = GDN_HEADS * GDN_DK
GDN_V = GDN_HEADS * GDN_DV
GDN_CONV = 4
GDN_CONV_CH = 2 * GDN_QK + GDN_V
GDN_SECTIONS = GDN_CONV_CH // 128
SWA_Q_HEADS = 16
SWA_KV_HEADS = 4
SWA_GROUP = SWA_Q_HEADS // SWA_KV_HEADS
SWA_HD = 64
SWA_Q = SWA_Q_HEADS * SWA_HD
SWA_KV = SWA_KV_HEADS * SWA_HD
WINDOW = 128
D_FF = 2816
FFN_CONV = 3
EPS = 1e-6

LANES = 128
SUBLANES = 8
VMEM_LIMIT = 56 * 1024 * 1024

COL_QKV = 0
COL_GATE = COL_QKV + GDN_CONV_CH
COL_SQ = COL_GATE + GDN_V
COL_SKV = COL_SQ + SWA_Q
COL_GAB = COL_SKV + 2 * SWA_KV
COL_BA = COL_GAB + 2 * D_MODEL
IN_COLS = COL_BA + LANES

CHUNK = 128
FFN_COLS = 256


def _mm(a, b):
    return jnp.dot(a.astype(BF16), b.astype(BF16), preferred_element_type=F32)


def _mm_nt(a, b):
    return lax.dot_general(a.astype(BF16), b.astype(BF16), (((1,), (1,)), ((), ())),
                           preferred_element_type=F32)


def _silu(x):
    return x * jax.nn.sigmoid(x)


def _softplus(x):
    return jnp.maximum(x, 0.0) + jnp.log1p(jnp.exp(-jnp.abs(x)))


def _rms(x, w):
    return x * lax.rsqrt(jnp.mean(x * x, axis=-1, keepdims=True) + EPS) * w


def _const_spec(shape):
    n = len(shape)
    return pl.BlockSpec(shape, lambda *_: (0,) * n, pipeline_mode=pl.Buffered(1))


def _params(sem):
    return pltpu.CompilerParams(dimension_semantics=sem, vmem_limit_bytes=VMEM_LIMIT)


def _mod_body(c_ref, w_ref, b_ref, o_ref):
    o_ref[...] = _mm(_silu(c_ref[...]), w_ref[...]) + b_ref[...]


def _modulation(c_all, w_mod, b_mod):
    rows = c_all.shape[0]
    n_out = w_mod.shape[1]
    tn = D_MODEL
    return pl.pallas_call(
        _mod_body,
        grid=(n_out // tn,),
        in_specs=[pl.BlockSpec((rows, D_MODEL), lambda j: (0, 0)),
                  pl.BlockSpec((D_MODEL, tn), lambda j: (0, j)),
                  pl.BlockSpec((1, tn), lambda j: (0, j))],
        out_specs=pl.BlockSpec((rows, tn), lambda j: (0, j)),
        out_shape=jax.ShapeDtypeStruct((rows, n_out), F32),
        compiler_params=_params(("arbitrary",)),
        name="modulation",
    )(c_all, w_mod, b_mod)


IN_WEIGHT_COLS = 256


def _in_weight_body(n_main, wt_ref, ba_ref, o_ref):
    j = pl.program_id(0)

    @pl.when(j < n_main)
    def _():
        o_ref[...] = wt_ref[...].T.astype(BF16)

    @pl.when(j == n_main)
    def _():
        n_ba = ba_ref.shape[0]
        ba = jnp.concatenate([ba_ref[...].T, jnp.zeros((D_MODEL, IN_WEIGHT_COLS - n_ba), F32)], axis=1)
        o_ref[...] = ba.astype(BF16)


def _in_weight(w_t):
    n_ba = 2 * GDN_HEADS
    split = GDN_CONV_CH + GDN_V
    tc = IN_WEIGHT_COLS
    n_main = COL_BA // tc

    def src_row(j):
        jj = jnp.minimum(j, n_main - 1)
        return pl.multiple_of(jnp.where(jj * tc < split, jj * tc, jj * tc + n_ba), n_ba)

    return pl.pallas_call(
        functools.partial(_in_weight_body, n_main),
        grid=(n_main + 1,),
        in_specs=[pl.BlockSpec((pl.Element(tc), pl.Element(D_MODEL)), lambda j: (src_row(j), 0)),
                  pl.BlockSpec((pl.Element(n_ba), pl.Element(D_MODEL)), lambda j: (split, 0))],
        out_specs=pl.BlockSpec((D_MODEL, tc), lambda j: (0, j)),
        out_shape=jax.ShapeDtypeStruct((D_MODEL, COL_BA + tc), BF16),
        compiler_params=_params(("arbitrary",)),
        name="in_weight",
    )(w_t, w_t)


def _l2norm(x):
    return x * lax.rsqrt(jnp.sum(x * x, axis=-1, keepdims=True) + EPS)


def _inproj_body(seq_rows, tm, x_ref, sh_ref, sc_ref, nw_ref, w_ref, *rest):
    if seq_rows:
        cw_ref, qkv_ref, gg_ref, ba_ref, sq_ref, skv_ref, gab_ref, tail_ref, xe_ref = rest

        @pl.when(pl.program_id(1) == 0)
        def _():
            xe_ref[:, 0:SUBLANES, :] = jnp.zeros((GDN_SECTIONS, SUBLANES, LANES), F32)
    else:
        qkv_ref, gg_ref, ba_ref, sq_ref, skv_ref, gab_ref = rest

    h = _rms(x_ref[...], nw_ref[...]) * (1.0 + sc_ref[...]) + sh_ref[...]
    hb = h.astype(BF16)

    def proj(lo, width):
        return jnp.dot(hb, w_ref[:, lo:lo + width], preferred_element_type=F32)

    step = 512
    per = step // LANES
    for c in range(GDN_CONV_CH // step):
        z = proj(COL_QKV + c * step, step)
        for k in range(per):
            s = c * per + k
            zs = z[:, k * LANES:(k + 1) * LANES]
            if not seq_rows:
                qkv_ref[:, s * LANES:(s + 1) * LANES] = zs
                continue
            xe_ref[s, SUBLANES:SUBLANES + tm, :] = zs
            w = cw_ref[s]
            y = w[0:1] * xe_ref[s, SUBLANES - 3:SUBLANES - 3 + tm, :]
            for tap in range(1, GDN_CONV):
                lo = SUBLANES - 3 + tap
                y = y + w[tap:tap + 1] * xe_ref[s, lo:lo + tm, :]
            xe_ref[s, 0:SUBLANES, :] = xe_ref[s, tm:tm + SUBLANES, :]
            f = _silu(y)
            if s < GDN_HEADS:
                f = _l2norm(f) * (GDN_DK ** -0.5)
            elif s < 2 * GDN_HEADS:
                f = _l2norm(f)
            qkv_ref[s] = f
    if seq_rows:
        tail_ref[...] = xe_ref[:, 0:SUBLANES, :]
    for c in range(GDN_V // step):
        z = proj(COL_GATE + c * step, step)
        for k in range(per):
            zs = z[:, k * LANES:(k + 1) * LANES]
            if seq_rows:
                gg_ref[c * per + k] = _silu(zs)
            else:
                gg_ref[:, (c * per + k) * LANES:(c * per + k + 1) * LANES] = zs
    ba_ref[...] = proj(COL_BA, LANES)
    for c in range(SWA_Q // step):
        sq_ref[:, c * step:(c + 1) * step] = proj(COL_SQ + c * step, step)
    skv_ref[...] = proj(COL_SKV, 2 * SWA_KV)
    for c in range(2 * D_MODEL // step):
        gab_ref[:, c * step:(c + 1) * step] = proj(COL_GAB + c * step, step)


def _inproj(x, sh, sc, nw, w_all, cw, tm):
    b_, l_, _ = x.shape
    r_ = sh.shape[1]
    rt = 1 if r_ == 1 else tm
    mod_map = (lambda b, t: (b, 0, 0)) if r_ == 1 else (lambda b, t: (b, t, 0))
    row_map = lambda b, t: (b, t, 0)
    head_map = lambda b, t: (b, 0, t, 0)
    seq_rows = cw is not None
    if seq_rows:
        gdn_shapes = (jax.ShapeDtypeStruct((b_, GDN_SECTIONS, l_, LANES), F32),
                      jax.ShapeDtypeStruct((b_, GDN_HEADS, l_, LANES), F32))
        gdn_specs = (pl.BlockSpec((None, GDN_SECTIONS, tm, LANES), head_map),
                     pl.BlockSpec((None, GDN_HEADS, tm, LANES), head_map))
    else:
        gdn_shapes = (jax.ShapeDtypeStruct((b_, l_, GDN_CONV_CH), F32),
                      jax.ShapeDtypeStruct((b_, l_, GDN_V), F32))
        gdn_specs = (pl.BlockSpec((None, tm, GDN_CONV_CH), row_map),
                     pl.BlockSpec((None, tm, GDN_V), row_map))
    out_shape = gdn_shapes + (
        jax.ShapeDtypeStruct((b_, l_, LANES), F32),
        jax.ShapeDtypeStruct((b_, l_, SWA_Q), F32),
        jax.ShapeDtypeStruct((b_, l_, 2 * SWA_KV), F32),
        jax.ShapeDtypeStruct((b_, l_, 2 * D_MODEL), F32),
    )
    out_specs = gdn_specs + (
        pl.BlockSpec((None, tm, LANES), row_map),
        pl.BlockSpec((None, tm, SWA_Q), row_map),
        pl.BlockSpec((None, tm, 2 * SWA_KV), row_map),
        pl.BlockSpec((None, tm, 2 * D_MODEL), row_map),
    )
    in_specs = [
        pl.BlockSpec((None, tm, D_MODEL), row_map),
        pl.BlockSpec((None, rt, D_MODEL), mod_map),
        pl.BlockSpec((None, rt, D_MODEL), mod_map),
        _const_spec(nw.shape),
        _const_spec(w_all.shape),
    ]
    args = [x, sh, sc, nw, w_all]
    scratch = []
    if seq_rows:
        in_specs.append(_const_spec(cw.shape))
        args.append(cw)
        out_shape += (jax.ShapeDtypeStruct((b_, GDN_SECTIONS, SUBLANES, LANES), F32),)
        out_specs += (pl.BlockSpec((None, GDN_SECTIONS, SUBLANES, LANES), lambda b, t: (b, 0, 0, 0)),)
        scratch.append(pltpu.VMEM((GDN_SECTIONS, tm + SUBLANES, LANES), F32))
    return pl.pallas_call(
        functools.partial(_inproj_body, seq_rows, tm),
        grid=(b_, l_ // tm),
        in_specs=in_specs,
        out_specs=out_specs,
        out_shape=out_shape,
        scratch_shapes=scratch,
        compiler_params=_params(("arbitrary", "arbitrary")),
        name="inproj_seq" if seq_rows else "inproj_rows",
    )(*args)


def _delta_gates(ba, alog_row, dtb_row):
    beta_all = jax.nn.sigmoid(ba)
    g_all = -jnp.exp(alog_row) * _softplus(ba + dtb_row)
    return beta_all, g_all


def _lane_column(x, lane_idx, lane):
    return jnp.sum(jnp.where(lane_idx == lane, x, 0.0), axis=1, keepdims=True)


def _level_masks():
    r = np.arange(CHUNK)[:, None]
    c = np.arange(CHUNK)[None, :]
    masks = [(r == c + 1) & (r % 2 == 1)]
    half = 2
    while half < CHUNK:
        full = 2 * half
        masks.append((r // full == c // full) & (r % full >= half) & (c % full < half))
        half = full
    return jnp.asarray(np.stack(masks), dtype=BF16)


def _unit_lower_inverses(ms, masks_ref, eye):
    ts = [eye - m * masks_ref[0] for m in ms]
    for lvl in range(1, masks_ref.shape[0]):
        off = masks_ref[lvl]
        xs = [jnp.dot(m * off, t, preferred_element_type=F32).astype(BF16) for m, t in zip(ms, ts)]
        ys = [jnp.dot(t, x, preferred_element_type=F32).astype(BF16) for t, x in zip(ts, xs)]
        ts = [t - y for t, y in zip(ts, ys)]
    return ts


def _cumsum_rows(g, ltri):
    hi = g.astype(BF16)
    r1 = g - hi.astype(F32)
    mid = r1.astype(BF16)
    lo = (r1 - mid.astype(F32)).astype(BF16)
    return (jnp.dot(ltri, hi, preferred_element_type=F32) + jnp.dot(ltri, mid, preferred_element_type=F32)
            + jnp.dot(ltri, lo, preferred_element_type=F32))


def _gated_out_norm(o, gate_act, onw):
    on = o * lax.rsqrt(jnp.mean(o * o, axis=-1, keepdims=True) + EPS) * onw
    return on * gate_act


def _gdn_prompt_body(lt, q_ref, k_ref, v_ref, ba_ref, alog_ref, dtb_ref, gate_ref, onw_ref, masks_ref,
                     og_ref, s_ref):
    @pl.when(pl.program_id(1) == 0)
    def _():
        s_ref[...] = jnp.zeros_like(s_ref)

    beta_all, g_all = _delta_gates(ba_ref[...], alog_ref[...], dtb_ref[...])
    lane_idx = lax.broadcasted_iota(jnp.int32, (CHUNK, LANES), 1)
    row = lax.broadcasted_iota(jnp.int32, (CHUNK, CHUNK), 0)
    col = lax.broadcasted_iota(jnp.int32, (CHUNK, CHUNK), 1)
    tril = row >= col
    strict = row > col
    ltri = jnp.where(tril, 1.0, 0.0).astype(BF16)
    eye = jnp.where(row == col, 1.0, 0.0).astype(BF16)
    onw = onw_ref[...]
    heads = range(GDN_HEADS)
    chunks = range(lt // CHUNK)

    blocks = [(c, j) for c in chunks for j in heads]
    pre = {}
    for c in chunks:
        rows = slice(c * CHUNK, (c + 1) * CHUNK)
        dec = _cumsum_rows(g_all[rows], ltri)
        dec_t = dec.T
        for j in heads:
            q, k, v = q_ref[j, rows, :], k_ref[j, rows, :], v_ref[j, rows, :]
            beta_col = _lane_column(beta_all[rows], lane_idx, j)
            dec_col = _lane_column(dec, lane_idx, GDN_HEADS + j)
            dec_row = dec_t[GDN_HEADS + j:GDN_HEADS + j + 1, :]
            dec_last = dec_row[:, CHUNK - 1:CHUNK]
            gam = jnp.exp(jnp.minimum(dec_col - dec_row, 0.0))
            e_col = jnp.exp(dec_col)
            kb = k * beta_col
            pre[c, j] = dict(q=q, k=k, gam=gam, kb=kb, qe=q * e_col, e_last=jnp.exp(dec_last),
                             kd=k * jnp.exp(dec_last - dec_col),
                             rhs=jnp.concatenate([v * beta_col, kb * e_col], axis=1).astype(BF16))
    grams = [_mm_nt(jnp.concatenate([pre[b]["kb"], pre[b]["q"]], axis=0), pre[b]["k"]) for b in blocks]
    ms = [jnp.where(strict, g[:CHUNK] * pre[b]["gam"], 0.0).astype(BF16) for g, b in zip(grams, blocks)]
    a_intra = {b: jnp.where(tril, g[CHUNK:] * pre[b]["gam"], 0.0) for g, b in zip(grams, blocks)}
    t_inv = _unit_lower_inverses(ms, masks_ref, eye)
    uw = {b: jnp.dot(t, pre[b]["rhs"], preferred_element_type=F32) for t, b in zip(t_inv, blocks)}

    for c in chunks:
        rows = slice(c * CHUNK, (c + 1) * CHUNK)
        s_prev = [s_ref[j] for j in heads]
        ws_qs = [_mm(jnp.concatenate([uw[c, j][:, GDN_DV:], pre[c, j]["qe"]], axis=0), s_prev[j]) for j in heads]
        v_new = [uw[c, j][:, :GDN_DV] - ws_qs[j][:CHUNK] for j in heads]
        o = [ws_qs[j][CHUNK:] + _mm(a_intra[c, j], v_new[j]) for j in heads]
        s_new = [s_prev[j] * pre[c, j]["e_last"] + _mm(pre[c, j]["kd"].T, v_new[j]) for j in heads]
        for j in heads:
            s_ref[j] = s_new[j]
            og = _gated_out_norm(o[j], gate_ref[j, rows, :], onw)
            og_ref[rows, j * GDN_DV:(j + 1) * GDN_DV] = og.astype(og_ref.dtype)


def _gdn_prompt(qkvf, gact, ba, alog_row, dtb_row, onw, masks, lt):
    b_, _, l_, _ = qkvf.shape
    sec = lambda s: pl.BlockSpec((None, GDN_HEADS, lt, LANES), lambda b, t, s=s: (b, s, t, 0))
    return pl.pallas_call(
        functools.partial(_gdn_prompt_body, lt),
        grid=(b_, l_ // lt),
        in_specs=[sec(0), sec(1), sec(2),
                  pl.BlockSpec((None, lt, LANES), lambda b, t: (b, t, 0)),
                  _const_spec(alog_row.shape), _const_spec(dtb_row.shape),
                  pl.BlockSpec((None, GDN_HEADS, lt, LANES), lambda b, t: (b, 0, t, 0)),
                  _const_spec(onw.shape), _const_spec(masks.shape)],
        out_specs=(pl.BlockSpec((None, lt, GDN_V), lambda b, t: (b, t, 0)),
                   pl.BlockSpec((None, GDN_HEADS, GDN_DK, GDN_DV), lambda b, t: (b, 0, 0, 0))),
        out_shape=(jax.ShapeDtypeStruct((b_, l_, GDN_V), BF16),
                   jax.ShapeDtypeStruct((b_, GDN_HEADS, GDN_DK, GDN_DV), F32)),
        compiler_params=_params(("arbitrary", "arbitrary")),
        name="gdn_prompt",
    )(qkvf, qkvf, qkvf, ba, alog_row, dtb_row, gact, onw, masks)


def _gdn_step_body(bb, x_ref, st_ref, cw_ref, ba_ref, alog_ref, dtb_ref, gate_ref, onw_ref, s0_ref,
                   og_ref, sn_ref, q_s, k_s, v_s, b_s, e_s, o_s):
    beta_all, g_all = _delta_gates(ba_ref[...], alog_ref[...], dtb_ref[...])
    lane_idx = lax.broadcasted_iota(jnp.int32, (bb, LANES), 1)
    for h in range(GDN_HEADS):
        feats = []
        for s in range(3):
            idx = s * GDN_HEADS + h
            cols = slice(idx * LANES, (idx + 1) * LANES)
            w = cw_ref[idx]
            y = w[0:1] * st_ref[0, :, cols]
            for tap in range(1, GDN_CONV - 1):
                y = y + w[tap:tap + 1] * st_ref[tap, :, cols]
            y = y + w[GDN_CONV - 1:GDN_CONV] * x_ref[:, cols]
            feats.append(_silu(y))
        q, k, v = feats
        q_s[h] = q * lax.rsqrt(jnp.sum(q * q, axis=-1, keepdims=True) + EPS) * (GDN_DK ** -0.5)
        k_s[h] = k * lax.rsqrt(jnp.sum(k * k, axis=-1, keepdims=True) + EPS)
        v_s[h] = v
        b_s[h] = jnp.broadcast_to(_lane_column(beta_all, lane_idx, h), (bb, LANES))
        e_s[h] = jnp.broadcast_to(jnp.exp(_lane_column(g_all, lane_idx, h + GDN_HEADS)), (bb, LANES))

    eye = (lax.broadcasted_iota(jnp.int32, (GDN_DK, GDN_DK), 0)
           == lax.broadcasted_iota(jnp.int32, (GDN_DK, GDN_DK), 1))

    def to_col(r):
        return jnp.sum(jnp.where(eye, jnp.broadcast_to(r, (GDN_DK, GDN_DK)), 0.0), axis=1, keepdims=True)

    def seq_body(i, carry):
        for h in range(GDN_HEADS):
            one = pl.ds(i, 1)
            k_col = to_col(k_s[h, one, :])
            q_col = to_col(q_s[h, one, :])
            s1 = s0_ref[i, h] * e_s[h, one, :]
            ks = jnp.sum(s1 * k_col, axis=0, keepdims=True)
            delta = (v_s[h, one, :] - ks) * b_s[h, one, :]
            s2 = s1 + k_col * delta
            sn_ref[i, h] = s2
            o_s[h, one, :] = jnp.sum(s2 * q_col, axis=0, keepdims=True)
        return carry

    lax.fori_loop(0, bb, seq_body, 0)
    onw = onw_ref[...]
    for h in range(GDN_HEADS):
        cols = slice(h * GDN_DV, (h + 1) * GDN_DV)
        og = _gated_out_norm(o_s[h], _silu(gate_ref[:, cols]), onw)
        og_ref[:, cols] = og.astype(og_ref.dtype)


def _gdn_step(qkv, st, gate, ba, cw, alog_row, dtb_row, onw, s0, bb):
    n_ = ba.shape[0]
    vec = pltpu.VMEM((GDN_HEADS, bb, LANES), F32)
    return pl.pallas_call(
        functools.partial(_gdn_step_body, bb),
        grid=(n_ // bb,),
        in_specs=[pl.BlockSpec((bb, GDN_CONV_CH), lambda i: (i, 0)),
                  pl.BlockSpec((GDN_CONV - 1, bb, GDN_CONV_CH), lambda i: (0, i, 0)),
                  _const_spec(cw.shape),
                  pl.BlockSpec((bb, LANES), lambda i: (i, 0)),
                  _const_spec(alog_row.shape), _const_spec(dtb_row.shape),
                  pl.BlockSpec((bb, GDN_V), lambda i: (i, 0)),
                  _const_spec(onw.shape),
                  pl.BlockSpec((bb, GDN_HEADS, GDN_DK, GDN_DV), lambda i: (i, 0, 0, 0))],
        out_specs=(pl.BlockSpec((bb, GDN_V), lambda i: (i, 0)),
                   pl.BlockSpec((bb, GDN_HEADS, GDN_DK, GDN_DV), lambda i: (i, 0, 0, 0))),
        out_shape=(jax.ShapeDtypeStruct((n_, GDN_V), BF16),
                   jax.ShapeDtypeStruct(s0.shape, F32)),
        scratch_shapes=[vec, vec, vec, vec, vec, vec],
        compiler_params=_params(("arbitrary",)),
        name="gdn_step",
    )(qkv, st, cw, ba, alog_row, dtb_row, gate, onw, s0)


def _swa_prompt_body(nq, sinks_ref, q_ref, kvp_ref, kvc_ref, o_ref):
    n = pl.program_id(1)
    w = WINDOW
    tiles = SWA_KV // LANES
    pairs = 2
    lo_lane = lax.broadcasted_iota(jnp.int32, (w, LANES), 1) < SWA_HD
    lo_row = lax.broadcasted_iota(jnp.int32, (LANES, w), 0) < SWA_HD
    c = lax.broadcasted_iota(jnp.int32, (2 * w, pairs * w), 0)
    i = lax.broadcasted_iota(jnp.int32, (2 * w, pairs * w), 1) & (w - 1)
    banded = (c > i) & (c <= i + w)
    banded_first = banded & ((c >= w) | (n > 0))
    k_blk, vt_blk = [], []
    for j in range(nq + 1):
        src, rows = (kvp_ref, slice(0, w)) if j == 0 else (kvc_ref, slice((j - 1) * w, j * w))
        k_tiles, vt_tiles = [], []
        for t in range(tiles):
            kx = src[rows, t * LANES:(t + 1) * LANES]
            vt = src[rows, SWA_KV + t * LANES:SWA_KV + (t + 1) * LANES].T
            k_tiles.append((kx.astype(BF16), pltpu.roll(kx, SWA_HD, axis=1).astype(BF16)))
            vt_tiles.append((vt.astype(BF16),
                             jnp.concatenate([vt[SWA_HD:], vt[:SWA_HD]], axis=0).astype(BF16)))
        k_blk.append(k_tiles)
        vt_blk.append(vt_tiles)
    items = [(qb, g, p) for qb in range(nq) for g in range(SWA_KV_HEADS) for p in range(2)]
    scale = SWA_HD ** -0.5
    qm, kz, vzt, sink, valid = {}, {}, {}, {}, {}
    for qb, g, p in items:
        keep = lo_lane if p == 0 else jnp.logical_not(lo_lane)
        q_tiles = [q_ref[qb * w:(qb + 1) * w, (2 * g + r) * LANES:(2 * g + r + 1) * LANES] for r in range(pairs)]
        qm[qb, g, p] = jnp.concatenate([jnp.where(keep, x * scale, 0.0) for x in q_tiles], axis=0).astype(BF16)
        variant = 0 if p == g % 2 else 1
        kz[qb, g, p] = jnp.concatenate([k_blk[qb + d][g // 2][variant] for d in range(2)], axis=0)
        vzt[qb, g, p] = jnp.concatenate([vt_blk[qb + d][g // 2][variant] for d in range(2)], axis=1)
        sink[qb, g, p] = jnp.concatenate([jnp.full((1, w), sinks_ref[SWA_GROUP * g + 2 * r + p], F32)
                                          for r in range(pairs)], axis=1)
        valid[qb, g, p] = banded_first if qb == 0 else banded
    st = {b: jnp.where(valid[b], lax.dot_general(kz[b], qm[b], (((1,), (1,)), ((), ())),
                                                 preferred_element_type=F32), -jnp.inf) for b in items}
    m = {b: jnp.maximum(jnp.max(st[b], axis=0, keepdims=True), sink[b]) for b in items}
    et = {b: jnp.exp(st[b] - m[b]) for b in items}
    den = {b: jnp.sum(et[b], axis=0, keepdims=True) + jnp.exp(sink[b] - m[b]) for b in items}
    ot = {b: jnp.dot(vzt[b], et[b].astype(BF16), preferred_element_type=F32) / den[b] for b in items}
    for qb in range(nq):
        for g in range(SWA_KV_HEADS):
            for r in range(pairs):
                cols = slice(r * w, (r + 1) * w)
                tile_t = jnp.where(lo_row, ot[qb, g, 0][:, cols], ot[qb, g, 1][:, cols])
                o_ref[qb * w:(qb + 1) * w, (2 * g + r) * LANES:(2 * g + r + 1) * LANES] = (
                    tile_t.T.astype(o_ref.dtype))


def _swa_prompt(sq, skv, sinks, nq):
    b_, l_, _ = sq.shape
    rows = nq * WINDOW
    return pl.pallas_call(
        functools.partial(_swa_prompt_body, nq),
        grid=(b_, l_ // rows),
        in_specs=[pl.BlockSpec(memory_space=pltpu.SMEM),
                  pl.BlockSpec((None, rows, SWA_Q), lambda b, n: (b, n, 0)),
                  pl.BlockSpec((None, WINDOW, 2 * SWA_KV), lambda b, n: (b, jnp.maximum(n * nq - 1, 0), 0)),
                  pl.BlockSpec((None, rows, 2 * SWA_KV), lambda b, n: (b, n, 0))],
        out_specs=pl.BlockSpec((None, rows, SWA_Q), lambda b, n: (b, n, 0)),
        out_shape=jax.ShapeDtypeStruct((b_, l_, SWA_Q), BF16),
        compiler_params=_params(("arbitrary", "arbitrary")),
        name="swa_prompt",
    )(sinks, sq, skv, skv)


def _swa_step_body(bb, q_ref, kvn_ref, ck_ref, cv_ref, sink_ref, o_ref, nk_ref, nv_ref):
    w = WINDOW
    first = pl.program_id(0) * bb
    row = lax.broadcasted_iota(jnp.int32, (SWA_Q_HEADS, SWA_KV), 0)
    lane = lax.broadcasted_iota(jnp.int32, (SWA_Q_HEADS, SWA_KV), 1)
    own = (lane // SWA_HD) == (row // SWA_GROUP)
    newest = lax.broadcasted_iota(jnp.int32, (SWA_KV, w), 1) == w - 1
    sink = sink_ref[...]
    scale = SWA_HD ** -0.5
    seqs = range(bb)
    kn_all = kvn_ref[0:SWA_KV, :]
    vn_all = kvn_ref[SWA_KV:2 * SWA_KV, :]
    nk = [jnp.where(newest, pltpu.roll(kn_all, w - 1 - (first + i), axis=1), pltpu.roll(ck_ref[i], w - 1, axis=1))
          for i in seqs]
    nv = [jnp.where(newest, pltpu.roll(vn_all, w - 1 - (first + i), axis=1), pltpu.roll(cv_ref[i], w - 1, axis=1))
          for i in seqs]
    q_bd = [jnp.where(own, jnp.concatenate([q_ref[i]] * SWA_KV_HEADS, axis=1), 0.0) for i in seqs]
    s = [_mm(q_bd[i], nk[i]) * scale for i in seqs]
    m = [jnp.maximum(jnp.max(s[i], axis=1, keepdims=True), sink) for i in seqs]
    e = [jnp.exp(s[i] - m[i]) for i in seqs]
    den = [jnp.sum(e[i], axis=1, keepdims=True) + jnp.exp(sink - m[i]) for i in seqs]
    pv = [jnp.where(own, _mm_nt(e[i] / den[i], nv[i]), 0.0) for i in seqs]
    for i in seqs:
        o = pv[i][:, 0:SWA_HD]
        for g in range(1, SWA_KV_HEADS):
            o = o + pv[i][:, g * SWA_HD:(g + 1) * SWA_HD]
        o_ref[i] = o
        nk_ref[i] = nk[i]
        nv_ref[i] = nv[i]


def _swa_step(q3, kvn_t, ck_t, cv_t, sink_col, bb):
    n_ = q3.shape[0]
    assert n_ <= WINDOW
    cache = pl.BlockSpec((bb, SWA_KV, WINDOW), lambda i: (i, 0, 0))
    return pl.pallas_call(
        functools.partial(_swa_step_body, bb),
        grid=(n_ // bb,),
        in_specs=[pl.BlockSpec((bb, SWA_Q_HEADS, SWA_HD), lambda i: (i, 0, 0)),
                  _const_spec(kvn_t.shape),
                  cache, cache,
                  _const_spec(sink_col.shape)],
        out_specs=(pl.BlockSpec((bb, SWA_Q_HEADS, SWA_HD), lambda i: (i, 0, 0)), cache, cache),
        out_shape=(jax.ShapeDtypeStruct(q3.shape, F32),
                   jax.ShapeDtypeStruct(ck_t.shape, F32),
                   jax.ShapeDtypeStruct(cv_t.shape, F32)),
        compiler_params=_params(("arbitrary",)),
        name="swa_step",
    )(q3, kvn_t, ck_t, cv_t, sink_col)


def _dense_body(stateful, tm, og_ref, ob_ref, gab_ref, x_ref, gt1_ref, sh2_ref, sc2_ref, gt2_ref,
                n2w_ref, fnw_ref, wa_ref, wb_ref, wo_ref, wg_ref, wu_ref, cw_ref, cb_ref, wd_ref, *rest):
    if stateful:
        st_ref, y_ref, gout_ref, act_ref = rest
    else:
        y_ref, gout_ref, act_ref, gbuf_ref, carry_ref = rest

        @pl.when(pl.program_id(1) == 0)
        def _():
            carry_ref[...] = jnp.zeros_like(carry_ref)

    y_a = jnp.dot(og_ref[...], wa_ref[...], preferred_element_type=F32)
    y_b = jnp.dot(ob_ref[...], wb_ref[...], preferred_element_type=F32)
    merged = (jax.nn.sigmoid(gab_ref[:, 0:D_MODEL]) * y_a
              + jax.nn.sigmoid(gab_ref[:, D_MODEL:2 * D_MODEL]) * y_b)
    x1 = x_ref[...] + gt1_ref[...] * _mm(merged, wo_ref[...])
    h2 = (_rms(x1, n2w_ref[...]) * (1.0 + sc2_ref[...]) + sh2_ref[...]).astype(BF16)

    for c in range(D_FF // FFN_COLS):
        cols = slice(c * FFN_COLS, (c + 1) * FFN_COLS)
        gate = jnp.dot(h2, wg_ref[:, cols], preferred_element_type=F32)
        up = jnp.dot(h2, wu_ref[:, cols], preferred_element_type=F32)
        if stateful:
            g2 = st_ref[0, :, cols]
            g1 = st_ref[1, :, cols]
            gout_ref[:, cols] = gate
        else:
            gbuf_ref[0:SUBLANES, :] = carry_ref[:, cols]
            gbuf_ref[SUBLANES:SUBLANES + tm, :] = gate
            g2 = gbuf_ref[SUBLANES - 2:SUBLANES - 2 + tm, :]
            g1 = gbuf_ref[SUBLANES - 1:SUBLANES - 1 + tm, :]
            carry_ref[:, cols] = gbuf_ref[tm:tm + SUBLANES, :]
        gc = (cw_ref[0:1, cols] * g2 + cw_ref[1:2, cols] * g1 + cw_ref[2:3, cols] * gate) + cb_ref[:, cols]
        act_ref[:, cols] = (_silu(gc) * up).astype(BF16)
    if not stateful:
        gout_ref[...] = carry_ref[...]

    x2 = x1 + gt2_ref[...] * jnp.dot(act_ref[...], wd_ref[...], preferred_element_type=F32)
    y_ref[...] = _rms(x2, fnw_ref[...])


def _dense(og, ob, gab, x, mods, vecs, ws, st, tm):
    b_, l_, _ = x.shape
    r_ = mods[0].shape[1]
    rt = 1 if r_ == 1 else tm
    mod_map = (lambda b, t: (b, 0, 0)) if r_ == 1 else (lambda b, t: (b, t, 0))
    row_map = lambda b, t: (b, t, 0)
    stateful = st is not None
    in_specs = ([pl.BlockSpec((None, tm, D_MODEL), row_map),
                 pl.BlockSpec((None, tm, D_MODEL), row_map),
                 pl.BlockSpec((None, tm, 2 * D_MODEL), row_map),
                 pl.BlockSpec((None, tm, D_MODEL), row_map)]
                + [pl.BlockSpec((None, rt, D_MODEL), mod_map)] * 4
                + [_const_spec(a.shape) for a in vecs[:2]]
                + [_const_spec(ws[0].shape), _const_spec(ws[1].shape), _const_spec(ws[2].shape),
                   _const_spec(ws[3].shape), _const_spec(ws[4].shape),
                   _const_spec(vecs[2].shape), _const_spec(vecs[3].shape), _const_spec(ws[5].shape)])
    args = [og, ob, gab, x, *mods, vecs[0], vecs[1], ws[0], ws[1], ws[2], ws[3], ws[4], vecs[2], vecs[3], ws[5]]
    scratch = [pltpu.VMEM((tm, D_FF), BF16)]
    if stateful:
        in_specs.append(pl.BlockSpec((FFN_CONV - 1, None, tm, D_FF), lambda b, t: (0, b, t, 0)))
        args.append(st)
        gout_shape = jax.ShapeDtypeStruct((b_, l_, D_FF), F32)
        gout_spec = pl.BlockSpec((None, tm, D_FF), row_map)
    else:
        scratch += [pltpu.VMEM((tm + SUBLANES, FFN_COLS), F32), pltpu.VMEM((SUBLANES, D_FF), F32)]
        gout_shape = jax.ShapeDtypeStruct((b_, SUBLANES, D_FF), F32)
        gout_spec = pl.BlockSpec((None, SUBLANES, D_FF), lambda b, t: (b, 0, 0))
    return pl.pallas_call(
        functools.partial(_dense_body, stateful, tm),
        grid=(b_, l_ // tm),
        in_specs=in_specs,
        out_specs=(pl.BlockSpec((None, tm, D_MODEL), row_map), gout_spec),
        out_shape=(jax.ShapeDtypeStruct((b_, l_, D_MODEL), F32), gout_shape),
        scratch_shapes=scratch,
        compiler_params=_params(("arbitrary", "arbitrary")),
        name="dense_step" if stateful else "dense_prompt",
    )(*args)


def _lane_row(values, offset):
    return jnp.zeros((1, LANES), F32).at[0, offset:offset + values.shape[0]].set(values)


def kernel(x_prompt, x_sample, c_prompt, c_sample, state_gdn_S, state_gdn_conv, cache_swa_k, cache_swa_v,
           state_ffn_conv, w_mod, b_mod, norm1_w, norm2_w, w_in, gdn_conv_w, gdn_a_log, gdn_dt_bias,
           gdn_onorm_w, w_gdn_out, swa_sinks, w_swa_out, w_o, w_ffn_gate, w_ffn_up, ffn_conv_w, ffn_conv_b,
           w_ffn_down, final_norm_w):
    assert w_mod.shape[0] == 1, "single-layer trunk"
    nb, seq, _ = x_prompt.shape
    ns = x_sample.shape[0]
    assert x_sample.shape[1] == 1

    in_ws = _in_weight(jnp.transpose(w_in[0]))
    dense_ws = (w_gdn_out[0].astype(BF16), w_swa_out[0].astype(BF16), w_o[0].astype(BF16),
                w_ffn_gate[0].astype(BF16), w_ffn_up[0].astype(BF16), w_ffn_down[0].astype(BF16))
    dense_vecs = (norm2_w, final_norm_w[None, :], ffn_conv_w[0], ffn_conv_b)
    cw = jnp.transpose(gdn_conv_w[0].reshape(GDN_CONV, GDN_SECTIONS, LANES), (1, 0, 2))
    alog_row = _lane_row(gdn_a_log[0], GDN_HEADS)
    dtb_row = _lane_row(gdn_dt_bias[0], GDN_HEADS)

    mod = _modulation(jnp.concatenate([c_prompt, c_sample], axis=0), w_mod[0], b_mod)
    mod_p = [mod[:nb, i * D_MODEL:(i + 1) * D_MODEL][:, None, :] for i in range(6)]
    mod_s = [mod[nb:, i * D_MODEL:(i + 1) * D_MODEL][None, :, :] for i in range(6)]

    qkvf, gact, ba, sq, skv, gab, qkv_tail = _inproj(x_prompt, mod_p[0], mod_p[1], norm1_w, in_ws, cw, tm=256)
    og, gdn_s_p = _gdn_prompt(qkvf, gact, ba, alog_row, dtb_row, gdn_onorm_w, _level_masks(), lt=2 * CHUNK)
    ob = _swa_prompt(sq, skv, swa_sinks[0], nq=2)
    y_p, gate_tail = _dense(og, ob, gab, x_prompt, (mod_p[2], mod_p[3], mod_p[4], mod_p[5]),
                            dense_vecs, dense_ws, None, tm=256)
    gdn_conv_p = jnp.transpose(qkv_tail[:, :, SUBLANES - (GDN_CONV - 1):, :], (0, 2, 1, 3)).reshape(
        nb, GDN_CONV - 1, GDN_CONV_CH)
    k_p = skv[:, seq - WINDOW:, :SWA_KV].reshape(nb, WINDOW, SWA_KV_HEADS, SWA_HD)
    v_p = skv[:, seq - WINDOW:, SWA_KV:].reshape(nb, WINDOW, SWA_KV_HEADS, SWA_HD)
    ffn_conv_p = gate_tail[:, SUBLANES - (FFN_CONV - 1):, :]

    xs = x_sample.reshape(1, ns, D_MODEL)
    qkvs, gates, bas, sqs, skvs, gabs = _inproj(xs, mod_s[0], mod_s[1], norm1_w, in_ws, None, tm=ns)
    st_gdn = jnp.transpose(state_gdn_conv[0], (1, 0, 2))
    og_s, gdn_s_s = _gdn_step(qkvs[0], st_gdn, gates[0], bas[0], cw, alog_row, dtb_row, gdn_onorm_w,
                              state_gdn_S[0], bb=8)
    to_channel_major = lambda c: jnp.transpose(c, (0, 2, 3, 1)).reshape(ns, SWA_KV, WINDOW)
    from_channel_major = lambda c: jnp.transpose(c.reshape(ns, SWA_KV_HEADS, SWA_HD, WINDOW), (0, 3, 1, 2))
    kvn_t = jnp.pad(jnp.transpose(skvs[0]), ((0, 0), (0, WINDOW - ns)))
    o3, k_s, v_s = _swa_step(sqs[0].reshape(ns, SWA_Q_HEADS, SWA_HD), kvn_t,
                             to_channel_major(cache_swa_k[0]), to_channel_major(cache_swa_v[0]),
                             swa_sinks[0][:, None], bb=8)
    ob_s = o3.reshape(1, ns, SWA_Q).astype(BF16)
    st_ffn = jnp.transpose(state_ffn_conv[0], (1, 0, 2))[:, None]
    y_s, gate_new = _dense(og_s[None], ob_s, gabs, xs, (mod_s[2], mod_s[3], mod_s[4], mod_s[5]),
                           dense_vecs, dense_ws, st_ffn, tm=ns)
    gdn_conv_s = jnp.concatenate([state_gdn_conv[0][:, 1:], qkvs[0][:, None, :]], axis=1)
    ffn_conv_s = jnp.concatenate([state_ffn_conv[0][:, 1:], gate_new[0][:, None, :]], axis=1)

    return (y_p, y_s.reshape(ns, 1, D_MODEL),
            gdn_s_p[None], gdn_s_s[None],
            gdn_conv_p[None], gdn_conv_s[None],
            k_p[None], from_channel_major(k_s)[None],
            v_p[None], from_channel_major(v_s)[None],
            ffn_conv_p[None], ffn_conv_s[None])
```

```python
import functools
import math

import numpy as np
import jax
import jax.numpy as jnp
from jax import lax
from jax.experimental import pallas as pl
from jax.experimental.pallas import tpu as pltpu

F32 = jnp.float32
BF16 = jnp.bfloat16

D_MODEL = 1024
GDN_HEADS = 8
GDN_DK = 128
GDN_DV = 128
GDN_QK = GDN_HEADS * GDN_DK
GDN_V = GDN_HEADS * GDN_DV
GDN_CONV = 4
GDN_CONV_CH = 2 * GDN_QK + GDN_V
GDN_SECTIONS = GDN_CONV_CH // 128
SWA_Q_HEADS = 16
SWA_KV_HEADS = 4
SWA_GROUP = SWA_Q_HEADS // SWA_KV_HEADS
SWA_HD = 64
SWA_Q = SWA_Q_HEADS * SWA_HD
SWA_KV = SWA_KV_HEADS * SWA_HD
WINDOW = 128
D_FF = 2816
FFN_CONV = 3
EPS = 1e-6

LANES = 128
SUBLANES = 8
VMEM_LIMIT = 56 * 1024 * 1024

COL_QKV = 0
COL_GATE = COL_QKV + GDN_CONV_CH
COL_SQ = COL_GATE + GDN_V
COL_SKV = COL_SQ + SWA_Q
COL_GAB = COL_SKV + 2 * SWA_KV
COL_BA = COL_GAB + 2 * D_MODEL
IN_COLS = COL_BA + LANES

SWA_Q_SCALE = SWA_HD ** -0.5 * math.log2(math.e)

CHUNK = 128
FFN_COLS = 256


def _mm(a, b):
    return jnp.dot(a.astype(BF16), b.astype(BF16), preferred_element_type=F32)


def _mm_nt(a, b):
    return lax.dot_general(a.astype(BF16), b.astype(BF16), (((1,), (1,)), ((), ())),
                           preferred_element_type=F32)


def _silu(x):
    return x * jax.nn.sigmoid(x)


def _softplus(x):
    return jnp.maximum(x, 0.0) + jnp.log1p(jnp.exp(-jnp.abs(x)))


def _rms(x, w):
    return x * lax.rsqrt(jnp.mean(x * x, axis=-1, keepdims=True) + EPS) * w


def _const_spec(shape):
    n = len(shape)
    return pl.BlockSpec(shape, lambda *_: (0,) * n, pipeline_mode=pl.Buffered(1))


def _params(sem):
    return pltpu.CompilerParams(dimension_semantics=sem, vmem_limit_bytes=VMEM_LIMIT)


def _mod_body(c_ref, w_ref, b_ref, o_ref):
    o_ref[...] = _mm(_silu(c_ref[...]), w_ref[...]) + b_ref[...]


def _modulation(c_all, w_mod, b_mod):
    rows = c_all.shape[0]
    n_out = w_mod.shape[1]
    tn = D_MODEL
    return pl.pallas_call(
        _mod_body,
        grid=(n_out // tn,),
        in_specs=[pl.BlockSpec((rows, D_MODEL), lambda j: (0, 0)),
                  pl.BlockSpec((D_MODEL, tn), lambda j: (0, j)),
                  pl.BlockSpec((1, tn), lambda j: (0, j))],
        out_specs=pl.BlockSpec((rows, tn), lambda j: (0, j)),
        out_shape=jax.ShapeDtypeStruct((rows, n_out), F32),
        compiler_params=_params(("arbitrary",)),
        name="modulation",
    )(c_all, w_mod, b_mod)


IN_WEIGHT_COLS = 256


def _in_weight_body(n_main, wt_ref, ba_ref, o_ref):
    j = pl.program_id(0)

    @pl.when(j < n_main)
    def _():
        o_ref[...] = wt_ref[...].T.astype(BF16)

    @pl.when(j == n_main)
    def _():
        n_ba = ba_ref.shape[0]
        ba = jnp.concatenate([ba_ref[...].T, jnp.zeros((D_MODEL, IN_WEIGHT_COLS - n_ba), F32)], axis=1)
        o_ref[...] = ba.astype(BF16)


def _in_weight(w_t):
    n_ba = 2 * GDN_HEADS
    split = GDN_CONV_CH + GDN_V
    tc = IN_WEIGHT_COLS
    n_main = COL_BA // tc

    def src_row(j):
        jj = jnp.minimum(j, n_main - 1)
        return pl.multiple_of(jnp.where(jj * tc < split, jj * tc, jj * tc + n_ba), n_ba)

    return pl.pallas_call(
        functools.partial(_in_weight_body, n_main),
        grid=(n_main + 1,),
        in_specs=[pl.BlockSpec((pl.Element(tc), pl.Element(D_MODEL)), lambda j: (src_row(j), 0)),
                  pl.BlockSpec((pl.Element(n_ba), pl.Element(D_MODEL)), lambda j: (split, 0))],
        out_specs=pl.BlockSpec((D_MODEL, tc), lambda j: (0, j)),
        out_shape=jax.ShapeDtypeStruct((D_MODEL, COL_BA + tc), BF16),
        compiler_params=_params(("arbitrary",)),
        name="in_weight",
    )(w_t, w_t)


def _l2norm(x):
    return x * lax.rsqrt(jnp.sum(x * x, axis=-1, keepdims=True) + EPS)


def _inproj_body(seq_rows, tm, x_ref, sh_ref, sc_ref, nw_ref, w_ref, *rest):
    if seq_rows:
        cw_ref, qkv_ref, gg_ref, ba_ref, sq_ref, skv_ref, gab_ref, tail_ref, xe_ref = rest

        @pl.when(pl.program_id(1) == 0)
        def _():
            xe_ref[:, 0:SUBLANES, :] = jnp.zeros((GDN_SECTIONS, SUBLANES, LANES), F32)
    else:
        qkv_ref, gg_ref, ba_ref, sq_ref, skv_ref, gab_ref = rest

    h = _rms(x_ref[...], nw_ref[...]) * (1.0 + sc_ref[...]) + sh_ref[...]
    hb = h.astype(BF16)

    def proj(lo, width):
        return jnp.dot(hb, w_ref[:, lo:lo + width], preferred_element_type=F32)

    step = 512
    per = step // LANES
    for c in range(GDN_CONV_CH // step):
        z = proj(COL_QKV + c * step, step)
        for k in range(per):
            s = c * per + k
            zs = z[:, k * LANES:(k + 1) * LANES]
            if not seq_rows:
                qkv_ref[:, s * LANES:(s + 1) * LANES] = zs
                continue
            xe_ref[s, SUBLANES:SUBLANES + tm, :] = zs
            w = cw_ref[s]
            y = w[0:1] * xe_ref[s, SUBLANES - 3:SUBLANES - 3 + tm, :]
            for tap in range(1, GDN_CONV):
                lo = SUBLANES - 3 + tap
                y = y + w[tap:tap + 1] * xe_ref[s, lo:lo + tm, :]
            xe_ref[s, 0:SUBLANES, :] = xe_ref[s, tm:tm + SUBLANES, :]
            f = _silu(y)
            if s < GDN_HEADS:
                f = _l2norm(f) * (GDN_DK ** -0.5)
            elif s < 2 * GDN_HEADS:
                f = _l2norm(f)
            qkv_ref[s] = f
    if seq_rows:
        tail_ref[...] = xe_ref[:, 0:SUBLANES, :]
    for c in range(GDN_V // step):
        z = proj(COL_GATE + c * step, step)
        for k in range(per):
            zs = z[:, k * LANES:(k + 1) * LANES]
            if seq_rows:
                gg_ref[c * per + k] = _silu(zs)
            else:
                gg_ref[:, (c * per + k) * LANES:(c * per + k + 1) * LANES] = zs
    ba_ref[...] = proj(COL_BA, LANES)
    for c in range(SWA_Q // step):
        sq_ref[:, c * step:(c + 1) * step] = proj(COL_SQ + c * step, step)
    skv_ref[...] = proj(COL_SKV, 2 * SWA_KV)
    for c in range(2 * D_MODEL // step):
        gab_ref[:, c * step:(c + 1) * step] = proj(COL_GAB + c * step, step)


def _inproj(x, sh, sc, nw, w_all, cw, tm):
    b_, l_, _ = x.shape
    r_ = sh.shape[1]
    rt = 1 if r_ == 1 else tm
    mod_map = (lambda b, t: (b, 0, 0)) if r_ == 1 else (lambda b, t: (b, t, 0))
    row_map = lambda b, t: (b, t, 0)
    head_map = lambda b, t: (b, 0, t, 0)
    seq_rows = cw is not None
    if seq_rows:
        gdn_shapes = (jax.ShapeDtypeStruct((b_, GDN_SECTIONS, l_, LANES), F32),
                      jax.ShapeDtypeStruct((b_, GDN_HEADS, l_, LANES), F32))
        gdn_specs = (pl.BlockSpec((None, GDN_SECTIONS, tm, LANES), head_map),
                     pl.BlockSpec((None, GDN_HEADS, tm, LANES), head_map))
    else:
        gdn_shapes = (jax.ShapeDtypeStruct((b_, l_, GDN_CONV_CH), F32),
                      jax.ShapeDtypeStruct((b_, l_, GDN_V), F32))
        gdn_specs = (pl.BlockSpec((None, tm, GDN_CONV_CH), row_map),
                     pl.BlockSpec((None, tm, GDN_V), row_map))
    out_shape = gdn_shapes + (
        jax.ShapeDtypeStruct((b_, l_, LANES), F32),
        jax.ShapeDtypeStruct((b_, l_, SWA_Q), F32),
        jax.ShapeDtypeStruct((b_, l_, 2 * SWA_KV), F32),
        jax.ShapeDtypeStruct((b_, l_, 2 * D_MODEL), F32),
    )
    out_specs = gdn_specs + (
        pl.BlockSpec((None, tm, LANES), row_map),
        pl.BlockSpec((None, tm, SWA_Q), row_map),
        pl.BlockSpec((None, tm, 2 * SWA_KV), row_map),
        pl.BlockSpec((None, tm, 2 * D_MODEL), row_map),
    )
    in_specs = [
        pl.BlockSpec((None, tm, D_MODEL), row_map),
        pl.BlockSpec((None, rt, D_MODEL), mod_map),
        pl.BlockSpec((None, rt, D_MODEL), mod_map),
        _const_spec(nw.shape),
        _const_spec(w_all.shape),
    ]
    args = [x, sh, sc, nw, w_all]
    scratch = []
    if seq_rows:
        in_specs.append(_const_spec(cw.shape))
        args.append(cw)
        out_shape += (jax.ShapeDtypeStruct((b_, GDN_SECTIONS, SUBLANES, LANES), F32),)
        out_specs += (pl.BlockSpec((None, GDN_SECTIONS, SUBLANES, LANES), lambda b, t: (b, 0, 0, 0)),)
        scratch.append(pltpu.VMEM((GDN_SECTIONS, tm + SUBLANES, LANES), F32))
    return pl.pallas_call(
        functools.partial(_inproj_body, seq_rows, tm),
        grid=(b_, l_ // tm),
        in_specs=in_specs,
        out_specs=out_specs,
        out_shape=out_shape,
        scratch_shapes=scratch,
        compiler_params=_params(("arbitrary", "arbitrary")),
        name="inproj_seq" if seq_rows else "inproj_rows",
    )(*args)


def _delta_gates(ba, alog_row, dtb_row):
    beta_all = jax.nn.sigmoid(ba)
    g_all = -jnp.exp(alog_row) * _softplus(ba + dtb_row)
    return beta_all, g_all


def _lane_column(x, lane_idx, lane):
    return jnp.sum(jnp.where(lane_idx == lane, x, 0.0), axis=1, keepdims=True)


def _level_masks():
    r = np.arange(CHUNK)[:, None]
    c = np.arange(CHUNK)[None, :]
    masks = [(r == c + 1) & (r % 2 == 1)]
    half = 2
    while half < CHUNK:
        full = 2 * half
        masks.append((r // full == c // full) & (r % full >= half) & (c % full < half))
        half = full
    return jnp.asarray(np.stack(masks), dtype=BF16)


def _unit_lower_inverses(ms, masks_ref, eye):
    ts = [eye - m * masks_ref[0] for m in ms]
    for lvl in range(1, masks_ref.shape[0]):
        off = masks_ref[lvl]
        xs = [jnp.dot(m * off, t, preferred_element_type=F32).astype(BF16) for m, t in zip(ms, ts)]
        ys = [jnp.dot(t, x, preferred_element_type=F32).astype(BF16) for t, x in zip(ts, xs)]
        ts = [t - y for t, y in zip(ts, ys)]
    return ts


def _cumsum_rows(g, ltri):
    hi = g.astype(BF16)
    r1 = g - hi.astype(F32)
    mid = r1.astype(BF16)
    lo = (r1 - mid.astype(F32)).astype(BF16)
    return (jnp.dot(ltri, hi, preferred_element_type=F32) + jnp.dot(ltri, mid, preferred_element_type=F32)
            + jnp.dot(ltri, lo, preferred_element_type=F32))


def _gated_out_norm(o, gate_act, onw):
    on = o * lax.rsqrt(jnp.mean(o * o, axis=-1, keepdims=True) + EPS) * onw
    return on * gate_act


def _gdn_prompt_body(lt, q_ref, k_ref, v_ref, ba_ref, alog_ref, dtb_ref, gate_ref, onw_ref, masks_ref,
                     og_ref, s_ref):
    @pl.when(pl.program_id(1) == 0)
    def _():
        s_ref[...] = jnp.zeros_like(s_ref)

    beta_all, g_all = _delta_gates(ba_ref[...], alog_ref[...], dtb_ref[...])
    lane_idx = lax.broadcasted_iota(jnp.int32, (CHUNK, LANES), 1)
    row = lax.broadcasted_iota(jnp.int32, (CHUNK, CHUNK), 0)
    col = lax.broadcasted_iota(jnp.int32, (CHUNK, CHUNK), 1)
    tril = row >= col
    strict = row > col
    ltri = jnp.where(tril, 1.0, 0.0).astype(BF16)
    eye = jnp.where(row == col, 1.0, 0.0).astype(BF16)
    onw = onw_ref[...]
    heads = range(GDN_HEADS)
    chunks = range(lt // CHUNK)

    blocks = [(c, j) for c in chunks for j in heads]
    pre = {}
    for c in chunks:
        rows = slice(c * CHUNK, (c + 1) * CHUNK)
        dec = _cumsum_rows(g_all[rows], ltri)
        dec_t = dec.T
        for j in heads:
            q, k, v = q_ref[j, rows, :], k_ref[j, rows, :], v_ref[j, rows, :]
            beta_col = _lane_column(beta_all[rows], lane_idx, j)
            dec_col = _lane_column(dec, lane_idx, GDN_HEADS + j)
            dec_row = dec_t[GDN_HEADS + j:GDN_HEADS + j + 1, :]
            dec_last = dec_row[:, CHUNK - 1:CHUNK]
            gam = jnp.exp(jnp.minimum(dec_col - dec_row, 0.0))
            e_col = jnp.exp(dec_col)
            kb = k * beta_col
            pre[c, j] = dict(q=q, k=k, gam=gam, kb=kb, qe=q * e_col, e_last=jnp.exp(dec_last),
                             kd=k * jnp.exp(dec_last - dec_col),
                             rhs=jnp.concatenate([v * beta_col, kb * e_col], axis=1).astype(BF16))
    grams = [_mm_nt(jnp.concatenate([pre[b]["kb"], pre[b]["q"]], axis=0), pre[b]["k"]) for b in blocks]
    ms = [jnp.where(strict, g[:CHUNK] * pre[b]["gam"], 0.0).astype(BF16) for g, b in zip(grams, blocks)]
    a_intra = {b: jnp.where(tril, g[CHUNK:] * pre[b]["gam"], 0.0) for g, b in zip(grams, blocks)}
    t_inv = _unit_lower_inverses(ms, masks_ref, eye)
    uw = {b: jnp.dot(t, pre[b]["rhs"], preferred_element_type=F32) for t, b in zip(t_inv, blocks)}

    for c in chunks:
        rows = slice(c * CHUNK, (c + 1) * CHUNK)
        s_prev = [s_ref[j] for j in heads]
        ws_qs = [_mm(jnp.concatenate([uw[c, j][:, GDN_DV:], pre[c, j]["qe"]], axis=0), s_prev[j]) for j in heads]
        v_new = [uw[c, j][:, :GDN_DV] - ws_qs[j][:CHUNK] for j in heads]
        o = [ws_qs[j][CHUNK:] + _mm(a_intra[c, j], v_new[j]) for j in heads]
        s_new = [s_prev[j] * pre[c, j]["e_last"] + _mm(pre[c, j]["kd"].T, v_new[j]) for j in heads]
        for j in heads:
            s_ref[j] = s_new[j]
            og = _gated_out_norm(o[j], gate_ref[j, rows, :], onw)
            og_ref[rows, j * GDN_DV:(j + 1) * GDN_DV] = og.astype(og_ref.dtype)


def _gdn_prompt(qkvf, gact, ba, alog_row, dtb_row, onw, masks, lt):
    b_, _, l_, _ = qkvf.shape
    sec = lambda s: pl.BlockSpec((None, GDN_HEADS, lt, LANES), lambda b, t, s=s: (b, s, t, 0))
    return pl.pallas_call(
        functools.partial(_gdn_prompt_body, lt),
        grid=(b_, l_ // lt),
        in_specs=[sec(0), sec(1), sec(2),
                  pl.BlockSpec((None, lt, LANES), lambda b, t: (b, t, 0)),
                  _const_spec(alog_row.shape), _const_spec(dtb_row.shape),
                  pl.BlockSpec((None, GDN_HEADS, lt, LANES), lambda b, t: (b, 0, t, 0)),
                  _const_spec(onw.shape), _const_spec(masks.shape)],
        out_specs=(pl.BlockSpec((None, lt, GDN_V), lambda b, t: (b, t, 0)),
                   pl.BlockSpec((None, GDN_HEADS, GDN_DK, GDN_DV), lambda b, t: (b, 0, 0, 0))),
        out_shape=(jax.ShapeDtypeStruct((b_, l_, GDN_V), BF16),
                   jax.ShapeDtypeStruct((b_, GDN_HEADS, GDN_DK, GDN_DV), F32)),
        compiler_params=_params(("arbitrary", "arbitrary")),
        name="gdn_prompt",
    )(qkvf, qkvf, qkvf, ba, alog_row, dtb_row, gact, onw, masks)


def _gdn_step_body(bb, x_ref, st_ref, cw_ref, ba_ref, alog_ref, dtb_ref, gate_ref, onw_ref, s0_ref,
                   og_ref, sn_ref, q_s, k_s, v_s, b_s, e_s, o_s):
    beta_all, g_all = _delta_gates(ba_ref[...], alog_ref[...], dtb_ref[...])
    lane_idx = lax.broadcasted_iota(jnp.int32, (bb, LANES), 1)
    for h in range(GDN_HEADS):
        feats = []
        for s in range(3):
            idx = s * GDN_HEADS + h
            cols = slice(idx * LANES, (idx + 1) * LANES)
            w = cw_ref[idx]
            y = w[0:1] * st_ref[0, :, cols]
            for tap in range(1, GDN_CONV - 1):
                y = y + w[tap:tap + 1] * st_ref[tap, :, cols]
            y = y + w[GDN_CONV - 1:GDN_CONV] * x_ref[:, cols]
            feats.append(_silu(y))
        q, k, v = feats
        q_s[h] = q * lax.rsqrt(jnp.sum(q * q, axis=-1, keepdims=True) + EPS) * (GDN_DK ** -0.5)
        k_s[h] = k * lax.rsqrt(jnp.sum(k * k, axis=-1, keepdims=True) + EPS)
        v_s[h] = v
        b_s[h] = jnp.broadcast_to(_lane_column(beta_all, lane_idx, h), (bb, LANES))
        e_s[h] = jnp.broadcast_to(jnp.exp(_lane_column(g_all, lane_idx, h + GDN_HEADS)), (bb, LANES))

    eye = (lax.broadcasted_iota(jnp.int32, (GDN_DK, GDN_DK), 0)
           == lax.broadcasted_iota(jnp.int32, (GDN_DK, GDN_DK), 1))

    def to_col(r):
        return jnp.sum(jnp.where(eye, jnp.broadcast_to(r, (GDN_DK, GDN_DK)), 0.0), axis=1, keepdims=True)

    def seq_body(i, carry):
        for h in range(GDN_HEADS):
            one = pl.ds(i, 1)
            k_col = to_col(k_s[h, one, :])
            q_col = to_col(q_s[h, one, :])
            s1 = s0_ref[i, h] * e_s[h, one, :]
            ks = jnp.sum(s1 * k_col, axis=0, keepdims=True)
            delta = (v_s[h, one, :] - ks) * b_s[h, one, :]
            s2 = s1 + k_col * delta
            sn_ref[i, h] = s2
            o_s[h, one, :] = jnp.sum(s2 * q_col, axis=0, keepdims=True)
        return carry

    lax.fori_loop(0, bb, seq_body, 0)
    onw = onw_ref[...]
    for h in range(GDN_HEADS):
        cols = slice(h * GDN_DV, (h + 1) * GDN_DV)
        og = _gated_out_norm(o_s[h], _silu(gate_ref[:, cols]), onw)
        og_ref[:, cols] = og.astype(og_ref.dtype)


def _gdn_step(qkv, st, gate, ba, cw, alog_row, dtb_row, onw, s0, bb):
    n_ = ba.shape[0]
    vec = pltpu.VMEM((GDN_HEADS, bb, LANES), F32)
    return pl.pallas_call(
        functools.partial(_gdn_step_body, bb),
        grid=(n_ // bb,),
        in_specs=[pl.BlockSpec((bb, GDN_CONV_CH), lambda i: (i, 0)),
                  pl.BlockSpec((GDN_CONV - 1, bb, GDN_CONV_CH), lambda i: (0, i, 0)),
                  _const_spec(cw.shape),
                  pl.BlockSpec((bb, LANES), lambda i: (i, 0)),
                  _const_spec(alog_row.shape), _const_spec(dtb_row.shape),
                  pl.BlockSpec((bb, GDN_V), lambda i: (i, 0)),
                  _const_spec(onw.shape),
                  pl.BlockSpec((bb, GDN_HEADS, GDN_DK, GDN_DV), lambda i: (i, 0, 0, 0))],
        out_specs=(pl.BlockSpec((bb, GDN_V), lambda i: (i, 0)),
                   pl.BlockSpec((bb, GDN_HEADS, GDN_DK, GDN_DV), lambda i: (i, 0, 0, 0))),
        out_shape=(jax.ShapeDtypeStruct((n_, GDN_V), BF16),
                   jax.ShapeDtypeStruct(s0.shape, F32)),
        scratch_shapes=[vec, vec, vec, vec, vec, vec],
        compiler_params=_params(("arbitrary",)),
        name="gdn_step",
    )(qkv, st, cw, ba, alog_row, dtb_row, gate, onw, s0)


def _swa_prompt_body(nq, sinks_ref, q_ref, kvp_ref, kvc_ref, o_ref):
    n = pl.program_id(1)
    w = WINDOW
    tiles = SWA_KV // LANES
    pairs = 2
    lo_lane = lax.broadcasted_iota(jnp.int32, (w, LANES), 1) < SWA_HD
    lo_row = lax.broadcasted_iota(jnp.int32, (LANES, w), 0) < SWA_HD
    c = lax.broadcasted_iota(jnp.int32, (2 * w, pairs * w), 0)
    i = lax.broadcasted_iota(jnp.int32, (2 * w, pairs * w), 1) & (w - 1)
    banded = (c > i) & (c <= i + w)
    banded_first = banded & ((c >= w) | (n > 0))
    k_blk, vt_blk = [], []
    for j in range(nq + 1):
        src, rows = (kvp_ref, slice(0, w)) if j == 0 else (kvc_ref, slice((j - 1) * w, j * w))
        k_tiles, vt_tiles = [], []
        for t in range(tiles):
            kx = src[rows, t * LANES:(t + 1) * LANES]
            vt = src[rows, SWA_KV + t * LANES:SWA_KV + (t + 1) * LANES].T
            k_tiles.append((kx.astype(BF16), pltpu.roll(kx, SWA_HD, axis=1).astype(BF16)))
            vt_tiles.append((vt.astype(BF16),
                             jnp.concatenate([vt[SWA_HD:], vt[:SWA_HD]], axis=0).astype(BF16)))
        k_blk.append(k_tiles)
        vt_blk.append(vt_tiles)
    items = [(qb, g, p) for qb in range(nq) for g in range(SWA_KV_HEADS) for p in range(2)]
    log2e = math.log2(math.e)
    qm, kz, vzt, sink, valid = {}, {}, {}, {}, {}
    for qb, g, p in items:
        keep = lo_lane if p == 0 else jnp.logical_not(lo_lane)
        q_tiles = [q_ref[qb * w:(qb + 1) * w, (2 * g + r) * LANES:(2 * g + r + 1) * LANES] for r in range(pairs)]
        qm[qb, g, p] = jnp.concatenate([jnp.where(keep, x * SWA_Q_SCALE, 0.0) for x in q_tiles],
                                       axis=0).astype(BF16)
        variant = 0 if p == g % 2 else 1
        kz[qb, g, p] = jnp.concatenate([k_blk[qb + d][g // 2][variant] for d in range(2)], axis=0)
        vzt[qb, g, p] = jnp.concatenate([vt_blk[qb + d][g // 2][variant] for d in range(2)], axis=1)
        sink[qb, g, p] = jnp.concatenate([jnp.full((1, w), sinks_ref[SWA_GROUP * g + 2 * r + p] * log2e, F32)
                                          for r in range(pairs)], axis=1)
        valid[qb, g, p] = banded_first if qb == 0 else banded
    st = {b: jnp.where(valid[b], lax.dot_general(kz[b], qm[b], (((1,), (1,)), ((), ())),
                                                 preferred_element_type=F32), -jnp.inf) for b in items}
    m = {b: jnp.maximum(jnp.max(st[b], axis=0, keepdims=True), sink[b]) for b in items}
    et = {b: jnp.exp2(st[b] - m[b]) for b in items}
    den = {b: jnp.sum(et[b], axis=0, keepdims=True) + jnp.exp2(sink[b] - m[b]) for b in items}
    ot = {b: jnp.dot(vzt[b], et[b].astype(BF16), preferred_element_type=F32) / den[b] for b in items}
    for qb in range(nq):
        for g in range(SWA_KV_HEADS):
            for r in range(pairs):
                cols = slice(r * w, (r + 1) * w)
                tile_t = jnp.where(lo_row, ot[qb, g, 0][:, cols], ot[qb, g, 1][:, cols])
                o_ref[qb * w:(qb + 1) * w, (2 * g + r) * LANES:(2 * g + r + 1) * LANES] = (
                    tile_t.T.astype(o_ref.dtype))


def _swa_prompt(sq, skv, sinks, nq):
    b_, l_, _ = sq.shape
    rows = nq * WINDOW
    return pl.pallas_call(
        functools.partial(_swa_prompt_body, nq),
        grid=(b_, l_ // rows),
        in_specs=[pl.BlockSpec(memory_space=pltpu.SMEM),
                  pl.BlockSpec((None, rows, SWA_Q), lambda b, n: (b, n, 0)),
                  pl.BlockSpec((None, WINDOW, 2 * SWA_KV), lambda b, n: (b, jnp.maximum(n * nq - 1, 0), 0)),
                  pl.BlockSpec((None, rows, 2 * SWA_KV), lambda b, n: (b, n, 0))],
        out_specs=pl.BlockSpec((None, rows, SWA_Q), lambda b, n: (b, n, 0)),
        out_shape=jax.ShapeDtypeStruct((b_, l_, SWA_Q), BF16),
        compiler_params=_params(("arbitrary", "arbitrary")),
        name="swa_prompt",
    )(sinks, sq, skv, skv)


def _swa_step_body(bb, q_ref, kvn_ref, ck_ref, cv_ref, sink_ref, o_ref, nk_ref, nv_ref):
    w = WINDOW
    first = pl.program_id(0) * bb
    row = lax.broadcasted_iota(jnp.int32, (SWA_Q_HEADS, SWA_KV), 0)
    lane = lax.broadcasted_iota(jnp.int32, (SWA_Q_HEADS, SWA_KV), 1)
    own = (lane // SWA_HD) == (row // SWA_GROUP)
    newest = lax.broadcasted_iota(jnp.int32, (SWA_KV, w), 1) == w - 1
    sink = sink_ref[...]
    scale = SWA_HD ** -0.5
    seqs = range(bb)
    kn_all = kvn_ref[0:SWA_KV, :]
    vn_all = kvn_ref[SWA_KV:2 * SWA_KV, :]
    nk = [jnp.where(newest, pltpu.roll(kn_all, w - 1 - (first + i), axis=1), pltpu.roll(ck_ref[i], w - 1, axis=1))
          for i in seqs]
    nv = [jnp.where(newest, pltpu.roll(vn_all, w - 1 - (first + i), axis=1), pltpu.roll(cv_ref[i], w - 1, axis=1))
          for i in seqs]
    q_bd = [jnp.where(own, jnp.concatenate([q_ref[i]] * SWA_KV_HEADS, axis=1), 0.0) for i in seqs]
    s = [_mm(q_bd[i], nk[i]) * scale for i in seqs]
    m = [jnp.maximum(jnp.max(s[i], axis=1, keepdims=True), sink) for i in seqs]
    e = [jnp.exp(s[i] - m[i]) for i in seqs]
    den = [jnp.sum(e[i], axis=1, keepdims=True) + jnp.exp(sink - m[i]) for i in seqs]
    pv = [jnp.where(own, _mm_nt(e[i] / den[i], nv[i]), 0.0) for i in seqs]
    for i in seqs:
        o = pv[i][:, 0:SWA_HD]
        for g in range(1, SWA_KV_HEADS):
            o = o + pv[i][:, g * SWA_HD:(g + 1) * SWA_HD]
        o_ref[i] = o
        nk_ref[i] = nk[i]
        nv_ref[i] = nv[i]


def _swa_step(q3, kvn_t, ck_t, cv_t, sink_col, bb):
    n_ = q3.shape[0]
    assert n_ <= WINDOW
    cache = pl.BlockSpec((bb, SWA_KV, WINDOW), lambda i: (i, 0, 0))
    return pl.pallas_call(
        functools.partial(_swa_step_body, bb),
        grid=(n_ // bb,),
        in_specs=[pl.BlockSpec((bb, SWA_Q_HEADS, SWA_HD), lambda i: (i, 0, 0)),
                  _const_spec(kvn_t.shape),
                  cache, cache,
                  _const_spec(sink_col.shape)],
        out_specs=(pl.BlockSpec((bb, SWA_Q_HEADS, SWA_HD), lambda i: (i, 0, 0)), cache, cache),
        out_shape=(jax.ShapeDtypeStruct(q3.shape, F32),
                   jax.ShapeDtypeStruct(ck_t.shape, F32),
                   jax.ShapeDtypeStruct(cv_t.shape, F32)),
        compiler_params=_params(("arbitrary",)),
        name="swa_step",
    )(q3, kvn_t, ck_t, cv_t, sink_col)


def _dense_body(stateful, tm, og_ref, ob_ref, gab_ref, x_ref, gt1_ref, sh2_ref, sc2_ref, gt2_ref,
                n2w_ref, fnw_ref, wa_ref, wb_ref, wo_ref, wg_ref, wu_ref, cw_ref, cb_ref, wd_ref, *rest):
    if stateful:
        st_ref, y_ref, gout_ref, act_ref = rest
    else:
        y_ref, gout_ref, act_ref, gbuf_ref, carry_ref = rest

        @pl.when(pl.program_id(1) == 0)
        def _():
            carry_ref[...] = jnp.zeros_like(carry_ref)

    y_a = jnp.dot(og_ref[...], wa_ref[...], preferred_element_type=F32)
    y_b = jnp.dot(ob_ref[...], wb_ref[...], preferred_element_type=F32)
    merged = (jax.nn.sigmoid(gab_ref[:, 0:D_MODEL]) * y_a
              + jax.nn.sigmoid(gab_ref[:, D_MODEL:2 * D_MODEL]) * y_b)
    x1 = x_ref[...] + gt1_ref[...] * _mm(merged, wo_ref[...])
    h2 = (_rms(x1, n2w_ref[...]) * (1.0 + sc2_ref[...]) + sh2_ref[...]).astype(BF16)

    for c in range(D_FF // FFN_COLS):
        cols = slice(c * FFN_COLS, (c + 1) * FFN_COLS)
        gate = jnp.dot(h2, wg_ref[:, cols], preferred_element_type=F32)
        up = jnp.dot(h2, wu_ref[:, cols], preferred_element_type=F32)
        if stateful:
            g2 = st_ref[0, :, cols]
            g1 = st_ref[1, :, cols]
            gout_ref[:, cols] = gate
        else:
            gbuf_ref[0:SUBLANES, :] = carry_ref[:, cols]
            gbuf_ref[SUBLANES:SUBLANES + tm, :] = gate
            g2 = gbuf_ref[SUBLANES - 2:SUBLANES - 2 + tm, :]
            g1 = gbuf_ref[SUBLANES - 1:SUBLANES - 1 + tm, :]
            carry_ref[:, cols] = gbuf_ref[tm:tm + SUBLANES, :]
        gc = (cw_ref[0:1, cols] * g2 + cw_ref[1:2, cols] * g1 + cw_ref[2:3, cols] * gate) + cb_ref[:, cols]
        act_ref[:, cols] = (_silu(gc) * up).astype(BF16)
    if not stateful:
        gout_ref[...] = carry_ref[...]

    x2 = x1 + gt2_ref[...] * jnp.dot(act_ref[...], wd_ref[...], preferred_element_type=F32)
    y_ref[...] = _rms(x2, fnw_ref[...])


def _dense(og, ob, gab, x, mods, vecs, ws, st, tm):
    b_, l_, _ = x.shape
    r_ = mods[0].shape[1]
    rt = 1 if r_ == 1 else tm
    mod_map = (lambda b, t: (b, 0, 0)) if r_ == 1 else (lambda b, t: (b, t, 0))
    row_map = lambda b, t: (b, t, 0)
    stateful = st is not None
    in_specs = ([pl.BlockSpec((None, tm, D_MODEL), row_map),
                 pl.BlockSpec((None, tm, D_MODEL), row_map),
                 pl.BlockSpec((None, tm, 2 * D_MODEL), row_map),
                 pl.BlockSpec((None, tm, D_MODEL), row_map)]
                + [pl.BlockSpec((None, rt, D_MODEL), mod_map)] * 4
                + [_const_spec(a.shape) for a in vecs[:2]]
                + [_const_spec(ws[0].shape), _const_spec(ws[1].shape), _const_spec(ws[2].shape),
                   _const_spec(ws[3].shape), _const_spec(ws[4].shape),
                   _const_spec(vecs[2].shape), _const_spec(vecs[3].shape), _const_spec(ws[5].shape)])
    args = [og, ob, gab, x, *mods, vecs[0], vecs[1], ws[0], ws[1], ws[2], ws[3], ws[4], vecs[2], vecs[3], ws[5]]
    scratch = [pltpu.VMEM((tm, D_FF), BF16)]
    if stateful:
        in_specs.append(pl.BlockSpec((FFN_CONV - 1, None, tm, D_FF), lambda b, t: (0, b, t, 0)))
        args.append(st)
        gout_shape = jax.ShapeDtypeStruct((b_, l_, D_FF), F32)
        gout_spec = pl.BlockSpec((None, tm, D_FF), row_map)
    else:
        scratch += [pltpu.VMEM((tm + SUBLANES, FFN_COLS), F32), pltpu.VMEM((SUBLANES, D_FF), F32)]
        gout_shape = jax.ShapeDtypeStruct((b_, SUBLANES, D_FF), F32)
        gout_spec = pl.BlockSpec((None, SUBLANES, D_FF), lambda b, t: (b, 0, 0))
    return pl.pallas_call(
        functools.partial(_dense_body, stateful, tm),
        grid=(b_, l_ // tm),
        in_specs=in_specs,
        out_specs=(pl.BlockSpec((None, tm, D_MODEL), row_map), gout_spec),
        out_shape=(jax.ShapeDtypeStruct((b_, l_, D_MODEL), F32), gout_shape),
        scratch_shapes=scratch,
        compiler_params=_params(("arbitrary", "arbitrary")),
        name="dense_step" if stateful else "dense_prompt",
    )(*args)


def _lane_row(values, offset):
    return jnp.zeros((1, LANES), F32).at[0, offset:offset + values.shape[0]].set(values)


def kernel(x_prompt, x_sample, c_prompt, c_sample, state_gdn_S, state_gdn_conv, cache_swa_k, cache_swa_v,
           state_ffn_conv, w_mod, b_mod, norm1_w, norm2_w, w_in, gdn_conv_w, gdn_a_log, gdn_dt_bias,
           gdn_onorm_w, w_gdn_out, swa_sinks, w_swa_out, w_o, w_ffn_gate, w_ffn_up, ffn_conv_w, ffn_conv_b,
           w_ffn_down, final_norm_w):
    assert w_mod.shape[0] == 1, "single-layer trunk"
    nb, seq, _ = x_prompt.shape
    ns = x_sample.shape[0]
    assert x_sample.shape[1] == 1

    in_ws = _in_weight(jnp.transpose(w_in[0]))
    dense_ws = (w_gdn_out[0].astype(BF16), w_swa_out[0].astype(BF16), w_o[0].astype(BF16),
                w_ffn_gate[0].astype(BF16), w_ffn_up[0].astype(BF16), w_ffn_down[0].astype(BF16))
    dense_vecs = (norm2_w, final_norm_w[None, :], ffn_conv_w[0], ffn_conv_b)
    cw = jnp.transpose(gdn_conv_w[0].reshape(GDN_CONV, GDN_SECTIONS, LANES), (1, 0, 2))
    alog_row = _lane_row(gdn_a_log[0], GDN_HEADS)
    dtb_row = _lane_row(gdn_dt_bias[0], GDN_HEADS)

    mod = _modulation(jnp.concatenate([c_prompt, c_sample], axis=0), w_mod[0], b_mod)
    mod_p = [mod[:nb, i * D_MODEL:(i + 1) * D_MODEL][:, None, :] for i in range(6)]
    mod_s = [mod[nb:, i * D_MODEL:(i + 1) * D_MODEL][None, :, :] for i in range(6)]

    qkvf, gact, ba, sq, skv, gab, qkv_tail = _inproj(x_prompt, mod_p[0], mod_p[1], norm1_w, in_ws, cw, tm=256)
    og, gdn_s_p = _gdn_prompt(qkvf, gact, ba, alog_row, dtb_row, gdn_onorm_w, _level_masks(), lt=4 * CHUNK)
    ob = _swa_prompt(sq, skv, swa_sinks[0], nq=4)
    y_p, gate_tail = _dense(og, ob, gab, x_prompt, (mod_p[2], mod_p[3], mod_p[4], mod_p[5]),
                            dense_vecs, dense_ws, None, tm=512)
    gdn_conv_p = jnp.transpose(qkv_tail[:, :, SUBLANES - (GDN_CONV - 1):, :], (0, 2, 1, 3)).reshape(
        nb, GDN_CONV - 1, GDN_CONV_CH)
    k_p = skv[:, seq - WINDOW:, :SWA_KV].reshape(nb, WINDOW, SWA_KV_HEADS, SWA_HD)
    v_p = skv[:, seq - WINDOW:, SWA_KV:].reshape(nb, WINDOW, SWA_KV_HEADS, SWA_HD)
    ffn_conv_p = gate_tail[:, SUBLANES - (FFN_CONV - 1):, :]

    xs = x_sample.reshape(1, ns, D_MODEL)
    qkvs, gates, bas, sqs, skvs, gabs = _inproj(xs, mod_s[0], mod_s[1], norm1_w, in_ws, None, tm=ns)
    st_gdn = jnp.transpose(state_gdn_conv[0], (1, 0, 2))
    og_s, gdn_s_s = _gdn_step(qkvs[0], st_gdn, gates[0], bas[0], cw, alog_row, dtb_row, gdn_onorm_w,
                              state_gdn_S[0], bb=8)
    to_channel_major = lambda c: jnp.transpose(c, (0, 2, 3, 1)).reshape(ns, SWA_KV, WINDOW)
    from_channel_major = lambda c: jnp.transpose(c.reshape(ns, SWA_KV_HEADS, SWA_HD, WINDOW), (0, 3, 1, 2))
    kvn_t = jnp.pad(jnp.transpose(skvs[0]), ((0, 0), (0, WINDOW - ns)))
    o3, k_s, v_s = _swa_step(sqs[0].reshape(ns, SWA_Q_HEADS, SWA_HD), kvn_t,
                             to_channel_major(cache_swa_k[0]), to_channel_major(cache_swa_v[0]),
                             swa_sinks[0][:, None], bb=8)
    ob_s = o3.reshape(1, ns, SWA_Q).astype(BF16)
    st_ffn = jnp.transpose(state_ffn_conv[0], (1, 0, 2))[:, None]
    y_s, gate_new = _dense(og_s[None], ob_s, gabs, xs, (mod_s[2], mod_s[3], mod_s[4], mod_s[5]),
                           dense_vecs, dense_ws, st_ffn, tm=ns)
    gdn_conv_s = jnp.concatenate([state_gdn_conv[0][:, 1:], qkvs[0][:, None, :]], axis=1)
    ffn_conv_s = jnp.concatenate([state_ffn_conv[0][:, 1:], gate_new[0][:, None, :]], axis=1)

    return (y_p, y_s.reshape(ns, 1, D_MODEL),
            gdn_s_p[None], gdn_s_s[None],
            gdn_conv_p[None], gdn_conv_s[None],
            k_p[None], from_channel_major(k_s)[None],
            v_p[None], from_channel_major(v_s)[None],
            ffn_conv_p[None], ffn_conv_s[None])
```

```python
import functools
import math

import numpy as np
import jax
import jax.numpy as jnp
from jax import lax
from jax.experimental import pallas as pl
from jax.experimental.pallas import tpu as pltpu

F32 = jnp.float32
BF16 = jnp.bfloat16

D_MODEL = 1024
GDN_HEADS = 8
GDN_DK = 128
GDN_DV = 128
GDN_QK = GDN_HEADS * GDN_DK
GDN_V = GDN_HEADS * GDN_DV
GDN_CONV = 4
GDN_CONV_CH = 2 * GDN_QK + GDN_V
GDN_SECTIONS = GDN_CONV_CH // 128
SWA_Q_HEADS = 16
SWA_KV_HEADS = 4
SWA_GROUP = SWA_Q_HEADS // SWA_KV_HEADS
SWA_HD = 64
SWA_Q = SWA_Q_HEADS * SWA_HD
SWA_KV = SWA_KV_HEADS * SWA_HD
WINDOW = 128
D_FF = 2816
FFN_CONV = 3
EPS = 1e-6

LANES = 128
SUBLANES = 8
VMEM_LIMIT = 56 * 1024 * 1024

COL_QKV = 0
COL_GATE = COL_QKV + GDN_CONV_CH
COL_SQ = COL_GATE + GDN_V
COL_SKV = COL_SQ + SWA_Q
COL_GAB = COL_SKV + 2 * SWA_KV
COL_BA = COL_GAB + 2 * D_MODEL
IN_COLS = COL_BA + LANES

SWA_Q_SCALE = SWA_HD ** -0.5 * math.log2(math.e)

CONV_ROWS = 64
CHUNK = 128
FFN_COLS = 256


def _mm(a, b):
    return jnp.dot(a.astype(BF16), b.astype(BF16), preferred_element_type=F32)


def _mm_nt(a, b):
    return lax.dot_general(a.astype(BF16), b.astype(BF16), (((1,), (1,)), ((), ())),
                           preferred_element_type=F32)


def _silu(x):
    return x * jax.nn.sigmoid(x)


def _softplus(x):
    return jnp.maximum(x, 0.0) + jnp.log1p(jnp.exp(-jnp.abs(x)))


def _rms(x, w):
    return x * lax.rsqrt(jnp.mean(x * x, axis=-1, keepdims=True) + EPS) * w


def _const_spec(shape):
    n = len(shape)
    return pl.BlockSpec(shape, lambda *_: (0,) * n, pipeline_mode=pl.Buffered(1))


def _params(sem):
    return pltpu.CompilerParams(dimension_semantics=sem, vmem_limit_bytes=VMEM_LIMIT)


def _mod_body(c_ref, w_ref, b_ref, o_ref):
    o_ref[...] = _mm(_silu(c_ref[...]), w_ref[...]) + b_ref[...]


def _modulation(c_all, w_mod, b_mod):
    rows = c_all.shape[0]
    n_out = w_mod.shape[1]
    tn = D_MODEL
    return pl.pallas_call(
        _mod_body,
        grid=(n_out // tn,),
        in_specs=[pl.BlockSpec((rows, D_MODEL), lambda j: (0, 0)),
                  pl.BlockSpec((D_MODEL, tn), lambda j: (0, j)),
                  pl.BlockSpec((1, tn), lambda j: (0, j))],
        out_specs=pl.BlockSpec((rows, tn), lambda j: (0, j)),
        out_shape=jax.ShapeDtypeStruct((rows, n_out), F32),
        compiler_params=_params(("arbitrary",)),
        name="modulation",
    )(c_all, w_mod, b_mod)


IN_WEIGHT_COLS = 512
IN_WEIGHT_PAD = 256


def _in_weight_body(n_main, wt_ref, ba_ref, o_ref):
    j = pl.program_id(0)

    @pl.when(j < n_main)
    def _():
        o_ref[...] = wt_ref[...].T.astype(BF16)

    @pl.when(j == n_main)
    def _():
        n_ba = ba_ref.shape[0]
        ba = jnp.concatenate([ba_ref[...].T, jnp.zeros((D_MODEL, IN_WEIGHT_COLS - n_ba), F32)], axis=1)
        o_ref[...] = ba.astype(BF16)


def _in_weight(w_t):
    n_ba = 2 * GDN_HEADS
    split = GDN_CONV_CH + GDN_V
    tc = IN_WEIGHT_COLS
    n_main = COL_BA // tc

    def src_row(j):
        jj = jnp.minimum(j, n_main - 1)
        return pl.multiple_of(jnp.where(jj * tc < split, jj * tc, jj * tc + n_ba), n_ba)

    return pl.pallas_call(
        functools.partial(_in_weight_body, n_main),
        grid=(n_main + 1,),
        in_specs=[pl.BlockSpec((pl.Element(tc), pl.Element(D_MODEL)), lambda j: (src_row(j), 0)),
                  pl.BlockSpec((pl.Element(n_ba), pl.Element(D_MODEL)), lambda j: (split, 0))],
        out_specs=pl.BlockSpec((D_MODEL, tc), lambda j: (0, j)),
        out_shape=jax.ShapeDtypeStruct((D_MODEL, COL_BA + IN_WEIGHT_PAD), BF16),
        compiler_params=_params(("arbitrary",)),
        name="in_weight",
    )(w_t, w_t)


def _l2norm(x):
    return x * lax.rsqrt(jnp.sum(x * x, axis=-1, keepdims=True) + EPS)


def _inproj_body(seq_rows, tm, x_ref, sh_ref, sc_ref, nw_ref, w_ref, *rest):
    if seq_rows:
        cw_ref, qkv_ref, gg_ref, ba_ref, sq_ref, skv_ref, gab_ref, tail_ref, xe_ref = rest

        @pl.when(pl.program_id(1) == 0)
        def _():
            xe_ref[:, 0:SUBLANES, :] = jnp.zeros((GDN_SECTIONS, SUBLANES, LANES), F32)
    else:
        qkv_ref, gg_ref, ba_ref, sq_ref, skv_ref, gab_ref = rest

    h = _rms(x_ref[...], nw_ref[...]) * (1.0 + sc_ref[...]) + sh_ref[...]
    hb = h.astype(BF16)

    def proj(lo, width):
        return jnp.dot(hb, w_ref[:, lo:lo + width], preferred_element_type=F32)

    step = 512
    per = step // LANES
    for c in range(GDN_CONV_CH // step):
        z = proj(COL_QKV + c * step, step)
        for k in range(per):
            s = c * per + k
            zs = z[:, k * LANES:(k + 1) * LANES]
            if not seq_rows:
                qkv_ref[:, s * LANES:(s + 1) * LANES] = zs
                continue
            xe_ref[s, SUBLANES:SUBLANES + tm, :] = zs
            w = cw_ref[s]
            for r0 in range(0, tm, CONV_ROWS):
                y = w[0:1] * xe_ref[s, r0 + SUBLANES - 3:r0 + SUBLANES - 3 + CONV_ROWS, :]
                for tap in range(1, GDN_CONV):
                    lo = r0 + SUBLANES - 3 + tap
                    y = y + w[tap:tap + 1] * xe_ref[s, lo:lo + CONV_ROWS, :]
                f = _silu(y)
                if s < GDN_HEADS:
                    f = _l2norm(f) * (GDN_DK ** -0.5)
                elif s < 2 * GDN_HEADS:
                    f = _l2norm(f)
                qkv_ref[s, r0:r0 + CONV_ROWS, :] = f
            xe_ref[s, 0:SUBLANES, :] = xe_ref[s, tm:tm + SUBLANES, :]
    if seq_rows:
        tail_ref[...] = xe_ref[:, 0:SUBLANES, :]
    for c in range(GDN_V // step):
        z = proj(COL_GATE + c * step, step)
        for k in range(per):
            zs = z[:, k * LANES:(k + 1) * LANES]
            if seq_rows:
                gg_ref[c * per + k] = _silu(zs)
            else:
                gg_ref[:, (c * per + k) * LANES:(c * per + k + 1) * LANES] = zs
    ba_ref[...] = proj(COL_BA, LANES)
    for c in range(SWA_Q // step):
        sq_ref[:, c * step:(c + 1) * step] = proj(COL_SQ + c * step, step)
    skv_ref[...] = proj(COL_SKV, 2 * SWA_KV)
    for c in range(2 * D_MODEL // step):
        gab_ref[:, c * step:(c + 1) * step] = proj(COL_GAB + c * step, step)


def _inproj(x, sh, sc, nw, w_all, cw, tm):
    b_, l_, _ = x.shape
    r_ = sh.shape[1]
    rt = 1 if r_ == 1 else tm
    mod_map = (lambda b, t: (b, 0, 0)) if r_ == 1 else (lambda b, t: (b, t, 0))
    row_map = lambda b, t: (b, t, 0)
    head_map = lambda b, t: (b, 0, t, 0)
    seq_rows = cw is not None
    if seq_rows:
        gdn_shapes = (jax.ShapeDtypeStruct((b_, GDN_SECTIONS, l_, LANES), F32),
                      jax.ShapeDtypeStruct((b_, GDN_HEADS, l_, LANES), F32))
        gdn_specs = (pl.BlockSpec((None, GDN_SECTIONS, tm, LANES), head_map),
                     pl.BlockSpec((None, GDN_HEADS, tm, LANES), head_map))
    else:
        gdn_shapes = (jax.ShapeDtypeStruct((b_, l_, GDN_CONV_CH), F32),
                      jax.ShapeDtypeStruct((b_, l_, GDN_V), F32))
        gdn_specs = (pl.BlockSpec((None, tm, GDN_CONV_CH), row_map),
                     pl.BlockSpec((None, tm, GDN_V), row_map))
    out_shape = gdn_shapes + (
        jax.ShapeDtypeStruct((b_, l_, LANES), F32),
        jax.ShapeDtypeStruct((b_, l_, SWA_Q), F32),
        jax.ShapeDtypeStruct((b_, l_, 2 * SWA_KV), F32),
        jax.ShapeDtypeStruct((b_, l_, 2 * D_MODEL), F32),
    )
    out_specs = gdn_specs + (
        pl.BlockSpec((None, tm, LANES), row_map),
        pl.BlockSpec((None, tm, SWA_Q), row_map),
        pl.BlockSpec((None, tm, 2 * SWA_KV), row_map),
        pl.BlockSpec((None, tm, 2 * D_MODEL), row_map),
    )
    in_specs = [
        pl.BlockSpec((None, tm, D_MODEL), row_map),
        pl.BlockSpec((None, rt, D_MODEL), mod_map),
        pl.BlockSpec((None, rt, D_MODEL), mod_map),
        _const_spec(nw.shape),
        _const_spec(w_all.shape),
    ]
    args = [x, sh, sc, nw, w_all]
    scratch = []
    if seq_rows:
        in_specs.append(_const_spec(cw.shape))
        args.append(cw)
        out_shape += (jax.ShapeDtypeStruct((b_, GDN_SECTIONS, SUBLANES, LANES), F32),)
        out_specs += (pl.BlockSpec((None, GDN_SECTIONS, SUBLANES, LANES), lambda b, t: (b, 0, 0, 0)),)
        scratch.append(pltpu.VMEM((GDN_SECTIONS, tm + SUBLANES, LANES), F32))
    return pl.pallas_call(
        functools.partial(_inproj_body, seq_rows, tm),
        grid=(b_, l_ // tm),
        in_specs=in_specs,
        out_specs=out_specs,
        out_shape=out_shape,
        scratch_shapes=scratch,
        compiler_params=_params(("arbitrary", "arbitrary")),
        name="inproj_seq" if seq_rows else "inproj_rows",
    )(*args)


def _delta_gates(ba, alog_row, dtb_row):
    beta_all = jax.nn.sigmoid(ba)
    g_all = -jnp.exp(alog_row) * _softplus(ba + dtb_row)
    return beta_all, g_all


def _lane_column(x, lane_idx, lane):
    return jnp.sum(jnp.where(lane_idx == lane, x, 0.0), axis=1, keepdims=True)


def _level_masks():
    r = np.arange(CHUNK)[:, None]
    c = np.arange(CHUNK)[None, :]
    masks = [(r == c + 1) & (r % 2 == 1)]
    half = 2
    while half < CHUNK:
        full = 2 * half
        masks.append((r // full == c // full) & (r % full >= half) & (c % full < half))
        half = full
    return jnp.asarray(np.stack(masks), dtype=BF16)


def _unit_lower_inverses(ms, masks_ref, eye):
    ts = [eye - m * masks_ref[0] for m in ms]
    for lvl in range(1, masks_ref.shape[0]):
        off = masks_ref[lvl]
        xs = [jnp.dot(m * off, t, preferred_element_type=F32).astype(BF16) for m, t in zip(ms, ts)]
        ys = [jnp.dot(t, x, preferred_element_type=F32).astype(BF16) for t, x in zip(ts, xs)]
        ts = [t - y for t, y in zip(ts, ys)]
    return ts


def _cumsum_rows(g, ltri):
    hi = g.astype(BF16)
    r1 = g - hi.astype(F32)
    mid = r1.astype(BF16)
    lo = (r1 - mid.astype(F32)).astype(BF16)
    return (jnp.dot(ltri, hi, preferred_element_type=F32) + jnp.dot(ltri, mid, preferred_element_type=F32)
            + jnp.dot(ltri, lo, preferred_element_type=F32))


def _gated_out_norm(o, gate_act, onw):
    on = o * lax.rsqrt(jnp.mean(o * o, axis=-1, keepdims=True) + EPS) * onw
    return on * gate_act


def _gdn_prompt_body(lt, q_ref, k_ref, v_ref, ba_ref, alog_ref, dtb_ref, gate_ref, onw_ref, masks_ref,
                     og_ref, s_ref):
    @pl.when(pl.program_id(1) == 0)
    def _():
        s_ref[...] = jnp.zeros_like(s_ref)

    beta_all, g_all = _delta_gates(ba_ref[...], alog_ref[...], dtb_ref[...])
    lane_idx = lax.broadcasted_iota(jnp.int32, (CHUNK, LANES), 1)
    row = lax.broadcasted_iota(jnp.int32, (CHUNK, CHUNK), 0)
    col = lax.broadcasted_iota(jnp.int32, (CHUNK, CHUNK), 1)
    tril = row >= col
    strict = row > col
    ltri = jnp.where(tril, 1.0, 0.0).astype(BF16)
    eye = jnp.where(row == col, 1.0, 0.0).astype(BF16)
    onw = onw_ref[...]
    heads = range(GDN_HEADS)
    chunks = range(lt // CHUNK)

    blocks = [(c, j) for c in chunks for j in heads]
    pre = {}
    for c in chunks:
        rows = slice(c * CHUNK, (c + 1) * CHUNK)
        dec = _cumsum_rows(g_all[rows], ltri)
        dec_t = dec.T
        for j in heads:
            q, k, v = q_ref[j, rows, :], k_ref[j, rows, :], v_ref[j, rows, :]
            beta_col = _lane_column(beta_all[rows], lane_idx, j)
            dec_col = _lane_column(dec, lane_idx, GDN_HEADS + j)
            dec_row = dec_t[GDN_HEADS + j:GDN_HEADS + j + 1, :]
            dec_last = dec_row[:, CHUNK - 1:CHUNK]
            gam = jnp.exp(jnp.minimum(dec_col - dec_row, 0.0))
            e_col = jnp.exp(dec_col)
            kb = k * beta_col
            pre[c, j] = dict(q=q, k=k, gam=gam, kb=kb, qe=q * e_col, e_last=jnp.exp(dec_last),
                             kd=k * jnp.exp(dec_last - dec_col),
                             rhs=jnp.concatenate([v * beta_col, kb * e_col], axis=1).astype(BF16))
    grams = [_mm_nt(jnp.concatenate([pre[b]["kb"], pre[b]["q"]], axis=0), pre[b]["k"]) for b in blocks]
    ms = [jnp.where(strict, g[:CHUNK] * pre[b]["gam"], 0.0).astype(BF16) for g, b in zip(grams, blocks)]
    a_intra = {b: jnp.where(tril, g[CHUNK:] * pre[b]["gam"], 0.0) for g, b in zip(grams, blocks)}
    t_inv = _unit_lower_inverses(ms, masks_ref, eye)
    uw = {b: jnp.dot(t, pre[b]["rhs"], preferred_element_type=F32) for t, b in zip(t_inv, blocks)}

    for c in chunks:
        rows = slice(c * CHUNK, (c + 1) * CHUNK)
        s_prev = [s_ref[j] for j in heads]
        ws_qs = [_mm(jnp.concatenate([uw[c, j][:, GDN_DV:], pre[c, j]["qe"]], axis=0), s_prev[j]) for j in heads]
        v_new = [uw[c, j][:, :GDN_DV] - ws_qs[j][:CHUNK] for j in heads]
        o = [ws_qs[j][CHUNK:] + _mm(a_intra[c, j], v_new[j]) for j in heads]
        s_new = [s_prev[j] * pre[c, j]["e_last"] + _mm(pre[c, j]["kd"].T, v_new[j]) for j in heads]
        for j in heads:
            s_ref[j] = s_new[j]
            og = _gated_out_norm(o[j], gate_ref[j, rows, :], onw)
            og_ref[rows, j * GDN_DV:(j + 1) * GDN_DV] = og.astype(og_ref.dtype)


def _gdn_prompt(qkvf, gact, ba, alog_row, dtb_row, onw, masks, lt):
    b_, _, l_, _ = qkvf.shape
    sec = lambda s: pl.BlockSpec((None, GDN_HEADS, lt, LANES), lambda b, t, s=s: (b, s, t, 0))
    return pl.pallas_call(
        functools.partial(_gdn_prompt_body, lt),
        grid=(b_, l_ // lt),
        in_specs=[sec(0), sec(1), sec(2),
                  pl.BlockSpec((None, lt, LANES), lambda b, t: (b, t, 0)),
                  _const_spec(alog_row.shape), _const_spec(dtb_row.shape),
                  pl.BlockSpec((None, GDN_HEADS, lt, LANES), lambda b, t: (b, 0, t, 0)),
                  _const_spec(onw.shape), _const_spec(masks.shape)],
        out_specs=(pl.BlockSpec((None, lt, GDN_V), lambda b, t: (b, t, 0)),
                   pl.BlockSpec((None, GDN_HEADS, GDN_DK, GDN_DV), lambda b, t: (b, 0, 0, 0))),
        out_shape=(jax.ShapeDtypeStruct((b_, l_, GDN_V), BF16),
                   jax.ShapeDtypeStruct((b_, GDN_HEADS, GDN_DK, GDN_DV), F32)),
        compiler_params=_params(("arbitrary", "arbitrary")),
        name="gdn_prompt",
    )(qkvf, qkvf, qkvf, ba, alog_row, dtb_row, gact, onw, masks)


def _gdn_step_body(bb, x_ref, st_ref, cw_ref, ba_ref, alog_ref, dtb_ref, gate_ref, onw_ref, s0_ref,
                   og_ref, sn_ref, q_s, k_s, v_s, b_s, e_s, o_s):
    beta_all, g_all = _delta_gates(ba_ref[...], alog_ref[...], dtb_ref[...])
    lane_idx = lax.broadcasted_iota(jnp.int32, (bb, LANES), 1)
    for h in range(GDN_HEADS):
        feats = []
        for s in range(3):
            idx = s * GDN_HEADS + h
            cols = slice(idx * LANES, (idx + 1) * LANES)
            w = cw_ref[idx]
            y = w[0:1] * st_ref[0, :, cols]
            for tap in range(1, GDN_CONV - 1):
                y = y + w[tap:tap + 1] * st_ref[tap, :, cols]
            y = y + w[GDN_CONV - 1:GDN_CONV] * x_ref[:, cols]
            feats.append(_silu(y))
        q, k, v = feats
        q_s[h] = q * lax.rsqrt(jnp.sum(q * q, axis=-1, keepdims=True) + EPS) * (GDN_DK ** -0.5)
        k_s[h] = k * lax.rsqrt(jnp.sum(k * k, axis=-1, keepdims=True) + EPS)
        v_s[h] = v
        b_s[h] = jnp.broadcast_to(_lane_column(beta_all, lane_idx, h), (bb, LANES))
        e_s[h] = jnp.broadcast_to(jnp.exp(_lane_column(g_all, lane_idx, h + GDN_HEADS)), (bb, LANES))

    eye = (lax.broadcasted_iota(jnp.int32, (GDN_DK, GDN_DK), 0)
           == lax.broadcasted_iota(jnp.int32, (GDN_DK, GDN_DK), 1))

    def to_col(r):
        return jnp.sum(jnp.where(eye, jnp.broadcast_to(r, (GDN_DK, GDN_DK)), 0.0), axis=1, keepdims=True)

    sub = lax.broadcasted_iota(jnp.int32, (SUBLANES, GDN_DK), 0)

    def seq_body(i, carry):
        for h in range(GDN_HEADS):
            one = pl.ds(i, 1)
            k_row = k_s[h, one, :]
            q_row = q_s[h, one, :]
            s1 = s0_ref[i, h] * e_s[h, one, :]
            kq = jnp.where(sub == 0, k_row, jnp.where(sub == 1, q_row, 0.0))
            kq_s1 = _mm(kq, s1)
            delta = (v_s[h, one, :] - kq_s1[0:1, :]) * b_s[h, one, :]
            sn_ref[i, h] = s1 + to_col(k_row) * delta
            qk = jnp.sum(q_row * k_row, axis=1, keepdims=True)
            o_s[h, one, :] = kq_s1[1:2, :] + qk * delta
        return carry

    lax.fori_loop(0, bb, seq_body, 0)
    onw = onw_ref[...]
    for h in range(GDN_HEADS):
        cols = slice(h * GDN_DV, (h + 1) * GDN_DV)
        og = _gated_out_norm(o_s[h], _silu(gate_ref[:, cols]), onw)
        og_ref[:, cols] = og.astype(og_ref.dtype)


def _gdn_step(qkv, st, gate, ba, cw, alog_row, dtb_row, onw, s0, bb):
    n_ = ba.shape[0]
    vec = pltpu.VMEM((GDN_HEADS, bb, LANES), F32)
    return pl.pallas_call(
        functools.partial(_gdn_step_body, bb),
        grid=(n_ // bb,),
        in_specs=[pl.BlockSpec((bb, GDN_CONV_CH), lambda i: (i, 0)),
                  pl.BlockSpec((GDN_CONV - 1, bb, GDN_CONV_CH), lambda i: (0, i, 0)),
                  _const_spec(cw.shape),
                  pl.BlockSpec((bb, LANES), lambda i: (i, 0)),
                  _const_spec(alog_row.shape), _const_spec(dtb_row.shape),
                  pl.BlockSpec((bb, GDN_V), lambda i: (i, 0)),
                  _const_spec(onw.shape),
                  pl.BlockSpec((bb, GDN_HEADS, GDN_DK, GDN_DV), lambda i: (i, 0, 0, 0))],
        out_specs=(pl.BlockSpec((bb, GDN_V), lambda i: (i, 0)),
                   pl.BlockSpec((bb, GDN_HEADS, GDN_DK, GDN_DV), lambda i: (i, 0, 0, 0))),
        out_shape=(jax.ShapeDtypeStruct((n_, GDN_V), BF16),
                   jax.ShapeDtypeStruct(s0.shape, F32)),
        scratch_shapes=[vec, vec, vec, vec, vec, vec],
        compiler_params=_params(("arbitrary",)),
        name="gdn_step",
    )(qkv, st, cw, ba, alog_row, dtb_row, gate, onw, s0)


def _swa_prompt_body(nq, sinks_ref, q_ref, kvp_ref, kvc_ref, o_ref):
    n = pl.program_id(1)
    w = WINDOW
    tiles = SWA_KV // LANES
    pairs = 2
    lo_lane = lax.broadcasted_iota(jnp.int32, (w, LANES), 1) < SWA_HD
    lo_row = lax.broadcasted_iota(jnp.int32, (LANES, w), 0) < SWA_HD
    c = lax.broadcasted_iota(jnp.int32, (2 * w, pairs * w), 0)
    i = lax.broadcasted_iota(jnp.int32, (2 * w, pairs * w), 1) & (w - 1)
    banded = (c > i) & (c <= i + w)
    banded_first = banded & ((c >= w) | (n > 0))
    k_blk, vt_blk = [], []
    for j in range(nq + 1):
        src, rows = (kvp_ref, slice(0, w)) if j == 0 else (kvc_ref, slice((j - 1) * w, j * w))
        k_tiles, vt_tiles = [], []
        for t in range(tiles):
            kx = src[rows, t * LANES:(t + 1) * LANES]
            vt = src[rows, SWA_KV + t * LANES:SWA_KV + (t + 1) * LANES].T
            k_tiles.append((kx.astype(BF16), pltpu.roll(kx, SWA_HD, axis=1).astype(BF16)))
            vt_tiles.append((vt.astype(BF16),
                             jnp.concatenate([vt[SWA_HD:], vt[:SWA_HD]], axis=0).astype(BF16)))
        k_blk.append(k_tiles)
        vt_blk.append(vt_tiles)
    items = [(qb, g, p) for qb in range(nq) for g in range(SWA_KV_HEADS) for p in range(2)]
    log2e = math.log2(math.e)
    qm, kz, vzt, sink, valid = {}, {}, {}, {}, {}
    for qb, g, p in items:
        keep = lo_lane if p == 0 else jnp.logical_not(lo_lane)
        q_tiles = [q_ref[qb * w:(qb + 1) * w, (2 * g + r) * LANES:(2 * g + r + 1) * LANES] for r in range(pairs)]
        qm[qb, g, p] = jnp.concatenate([jnp.where(keep, x * SWA_Q_SCALE, 0.0) for x in q_tiles],
                                       axis=0).astype(BF16)
        variant = 0 if p == g % 2 else 1
        kz[qb, g, p] = jnp.concatenate([k_blk[qb + d][g // 2][variant] for d in range(2)], axis=0)
        vzt[qb, g, p] = jnp.concatenate([vt_blk[qb + d][g // 2][variant] for d in range(2)], axis=1)
        sink[qb, g, p] = jnp.concatenate([jnp.full((1, w), sinks_ref[SWA_GROUP * g + 2 * r + p] * log2e, F32)
                                          for r in range(pairs)], axis=1)
        valid[qb, g, p] = banded_first if qb == 0 else banded
    st = {b: jnp.where(valid[b], lax.dot_general(kz[b], qm[b], (((1,), (1,)), ((), ())),
                                                 preferred_element_type=F32), -jnp.inf) for b in items}
    m = {b: jnp.maximum(jnp.max(st[b], axis=0, keepdims=True), sink[b]) for b in items}
    et = {b: jnp.exp2(st[b] - m[b]) for b in items}
    den = {b: jnp.sum(et[b], axis=0, keepdims=True) + jnp.exp2(sink[b] - m[b]) for b in items}
    ot = {b: jnp.dot(vzt[b], et[b].astype(BF16), preferred_element_type=F32) / den[b] for b in items}
    for qb in range(nq):
        for g in range(SWA_KV_HEADS):
            for r in range(pairs):
                cols = slice(r * w, (r + 1) * w)
                tile_t = jnp.where(lo_row, ot[qb, g, 0][:, cols], ot[qb, g, 1][:, cols])
                o_ref[qb * w:(qb + 1) * w, (2 * g + r) * LANES:(2 * g + r + 1) * LANES] = (
                    tile_t.T.astype(o_ref.dtype))


def _swa_prompt(sq, skv, sinks, nq):
    b_, l_, _ = sq.shape
    rows = nq * WINDOW
    return pl.pallas_call(
        functools.partial(_swa_prompt_body, nq),
        grid=(b_, l_ // rows),
        in_specs=[pl.BlockSpec(memory_space=pltpu.SMEM),
                  pl.BlockSpec((None, rows, SWA_Q), lambda b, n: (b, n, 0)),
                  pl.BlockSpec((None, WINDOW, 2 * SWA_KV), lambda b, n: (b, jnp.maximum(n * nq - 1, 0), 0)),
                  pl.BlockSpec((None, rows, 2 * SWA_KV), lambda b, n: (b, n, 0))],
        out_specs=pl.BlockSpec((None, rows, SWA_Q), lambda b, n: (b, n, 0)),
        out_shape=jax.ShapeDtypeStruct((b_, l_, SWA_Q), BF16),
        compiler_params=_params(("arbitrary", "arbitrary")),
        name="swa_prompt",
    )(sinks, sq, skv, skv)


def _swa_step_body(bb, q_ref, kvn_ref, ck_ref, cv_ref, sink_ref, o_ref, nk_ref, nv_ref):
    w = WINDOW
    first = pl.program_id(0) * bb
    row = lax.broadcasted_iota(jnp.int32, (SWA_Q_HEADS, SWA_KV), 0)
    lane = lax.broadcasted_iota(jnp.int32, (SWA_Q_HEADS, SWA_KV), 1)
    own = (lane // SWA_HD) == (row // SWA_GROUP)
    newest = lax.broadcasted_iota(jnp.int32, (SWA_KV, w), 1) == w - 1
    sink = sink_ref[...]
    scale = SWA_HD ** -0.5
    seqs = range(bb)
    kn_all = kvn_ref[0:SWA_KV, :]
    vn_all = kvn_ref[SWA_KV:2 * SWA_KV, :]
    nk = [jnp.where(newest, pltpu.roll(kn_all, w - 1 - (first + i), axis=1), pltpu.roll(ck_ref[i], w - 1, axis=1))
          for i in seqs]
    nv = [jnp.where(newest, pltpu.roll(vn_all, w - 1 - (first + i), axis=1), pltpu.roll(cv_ref[i], w - 1, axis=1))
          for i in seqs]
    q_bd = [jnp.where(own, jnp.concatenate([q_ref[i]] * SWA_KV_HEADS, axis=1), 0.0) for i in seqs]
    s = [_mm(q_bd[i], nk[i]) * scale for i in seqs]
    m = [jnp.maximum(jnp.max(s[i], axis=1, keepdims=True), sink) for i in seqs]
    e = [jnp.exp(s[i] - m[i]) for i in seqs]
    den = [jnp.sum(e[i], axis=1, keepdims=True) + jnp.exp(sink - m[i]) for i in seqs]
    pv = [jnp.where(own, _mm_nt(e[i] / den[i], nv[i]), 0.0) for i in seqs]
    for i in seqs:
        o = pv[i][:, 0:SWA_HD]
        for g in range(1, SWA_KV_HEADS):
            o = o + pv[i][:, g * SWA_HD:(g + 1) * SWA_HD]
        o_ref[i] = o
        nk_ref[i] = nk[i]
        nv_ref[i] = nv[i]


def _swa_step(q3, kvn_t, ck_t, cv_t, sink_col, bb):
    n_ = q3.shape[0]
    assert n_ <= WINDOW
    cache = pl.BlockSpec((bb, SWA_KV, WINDOW), lambda i: (i, 0, 0))
    return pl.pallas_call(
        functools.partial(_swa_step_body, bb),
        grid=(n_ // bb,),
        in_specs=[pl.BlockSpec((bb, SWA_Q_HEADS, SWA_HD), lambda i: (i, 0, 0)),
                  _const_spec(kvn_t.shape),
                  cache, cache,
                  _const_spec(sink_col.shape)],
        out_specs=(pl.BlockSpec((bb, SWA_Q_HEADS, SWA_HD), lambda i: (i, 0, 0)), cache, cache),
        out_shape=(jax.ShapeDtypeStruct(q3.shape, F32),
                   jax.ShapeDtypeStruct(ck_t.shape, F32),
                   jax.ShapeDtypeStruct(cv_t.shape, F32)),
        compiler_params=_params(("arbitrary",)),
        name="swa_step",
    )(q3, kvn_t, ck_t, cv_t, sink_col)


def _dense_body(stateful, tm, og_ref, ob_ref, gab_ref, x_ref, gt1_ref, sh2_ref, sc2_ref, gt2_ref,
                n2w_ref, fnw_ref, wa_ref, wb_ref, wo_ref, wg_ref, wu_ref, cw_ref, cb_ref, wd_ref, *rest):
    if stateful:
        st_ref, y_ref, gout_ref, act_ref = rest
    else:
        y_ref, gout_ref, act_ref, gbuf_ref, carry_ref = rest

        @pl.when(pl.program_id(1) == 0)
        def _():
            carry_ref[...] = jnp.zeros_like(carry_ref)

    y_a = jnp.dot(og_ref[...], wa_ref[...], preferred_element_type=F32)
    y_b = jnp.dot(ob_ref[...], wb_ref[...], preferred_element_type=F32)
    merged = (jax.nn.sigmoid(gab_ref[:, 0:D_MODEL]) * y_a
              + jax.nn.sigmoid(gab_ref[:, D_MODEL:2 * D_MODEL]) * y_b)
    x1 = x_ref[...] + gt1_ref[...] * _mm(merged, wo_ref[...])
    h2 = (_rms(x1, n2w_ref[...]) * (1.0 + sc2_ref[...]) + sh2_ref[...]).astype(BF16)

    for c in range(D_FF // FFN_COLS):
        cols = slice(c * FFN_COLS, (c + 1) * FFN_COLS)
        gate = jnp.dot(h2, wg_ref[:, cols], preferred_element_type=F32)
        up = jnp.dot(h2, wu_ref[:, cols], preferred_element_type=F32)
        if stateful:
            g2 = st_ref[0, :, cols]
            g1 = st_ref[1, :, cols]
            gout_ref[:, cols] = gate
        else:
            gbuf_ref[0:SUBLANES, :] = carry_ref[:, cols]
            gbuf_ref[SUBLANES:SUBLANES + tm, :] = gate
            g2 = gbuf_ref[SUBLANES - 2:SUBLANES - 2 + tm, :]
            g1 = gbuf_ref[SUBLANES - 1:SUBLANES - 1 + tm, :]
            carry_ref[:, cols] = gbuf_ref[tm:tm + SUBLANES, :]
        gc = (cw_ref[0:1, cols] * g2 + cw_ref[1:2, cols] * g1 + cw_ref[2:3, cols] * gate) + cb_ref[:, cols]
        act_ref[:, cols] = (_silu(gc) * up).astype(BF16)
    if not stateful:
        gout_ref[...] = carry_ref[...]

    x2 = x1 + gt2_ref[...] * jnp.dot(act_ref[...], wd_ref[...], preferred_element_type=F32)
    y_ref[...] = _rms(x2, fnw_ref[...])


def _dense(og, ob, gab, x, mods, vecs, ws, st, tm):
    b_, l_, _ = x.shape
    r_ = mods[0].shape[1]
    rt = 1 if r_ == 1 else tm
    mod_map = (lambda b, t: (b, 0, 0)) if r_ == 1 else (lambda b, t: (b, t, 0))
    row_map = lambda b, t: (b, t, 0)
    stateful = st is not None
    in_specs = ([pl.BlockSpec((None, tm, D_MODEL), row_map),
                 pl.BlockSpec((None, tm, D_MODEL), row_map),
                 pl.BlockSpec((None, tm, 2 * D_MODEL), row_map),
                 pl.BlockSpec((None, tm, D_MODEL), row_map)]
                + [pl.BlockSpec((None, rt, D_MODEL), mod_map)] * 4
                + [_const_spec(a.shape) for a in vecs[:2]]
                + [_const_spec(ws[0].shape), _const_spec(ws[1].shape), _const_spec(ws[2].shape),
                   _const_spec(ws[3].shape), _const_spec(ws[4].shape),
                   _const_spec(vecs[2].shape), _const_spec(vecs[3].shape), _const_spec(ws[5].shape)])
    args = [og, ob, gab, x, *mods, vecs[0], vecs[1], ws[0], ws[1], ws[2], ws[3], ws[4], vecs[2], vecs[3], ws[5]]
    scratch = [pltpu.VMEM((tm, D_FF), BF16)]
    if stateful:
        in_specs.append(pl.BlockSpec((FFN_CONV - 1, None, tm, D_FF), lambda b, t: (0, b, t, 0)))
        args.append(st)
        gout_shape = jax.ShapeDtypeStruct((b_, l_, D_FF), F32)
        gout_spec = pl.BlockSpec((None, tm, D_FF), row_map)
    else:
        scratch += [pltpu.VMEM((tm + SUBLANES, FFN_COLS), F32), pltpu.VMEM((SUBLANES, D_FF), F32)]
        gout_shape = jax.ShapeDtypeStruct((b_, SUBLANES, D_FF), F32)
        gout_spec = pl.BlockSpec((None, SUBLANES, D_FF), lambda b, t: (b, 0, 0))
    return pl.pallas_call(
        functools.partial(_dense_body, stateful, tm),
        grid=(b_, l_ // tm),
        in_specs=in_specs,
        out_specs=(pl.BlockSpec((None, tm, D_MODEL), row_map), gout_spec),
        out_shape=(jax.ShapeDtypeStruct((b_, l_, D_MODEL), F32), gout_shape),
        scratch_shapes=scratch,
        compiler_params=_params(("arbitrary", "arbitrary")),
        name="dense_step" if stateful else "dense_prompt",
    )(*args)


def _lane_row(values, offset):
    return jnp.zeros((1, LANES), F32).at[0, offset:offset + values.shape[0]].set(values)


def kernel(x_prompt, x_sample, c_prompt, c_sample, state_gdn_S, state_gdn_conv, cache_swa_k, cache_swa_v,
           state_ffn_conv, w_mod, b_mod, norm1_w, norm2_w, w_in, gdn_conv_w, gdn_a_log, gdn_dt_bias,
           gdn_onorm_w, w_gdn_out, swa_sinks, w_swa_out, w_o, w_ffn_gate, w_ffn_up, ffn_conv_w, ffn_conv_b,
           w_ffn_down, final_norm_w):
    assert w_mod.shape[0] == 1, "single-layer trunk"
    nb, seq, _ = x_prompt.shape
    ns = x_sample.shape[0]
    assert x_sample.shape[1] == 1

    in_ws = _in_weight(jnp.transpose(w_in[0]))
    dense_ws = (w_gdn_out[0].astype(BF16), w_swa_out[0].astype(BF16), w_o[0].astype(BF16),
                w_ffn_gate[0].astype(BF16), w_ffn_up[0].astype(BF16), w_ffn_down[0].astype(BF16))
    dense_vecs = (norm2_w, final_norm_w[None, :], ffn_conv_w[0], ffn_conv_b)
    cw = jnp.transpose(gdn_conv_w[0].reshape(GDN_CONV, GDN_SECTIONS, LANES), (1, 0, 2))
    alog_row = _lane_row(gdn_a_log[0], GDN_HEADS)
    dtb_row = _lane_row(gdn_dt_bias[0], GDN_HEADS)

    mod = _modulation(jnp.concatenate([c_prompt, c_sample], axis=0), w_mod[0], b_mod)
    mod_p = [mod[:nb, i * D_MODEL:(i + 1) * D_MODEL][:, None, :] for i in range(6)]
    mod_s = [mod[nb:, i * D_MODEL:(i + 1) * D_MODEL][None, :, :] for i in range(6)]

    qkvf, gact, ba, sq, skv, gab, qkv_tail = _inproj(x_prompt, mod_p[0], mod_p[1], norm1_w, in_ws, cw, tm=256)
    og, gdn_s_p = _gdn_prompt(qkvf, gact, ba, alog_row, dtb_row, gdn_onorm_w, _level_masks(), lt=4 * CHUNK)
    ob = _swa_prompt(sq, skv, swa_sinks[0], nq=4)
    y_p, gate_tail = _dense(og, ob, gab, x_prompt, (mod_p[2], mod_p[3], mod_p[4], mod_p[5]),
                            dense_vecs, dense_ws, None, tm=512)
    gdn_conv_p = jnp.transpose(qkv_tail[:, :, SUBLANES - (GDN_CONV - 1):, :], (0, 2, 1, 3)).reshape(
        nb, GDN_CONV - 1, GDN_CONV_CH)
    k_p = skv[:, seq - WINDOW:, :SWA_KV].reshape(nb, WINDOW, SWA_KV_HEADS, SWA_HD)
    v_p = skv[:, seq - WINDOW:, SWA_KV:].reshape(nb, WINDOW, SWA_KV_HEADS, SWA_HD)
    ffn_conv_p = gate_tail[:, SUBLANES - (FFN_CONV - 1):, :]

    xs = x_sample.reshape(1, ns, D_MODEL)
    qkvs, gates, bas, sqs, skvs, gabs = _inproj(xs, mod_s[0], mod_s[1], norm1_w, in_ws, None, tm=ns)
    st_gdn = jnp.transpose(state_gdn_conv[0], (1, 0, 2))
    og_s, gdn_s_s = _gdn_step(qkvs[0], st_gdn, gates[0], bas[0], cw, alog_row, dtb_row, gdn_onorm_w,
                              state_gdn_S[0], bb=8)
    to_channel_major = lambda c: jnp.transpose(c, (0, 2, 3, 1)).reshape(ns, SWA_KV, WINDOW)
    from_channel_major = lambda c: jnp.transpose(c.reshape(ns, SWA_KV_HEADS, SWA_HD, WINDOW), (0, 3, 1, 2))
    kvn_t = jnp.pad(jnp.transpose(skvs[0]), ((0, 0), (0, WINDOW - ns)))
    o3, k_s, v_s = _swa_step(sqs[0].reshape(ns, SWA_Q_HEADS, SWA_HD), kvn_t,
                             to_channel_major(cache_swa_k[0]), to_channel_major(cache_swa_v[0]),
                             swa_sinks[0][:, None], bb=8)
    ob_s = o3.reshape(1, ns, SWA_Q).astype(BF16)
    st_ffn = jnp.transpose(state_ffn_conv[0], (1, 0, 2))[:, None]
    y_s, gate_new = _dense(og_s[None], ob_s, gabs, xs, (mod_s[2], mod_s[3], mod_s[4], mod_s[5]),
                           dense_vecs, dense_ws, st_ffn, tm=ns)
    gdn_conv_s = jnp.concatenate([state_gdn_conv[0][:, 1:], qkvs[0][:, None, :]], axis=1)
    ffn_conv_s = jnp.concatenate([state_ffn_conv[0][:, 1:], gate_new[0][:, None, :]], axis=1)

    return (y_p, y_s.reshape(ns, 1, D_MODEL),
            gdn_s_p[None], gdn_s_s[None],
            gdn_conv_p[None], gdn_conv_s[None],
            k_p[None], from_channel_major(k_s)[None],
            v_p[None], from_channel_major(v_s)[None],
            ffn_conv_p[None], ffn_conv_s[None])
```

```python
import functools
import math

import numpy as np
import jax
import jax.numpy as jnp
from jax import lax
from jax.experimental import pallas as pl
from jax.experimental.pallas import tpu as pltpu

F32 = jnp.float32
BF16 = jnp.bfloat16

D_MODEL = 1024
GDN_HEADS = 8
GDN_DK = 128
GDN_DV = 128
GDN_QK = GDN_HEADS * GDN_DK
GDN_V = GDN_HEADS * GDN_DV
GDN_CONV = 4
GDN_CONV_CH = 2 * GDN_QK + GDN_V
GDN_SECTIONS = GDN_CONV_CH // 128
SWA_Q_HEADS = 16
SWA_KV_HEADS = 4
SWA_GROUP = SWA_Q_HEADS // SWA_KV_HEADS
SWA_HD = 64
SWA_Q = SWA_Q_HEADS * SWA_HD
SWA_KV = SWA_KV_HEADS * SWA_HD
WINDOW = 128
D_FF = 2816
FFN_CONV = 3
EPS = 1e-6

LANES = 128
SUBLANES = 8
VMEM_LIMIT = 56 * 1024 * 1024

COL_QKV = 0
COL_GATE = COL_QKV + GDN_CONV_CH
COL_SQ = COL_GATE + GDN_V
COL_SKV = COL_SQ + SWA_Q
COL_GAB = COL_SKV + 2 * SWA_KV
COL_BA = COL_GAB + 2 * D_MODEL
IN_COLS = COL_BA + LANES

SWA_Q_SCALE = SWA_HD ** -0.5 * math.log2(math.e)

CONV_ROWS = 64
CHUNK = 128
FFN_COLS = 256


def _mm(a, b):
    return jnp.dot(a.astype(BF16), b.astype(BF16), preferred_element_type=F32)


def _mm_nt(a, b):
    return lax.dot_general(a.astype(BF16), b.astype(BF16), (((1,), (1,)), ((), ())),
                           preferred_element_type=F32)


def _silu(x):
    return x * jax.nn.sigmoid(x)


def _softplus(x):
    return jnp.maximum(x, 0.0) + jnp.log1p(jnp.exp(-jnp.abs(x)))


def _rms(x, w):
    return x * lax.rsqrt(jnp.mean(x * x, axis=-1, keepdims=True) + EPS) * w


def _const_spec(shape):
    n = len(shape)
    return pl.BlockSpec(shape, lambda *_: (0,) * n, pipeline_mode=pl.Buffered(1))


def _params(sem):
    return pltpu.CompilerParams(dimension_semantics=sem, vmem_limit_bytes=VMEM_LIMIT)


def _mod_body(c_ref, w_ref, b_ref, o_ref):
    o_ref[...] = _mm(_silu(c_ref[...]), w_ref[...]) + b_ref[...]


def _modulation(c_all, w_mod, b_mod):
    rows = c_all.shape[0]
    n_out = w_mod.shape[1]
    tn = D_MODEL
    return pl.pallas_call(
        _mod_body,
        grid=(n_out // tn,),
        in_specs=[pl.BlockSpec((rows, D_MODEL), lambda j: (0, 0)),
                  pl.BlockSpec((D_MODEL, tn), lambda j: (0, j)),
                  pl.BlockSpec((1, tn), lambda j: (0, j))],
        out_specs=pl.BlockSpec((rows, tn), lambda j: (0, j)),
        out_shape=jax.ShapeDtypeStruct((rows, n_out), F32),
        compiler_params=_params(("arbitrary",)),
        name="modulation",
    )(c_all, w_mod, b_mod)


IN_WEIGHT_COLS = 512
IN_WEIGHT_PAD = 256


def _in_weight_body(n_main, wt_ref, ba_ref, o_ref):
    j = pl.program_id(0)

    @pl.when(j < n_main)
    def _():
        o_ref[...] = wt_ref[...].T.astype(BF16)

    @pl.when(j == n_main)
    def _():
        n_ba = ba_ref.shape[0]
        ba = jnp.concatenate([ba_ref[...].T, jnp.zeros((D_MODEL, IN_WEIGHT_COLS - n_ba), F32)], axis=1)
        o_ref[...] = ba.astype(BF16)


def _in_weight(w_t):
    n_ba = 2 * GDN_HEADS
    split = GDN_CONV_CH + GDN_V
    tc = IN_WEIGHT_COLS
    n_main = COL_BA // tc

    def src_row(j):
        jj = jnp.minimum(j, n_main - 1)
        return pl.multiple_of(jnp.where(jj * tc < split, jj * tc, jj * tc + n_ba), n_ba)

    return pl.pallas_call(
        functools.partial(_in_weight_body, n_main),
        grid=(n_main + 1,),
        in_specs=[pl.BlockSpec((pl.Element(tc), pl.Element(D_MODEL)), lambda j: (src_row(j), 0)),
                  pl.BlockSpec((pl.Element(n_ba), pl.Element(D_MODEL)), lambda j: (split, 0))],
        out_specs=pl.BlockSpec((D_MODEL, tc), lambda j: (0, j)),
        out_shape=jax.ShapeDtypeStruct((D_MODEL, COL_BA + IN_WEIGHT_PAD), BF16),
        compiler_params=_params(("arbitrary",)),
        name="in_weight",
    )(w_t, w_t)


def _l2norm(x):
    return x * lax.rsqrt(jnp.sum(x * x, axis=-1, keepdims=True) + EPS)


def _inproj_body(seq_rows, tm, x_ref, sh_ref, sc_ref, nw_ref, w_ref, *rest):
    if seq_rows:
        cw_ref, qkv_ref, gg_ref, ba_ref, sq_ref, skv_ref, gab_ref, tail_ref, xe_ref = rest

        @pl.when(pl.program_id(1) == 0)
        def _():
            xe_ref[:, 0:SUBLANES, :] = jnp.zeros((GDN_SECTIONS, SUBLANES, LANES), F32)
    else:
        qkv_ref, gg_ref, ba_ref, sq_ref, skv_ref, gab_ref = rest

    h = _rms(x_ref[...], nw_ref[...]) * (1.0 + sc_ref[...]) + sh_ref[...]
    hb = h.astype(BF16)

    def proj(lo, width):
        return jnp.dot(hb, w_ref[:, lo:lo + width], preferred_element_type=F32)

    step = 512
    per = step // LANES
    for c in range(GDN_CONV_CH // step):
        z = proj(COL_QKV + c * step, step)
        for k in range(per):
            s = c * per + k
            zs = z[:, k * LANES:(k + 1) * LANES]
            if not seq_rows:
                qkv_ref[:, s * LANES:(s + 1) * LANES] = zs
                continue
            xe_ref[s, SUBLANES:SUBLANES + tm, :] = zs
            w = cw_ref[s]
            for r0 in range(0, tm, CONV_ROWS):
                y = w[0:1] * xe_ref[s, r0 + SUBLANES - 3:r0 + SUBLANES - 3 + CONV_ROWS, :]
                for tap in range(1, GDN_CONV):
                    lo = r0 + SUBLANES - 3 + tap
                    y = y + w[tap:tap + 1] * xe_ref[s, lo:lo + CONV_ROWS, :]
                f = _silu(y)
                if s < GDN_HEADS:
                    f = _l2norm(f) * (GDN_DK ** -0.5)
                elif s < 2 * GDN_HEADS:
                    f = _l2norm(f)
                qkv_ref[s, r0:r0 + CONV_ROWS, :] = f
            xe_ref[s, 0:SUBLANES, :] = xe_ref[s, tm:tm + SUBLANES, :]
    if seq_rows:
        tail_ref[...] = xe_ref[:, 0:SUBLANES, :]
    for c in range(GDN_V // step):
        z = proj(COL_GATE + c * step, step)
        for k in range(per):
            zs = z[:, k * LANES:(k + 1) * LANES]
            if seq_rows:
                gg_ref[c * per + k] = _silu(zs)
            else:
                gg_ref[:, (c * per + k) * LANES:(c * per + k + 1) * LANES] = zs
    ba_ref[...] = proj(COL_BA, LANES)
    for c in range(SWA_Q // step):
        sq_ref[:, c * step:(c + 1) * step] = proj(COL_SQ + c * step, step)
    skv_ref[...] = proj(COL_SKV, 2 * SWA_KV)
    for c in range(2 * D_MODEL // step):
        gab_ref[:, c * step:(c + 1) * step] = proj(COL_GAB + c * step, step)


def _inproj(x, sh, sc, nw, w_all, cw, tm):
    b_, l_, _ = x.shape
    r_ = sh.shape[1]
    rt = 1 if r_ == 1 else tm
    mod_map = (lambda b, t: (b, 0, 0)) if r_ == 1 else (lambda b, t: (b, t, 0))
    row_map = lambda b, t: (b, t, 0)
    head_map = lambda b, t: (b, 0, t, 0)
    seq_rows = cw is not None
    if seq_rows:
        gdn_shapes = (jax.ShapeDtypeStruct((b_, GDN_SECTIONS, l_, LANES), F32),
                      jax.ShapeDtypeStruct((b_, GDN_HEADS, l_, LANES), F32))
        gdn_specs = (pl.BlockSpec((None, GDN_SECTIONS, tm, LANES), head_map),
                     pl.BlockSpec((None, GDN_HEADS, tm, LANES), head_map))
    else:
        gdn_shapes = (jax.ShapeDtypeStruct((b_, l_, GDN_CONV_CH), F32),
                      jax.ShapeDtypeStruct((b_, l_, GDN_V), F32))
        gdn_specs = (pl.BlockSpec((None, tm, GDN_CONV_CH), row_map),
                     pl.BlockSpec((None, tm, GDN_V), row_map))
    out_shape = gdn_shapes + (
        jax.ShapeDtypeStruct((b_, l_, LANES), F32),
        jax.ShapeDtypeStruct((b_, l_, SWA_Q), F32),
        jax.ShapeDtypeStruct((b_, l_, 2 * SWA_KV), F32),
        jax.ShapeDtypeStruct((b_, l_, 2 * D_MODEL), F32),
    )
    out_specs = gdn_specs + (
        pl.BlockSpec((None, tm, LANES), row_map),
        pl.BlockSpec((None, tm, SWA_Q), row_map),
        pl.BlockSpec((None, tm, 2 * SWA_KV), row_map),
        pl.BlockSpec((None, tm, 2 * D_MODEL), row_map),
    )
    in_specs = [
        pl.BlockSpec((None, tm, D_MODEL), row_map),
        pl.BlockSpec((None, rt, D_MODEL), mod_map),
        pl.BlockSpec((None, rt, D_MODEL), mod_map),
        _const_spec(nw.shape),
        _const_spec(w_all.shape),
    ]
    args = [x, sh, sc, nw, w_all]
    scratch = []
    if seq_rows:
        in_specs.append(_const_spec(cw.shape))
        args.append(cw)
        out_shape += (jax.ShapeDtypeStruct((b_, GDN_SECTIONS, SUBLANES, LANES), F32),)
        out_specs += (pl.BlockSpec((None, GDN_SECTIONS, SUBLANES, LANES), lambda b, t: (b, 0, 0, 0)),)
        scratch.append(pltpu.VMEM((GDN_SECTIONS, tm + SUBLANES, LANES), F32))
    return pl.pallas_call(
        functools.partial(_inproj_body, seq_rows, tm),
        grid=(b_, l_ // tm),
        in_specs=in_specs,
        out_specs=out_specs,
        out_shape=out_shape,
        scratch_shapes=scratch,
        compiler_params=_params(("arbitrary", "arbitrary")),
        name="inproj_seq" if seq_rows else "inproj_rows",
    )(*args)


def _delta_gates(ba, alog_row, dtb_row):
    beta_all = jax.nn.sigmoid(ba)
    g_all = -jnp.exp(alog_row) * _softplus(ba + dtb_row)
    return beta_all, g_all


def _lane_column(x, lane_idx, lane):
    return jnp.sum(jnp.where(lane_idx == lane, x, 0.0), axis=1, keepdims=True)


def _level_masks():
    r = np.arange(CHUNK)[:, None]
    c = np.arange(CHUNK)[None, :]
    masks = [(r == c + 1) & (r % 2 == 1)]
    half = 2
    while half < CHUNK:
        full = 2 * half
        masks.append((r // full == c // full) & (r % full >= half) & (c % full < half))
        half = full
    return jnp.asarray(np.stack(masks), dtype=BF16)


def _unit_lower_inverses(ms, masks_ref, eye):
    ts = [eye - m * masks_ref[0] for m in ms]
    for lvl in range(1, masks_ref.shape[0]):
        off = masks_ref[lvl]
        xs = [jnp.dot(m * off, t, preferred_element_type=F32).astype(BF16) for m, t in zip(ms, ts)]
        ys = [jnp.dot(t, x, preferred_element_type=F32).astype(BF16) for t, x in zip(ts, xs)]
        ts = [t - y for t, y in zip(ts, ys)]
    return ts


def _cumsum_rows(g, ltri):
    hi = g.astype(BF16)
    r1 = g - hi.astype(F32)
    mid = r1.astype(BF16)
    lo = (r1 - mid.astype(F32)).astype(BF16)
    return (jnp.dot(ltri, hi, preferred_element_type=F32) + jnp.dot(ltri, mid, preferred_element_type=F32)
            + jnp.dot(ltri, lo, preferred_element_type=F32))


def _gated_out_norm(o, gate_act, onw):
    on = o * lax.rsqrt(jnp.mean(o * o, axis=-1, keepdims=True) + EPS) * onw
    return on * gate_act


def _gdn_prompt_body(lt, q_ref, k_ref, v_ref, ba_ref, alog_ref, dtb_ref, gate_ref, onw_ref, masks_ref,
                     og_ref, s_ref):
    @pl.when(pl.program_id(1) == 0)
    def _():
        s_ref[...] = jnp.zeros_like(s_ref)

    beta_all, g_all = _delta_gates(ba_ref[...], alog_ref[...], dtb_ref[...])
    lane_idx = lax.broadcasted_iota(jnp.int32, (CHUNK, LANES), 1)
    row = lax.broadcasted_iota(jnp.int32, (CHUNK, CHUNK), 0)
    col = lax.broadcasted_iota(jnp.int32, (CHUNK, CHUNK), 1)
    tril = row >= col
    strict = row > col
    ltri = jnp.where(tril, 1.0, 0.0).astype(BF16)
    eye = jnp.where(row == col, 1.0, 0.0).astype(BF16)
    onw = onw_ref[...]
    heads = range(GDN_HEADS)
    chunks = range(lt // CHUNK)

    blocks = [(c, j) for c in chunks for j in heads]
    pre = {}
    for c in chunks:
        rows = slice(c * CHUNK, (c + 1) * CHUNK)
        dec = _cumsum_rows(g_all[rows], ltri)
        dec_t = dec.T
        for j in heads:
            q, k, v = q_ref[j, rows, :], k_ref[j, rows, :], v_ref[j, rows, :]
            beta_col = _lane_column(beta_all[rows], lane_idx, j)
            dec_col = _lane_column(dec, lane_idx, GDN_HEADS + j)
            dec_row = dec_t[GDN_HEADS + j:GDN_HEADS + j + 1, :]
            dec_last = dec_row[:, CHUNK - 1:CHUNK]
            gam = jnp.exp(jnp.minimum(dec_col - dec_row, 0.0))
            e_col = jnp.exp(dec_col)
            kb = k * beta_col
            pre[c, j] = dict(q=q, k=k, gam=gam, kb=kb, qe=q * e_col, e_last=jnp.exp(dec_last),
                             kd=k * jnp.exp(dec_last - dec_col),
                             rhs=jnp.concatenate([v * beta_col, kb * e_col], axis=1).astype(BF16))
    grams = [_mm_nt(jnp.concatenate([pre[b]["kb"], pre[b]["q"]], axis=0), pre[b]["k"]) for b in blocks]
    ms = [jnp.where(strict, g[:CHUNK] * pre[b]["gam"], 0.0).astype(BF16) for g, b in zip(grams, blocks)]
    a_intra = {b: jnp.where(tril, g[CHUNK:] * pre[b]["gam"], 0.0) for g, b in zip(grams, blocks)}
    t_inv = _unit_lower_inverses(ms, masks_ref, eye)
    uw = {b: jnp.dot(t, pre[b]["rhs"], preferred_element_type=F32) for t, b in zip(t_inv, blocks)}

    for c in chunks:
        rows = slice(c * CHUNK, (c + 1) * CHUNK)
        s_prev = [s_ref[j] for j in heads]
        ws_qs = [_mm(jnp.concatenate([uw[c, j][:, GDN_DV:], pre[c, j]["qe"]], axis=0), s_prev[j]) for j in heads]
        v_new = [uw[c, j][:, :GDN_DV] - ws_qs[j][:CHUNK] for j in heads]
        o = [ws_qs[j][CHUNK:] + _mm(a_intra[c, j], v_new[j]) for j in heads]
        s_new = [s_prev[j] * pre[c, j]["e_last"] + _mm(pre[c, j]["kd"].T, v_new[j]) for j in heads]
        for j in heads:
            s_ref[j] = s_new[j]
            og = _gated_out_norm(o[j], gate_ref[j, rows, :], onw)
            og_ref[rows, j * GDN_DV:(j + 1) * GDN_DV] = og.astype(og_ref.dtype)


def _gdn_prompt(qkvf, gact, ba, alog_row, dtb_row, onw, masks, lt):
    b_, _, l_, _ = qkvf.shape
    sec = lambda s: pl.BlockSpec((None, GDN_HEADS, lt, LANES), lambda b, t, s=s: (b, s, t, 0))
    return pl.pallas_call(
        functools.partial(_gdn_prompt_body, lt),
        grid=(b_, l_ // lt),
        in_specs=[sec(0), sec(1), sec(2),
                  pl.BlockSpec((None, lt, LANES), lambda b, t: (b, t, 0)),
                  _const_spec(alog_row.shape), _const_spec(dtb_row.shape),
                  pl.BlockSpec((None, GDN_HEADS, lt, LANES), lambda b, t: (b, 0, t, 0)),
                  _const_spec(onw.shape), _const_spec(masks.shape)],
        out_specs=(pl.BlockSpec((None, lt, GDN_V), lambda b, t: (b, t, 0)),
                   pl.BlockSpec((None, GDN_HEADS, GDN_DK, GDN_DV), lambda b, t: (b, 0, 0, 0))),
        out_shape=(jax.ShapeDtypeStruct((b_, l_, GDN_V), BF16),
                   jax.ShapeDtypeStruct((b_, GDN_HEADS, GDN_DK, GDN_DV), F32)),
        compiler_params=_params(("arbitrary", "arbitrary")),
        name="gdn_prompt",
    )(qkvf, qkvf, qkvf, ba, alog_row, dtb_row, gact, onw, masks)


def _gdn_step_body(bb, x_ref, st_ref, cw_ref, ba_ref, alog_ref, dtb_ref, gate_ref, onw_ref, s0_ref,
                   og_ref, sn_ref, q_s, k_s, v_s, b_s, e_s, o_s):
    beta_all, g_all = _delta_gates(ba_ref[...], alog_ref[...], dtb_ref[...])
    lane_idx = lax.broadcasted_iota(jnp.int32, (bb, LANES), 1)
    for h in range(GDN_HEADS):
        feats = []
        for s in range(3):
            idx = s * GDN_HEADS + h
            cols = slice(idx * LANES, (idx + 1) * LANES)
            w = cw_ref[idx]
            y = w[0:1] * st_ref[0, :, cols]
            for tap in range(1, GDN_CONV - 1):
                y = y + w[tap:tap + 1] * st_ref[tap, :, cols]
            y = y + w[GDN_CONV - 1:GDN_CONV] * x_ref[:, cols]
            feats.append(_silu(y))
        q, k, v = feats
        q_s[h] = q * lax.rsqrt(jnp.sum(q * q, axis=-1, keepdims=True) + EPS) * (GDN_DK ** -0.5)
        k_s[h] = k * lax.rsqrt(jnp.sum(k * k, axis=-1, keepdims=True) + EPS)
        v_s[h] = v
        b_s[h] = jnp.broadcast_to(_lane_column(beta_all, lane_idx, h), (bb, LANES))
        e_s[h] = jnp.broadcast_to(jnp.exp(_lane_column(g_all, lane_idx, h + GDN_HEADS)), (bb, LANES))

    eye = (lax.broadcasted_iota(jnp.int32, (GDN_DK, GDN_DK), 0)
           == lax.broadcasted_iota(jnp.int32, (GDN_DK, GDN_DK), 1))

    def to_col(r):
        return jnp.sum(jnp.where(eye, jnp.broadcast_to(r, (GDN_DK, GDN_DK)), 0.0), axis=1, keepdims=True)

    sub = lax.broadcasted_iota(jnp.int32, (SUBLANES, GDN_DK), 0)

    def seq_body(i, carry):
        for h in range(GDN_HEADS):
            one = pl.ds(i, 1)
            k_row = k_s[h, one, :]
            q_row = q_s[h, one, :]
            s1 = s0_ref[i, h] * e_s[h, one, :]
            kq = jnp.where(sub == 0, k_row, jnp.where(sub == 1, q_row, 0.0))
            kq_s1 = _mm(kq, s1)
            delta = (v_s[h, one, :] - kq_s1[0:1, :]) * b_s[h, one, :]
            sn_ref[i, h] = s1 + to_col(k_row) * delta
            qk = jnp.sum(q_row * k_row, axis=1, keepdims=True)
            o_s[h, one, :] = kq_s1[1:2, :] + qk * delta
        return carry

    lax.fori_loop(0, bb, seq_body, 0)
    onw = onw_ref[...]
    for h in range(GDN_HEADS):
        cols = slice(h * GDN_DV, (h + 1) * GDN_DV)
        og = _gated_out_norm(o_s[h], _silu(gate_ref[:, cols]), onw)
        og_ref[:, cols] = og.astype(og_ref.dtype)


def _gdn_step(qkv, st, gate, ba, cw, alog_row, dtb_row, onw, s0, bb):
    n_ = ba.shape[0]
    vec = pltpu.VMEM((GDN_HEADS, bb, LANES), F32)
    return pl.pallas_call(
        functools.partial(_gdn_step_body, bb),
        grid=(n_ // bb,),
        in_specs=[pl.BlockSpec((bb, GDN_CONV_CH), lambda i: (i, 0)),
                  pl.BlockSpec((GDN_CONV - 1, bb, GDN_CONV_CH), lambda i: (0, i, 0)),
                  _const_spec(cw.shape),
                  pl.BlockSpec((bb, LANES), lambda i: (i, 0)),
                  _const_spec(alog_row.shape), _const_spec(dtb_row.shape),
                  pl.BlockSpec((bb, GDN_V), lambda i: (i, 0)),
                  _const_spec(onw.shape),
                  pl.BlockSpec((bb, GDN_HEADS, GDN_DK, GDN_DV), lambda i: (i, 0, 0, 0))],
        out_specs=(pl.BlockSpec((bb, GDN_V), lambda i: (i, 0)),
                   pl.BlockSpec((bb, GDN_HEADS, GDN_DK, GDN_DV), lambda i: (i, 0, 0, 0))),
        out_shape=(jax.ShapeDtypeStruct((n_, GDN_V), BF16),
                   jax.ShapeDtypeStruct(s0.shape, F32)),
        scratch_shapes=[vec, vec, vec, vec, vec, vec],
        compiler_params=_params(("arbitrary",)),
        name="gdn_step",
    )(qkv, st, cw, ba, alog_row, dtb_row, gate, onw, s0)


def _swa_prompt_body(nq, sinks_ref, q_ref, kvp_ref, kvc_ref, o_ref):
    n = pl.program_id(1)
    w = WINDOW
    tiles = SWA_KV // LANES
    pairs = 2
    lo_lane = lax.broadcasted_iota(jnp.int32, (w, LANES), 1) < SWA_HD
    lo_row = lax.broadcasted_iota(jnp.int32, (LANES, w), 0) < SWA_HD
    c = lax.broadcasted_iota(jnp.int32, (w, pairs * w), 0)
    i = lax.broadcasted_iota(jnp.int32, (w, pairs * w), 1) & (w - 1)
    from_prev = c > i
    k_blk, vt_blk = [], []
    for j in range(nq + 1):
        src, rows = (kvp_ref, slice(0, w)) if j == 0 else (kvc_ref, slice((j - 1) * w, j * w))
        k_tiles, vt_tiles = [], []
        for t in range(tiles):
            kx = src[rows, t * LANES:(t + 1) * LANES]
            vt = src[rows, SWA_KV + t * LANES:SWA_KV + (t + 1) * LANES].T
            k_tiles.append((kx.astype(BF16), pltpu.roll(kx, SWA_HD, axis=1).astype(BF16)))
            vt_tiles.append((vt.astype(BF16),
                             jnp.concatenate([vt[SWA_HD:], vt[:SWA_HD]], axis=0).astype(BF16)))
        k_blk.append(k_tiles)
        vt_blk.append(vt_tiles)
    items = [(qb, g, p) for qb in range(nq) for g in range(SWA_KV_HEADS) for p in range(2)]
    log2e = math.log2(math.e)
    qm, kz, vzt, sink = {}, {}, {}, {}
    for qb, g, p in items:
        keep = lo_lane if p == 0 else jnp.logical_not(lo_lane)
        q_tiles = [q_ref[qb * w:(qb + 1) * w, (2 * g + r) * LANES:(2 * g + r + 1) * LANES] for r in range(pairs)]
        qm[qb, g, p] = jnp.concatenate([jnp.where(keep, x * SWA_Q_SCALE, 0.0) for x in q_tiles],
                                       axis=0).astype(BF16)
        variant = 0 if p == g % 2 else 1
        kz[qb, g, p] = jnp.concatenate([k_blk[qb + d][g // 2][variant] for d in range(2)], axis=0)
        vzt[qb, g, p] = jnp.concatenate([vt_blk[qb + d][g // 2][variant] for d in range(2)], axis=1)
        sink[qb, g, p] = jnp.concatenate([jnp.full((1, w), sinks_ref[SWA_GROUP * g + 2 * r + p] * log2e, F32)
                                          for r in range(pairs)], axis=1)
    st = {b: lax.dot_general(kz[b], qm[b], (((1,), (1,)), ((), ())), preferred_element_type=F32) for b in items}
    prev = {b: jnp.where(n > 0, st[b][:w], -jnp.inf) if b[0] == 0 else st[b][:w] for b in items}
    u = {b: jnp.where(from_prev, prev[b], st[b][w:]) for b in items}
    m = {b: jnp.maximum(jnp.max(u[b], axis=0, keepdims=True), sink[b]) for b in items}
    eu = {b: jnp.exp2(u[b] - m[b]) for b in items}
    den = {b: jnp.sum(eu[b], axis=0, keepdims=True) + jnp.exp2(sink[b] - m[b]) for b in items}
    et = {b: jnp.concatenate([jnp.where(from_prev, eu[b], 0.0), jnp.where(from_prev, 0.0, eu[b])],
                             axis=0).astype(BF16) for b in items}
    ot = {b: jnp.dot(vzt[b], et[b], preferred_element_type=F32) / den[b] for b in items}
    for qb in range(nq):
        for g in range(SWA_KV_HEADS):
            for r in range(pairs):
                cols = slice(r * w, (r + 1) * w)
                tile_t = jnp.where(lo_row, ot[qb, g, 0][:, cols], ot[qb, g, 1][:, cols])
                o_ref[qb * w:(qb + 1) * w, (2 * g + r) * LANES:(2 * g + r + 1) * LANES] = (
                    tile_t.T.astype(o_ref.dtype))


def _swa_prompt(sq, skv, sinks, nq):
    b_, l_, _ = sq.shape
    rows = nq * WINDOW
    return pl.pallas_call(
        functools.partial(_swa_prompt_body, nq),
        grid=(b_, l_ // rows),
        in_specs=[pl.BlockSpec(memory_space=pltpu.SMEM),
                  pl.BlockSpec((None, rows, SWA_Q), lambda b, n: (b, n, 0)),
                  pl.BlockSpec((None, WINDOW, 2 * SWA_KV), lambda b, n: (b, jnp.maximum(n * nq - 1, 0), 0)),
                  pl.BlockSpec((None, rows, 2 * SWA_KV), lambda b, n: (b, n, 0))],
        out_specs=pl.BlockSpec((None, rows, SWA_Q), lambda b, n: (b, n, 0)),
        out_shape=jax.ShapeDtypeStruct((b_, l_, SWA_Q), BF16),
        compiler_params=_params(("arbitrary", "arbitrary")),
        name="swa_prompt",
    )(sinks, sq, skv, skv)


def _swa_step_body(bb, q_ref, kvn_ref, ck_ref, cv_ref, sink_ref, o_ref, nk_ref, nv_ref):
    w = WINDOW
    first = pl.program_id(0) * bb
    row = lax.broadcasted_iota(jnp.int32, (SWA_Q_HEADS, SWA_KV), 0)
    lane = lax.broadcasted_iota(jnp.int32, (SWA_Q_HEADS, SWA_KV), 1)
    own = (lane // SWA_HD) == (row // SWA_GROUP)
    newest = lax.broadcasted_iota(jnp.int32, (SWA_KV, w), 1) == w - 1
    sink = sink_ref[...]
    scale = SWA_HD ** -0.5
    seqs = range(bb)
    kn_all = kvn_ref[0:SWA_KV, :]
    vn_all = kvn_ref[SWA_KV:2 * SWA_KV, :]
    nk = [jnp.where(newest, pltpu.roll(kn_all, w - 1 - (first + i), axis=1), pltpu.roll(ck_ref[i], w - 1, axis=1))
          for i in seqs]
    nv = [jnp.where(newest, pltpu.roll(vn_all, w - 1 - (first + i), axis=1), pltpu.roll(cv_ref[i], w - 1, axis=1))
          for i in seqs]
    q_bd = [jnp.where(own, jnp.concatenate([q_ref[i]] * SWA_KV_HEADS, axis=1), 0.0) for i in seqs]
    s = [_mm(q_bd[i], nk[i]) * scale for i in seqs]
    m = [jnp.maximum(jnp.max(s[i], axis=1, keepdims=True), sink) for i in seqs]
    e = [jnp.exp(s[i] - m[i]) for i in seqs]
    den = [jnp.sum(e[i], axis=1, keepdims=True) + jnp.exp(sink - m[i]) for i in seqs]
    pv = [jnp.where(own, _mm_nt(e[i] / den[i], nv[i]), 0.0) for i in seqs]
    for i in seqs:
        o = pv[i][:, 0:SWA_HD]
        for g in range(1, SWA_KV_HEADS):
            o = o + pv[i][:, g * SWA_HD:(g + 1) * SWA_HD]
        o_ref[i] = o
        nk_ref[i] = nk[i]
        nv_ref[i] = nv[i]


def _swa_step(q3, kvn_t, ck_t, cv_t, sink_col, bb):
    n_ = q3.shape[0]
    assert n_ <= WINDOW
    cache = pl.BlockSpec((bb, SWA_KV, WINDOW), lambda i: (i, 0, 0))
    return pl.pallas_call(
        functools.partial(_swa_step_body, bb),
        grid=(n_ // bb,),
        in_specs=[pl.BlockSpec((bb, SWA_Q_HEADS, SWA_HD), lambda i: (i, 0, 0)),
                  _const_spec(kvn_t.shape),
                  cache, cache,
                  _const_spec(sink_col.shape)],
        out_specs=(pl.BlockSpec((bb, SWA_Q_HEADS, SWA_HD), lambda i: (i, 0, 0)), cache, cache),
        out_shape=(jax.ShapeDtypeStruct(q3.shape, F32),
                   jax.ShapeDtypeStruct(ck_t.shape, F32),
                   jax.ShapeDtypeStruct(cv_t.shape, F32)),
        compiler_params=_params(("arbitrary",)),
        name="swa_step",
    )(q3, kvn_t, ck_t, cv_t, sink_col)


def _dense_body(stateful, tm, og_ref, ob_ref, gab_ref, x_ref, gt1_ref, sh2_ref, sc2_ref, gt2_ref,
                n2w_ref, fnw_ref, wa_ref, wb_ref, wo_ref, wg_ref, wu_ref, cw_ref, cb_ref, wd_ref, *rest):
    if stateful:
        st_ref, y_ref, gout_ref, act_ref = rest
    else:
        y_ref, gout_ref, act_ref, gbuf_ref, carry_ref = rest

        @pl.when(pl.program_id(1) == 0)
        def _():
            carry_ref[...] = jnp.zeros_like(carry_ref)

    y_a = jnp.dot(og_ref[...], wa_ref[...], preferred_element_type=F32)
    y_b = jnp.dot(ob_ref[...], wb_ref[...], preferred_element_type=F32)
    merged = (jax.nn.sigmoid(gab_ref[:, 0:D_MODEL]) * y_a
              + jax.nn.sigmoid(gab_ref[:, D_MODEL:2 * D_MODEL]) * y_b)
    x1 = x_ref[...] + gt1_ref[...] * _mm(merged, wo_ref[...])
    h2 = (_rms(x1, n2w_ref[...]) * (1.0 + sc2_ref[...]) + sh2_ref[...]).astype(BF16)

    for c in range(D_FF // FFN_COLS):
        cols = slice(c * FFN_COLS, (c + 1) * FFN_COLS)
        gate = jnp.dot(h2, wg_ref[:, cols], preferred_element_type=F32)
        up = jnp.dot(h2, wu_ref[:, cols], preferred_element_type=F32)
        if stateful:
            g2 = st_ref[0, :, cols]
            g1 = st_ref[1, :, cols]
            gout_ref[:, cols] = gate
        else:
            gbuf_ref[0:SUBLANES, :] = carry_ref[:, cols]
            gbuf_ref[SUBLANES:SUBLANES + tm, :] = gate
            g2 = gbuf_ref[SUBLANES - 2:SUBLANES - 2 + tm, :]
            g1 = gbuf_ref[SUBLANES - 1:SUBLANES - 1 + tm, :]
            carry_ref[:, cols] = gbuf_ref[tm:tm + SUBLANES, :]
        gc = (cw_ref[0:1, cols] * g2 + cw_ref[1:2, cols] * g1 + cw_ref[2:3, cols] * gate) + cb_ref[:, cols]
        act_ref[:, cols] = (_silu(gc) * up).astype(BF16)
    if not stateful:
        gout_ref[...] = carry_ref[...]

    x2 = x1 + gt2_ref[...] * jnp.dot(act_ref[...], wd_ref[...], preferred_element_type=F32)
    y_ref[...] = _rms(x2, fnw_ref[...])


def _dense(og, ob, gab, x, mods, vecs, ws, st, tm):
    b_, l_, _ = x.shape
    r_ = mods[0].shape[1]
    rt = 1 if r_ == 1 else tm
    mod_map = (lambda b, t: (b, 0, 0)) if r_ == 1 else (lambda b, t: (b, t, 0))
    row_map = lambda b, t: (b, t, 0)
    stateful = st is not None
    in_specs = ([pl.BlockSpec((None, tm, D_MODEL), row_map),
                 pl.BlockSpec((None, tm, D_MODEL), row_map),
                 pl.BlockSpec((None, tm, 2 * D_MODEL), row_map),
                 pl.BlockSpec((None, tm, D_MODEL), row_map)]
                + [pl.BlockSpec((None, rt, D_MODEL), mod_map)] * 4
                + [_const_spec(a.shape) for a in vecs[:2]]
                + [_const_spec(ws[0].shape), _const_spec(ws[1].shape), _const_spec(ws[2].shape),
                   _const_spec(ws[3].shape), _const_spec(ws[4].shape),
                   _const_spec(vecs[2].shape), _const_spec(vecs[3].shape), _const_spec(ws[5].shape)])
    args = [og, ob, gab, x, *mods, vecs[0], vecs[1], ws[0], ws[1], ws[2], ws[3], ws[4], vecs[2], vecs[3], ws[5]]
    scratch = [pltpu.VMEM((tm, D_FF), BF16)]
    if stateful:
        in_specs.append(pl.BlockSpec((FFN_CONV - 1, None, tm, D_FF), lambda b, t: (0, b, t, 0)))
        args.append(st)
        gout_shape = jax.ShapeDtypeStruct((b_, l_, D_FF), F32)
        gout_spec = pl.BlockSpec((None, tm, D_FF), row_map)
    else:
        scratch += [pltpu.VMEM((tm + SUBLANES, FFN_COLS), F32), pltpu.VMEM((SUBLANES, D_FF), F32)]
        gout_shape = jax.ShapeDtypeStruct((b_, SUBLANES, D_FF), F32)
        gout_spec = pl.BlockSpec((None, SUBLANES, D_FF), lambda b, t: (b, 0, 0))
    return pl.pallas_call(
        functools.partial(_dense_body, stateful, tm),
        grid=(b_, l_ // tm),
        in_specs=in_specs,
        out_specs=(pl.BlockSpec((None, tm, D_MODEL), row_map), gout_spec),
        out_shape=(jax.ShapeDtypeStruct((b_, l_, D_MODEL), F32), gout_shape),
        scratch_shapes=scratch,
        compiler_params=_params(("arbitrary", "arbitrary")),
        name="dense_step" if stateful else "dense_prompt",
    )(*args)


def _lane_row(values, offset):
    return jnp.zeros((1, LANES), F32).at[0, offset:offset + values.shape[0]].set(values)


def kernel(x_prompt, x_sample, c_prompt, c_sample, state_gdn_S, state_gdn_conv, cache_swa_k, cache_swa_v,
           state_ffn_conv, w_mod, b_mod, norm1_w, norm2_w, w_in, gdn_conv_w, gdn_a_log, gdn_dt_bias,
           gdn_onorm_w, w_gdn_out, swa_sinks, w_swa_out, w_o, w_ffn_gate, w_ffn_up, ffn_conv_w, ffn_conv_b,
           w_ffn_down, final_norm_w):
    assert w_mod.shape[0] == 1, "single-layer trunk"
    nb, seq, _ = x_prompt.shape
    ns = x_sample.shape[0]
    assert x_sample.shape[1] == 1

    in_ws = _in_weight(jnp.transpose(w_in[0]))
    dense_ws = (w_gdn_out[0].astype(BF16), w_swa_out[0].astype(BF16), w_o[0].astype(BF16),
                w_ffn_gate[0].astype(BF16), w_ffn_up[0].astype(BF16), w_ffn_down[0].astype(BF16))
    dense_vecs = (norm2_w, final_norm_w[None, :], ffn_conv_w[0], ffn_conv_b)
    cw = jnp.transpose(gdn_conv_w[0].reshape(GDN_CONV, GDN_SECTIONS, LANES), (1, 0, 2))
    alog_row = _lane_row(gdn_a_log[0], GDN_HEADS)
    dtb_row = _lane_row(gdn_dt_bias[0], GDN_HEADS)

    mod = _modulation(jnp.concatenate([c_prompt, c_sample], axis=0), w_mod[0], b_mod)
    mod_p = [mod[:nb, i * D_MODEL:(i + 1) * D_MODEL][:, None, :] for i in range(6)]
    mod_s = [mod[nb:, i * D_MODEL:(i + 1) * D_MODEL][None, :, :] for i in range(6)]

    qkvf, gact, ba, sq, skv, gab, qkv_tail = _inproj(x_prompt, mod_p[0], mod_p[1], norm1_w, in_ws, cw, tm=256)
    og, gdn_s_p = _gdn_prompt(qkvf, gact, ba, alog_row, dtb_row, gdn_onorm_w, _level_masks(), lt=4 * CHUNK)
    ob = _swa_prompt(sq, skv, swa_sinks[0], nq=4)
    y_p, gate_tail = _dense(og, ob, gab, x_prompt, (mod_p[2], mod_p[3], mod_p[4], mod_p[5]),
                            dense_vecs, dense_ws, None, tm=512)
    gdn_conv_p = jnp.transpose(qkv_tail[:, :, SUBLANES - (GDN_CONV - 1):, :], (0, 2, 1, 3)).reshape(
        nb, GDN_CONV - 1, GDN_CONV_CH)
    k_p = skv[:, seq - WINDOW:, :SWA_KV].reshape(nb, WINDOW, SWA_KV_HEADS, SWA_HD)
    v_p = skv[:, seq - WINDOW:, SWA_KV:].reshape(nb, WINDOW, SWA_KV_HEADS, SWA_HD)
    ffn_conv_p = gate_tail[:, SUBLANES - (FFN_CONV - 1):, :]

    xs = x_sample.reshape(1, ns, D_MODEL)
    qkvs, gates, bas, sqs, skvs, gabs = _inproj(xs, mod_s[0], mod_s[1], norm1_w, in_ws, None, tm=ns)
    st_gdn = jnp.transpose(state_gdn_conv[0], (1, 0, 2))
    og_s, gdn_s_s = _gdn_step(qkvs[0], st_gdn, gates[0], bas[0], cw, alog_row, dtb_row, gdn_onorm_w,
                              state_gdn_S[0], bb=8)
    to_channel_major = lambda c: jnp.transpose(c, (0, 2, 3, 1)).reshape(ns, SWA_KV, WINDOW)
    from_channel_major = lambda c: jnp.transpose(c.reshape(ns, SWA_KV_HEADS, SWA_HD, WINDOW), (0, 3, 1, 2))
    kvn_t = jnp.pad(jnp.transpose(skvs[0]), ((0, 0), (0, WINDOW - ns)))
    o3, k_s, v_s = _swa_step(sqs[0].reshape(ns, SWA_Q_HEADS, SWA_HD), kvn_t,
                             to_channel_major(cache_swa_k[0]), to_channel_major(cache_swa_v[0]),
                             swa_sinks[0][:, None], bb=8)
    ob_s = o3.reshape(1, ns, SWA_Q).astype(BF16)
    st_ffn = jnp.transpose(state_ffn_conv[0], (1, 0, 2))[:, None]
    y_s, gate_new = _dense(og_s[None], ob_s, gabs, xs, (mod_s[2], mod_s[3], mod_s[4], mod_s[5]),
                           dense_vecs, dense_ws, st_ffn, tm=ns)
    gdn_conv_s = jnp.concatenate([state_gdn_conv[0][:, 1:], qkvs[0][:, None, :]], axis=1)
    ffn_conv_s = jnp.concatenate([state_ffn_conv[0][:, 1:], gate_new[0][:, None, :]], axis=1)

    return (y_p, y_s.reshape(ns, 1, D_MODEL),
            gdn_s_p[None], gdn_s_s[None],
            gdn_conv_p[None], gdn_conv_s[None],
            k_p[None], from_channel_major(k_s)[None],
            v_p[None], from_channel_major(v_s)[None],
            ffn_conv_p[None], ffn_conv_s[None])
```

```python
import functools
import math

import numpy as np
import jax
import jax.numpy as jnp
from jax import lax
from jax.experimental import pallas as pl
from jax.experimental.pallas import tpu as pltpu

F32 = jnp.float32
BF16 = jnp.bfloat16

D_MODEL = 1024
GDN_HEADS = 8
GDN_DK = 128
GDN_DV = 128
GDN_QK = GDN_HEADS * GDN_DK
GDN_V = GDN_HEADS * GDN_DV
GDN_CONV = 4
GDN_CONV_CH = 2 * GDN_QK + GDN_V
GDN_SECTIONS = GDN_CONV_CH // 128
SWA_Q_HEADS = 16
SWA_KV_HEADS = 4
SWA_GROUP = SWA_Q_HEADS // SWA_KV_HEADS
SWA_HD = 64
SWA_Q = SWA_Q_HEADS * SWA_HD
SWA_KV = SWA_KV_HEADS * SWA_HD
WINDOW = 128
D_FF = 2816
FFN_CONV = 3
EPS = 1e-6

LANES = 128
SUBLANES = 8
VMEM_LIMIT = 56 * 1024 * 1024

COL_QKV = 0
COL_GATE = COL_QKV + GDN_CONV_CH
COL_SQ = COL_GATE + GDN_V
COL_SKV = COL_SQ + SWA_Q
COL_GAB = COL_SKV + 2 * SWA_KV
COL_BA = COL_GAB + 2 * D_MODEL
IN_COLS = COL_BA + LANES

SWA_Q_SCALE = SWA_HD ** -0.5 * math.log2(math.e)

CONV_ROWS = 64
CHUNK = 128
GDN_GROUP = 16
FFN_COLS = 256


def _mm(a, b):
    return jnp.dot(a.astype(BF16), b.astype(BF16), preferred_element_type=F32)


def _mm_nt(a, b):
    return lax.dot_general(a.astype(BF16), b.astype(BF16), (((1,), (1,)), ((), ())),
                           preferred_element_type=F32)


def _silu(x):
    return x * jax.nn.sigmoid(x)


def _softplus(x):
    return jnp.maximum(x, 0.0) + jnp.log1p(jnp.exp(-jnp.abs(x)))


def _rms(x, w):
    return x * lax.rsqrt(jnp.mean(x * x, axis=-1, keepdims=True) + EPS) * w


def _const_spec(shape):
    n = len(shape)
    return pl.BlockSpec(shape, lambda *_: (0,) * n, pipeline_mode=pl.Buffered(1))


def _params(sem):
    return pltpu.CompilerParams(dimension_semantics=sem, vmem_limit_bytes=VMEM_LIMIT)


def _mod_body(c_ref, w_ref, b_ref, o_ref):
    o_ref[...] = _mm(_silu(c_ref[...]), w_ref[...]) + b_ref[...]


def _modulation(c_all, w_mod, b_mod):
    rows = c_all.shape[0]
    n_out = w_mod.shape[1]
    tn = D_MODEL
    return pl.pallas_call(
        _mod_body,
        grid=(n_out // tn,),
        in_specs=[pl.BlockSpec((rows, D_MODEL), lambda j: (0, 0)),
                  pl.BlockSpec((D_MODEL, tn), lambda j: (0, j)),
                  pl.BlockSpec((1, tn), lambda j: (0, j))],
        out_specs=pl.BlockSpec((rows, tn), lambda j: (0, j)),
        out_shape=jax.ShapeDtypeStruct((rows, n_out), F32),
        compiler_params=_params(("arbitrary",)),
        name="modulation",
    )(c_all, w_mod, b_mod)


IN_WEIGHT_COLS = 512
IN_WEIGHT_PAD = 256


def _in_weight_body(n_main, wt_ref, ba_ref, o_ref):
    j = pl.program_id(0)

    @pl.when(j < n_main)
    def _():
        o_ref[...] = wt_ref[...].T.astype(BF16)

    @pl.when(j == n_main)
    def _():
        n_ba = ba_ref.shape[0]
        ba = jnp.concatenate([ba_ref[...].T, jnp.zeros((D_MODEL, IN_WEIGHT_COLS - n_ba), F32)], axis=1)
        o_ref[...] = ba.astype(BF16)


def _in_weight(w_t):
    n_ba = 2 * GDN_HEADS
    split = GDN_CONV_CH + GDN_V
    tc = IN_WEIGHT_COLS
    n_main = COL_BA // tc

    def src_row(j):
        jj = jnp.minimum(j, n_main - 1)
        return pl.multiple_of(jnp.where(jj * tc < split, jj * tc, jj * tc + n_ba), n_ba)

    return pl.pallas_call(
        functools.partial(_in_weight_body, n_main),
        grid=(n_main + 1,),
        in_specs=[pl.BlockSpec((pl.Element(tc), pl.Element(D_MODEL)), lambda j: (src_row(j), 0)),
                  pl.BlockSpec((pl.Element(n_ba), pl.Element(D_MODEL)), lambda j: (split, 0))],
        out_specs=pl.BlockSpec((D_MODEL, tc), lambda j: (0, j)),
        out_shape=jax.ShapeDtypeStruct((D_MODEL, COL_BA + IN_WEIGHT_PAD), BF16),
        compiler_params=_params(("arbitrary",)),
        name="in_weight",
    )(w_t, w_t)


def _l2norm(x):
    return x * lax.rsqrt(jnp.sum(x * x, axis=-1, keepdims=True) + EPS)


def _inproj_body(seq_rows, tm, x_ref, sh_ref, sc_ref, nw_ref, w_ref, *rest):
    if seq_rows:
        cw_ref, qkv_ref, gg_ref, ba_ref, sq_ref, skv_ref, gab_ref, tail_ref, xe_ref = rest

        @pl.when(pl.program_id(1) == 0)
        def _():
            xe_ref[:, 0:SUBLANES, :] = jnp.zeros((GDN_SECTIONS, SUBLANES, LANES), F32)
    else:
        qkv_ref, gg_ref, ba_ref, sq_ref, skv_ref, gab_ref = rest

    h = _rms(x_ref[...], nw_ref[...]) * (1.0 + sc_ref[...]) + sh_ref[...]
    hb = h.astype(BF16)

    def proj(lo, width):
        return jnp.dot(hb, w_ref[:, lo:lo + width], preferred_element_type=F32)

    step = 512
    per = step // LANES
    for c in range(GDN_CONV_CH // step):
        z = proj(COL_QKV + c * step, step)
        for k in range(per):
            s = c * per + k
            zs = z[:, k * LANES:(k + 1) * LANES]
            if not seq_rows:
                qkv_ref[:, s * LANES:(s + 1) * LANES] = zs
                continue
            xe_ref[s, SUBLANES:SUBLANES + tm, :] = zs
            w = cw_ref[s]
            for r0 in range(0, tm, CONV_ROWS):
                y = w[0:1] * xe_ref[s, r0 + SUBLANES - 3:r0 + SUBLANES - 3 + CONV_ROWS, :]
                for tap in range(1, GDN_CONV):
                    lo = r0 + SUBLANES - 3 + tap
                    y = y + w[tap:tap + 1] * xe_ref[s, lo:lo + CONV_ROWS, :]
                f = _silu(y)
                if s < GDN_HEADS:
                    f = _l2norm(f) * (GDN_DK ** -0.5)
                elif s < 2 * GDN_HEADS:
                    f = _l2norm(f)
                qkv_ref[s, r0:r0 + CONV_ROWS, :] = f
            xe_ref[s, 0:SUBLANES, :] = xe_ref[s, tm:tm + SUBLANES, :]
    if seq_rows:
        tail_ref[...] = xe_ref[:, 0:SUBLANES, :]
    for c in range(GDN_V // step):
        z = proj(COL_GATE + c * step, step)
        for k in range(per):
            zs = z[:, k * LANES:(k + 1) * LANES]
            if seq_rows:
                gg_ref[c * per + k] = _silu(zs)
            else:
                gg_ref[:, (c * per + k) * LANES:(c * per + k + 1) * LANES] = zs
    ba_ref[...] = proj(COL_BA, LANES)
    for c in range(SWA_Q // step):
        sq_ref[:, c * step:(c + 1) * step] = proj(COL_SQ + c * step, step)
    skv_ref[...] = proj(COL_SKV, 2 * SWA_KV)
    for c in range(2 * D_MODEL // step):
        gab_ref[:, c * step:(c + 1) * step] = proj(COL_GAB + c * step, step)


def _inproj(x, sh, sc, nw, w_all, cw, tm):
    b_, l_, _ = x.shape
    r_ = sh.shape[1]
    rt = 1 if r_ == 1 else tm
    mod_map = (lambda b, t: (b, 0, 0)) if r_ == 1 else (lambda b, t: (b, t, 0))
    row_map = lambda b, t: (b, t, 0)
    head_map = lambda b, t: (b, 0, t, 0)
    seq_rows = cw is not None
    if seq_rows:
        gdn_shapes = (jax.ShapeDtypeStruct((b_, GDN_SECTIONS, l_, LANES), F32),
                      jax.ShapeDtypeStruct((b_, GDN_HEADS, l_, LANES), F32))
        gdn_specs = (pl.BlockSpec((None, GDN_SECTIONS, tm, LANES), head_map),
                     pl.BlockSpec((None, GDN_HEADS, tm, LANES), head_map))
    else:
        gdn_shapes = (jax.ShapeDtypeStruct((b_, l_, GDN_CONV_CH), F32),
                      jax.ShapeDtypeStruct((b_, l_, GDN_V), F32))
        gdn_specs = (pl.BlockSpec((None, tm, GDN_CONV_CH), row_map),
                     pl.BlockSpec((None, tm, GDN_V), row_map))
    out_shape = gdn_shapes + (
        jax.ShapeDtypeStruct((b_, l_, LANES), F32),
        jax.ShapeDtypeStruct((b_, l_, SWA_Q), F32),
        jax.ShapeDtypeStruct((b_, l_, 2 * SWA_KV), F32),
        jax.ShapeDtypeStruct((b_, l_, 2 * D_MODEL), F32),
    )
    out_specs = gdn_specs + (
        pl.BlockSpec((None, tm, LANES), row_map),
        pl.BlockSpec((None, tm, SWA_Q), row_map),
        pl.BlockSpec((None, tm, 2 * SWA_KV), row_map),
        pl.BlockSpec((None, tm, 2 * D_MODEL), row_map),
    )
    in_specs = [
        pl.BlockSpec((None, tm, D_MODEL), row_map),
        pl.BlockSpec((None, rt, D_MODEL), mod_map),
        pl.BlockSpec((None, rt, D_MODEL), mod_map),
        _const_spec(nw.shape),
        _const_spec(w_all.shape),
    ]
    args = [x, sh, sc, nw, w_all]
    scratch = []
    if seq_rows:
        in_specs.append(_const_spec(cw.shape))
        args.append(cw)
        out_shape += (jax.ShapeDtypeStruct((b_, GDN_SECTIONS, SUBLANES, LANES), F32),)
        out_specs += (pl.BlockSpec((None, GDN_SECTIONS, SUBLANES, LANES), lambda b, t: (b, 0, 0, 0)),)
        scratch.append(pltpu.VMEM((GDN_SECTIONS, tm + SUBLANES, LANES), F32))
    return pl.pallas_call(
        functools.partial(_inproj_body, seq_rows, tm),
        grid=(b_, l_ // tm),
        in_specs=in_specs,
        out_specs=out_specs,
        out_shape=out_shape,
        scratch_shapes=scratch,
        compiler_params=_params(("arbitrary", "arbitrary")),
        name="inproj_seq" if seq_rows else "inproj_rows",
    )(*args)


def _delta_gates(ba, alog_row, dtb_row):
    beta_all = jax.nn.sigmoid(ba)
    g_all = -jnp.exp(alog_row) * _softplus(ba + dtb_row)
    return beta_all, g_all


def _lane_column(x, lane_idx, lane):
    return jnp.sum(jnp.where(lane_idx == lane, x, 0.0), axis=1, keepdims=True)


def _level_masks():
    r = np.arange(CHUNK)[:, None]
    c = np.arange(CHUNK)[None, :]
    masks = [(r == c + 1) & (r % 2 == 1)]
    half = 2
    while half < CHUNK:
        full = 2 * half
        masks.append((r // full == c // full) & (r % full >= half) & (c % full < half))
        half = full
    return jnp.asarray(np.stack(masks), dtype=BF16)


def _unit_lower_inverses(ms, masks_ref, eye, between_levels=()):
    ts = [eye - m * masks_ref[0] for m in ms]
    pending = list(between_levels)
    for lvl in range(1, masks_ref.shape[0]):
        off = masks_ref[lvl]
        xs = [jnp.dot(m * off, t, preferred_element_type=F32).astype(BF16) for m, t in zip(ms, ts)]
        ys = [jnp.dot(t, x, preferred_element_type=F32).astype(BF16) for t, x in zip(ts, xs)]
        ts = [t - y for t, y in zip(ts, ys)]
        if pending:
            pending.pop(0)()
    for piece in pending:
        piece()
    return ts


def _cumsum_rows(g, ltri):
    hi = g.astype(BF16)
    r1 = g - hi.astype(F32)
    mid = r1.astype(BF16)
    lo = (r1 - mid.astype(F32)).astype(BF16)
    return (jnp.dot(ltri, hi, preferred_element_type=F32) + jnp.dot(ltri, mid, preferred_element_type=F32)
            + jnp.dot(ltri, lo, preferred_element_type=F32))


def _gated_out_norm(o, gate_act, onw):
    on = o * lax.rsqrt(jnp.mean(o * o, axis=-1, keepdims=True) + EPS) * onw
    return on * gate_act


def _gdn_prompt_body(lt, q_ref, k_ref, v_ref, ba_ref, alog_ref, dtb_ref, gate_ref, onw_ref, masks_ref,
                     og_ref, s_ref):
    @pl.when(pl.program_id(1) == 0)
    def _():
        s_ref[...] = jnp.zeros_like(s_ref)

    beta_all, g_all = _delta_gates(ba_ref[...], alog_ref[...], dtb_ref[...])
    lane_idx = lax.broadcasted_iota(jnp.int32, (CHUNK, LANES), 1)
    row = lax.broadcasted_iota(jnp.int32, (CHUNK, CHUNK), 0)
    col = lax.broadcasted_iota(jnp.int32, (CHUNK, CHUNK), 1)
    tril = row >= col
    strict = row > col
    ltri = jnp.where(tril, 1.0, 0.0).astype(BF16)
    eye = jnp.where(row == col, 1.0, 0.0).astype(BF16)
    onw = onw_ref[...]
    heads = range(GDN_HEADS)
    chunks = range(lt // CHUNK)

    blocks = [(c, j) for c in chunks for j in heads]
    pre = {}
    for c in chunks:
        rows = slice(c * CHUNK, (c + 1) * CHUNK)
        dec = _cumsum_rows(g_all[rows], ltri)
        dec_t = dec.T
        for j in heads:
            q, k, v = q_ref[j, rows, :], k_ref[j, rows, :], v_ref[j, rows, :]
            beta_col = _lane_column(beta_all[rows], lane_idx, j)
            dec_col = _lane_column(dec, lane_idx, GDN_HEADS + j)
            dec_row = dec_t[GDN_HEADS + j:GDN_HEADS + j + 1, :]
            dec_last = dec_row[:, CHUNK - 1:CHUNK]
            gam = jnp.exp(jnp.minimum(dec_col - dec_row, 0.0))
            e_col = jnp.exp(dec_col)
            kb = k * beta_col
            pre[c, j] = dict(q=q, k=k, gam=gam, kb=kb, qe=q * e_col, e_last=jnp.exp(dec_last),
                             kd=k * jnp.exp(dec_last - dec_col),
                             rhs=jnp.concatenate([v * beta_col, kb * e_col], axis=1).astype(BF16))
    a_intra, uw = {}, {}

    def recurrence(c):
        rows = slice(c * CHUNK, (c + 1) * CHUNK)
        mid = {}

        def read_out():
            mid["s"] = [s_ref[j] for j in heads]
            mid["ws_qs"] = [_mm(jnp.concatenate([uw[c, j][:, GDN_DV:], pre[c, j]["qe"]], axis=0), mid["s"][j])
                            for j in heads]

        def update():
            s_prev, ws_qs = mid["s"], mid["ws_qs"]
            v_new = [uw[c, j][:, :GDN_DV] - ws_qs[j][:CHUNK] for j in heads]
            o = [ws_qs[j][CHUNK:] + _mm(a_intra[c, j], v_new[j]) for j in heads]
            s_new = [s_prev[j] * pre[c, j]["e_last"] + _mm(pre[c, j]["kd"].T, v_new[j]) for j in heads]
            for j in heads:
                s_ref[j] = s_new[j]
                og = _gated_out_norm(o[j], gate_ref[j, rows, :], onw)
                og_ref[rows, j * GDN_DV:(j + 1) * GDN_DV] = og.astype(og_ref.dtype)

        return [read_out, update]

    carried = []
    for g0 in range(0, len(blocks), GDN_GROUP):
        grp = blocks[g0:g0 + GDN_GROUP]
        grams = [_mm_nt(jnp.concatenate([pre[b]["kb"], pre[b]["q"]], axis=0), pre[b]["k"]) for b in grp]
        ms = [jnp.where(strict, g[:CHUNK] * pre[b]["gam"], 0.0).astype(BF16) for g, b in zip(grams, grp)]
        a_intra.update({b: jnp.where(tril, g[CHUNK:] * pre[b]["gam"], 0.0) for g, b in zip(grams, grp)})
        t_inv = _unit_lower_inverses(ms, masks_ref, eye, carried)
        uw.update({b: jnp.dot(t, pre[b]["rhs"], preferred_element_type=F32) for t, b in zip(t_inv, grp)})
        carried = [piece for c in sorted({c for c, _ in grp}) for piece in recurrence(c)]
    for piece in carried:
        piece()


def _gdn_prompt(qkvf, gact, ba, alog_row, dtb_row, onw, masks, lt):
    b_, _, l_, _ = qkvf.shape
    sec = lambda s: pl.BlockSpec((None, GDN_HEADS, lt, LANES), lambda b, t, s=s: (b, s, t, 0))
    return pl.pallas_call(
        functools.partial(_gdn_prompt_body, lt),
        grid=(b_, l_ // lt),
        in_specs=[sec(0), sec(1), sec(2),
                  pl.BlockSpec((None, lt, LANES), lambda b, t: (b, t, 0)),
                  _const_spec(alog_row.shape), _const_spec(dtb_row.shape),
                  pl.BlockSpec((None, GDN_HEADS, lt, LANES), lambda b, t: (b, 0, t, 0)),
                  _const_spec(onw.shape), _const_spec(masks.shape)],
        out_specs=(pl.BlockSpec((None, lt, GDN_V), lambda b, t: (b, t, 0)),
                   pl.BlockSpec((None, GDN_HEADS, GDN_DK, GDN_DV), lambda b, t: (b, 0, 0, 0))),
        out_shape=(jax.ShapeDtypeStruct((b_, l_, GDN_V), BF16),
                   jax.ShapeDtypeStruct((b_, GDN_HEADS, GDN_DK, GDN_DV), F32)),
        compiler_params=_params(("arbitrary", "arbitrary")),
        name="gdn_prompt",
    )(qkvf, qkvf, qkvf, ba, alog_row, dtb_row, gact, onw, masks)


def _gdn_step_body(bb, x_ref, st_ref, cw_ref, ba_ref, alog_ref, dtb_ref, gate_ref, onw_ref, s0_ref,
                   og_ref, sn_ref, q_s, k_s, v_s, b_s, e_s, o_s):
    beta_all, g_all = _delta_gates(ba_ref[...], alog_ref[...], dtb_ref[...])
    lane_idx = lax.broadcasted_iota(jnp.int32, (bb, LANES), 1)
    for h in range(GDN_HEADS):
        feats = []
        for s in range(3):
            idx = s * GDN_HEADS + h
            cols = slice(idx * LANES, (idx + 1) * LANES)
            w = cw_ref[idx]
            y = w[0:1] * st_ref[0, :, cols]
            for tap in range(1, GDN_CONV - 1):
                y = y + w[tap:tap + 1] * st_ref[tap, :, cols]
            y = y + w[GDN_CONV - 1:GDN_CONV] * x_ref[:, cols]
            feats.append(_silu(y))
        q, k, v = feats
        q_s[h] = q * lax.rsqrt(jnp.sum(q * q, axis=-1, keepdims=True) + EPS) * (GDN_DK ** -0.5)
        k_s[h] = k * lax.rsqrt(jnp.sum(k * k, axis=-1, keepdims=True) + EPS)
        v_s[h] = v
        b_s[h] = jnp.broadcast_to(_lane_column(beta_all, lane_idx, h), (bb, LANES))
        e_s[h] = jnp.broadcast_to(jnp.exp(_lane_column(g_all, lane_idx, h + GDN_HEADS)), (bb, LANES))

    eye = (lax.broadcasted_iota(jnp.int32, (GDN_DK, GDN_DK), 0)
           == lax.broadcasted_iota(jnp.int32, (GDN_DK, GDN_DK), 1))

    def to_col(r):
        return jnp.sum(jnp.where(eye, jnp.broadcast_to(r, (GDN_DK, GDN_DK)), 0.0), axis=1, keepdims=True)

    sub = lax.broadcasted_iota(jnp.int32, (SUBLANES, GDN_DK), 0)

    def seq_body(i, carry):
        for h in range(GDN_HEADS):
            one = pl.ds(i, 1)
            k_row = k_s[h, one, :]
            q_row = q_s[h, one, :]
            s1 = s0_ref[i, h] * e_s[h, one, :]
            kq = jnp.where(sub == 0, k_row, jnp.where(sub == 1, q_row, 0.0))
            kq_s1 = _mm(kq, s1)
            delta = (v_s[h, one, :] - kq_s1[0:1, :]) * b_s[h, one, :]
            sn_ref[i, h] = s1 + to_col(k_row) * delta
            qk = jnp.sum(q_row * k_row, axis=1, keepdims=True)
            o_s[h, one, :] = kq_s1[1:2, :] + qk * delta
        return carry

    lax.fori_loop(0, bb, seq_body, 0)
    onw = onw_ref[...]
    for h in range(GDN_HEADS):
        cols = slice(h * GDN_DV, (h + 1) * GDN_DV)
        og = _gated_out_norm(o_s[h], _silu(gate_ref[:, cols]), onw)
        og_ref[:, cols] = og.astype(og_ref.dtype)


def _gdn_step(qkv, st, gate, ba, cw, alog_row, dtb_row, onw, s0, bb):
    n_ = ba.shape[0]
    vec = pltpu.VMEM((GDN_HEADS, bb, LANES), F32)
    return pl.pallas_call(
        functools.partial(_gdn_step_body, bb),
        grid=(n_ // bb,),
        in_specs=[pl.BlockSpec((bb, GDN_CONV_CH), lambda i: (i, 0)),
                  pl.BlockSpec((GDN_CONV - 1, bb, GDN_CONV_CH), lambda i: (0, i, 0)),
                  _const_spec(cw.shape),
                  pl.BlockSpec((bb, LANES), lambda i: (i, 0)),
                  _const_spec(alog_row.shape), _const_spec(dtb_row.shape),
                  pl.BlockSpec((bb, GDN_V), lambda i: (i, 0)),
                  _const_spec(onw.shape),
                  pl.BlockSpec((bb, GDN_HEADS, GDN_DK, GDN_DV), lambda i: (i, 0, 0, 0))],
        out_specs=(pl.BlockSpec((bb, GDN_V), lambda i: (i, 0)),
                   pl.BlockSpec((bb, GDN_HEADS, GDN_DK, GDN_DV), lambda i: (i, 0, 0, 0))),
        out_shape=(jax.ShapeDtypeStruct((n_, GDN_V), BF16),
                   jax.ShapeDtypeStruct(s0.shape, F32)),
        scratch_shapes=[vec, vec, vec, vec, vec, vec],
        compiler_params=_params(("arbitrary",)),
        name="gdn_step",
    )(qkv, st, cw, ba, alog_row, dtb_row, gate, onw, s0)


def _swa_prompt_body(nq, sinks_ref, q_ref, kvp_ref, kvc_ref, o_ref):
    n = pl.program_id(1)
    w = WINDOW
    tiles = SWA_KV // LANES
    pairs = 2
    lo_lane = lax.broadcasted_iota(jnp.int32, (w, LANES), 1) < SWA_HD
    lo_row = lax.broadcasted_iota(jnp.int32, (LANES, w), 0) < SWA_HD
    c = lax.broadcasted_iota(jnp.int32, (w, pairs * w), 0)
    i = lax.broadcasted_iota(jnp.int32, (w, pairs * w), 1) & (w - 1)
    from_prev = c > i
    k_blk, vt_blk = [], []
    for j in range(nq + 1):
        src, rows = (kvp_ref, slice(0, w)) if j == 0 else (kvc_ref, slice((j - 1) * w, j * w))
        k_tiles, vt_tiles = [], []
        for t in range(tiles):
            kx = src[rows, t * LANES:(t + 1) * LANES]
            vt = src[rows, SWA_KV + t * LANES:SWA_KV + (t + 1) * LANES].T
            k_tiles.append((kx.astype(BF16), pltpu.roll(kx, SWA_HD, axis=1).astype(BF16)))
            vt_tiles.append((vt.astype(BF16),
                             jnp.concatenate([vt[SWA_HD:], vt[:SWA_HD]], axis=0).astype(BF16)))
        k_blk.append(k_tiles)
        vt_blk.append(vt_tiles)
    items = [(qb, g, p) for qb in range(nq) for g in range(SWA_KV_HEADS) for p in range(2)]
    log2e = math.log2(math.e)
    qm, kz, vzt, sink = {}, {}, {}, {}
    for qb, g, p in items:
        keep = lo_lane if p == 0 else jnp.logical_not(lo_lane)
        q_tiles = [q_ref[qb * w:(qb + 1) * w, (2 * g + r) * LANES:(2 * g + r + 1) * LANES] for r in range(pairs)]
        qm[qb, g, p] = jnp.concatenate([jnp.where(keep, x * SWA_Q_SCALE, 0.0) for x in q_tiles],
                                       axis=0).astype(BF16)
        variant = 0 if p == g % 2 else 1
        kz[qb, g, p] = jnp.concatenate([k_blk[qb + d][g // 2][variant] for d in range(2)], axis=0)
        vzt[qb, g, p] = jnp.concatenate([vt_blk[qb + d][g // 2][variant] for d in range(2)], axis=1)
        sink[qb, g, p] = jnp.concatenate([jnp.full((1, w), sinks_ref[SWA_GROUP * g + 2 * r + p] * log2e, F32)
                                          for r in range(pairs)], axis=1)
    st = {b: lax.dot_general(kz[b], qm[b], (((1,), (1,)), ((), ())), preferred_element_type=F32) for b in items}
    prev = {b: jnp.where(n > 0, st[b][:w], -jnp.inf) if b[0] == 0 else st[b][:w] for b in items}
    u = {b: jnp.where(from_prev, prev[b], st[b][w:]) for b in items}
    m = {b: jnp.maximum(jnp.max(u[b], axis=0, keepdims=True), sink[b]) for b in items}
    eu = {b: jnp.exp2(u[b] - m[b]) for b in items}
    den = {b: jnp.sum(eu[b], axis=0, keepdims=True) + jnp.exp2(sink[b] - m[b]) for b in items}
    et = {b: jnp.concatenate([jnp.where(from_prev, eu[b], 0.0), jnp.where(from_prev, 0.0, eu[b])],
                             axis=0).astype(BF16) for b in items}
    ot = {b: jnp.dot(vzt[b], et[b], preferred_element_type=F32) / den[b] for b in items}
    for qb in range(nq):
        for g in range(SWA_KV_HEADS):
            for r in range(pairs):
                cols = slice(r * w, (r + 1) * w)
                tile_t = jnp.where(lo_row, ot[qb, g, 0][:, cols], ot[qb, g, 1][:, cols])
                o_ref[qb * w:(qb + 1) * w, (2 * g + r) * LANES:(2 * g + r + 1) * LANES] = (
                    tile_t.T.astype(o_ref.dtype))


def _swa_prompt(sq, skv, sinks, nq):
    b_, l_, _ = sq.shape
    rows = nq * WINDOW
    return pl.pallas_call(
        functools.partial(_swa_prompt_body, nq),
        grid=(b_, l_ // rows),
        in_specs=[pl.BlockSpec(memory_space=pltpu.SMEM),
                  pl.BlockSpec((None, rows, SWA_Q), lambda b, n: (b, n, 0)),
                  pl.BlockSpec((None, WINDOW, 2 * SWA_KV), lambda b, n: (b, jnp.maximum(n * nq - 1, 0), 0)),
                  pl.BlockSpec((None, rows, 2 * SWA_KV), lambda b, n: (b, n, 0))],
        out_specs=pl.BlockSpec((None, rows, SWA_Q), lambda b, n: (b, n, 0)),
        out_shape=jax.ShapeDtypeStruct((b_, l_, SWA_Q), BF16),
        compiler_params=_params(("arbitrary", "arbitrary")),
        name="swa_prompt",
    )(sinks, sq, skv, skv)


def _swa_step_body(bb, q_ref, kvn_ref, ck_ref, cv_ref, sink_ref, o_ref, nk_ref, nv_ref):
    w = WINDOW
    first = pl.program_id(0) * bb
    row = lax.broadcasted_iota(jnp.int32, (SWA_Q_HEADS, SWA_KV), 0)
    lane = lax.broadcasted_iota(jnp.int32, (SWA_Q_HEADS, SWA_KV), 1)
    own = (lane // SWA_HD) == (row // SWA_GROUP)
    newest = lax.broadcasted_iota(jnp.int32, (SWA_KV, w), 1) == w - 1
    sink = sink_ref[...]
    scale = SWA_HD ** -0.5
    seqs = range(bb)
    kn_all = kvn_ref[0:SWA_KV, :]
    vn_all = kvn_ref[SWA_KV:2 * SWA_KV, :]
    nk = [jnp.where(newest, pltpu.roll(kn_all, w - 1 - (first + i), axis=1), pltpu.roll(ck_ref[i], w - 1, axis=1))
          for i in seqs]
    nv = [jnp.where(newest, pltpu.roll(vn_all, w - 1 - (first + i), axis=1), pltpu.roll(cv_ref[i], w - 1, axis=1))
          for i in seqs]
    q_bd = [jnp.where(own, jnp.concatenate([q_ref[i]] * SWA_KV_HEADS, axis=1), 0.0) for i in seqs]
    s = [_mm(q_bd[i], nk[i]) * scale for i in seqs]
    m = [jnp.maximum(jnp.max(s[i], axis=1, keepdims=True), sink) for i in seqs]
    e = [jnp.exp(s[i] - m[i]) for i in seqs]
    den = [jnp.sum(e[i], axis=1, keepdims=True) + jnp.exp(sink - m[i]) for i in seqs]
    pv = [jnp.where(own, _mm_nt(e[i] / den[i], nv[i]), 0.0) for i in seqs]
    for i in seqs:
        o = pv[i][:, 0:SWA_HD]
        for g in range(1, SWA_KV_HEADS):
            o = o + pv[i][:, g * SWA_HD:(g + 1) * SWA_HD]
        o_ref[i] = o
        nk_ref[i] = nk[i]
        nv_ref[i] = nv[i]


def _swa_step(q3, kvn_t, ck_t, cv_t, sink_col, bb):
    n_ = q3.shape[0]
    assert n_ <= WINDOW
    cache = pl.BlockSpec((bb, SWA_KV, WINDOW), lambda i: (i, 0, 0))
    return pl.pallas_call(
        functools.partial(_swa_step_body, bb),
        grid=(n_ // bb,),
        in_specs=[pl.BlockSpec((bb, SWA_Q_HEADS, SWA_HD), lambda i: (i, 0, 0)),
                  _const_spec(kvn_t.shape),
                  cache, cache,
                  _const_spec(sink_col.shape)],
        out_specs=(pl.BlockSpec((bb, SWA_Q_HEADS, SWA_HD), lambda i: (i, 0, 0)), cache, cache),
        out_shape=(jax.ShapeDtypeStruct(q3.shape, F32),
                   jax.ShapeDtypeStruct(ck_t.shape, F32),
                   jax.ShapeDtypeStruct(cv_t.shape, F32)),
        compiler_params=_params(("arbitrary",)),
        name="swa_step",
    )(q3, kvn_t, ck_t, cv_t, sink_col)


def _dense_body(stateful, tm, og_ref, ob_ref, gab_ref, x_ref, gt1_ref, sh2_ref, sc2_ref, gt2_ref,
                n2w_ref, fnw_ref, wa_ref, wb_ref, wo_ref, wg_ref, wu_ref, cw_ref, cb_ref, wd_ref, *rest):
    if stateful:
        st_ref, y_ref, gout_ref, act_ref = rest
    else:
        y_ref, gout_ref, act_ref, gbuf_ref, carry_ref = rest

        @pl.when(pl.program_id(1) == 0)
        def _():
            carry_ref[...] = jnp.zeros_like(carry_ref)

    y_a = jnp.dot(og_ref[...], wa_ref[...], preferred_element_type=F32)
    y_b = jnp.dot(ob_ref[...], wb_ref[...], preferred_element_type=F32)
    merged = (jax.nn.sigmoid(gab_ref[:, 0:D_MODEL]) * y_a
              + jax.nn.sigmoid(gab_ref[:, D_MODEL:2 * D_MODEL]) * y_b)
    x1 = x_ref[...] + gt1_ref[...] * _mm(merged, wo_ref[...])
    h2 = (_rms(x1, n2w_ref[...]) * (1.0 + sc2_ref[...]) + sh2_ref[...]).astype(BF16)

    for c in range(D_FF // FFN_COLS):
        cols = slice(c * FFN_COLS, (c + 1) * FFN_COLS)
        gate = jnp.dot(h2, wg_ref[:, cols], preferred_element_type=F32)
        up = jnp.dot(h2, wu_ref[:, cols], preferred_element_type=F32)
        if stateful:
            g2 = st_ref[0, :, cols]
            g1 = st_ref[1, :, cols]
            gout_ref[:, cols] = gate
        else:
            gbuf_ref[0:SUBLANES, :] = carry_ref[:, cols]
            gbuf_ref[SUBLANES:SUBLANES + tm, :] = gate
            g2 = gbuf_ref[SUBLANES - 2:SUBLANES - 2 + tm, :]
            g1 = gbuf_ref[SUBLANES - 1:SUBLANES - 1 + tm, :]
            carry_ref[:, cols] = gbuf_ref[tm:tm + SUBLANES, :]
        gc = (cw_ref[0:1, cols] * g2 + cw_ref[1:2, cols] * g1 + cw_ref[2:3, cols] * gate) + cb_ref[:, cols]
        act_ref[:, cols] = (_silu(gc) * up).astype(BF16)
    if not stateful:
        gout_ref[...] = carry_ref[...]

    x2 = x1 + gt2_ref[...] * jnp.dot(act_ref[...], wd_ref[...], preferred_element_type=F32)
    y_ref[...] = _rms(x2, fnw_ref[...])


def _dense(og, ob, gab, x, mods, vecs, ws, st, tm):
    b_, l_, _ = x.shape
    r_ = mods[0].shape[1]
    rt = 1 if r_ == 1 else tm
    mod_map = (lambda b, t: (b, 0, 0)) if r_ == 1 else (lambda b, t: (b, t, 0))
    row_map = lambda b, t: (b, t, 0)
    stateful = st is not None
    in_specs = ([pl.BlockSpec((None, tm, D_MODEL), row_map),
                 pl.BlockSpec((None, tm, D_MODEL), row_map),
                 pl.BlockSpec((None, tm, 2 * D_MODEL), row_map),
                 pl.BlockSpec((None, tm, D_MODEL), row_map)]
                + [pl.BlockSpec((None, rt, D_MODEL), mod_map)] * 4
                + [_const_spec(a.shape) for a in vecs[:2]]
                + [_const_spec(ws[0].shape), _const_spec(ws[1].shape), _const_spec(ws[2].shape),
                   _const_spec(ws[3].shape), _const_spec(ws[4].shape),
                   _const_spec(vecs[2].shape), _const_spec(vecs[3].shape), _const_spec(ws[5].shape)])
    args = [og, ob, gab, x, *mods, vecs[0], vecs[1], ws[0], ws[1], ws[2], ws[3], ws[4], vecs[2], vecs[3], ws[5]]
    scratch = [pltpu.VMEM((tm, D_FF), BF16)]
    if stateful:
        in_specs.append(pl.BlockSpec((FFN_CONV - 1, None, tm, D_FF), lambda b, t: (0, b, t, 0)))
        args.append(st)
        gout_shape = jax.ShapeDtypeStruct((b_, l_, D_FF), F32)
        gout_spec = pl.BlockSpec((None, tm, D_FF), row_map)
    else:
        scratch += [pltpu.VMEM((tm + SUBLANES, FFN_COLS), F32), pltpu.VMEM((SUBLANES, D_FF), F32)]
        gout_shape = jax.ShapeDtypeStruct((b_, SUBLANES, D_FF), F32)
        gout_spec = pl.BlockSpec((None, SUBLANES, D_FF), lambda b, t: (b, 0, 0))
    return pl.pallas_call(
        functools.partial(_dense_body, stateful, tm),
        grid=(b_, l_ // tm),
        in_specs=in_specs,
        out_specs=(pl.BlockSpec((None, tm, D_MODEL), row_map), gout_spec),
        out_shape=(jax.ShapeDtypeStruct((b_, l_, D_MODEL), F32), gout_shape),
        scratch_shapes=scratch,
        compiler_params=_params(("arbitrary", "arbitrary")),
        name="dense_step" if stateful else "dense_prompt",
    )(*args)


def _lane_row(values, offset):
    return jnp.zeros((1, LANES), F32).at[0, offset:offset + values.shape[0]].set(values)


def kernel(x_prompt, x_sample, c_prompt, c_sample, state_gdn_S, state_gdn_conv, cache_swa_k, cache_swa_v,
           state_ffn_conv, w_mod, b_mod, norm1_w, norm2_w, w_in, gdn_conv_w, gdn_a_log, gdn_dt_bias,
           gdn_onorm_w, w_gdn_out, swa_sinks, w_swa_out, w_o, w_ffn_gate, w_ffn_up, ffn_conv_w, ffn_conv_b,
           w_ffn_down, final_norm_w):
    assert w_mod.shape[0] == 1, "single-layer trunk"
    nb, seq, _ = x_prompt.shape
    ns = x_sample.shape[0]
    assert x_sample.shape[1] == 1

    in_ws = _in_weight(jnp.transpose(w_in[0]))
    dense_ws = (w_gdn_out[0].astype(BF16), w_swa_out[0].astype(BF16), w_o[0].astype(BF16),
                w_ffn_gate[0].astype(BF16), w_ffn_up[0].astype(BF16), w_ffn_down[0].astype(BF16))
    dense_vecs = (norm2_w, final_norm_w[None, :], ffn_conv_w[0], ffn_conv_b)
    cw = jnp.transpose(gdn_conv_w[0].reshape(GDN_CONV, GDN_SECTIONS, LANES), (1, 0, 2))
    alog_row = _lane_row(gdn_a_log[0], GDN_HEADS)
    dtb_row = _lane_row(gdn_dt_bias[0], GDN_HEADS)

    mod = _modulation(jnp.concatenate([c_prompt, c_sample], axis=0), w_mod[0], b_mod)
    mod_p = [mod[:nb, i * D_MODEL:(i + 1) * D_MODEL][:, None, :] for i in range(6)]
    mod_s = [mod[nb:, i * D_MODEL:(i + 1) * D_MODEL][None, :, :] for i in range(6)]

    qkvf, gact, ba, sq, skv, gab, qkv_tail = _inproj(x_prompt, mod_p[0], mod_p[1], norm1_w, in_ws, cw, tm=256)
    og, gdn_s_p = _gdn_prompt(qkvf, gact, ba, alog_row, dtb_row, gdn_onorm_w, _level_masks(), lt=4 * CHUNK)
    ob = _swa_prompt(sq, skv, swa_sinks[0], nq=4)
    y_p, gate_tail = _dense(og, ob, gab, x_prompt, (mod_p[2], mod_p[3], mod_p[4], mod_p[5]),
                            dense_vecs, dense_ws, None, tm=512)
    gdn_conv_p = jnp.transpose(qkv_tail[:, :, SUBLANES - (GDN_CONV - 1):, :], (0, 2, 1, 3)).reshape(
        nb, GDN_CONV - 1, GDN_CONV_CH)
    k_p = skv[:, seq - WINDOW:, :SWA_KV].reshape(nb, WINDOW, SWA_KV_HEADS, SWA_HD)
    v_p = skv[:, seq - WINDOW:, SWA_KV:].reshape(nb, WINDOW, SWA_KV_HEADS, SWA_HD)
    ffn_conv_p = gate_tail[:, SUBLANES - (FFN_CONV - 1):, :]

    xs = x_sample.reshape(1, ns, D_MODEL)
    qkvs, gates, bas, sqs, skvs, gabs = _inproj(xs, mod_s[0], mod_s[1], norm1_w, in_ws, None, tm=ns)
    st_gdn = jnp.transpose(state_gdn_conv[0], (1, 0, 2))
    og_s, gdn_s_s = _gdn_step(qkvs[0], st_gdn, gates[0], bas[0], cw, alog_row, dtb_row, gdn_onorm_w,
                              state_gdn_S[0], bb=8)
    to_channel_major = lambda c: jnp.transpose(c, (0, 2, 3, 1)).reshape(ns, SWA_KV, WINDOW)
    from_channel_major = lambda c: jnp.transpose(c.reshape(ns, SWA_KV_HEADS, SWA_HD, WINDOW), (0, 3, 1, 2))
    kvn_t = jnp.pad(jnp.transpose(skvs[0]), ((0, 0), (0, WINDOW - ns)))
    o3, k_s, v_s = _swa_step(sqs[0].reshape(ns, SWA_Q_HEADS, SWA_HD), kvn_t,
                             to_channel_major(cache_swa_k[0]), to_channel_major(cache_swa_v[0]),
                             swa_sinks[0][:, None], bb=8)
    ob_s = o3.reshape(1, ns, SWA_Q).astype(BF16)
    st_ffn = jnp.transpose(state_ffn_conv[0], (1, 0, 2))[:, None]
    y_s, gate_new = _dense(og_s[None], ob_s, gabs, xs, (mod_s[2], mod_s[3], mod_s[4], mod_s[5]),
                           dense_vecs, dense_ws, st_ffn, tm=ns)
    gdn_conv_s = jnp.concatenate([state_gdn_conv[0][:, 1:], qkvs[0][:, None, :]], axis=1)
    ffn_conv_s = jnp.concatenate([state_ffn_conv[0][:, 1:], gate_new[0][:, None, :]], axis=1)

    return (y_p, y_s.reshape(ns, 1, D_MODEL),
            gdn_s_p[None], gdn_s_s[None],
            gdn_conv_p[None], gdn_conv_s[None],
            k_p[None], from_channel_major(k_s)[None],
            v_p[None], from_channel_major(v_s)[None],
            ffn_conv_p[None], ffn_conv_s[None])
```

```python
import functools
import math

import numpy as np
import jax
import jax.numpy as jnp
from jax import lax
from jax.experimental import pallas as pl
from jax.experimental.pallas import tpu as pltpu

F32 = jnp.float32
BF16 = jnp.bfloat16

D_MODEL = 1024
GDN_HEADS = 8
GDN_DK = 128
GDN_DV = 128
GDN_QK = GDN_HEADS * GDN_DK
GDN_V = GDN_HEADS * GDN_DV
GDN_CONV = 4
GDN_CONV_CH = 2 * GDN_QK + GDN_V
GDN_SECTIONS = GDN_CONV_CH // 128
SWA_Q_HEADS = 16
SWA_KV_HEADS = 4
SWA_GROUP = SWA_Q_HEADS // SWA_KV_HEADS
SWA_HD = 64
SWA_Q = SWA_Q_HEADS * SWA_HD
SWA_KV = SWA_KV_HEADS * SWA_HD
WINDOW = 128
D_FF = 2816
FFN_CONV = 3
EPS = 1e-6

LANES = 128
SUBLANES = 8
VMEM_LIMIT = 56 * 1024 * 1024

COL_QKV = 0
COL_GATE = COL_QKV + GDN_CONV_CH
COL_SQ = COL_GATE + GDN_V
COL_SKV = COL_SQ + SWA_Q
COL_GAB = COL_SKV + 2 * SWA_KV
COL_BA = COL_GAB + 2 * D_MODEL
IN_COLS = COL_BA + LANES

SWA_Q_SCALE = SWA_HD ** -0.5 * math.log2(math.e)

FEATURE_DTYPE = BF16
CONV_ROWS = 64
CHUNK = 128
GDN_GROUP = 16
FFN_COLS = 256


def _mm(a, b):
    return jnp.dot(a.astype(BF16), b.astype(BF16), preferred_element_type=F32)


def _mm_nt(a, b):
    return lax.dot_general(a.astype(BF16), b.astype(BF16), (((1,), (1,)), ((), ())),
                           preferred_element_type=F32)


def _silu(x):
    return x * jax.nn.sigmoid(x)


def _softplus(x):
    return jnp.maximum(x, 0.0) + jnp.log1p(jnp.exp(-jnp.abs(x)))


def _rms(x, w):
    return x * lax.rsqrt(jnp.mean(x * x, axis=-1, keepdims=True) + EPS) * w


def _const_spec(shape):
    n = len(shape)
    return pl.BlockSpec(shape, lambda *_: (0,) * n, pipeline_mode=pl.Buffered(1))


def _params(sem):
    return pltpu.CompilerParams(dimension_semantics=sem, vmem_limit_bytes=VMEM_LIMIT)


def _mod_body(c_ref, w_ref, b_ref, o_ref):
    o_ref[...] = _mm(_silu(c_ref[...]), w_ref[...]) + b_ref[...]


def _modulation(c_all, w_mod, b_mod):
    rows = c_all.shape[0]
    n_out = w_mod.shape[1]
    tn = D_MODEL
    return pl.pallas_call(
        _mod_body,
        grid=(n_out // tn,),
        in_specs=[pl.BlockSpec((rows, D_MODEL), lambda j: (0, 0)),
                  pl.BlockSpec((D_MODEL, tn), lambda j: (0, j)),
                  pl.BlockSpec((1, tn), lambda j: (0, j))],
        out_specs=pl.BlockSpec((rows, tn), lambda j: (0, j)),
        out_shape=jax.ShapeDtypeStruct((rows, n_out), F32),
        compiler_params=_params(("arbitrary",)),
        name="modulation",
    )(c_all, w_mod, b_mod)


IN_WEIGHT_COLS = 512
IN_WEIGHT_PAD = 256


def _in_weight_body(n_main, wt_ref, ba_ref, o_ref):
    j = pl.program_id(0)

    @pl.when(j < n_main)
    def _():
        o_ref[...] = wt_ref[...].T.astype(BF16)

    @pl.when(j == n_main)
    def _():
        n_ba = ba_ref.shape[0]
        ba = jnp.concatenate([ba_ref[...].T, jnp.zeros((D_MODEL, IN_WEIGHT_COLS - n_ba), F32)], axis=1)
        o_ref[...] = ba.astype(BF16)


def _in_weight(w_t):
    n_ba = 2 * GDN_HEADS
    split = GDN_CONV_CH + GDN_V
    tc = IN_WEIGHT_COLS
    n_main = COL_BA // tc

    def src_row(j):
        jj = jnp.minimum(j, n_main - 1)
        return pl.multiple_of(jnp.where(jj * tc < split, jj * tc, jj * tc + n_ba), n_ba)

    return pl.pallas_call(
        functools.partial(_in_weight_body, n_main),
        grid=(n_main + 1,),
        in_specs=[pl.BlockSpec((pl.Element(tc), pl.Element(D_MODEL)), lambda j: (src_row(j), 0)),
                  pl.BlockSpec((pl.Element(n_ba), pl.Element(D_MODEL)), lambda j: (split, 0))],
        out_specs=pl.BlockSpec((D_MODEL, tc), lambda j: (0, j)),
        out_shape=jax.ShapeDtypeStruct((D_MODEL, COL_BA + IN_WEIGHT_PAD), BF16),
        compiler_params=_params(("arbitrary",)),
        name="in_weight",
    )(w_t, w_t)


def _l2norm(x):
    return x * lax.rsqrt(jnp.sum(x * x, axis=-1, keepdims=True) + EPS)


def _inproj_body(seq_rows, tm, x_ref, sh_ref, sc_ref, nw_ref, w_ref, *rest):
    if seq_rows:
        cw_ref, qkv_ref, gg_ref, ba_ref, sq_ref, skv_ref, gab_ref, tail_ref, xe_ref = rest

        @pl.when(pl.program_id(1) == 0)
        def _():
            xe_ref[:, 0:SUBLANES, :] = jnp.zeros((GDN_SECTIONS, SUBLANES, LANES), F32)
    else:
        qkv_ref, gg_ref, ba_ref, sq_ref, skv_ref, gab_ref = rest

    h = _rms(x_ref[...], nw_ref[...]) * (1.0 + sc_ref[...]) + sh_ref[...]
    hb = h.astype(BF16)

    def proj(lo, width):
        return jnp.dot(hb, w_ref[:, lo:lo + width], preferred_element_type=F32)

    step = 512
    per = step // LANES
    for c in range(GDN_CONV_CH // step):
        z = proj(COL_QKV + c * step, step)
        for k in range(per):
            s = c * per + k
            zs = z[:, k * LANES:(k + 1) * LANES]
            if not seq_rows:
                qkv_ref[:, s * LANES:(s + 1) * LANES] = zs
                continue
            xe_ref[s, SUBLANES:SUBLANES + tm, :] = zs
            w = cw_ref[s]
            for r0 in range(0, tm, CONV_ROWS):
                y = w[0:1] * xe_ref[s, r0 + SUBLANES - 3:r0 + SUBLANES - 3 + CONV_ROWS, :]
                for tap in range(1, GDN_CONV):
                    lo = r0 + SUBLANES - 3 + tap
                    y = y + w[tap:tap + 1] * xe_ref[s, lo:lo + CONV_ROWS, :]
                f = _silu(y)
                if s < GDN_HEADS:
                    f = _l2norm(f) * (GDN_DK ** -0.5)
                elif s < 2 * GDN_HEADS:
                    f = _l2norm(f)
                qkv_ref[s, r0:r0 + CONV_ROWS, :] = f.astype(qkv_ref.dtype)
            xe_ref[s, 0:SUBLANES, :] = xe_ref[s, tm:tm + SUBLANES, :]
    if seq_rows:
        tail_ref[...] = xe_ref[:, 0:SUBLANES, :]
    for c in range(GDN_V // step):
        z = proj(COL_GATE + c * step, step)
        for k in range(per):
            zs = z[:, k * LANES:(k + 1) * LANES]
            if seq_rows:
                gg_ref[c * per + k] = _silu(zs).astype(gg_ref.dtype)
            else:
                gg_ref[:, (c * per + k) * LANES:(c * per + k + 1) * LANES] = zs
    ba_ref[...] = proj(COL_BA, LANES)
    for c in range(SWA_Q // step):
        sq_ref[:, c * step:(c + 1) * step] = proj(COL_SQ + c * step, step)
    skv_ref[...] = proj(COL_SKV, 2 * SWA_KV)
    for c in range(2 * D_MODEL // step):
        gab_ref[:, c * step:(c + 1) * step] = proj(COL_GAB + c * step, step)


def _inproj(x, sh, sc, nw, w_all, cw, tm):
    b_, l_, _ = x.shape
    r_ = sh.shape[1]
    rt = 1 if r_ == 1 else tm
    mod_map = (lambda b, t: (b, 0, 0)) if r_ == 1 else (lambda b, t: (b, t, 0))
    row_map = lambda b, t: (b, t, 0)
    head_map = lambda b, t: (b, 0, t, 0)
    seq_rows = cw is not None
    if seq_rows:
        gdn_shapes = (jax.ShapeDtypeStruct((b_, GDN_SECTIONS, l_, LANES), FEATURE_DTYPE),
                      jax.ShapeDtypeStruct((b_, GDN_HEADS, l_, LANES), FEATURE_DTYPE))
        gdn_specs = (pl.BlockSpec((None, GDN_SECTIONS, tm, LANES), head_map),
                     pl.BlockSpec((None, GDN_HEADS, tm, LANES), head_map))
    else:
        gdn_shapes = (jax.ShapeDtypeStruct((b_, l_, GDN_CONV_CH), F32),
                      jax.ShapeDtypeStruct((b_, l_, GDN_V), F32))
        gdn_specs = (pl.BlockSpec((None, tm, GDN_CONV_CH), row_map),
                     pl.BlockSpec((None, tm, GDN_V), row_map))
    out_shape = gdn_shapes + (
        jax.ShapeDtypeStruct((b_, l_, LANES), F32),
        jax.ShapeDtypeStruct((b_, l_, SWA_Q), F32),
        jax.ShapeDtypeStruct((b_, l_, 2 * SWA_KV), F32),
        jax.ShapeDtypeStruct((b_, l_, 2 * D_MODEL), F32),
    )
    out_specs = gdn_specs + (
        pl.BlockSpec((None, tm, LANES), row_map),
        pl.BlockSpec((None, tm, SWA_Q), row_map),
        pl.BlockSpec((None, tm, 2 * SWA_KV), row_map),
        pl.BlockSpec((None, tm, 2 * D_MODEL), row_map),
    )
    in_specs = [
        pl.BlockSpec((None, tm, D_MODEL), row_map),
        pl.BlockSpec((None, rt, D_MODEL), mod_map),
        pl.BlockSpec((None, rt, D_MODEL), mod_map),
        _const_spec(nw.shape),
        _const_spec(w_all.shape),
    ]
    args = [x, sh, sc, nw, w_all]
    scratch = []
    if seq_rows:
        in_specs.append(_const_spec(cw.shape))
        args.append(cw)
        out_shape += (jax.ShapeDtypeStruct((b_, GDN_SECTIONS, SUBLANES, LANES), F32),)
        out_specs += (pl.BlockSpec((None, GDN_SECTIONS, SUBLANES, LANES), lambda b, t: (b, 0, 0, 0)),)
        scratch.append(pltpu.VMEM((GDN_SECTIONS, tm + SUBLANES, LANES), F32))
    return pl.pallas_call(
        functools.partial(_inproj_body, seq_rows, tm),
        grid=(b_, l_ // tm),
        in_specs=in_specs,
        out_specs=out_specs,
        out_shape=out_shape,
        scratch_shapes=scratch,
        compiler_params=_params(("arbitrary", "arbitrary")),
        name="inproj_seq" if seq_rows else "inproj_rows",
    )(*args)


def _delta_gates(ba, alog_row, dtb_row):
    beta_all = jax.nn.sigmoid(ba)
    g_all = -jnp.exp(alog_row) * _softplus(ba + dtb_row)
    return beta_all, g_all


def _lane_column(x, lane_idx, lane):
    return jnp.sum(jnp.where(lane_idx == lane, x, 0.0), axis=1, keepdims=True)


def _level_masks():
    r = np.arange(CHUNK)[:, None]
    c = np.arange(CHUNK)[None, :]
    masks = [(r == c + 1) & (r % 2 == 1)]
    half = 2
    while half < CHUNK:
        full = 2 * half
        masks.append((r // full == c // full) & (r % full >= half) & (c % full < half))
        half = full
    return jnp.asarray(np.stack(masks), dtype=BF16)


def _unit_lower_inverses(ms, masks_ref, eye, between_levels=()):
    ts = [eye - m * masks_ref[0] for m in ms]
    pending = list(between_levels)
    for lvl in range(1, masks_ref.shape[0]):
        off = masks_ref[lvl]
        xs = [jnp.dot(m * off, t, preferred_element_type=F32).astype(BF16) for m, t in zip(ms, ts)]
        ys = [jnp.dot(t, x, preferred_element_type=F32).astype(BF16) for t, x in zip(ts, xs)]
        ts = [t - y for t, y in zip(ts, ys)]
        if pending:
            pending.pop(0)()
    for piece in pending:
        piece()
    return ts


def _cumsum_rows(g, ltri):
    hi = g.astype(BF16)
    r1 = g - hi.astype(F32)
    mid = r1.astype(BF16)
    lo = (r1 - mid.astype(F32)).astype(BF16)
    return (jnp.dot(ltri, hi, preferred_element_type=F32) + jnp.dot(ltri, mid, preferred_element_type=F32)
            + jnp.dot(ltri, lo, preferred_element_type=F32))


def _gated_out_norm(o, gate_act, onw):
    on = o * lax.rsqrt(jnp.mean(o * o, axis=-1, keepdims=True) + EPS) * onw
    return on * gate_act


def _gdn_prompt_body(lt, q_ref, k_ref, v_ref, ba_ref, alog_ref, dtb_ref, gate_ref, onw_ref, masks_ref,
                     og_ref, s_ref):
    @pl.when(pl.program_id(1) == 0)
    def _():
        s_ref[...] = jnp.zeros_like(s_ref)

    beta_all, g_all = _delta_gates(ba_ref[...], alog_ref[...], dtb_ref[...])
    lane_idx = lax.broadcasted_iota(jnp.int32, (CHUNK, LANES), 1)
    row = lax.broadcasted_iota(jnp.int32, (CHUNK, CHUNK), 0)
    col = lax.broadcasted_iota(jnp.int32, (CHUNK, CHUNK), 1)
    tril = row >= col
    strict = row > col
    ltri = jnp.where(tril, 1.0, 0.0).astype(BF16)
    eye = jnp.where(row == col, 1.0, 0.0).astype(BF16)
    onw = onw_ref[...]
    heads = range(GDN_HEADS)
    chunks = range(lt // CHUNK)

    blocks = [(c, j) for c in chunks for j in heads]
    pre = {}
    for c in chunks:
        rows = slice(c * CHUNK, (c + 1) * CHUNK)
        dec = _cumsum_rows(g_all[rows], ltri)
        dec_t = dec.T
        for j in heads:
            q, k, v = (r[j, rows, :].astype(F32) for r in (q_ref, k_ref, v_ref))
            beta_col = _lane_column(beta_all[rows], lane_idx, j)
            dec_col = _lane_column(dec, lane_idx, GDN_HEADS + j)
            dec_row = dec_t[GDN_HEADS + j:GDN_HEADS + j + 1, :]
            dec_last = dec_row[:, CHUNK - 1:CHUNK]
            gam = jnp.exp(jnp.minimum(dec_col - dec_row, 0.0))
            e_col = jnp.exp(dec_col)
            kb = k * beta_col
            pre[c, j] = dict(q=q, k=k, gam=gam, kb=kb, qe=q * e_col, e_last=jnp.exp(dec_last),
                             kd=k * jnp.exp(dec_last - dec_col),
                             rhs=jnp.concatenate([v * beta_col, kb * e_col], axis=1).astype(BF16))
    a_intra, uw = {}, {}

    def recurrence(c):
        rows = slice(c * CHUNK, (c + 1) * CHUNK)
        mid = {}

        def read_out():
            mid["s"] = [s_ref[j] for j in heads]
            mid["ws_qs"] = [_mm(jnp.concatenate([uw[c, j][:, GDN_DV:], pre[c, j]["qe"]], axis=0), mid["s"][j])
                            for j in heads]

        def update():
            s_prev, ws_qs = mid["s"], mid["ws_qs"]
            v_new = [uw[c, j][:, :GDN_DV] - ws_qs[j][:CHUNK] for j in heads]
            o = [ws_qs[j][CHUNK:] + _mm(a_intra[c, j], v_new[j]) for j in heads]
            s_new = [s_prev[j] * pre[c, j]["e_last"] + _mm(pre[c, j]["kd"].T, v_new[j]) for j in heads]
            for j in heads:
                s_ref[j] = s_new[j]
                og = _gated_out_norm(o[j], gate_ref[j, rows, :].astype(F32), onw)
                og_ref[rows, j * GDN_DV:(j + 1) * GDN_DV] = og.astype(og_ref.dtype)

        return [read_out, update]

    carried = []
    for g0 in range(0, len(blocks), GDN_GROUP):
        grp = blocks[g0:g0 + GDN_GROUP]
        grams = [_mm_nt(jnp.concatenate([pre[b]["kb"], pre[b]["q"]], axis=0), pre[b]["k"]) for b in grp]
        ms = [jnp.where(strict, g[:CHUNK] * pre[b]["gam"], 0.0).astype(BF16) for g, b in zip(grams, grp)]
        a_intra.update({b: jnp.where(tril, g[CHUNK:] * pre[b]["gam"], 0.0) for g, b in zip(grams, grp)})
        t_inv = _unit_lower_inverses(ms, masks_ref, eye, carried)
        uw.update({b: jnp.dot(t, pre[b]["rhs"], preferred_element_type=F32) for t, b in zip(t_inv, grp)})
        carried = [piece for c in sorted({c for c, _ in grp}) for piece in recurrence(c)]
    for piece in carried:
        piece()


def _gdn_prompt(qkvf, gact, ba, alog_row, dtb_row, onw, masks, lt):
    b_, _, l_, _ = qkvf.shape
    sec = lambda s: pl.BlockSpec((None, GDN_HEADS, lt, LANES), lambda b, t, s=s: (b, s, t, 0))
    return pl.pallas_call(
        functools.partial(_gdn_prompt_body, lt),
        grid=(b_, l_ // lt),
        in_specs=[sec(0), sec(1), sec(2),
                  pl.BlockSpec((None, lt, LANES), lambda b, t: (b, t, 0)),
                  _const_spec(alog_row.shape), _const_spec(dtb_row.shape),
                  pl.BlockSpec((None, GDN_HEADS, lt, LANES), lambda b, t: (b, 0, t, 0)),
                  _const_spec(onw.shape), _const_spec(masks.shape)],
        out_specs=(pl.BlockSpec((None, lt, GDN_V), lambda b, t: (b, t, 0)),
                   pl.BlockSpec((None, GDN_HEADS, GDN_DK, GDN_DV), lambda b, t: (b, 0, 0, 0))),
        out_shape=(jax.ShapeDtypeStruct((b_, l_, GDN_V), BF16),
                   jax.ShapeDtypeStruct((b_, GDN_HEADS, GDN_DK, GDN_DV), F32)),
        compiler_params=_params(("arbitrary", "arbitrary")),
        name="gdn_prompt",
    )(qkvf, qkvf, qkvf, ba, alog_row, dtb_row, gact, onw, masks)


def _gdn_step_body(bb, x_ref, st_ref, cw_ref, ba_ref, alog_ref, dtb_ref, gate_ref, onw_ref, s0_ref,
                   og_ref, sn_ref, q_s, k_s, v_s, b_s, e_s, o_s):
    beta_all, g_all = _delta_gates(ba_ref[...], alog_ref[...], dtb_ref[...])
    lane_idx = lax.broadcasted_iota(jnp.int32, (bb, LANES), 1)
    for h in range(GDN_HEADS):
        feats = []
        for s in range(3):
            idx = s * GDN_HEADS + h
            cols = slice(idx * LANES, (idx + 1) * LANES)
            w = cw_ref[idx]
            y = w[0:1] * st_ref[0, :, cols]
            for tap in range(1, GDN_CONV - 1):
                y = y + w[tap:tap + 1] * st_ref[tap, :, cols]
            y = y + w[GDN_CONV - 1:GDN_CONV] * x_ref[:, cols]
            feats.append(_silu(y))
        q, k, v = feats
        q_s[h] = q * lax.rsqrt(jnp.sum(q * q, axis=-1, keepdims=True) + EPS) * (GDN_DK ** -0.5)
        k_s[h] = k * lax.rsqrt(jnp.sum(k * k, axis=-1, keepdims=True) + EPS)
        v_s[h] = v
        b_s[h] = jnp.broadcast_to(_lane_column(beta_all, lane_idx, h), (bb, LANES))
        e_s[h] = jnp.broadcast_to(jnp.exp(_lane_column(g_all, lane_idx, h + GDN_HEADS)), (bb, LANES))

    eye = (lax.broadcasted_iota(jnp.int32, (GDN_DK, GDN_DK), 0)
           == lax.broadcasted_iota(jnp.int32, (GDN_DK, GDN_DK), 1))

    def to_col(r):
        return jnp.sum(jnp.where(eye, jnp.broadcast_to(r, (GDN_DK, GDN_DK)), 0.0), axis=1, keepdims=True)

    sub = lax.broadcasted_iota(jnp.int32, (SUBLANES, GDN_DK), 0)

    def seq_body(i, carry):
        for h in range(GDN_HEADS):
            one = pl.ds(i, 1)
            k_row = k_s[h, one, :]
            q_row = q_s[h, one, :]
            s1 = s0_ref[i, h] * e_s[h, one, :]
            kq = jnp.where(sub == 0, k_row, jnp.where(sub == 1, q_row, 0.0))
            kq_s1 = _mm(kq, s1)
            delta = (v_s[h, one, :] - kq_s1[0:1, :]) * b_s[h, one, :]
            sn_ref[i, h] = s1 + to_col(k_row) * delta
            qk = jnp.sum(q_row * k_row, axis=1, keepdims=True)
            o_s[h, one, :] = kq_s1[1:2, :] + qk * delta
        return carry

    lax.fori_loop(0, bb, seq_body, 0)
    onw = onw_ref[...]
    for h in range(GDN_HEADS):
        cols = slice(h * GDN_DV, (h + 1) * GDN_DV)
        og = _gated_out_norm(o_s[h], _silu(gate_ref[:, cols]), onw)
        og_ref[:, cols] = og.astype(og_ref.dtype)


def _gdn_step(qkv, st, gate, ba, cw, alog_row, dtb_row, onw, s0, bb):
    n_ = ba.shape[0]
    vec = pltpu.VMEM((GDN_HEADS, bb, LANES), F32)
    return pl.pallas_call(
        functools.partial(_gdn_step_body, bb),
        grid=(n_ // bb,),
        in_specs=[pl.BlockSpec((bb, GDN_CONV_CH), lambda i: (i, 0)),
                  pl.BlockSpec((GDN_CONV - 1, bb, GDN_CONV_CH), lambda i: (0, i, 0)),
                  _const_spec(cw.shape),
                  pl.BlockSpec((bb, LANES), lambda i: (i, 0)),
                  _const_spec(alog_row.shape), _const_spec(dtb_row.shape),
                  pl.BlockSpec((bb, GDN_V), lambda i: (i, 0)),
                  _const_spec(onw.shape),
                  pl.BlockSpec((bb, GDN_HEADS, GDN_DK, GDN_DV), lambda i: (i, 0, 0, 0))],
        out_specs=(pl.BlockSpec((bb, GDN_V), lambda i: (i, 0)),
                   pl.BlockSpec((bb, GDN_HEADS, GDN_DK, GDN_DV), lambda i: (i, 0, 0, 0))),
        out_shape=(jax.ShapeDtypeStruct((n_, GDN_V), BF16),
                   jax.ShapeDtypeStruct(s0.shape, F32)),
        scratch_shapes=[vec, vec, vec, vec, vec, vec],
        compiler_params=_params(("arbitrary",)),
        name="gdn_step",
    )(qkv, st, cw, ba, alog_row, dtb_row, gate, onw, s0)


def _swa_prompt_body(nq, sinks_ref, q_ref, kvp_ref, kvc_ref, o_ref):
    n = pl.program_id(1)
    w = WINDOW
    tiles = SWA_KV // LANES
    pairs = 2
    lo_lane = lax.broadcasted_iota(jnp.int32, (w, LANES), 1) < SWA_HD
    lo_row = lax.broadcasted_iota(jnp.int32, (LANES, w), 0) < SWA_HD
    c = lax.broadcasted_iota(jnp.int32, (w, pairs * w), 0)
    i = lax.broadcasted_iota(jnp.int32, (w, pairs * w), 1) & (w - 1)
    from_prev = c > i
    k_blk, vt_blk = [], []
    for j in range(nq + 1):
        src, rows = (kvp_ref, slice(0, w)) if j == 0 else (kvc_ref, slice((j - 1) * w, j * w))
        k_tiles, vt_tiles = [], []
        for t in range(tiles):
            kx = src[rows, t * LANES:(t + 1) * LANES]
            vt = src[rows, SWA_KV + t * LANES:SWA_KV + (t + 1) * LANES].T
            k_tiles.append((kx.astype(BF16), pltpu.roll(kx, SWA_HD, axis=1).astype(BF16)))
            vt_tiles.append((vt.astype(BF16),
                             jnp.concatenate([vt[SWA_HD:], vt[:SWA_HD]], axis=0).astype(BF16)))
        k_blk.append(k_tiles)
        vt_blk.append(vt_tiles)
    items = [(qb, g, p) for qb in range(nq) for g in range(SWA_KV_HEADS) for p in range(2)]
    log2e = math.log2(math.e)
    qm, kz, vzt, sink = {}, {}, {}, {}
    for qb, g, p in items:
        keep = lo_lane if p == 0 else jnp.logical_not(lo_lane)
        q_tiles = [q_ref[qb * w:(qb + 1) * w, (2 * g + r) * LANES:(2 * g + r + 1) * LANES] for r in range(pairs)]
        qm[qb, g, p] = jnp.concatenate([jnp.where(keep, x * SWA_Q_SCALE, 0.0) for x in q_tiles],
                                       axis=0).astype(BF16)
        variant = 0 if p == g % 2 else 1
        kz[qb, g, p] = jnp.concatenate([k_blk[qb + d][g // 2][variant] for d in range(2)], axis=0)
        vzt[qb, g, p] = jnp.concatenate([vt_blk[qb + d][g // 2][variant] for d in range(2)], axis=1)
        sink[qb, g, p] = jnp.concatenate([jnp.full((1, w), sinks_ref[SWA_GROUP * g + 2 * r + p] * log2e, F32)
                                          for r in range(pairs)], axis=1)
    st = {b: lax.dot_general(kz[b], qm[b], (((1,), (1,)), ((), ())), preferred_element_type=F32) for b in items}
    prev = {b: jnp.where(n > 0, st[b][:w], -jnp.inf) if b[0] == 0 else st[b][:w] for b in items}
    u = {b: jnp.where(from_prev, prev[b], st[b][w:]) for b in items}
    m = {b: jnp.maximum(jnp.max(u[b], axis=0, keepdims=True), sink[b]) for b in items}
    eu = {b: jnp.exp2(u[b] - m[b]) for b in items}
    den = {b: jnp.sum(eu[b], axis=0, keepdims=True) + jnp.exp2(sink[b] - m[b]) for b in items}
    et = {b: jnp.concatenate([jnp.where(from_prev, eu[b], 0.0), jnp.where(from_prev, 0.0, eu[b])],
                             axis=0).astype(BF16) for b in items}
    ot = {b: jnp.dot(vzt[b], et[b], preferred_element_type=F32) / den[b] for b in items}
    for qb in range(nq):
        for g in range(SWA_KV_HEADS):
            for r in range(pairs):
                cols = slice(r * w, (r + 1) * w)
                tile_t = jnp.where(lo_row, ot[qb, g, 0][:, cols], ot[qb, g, 1][:, cols])
                o_ref[qb * w:(qb + 1) * w, (2 * g + r) * LANES:(2 * g + r + 1) * LANES] = (
                    tile_t.T.astype(o_ref.dtype))


def _swa_prompt(sq, skv, sinks, nq):
    b_, l_, _ = sq.shape
    rows = nq * WINDOW
    return pl.pallas_call(
        functools.partial(_swa_prompt_body, nq),
        grid=(b_, l_ // rows),
        in_specs=[pl.BlockSpec(memory_space=pltpu.SMEM),
                  pl.BlockSpec((None, rows, SWA_Q), lambda b, n: (b, n, 0)),
                  pl.BlockSpec((None, WINDOW, 2 * SWA_KV), lambda b, n: (b, jnp.maximum(n * nq - 1, 0), 0)),
                  pl.BlockSpec((None, rows, 2 * SWA_KV), lambda b, n: (b, n, 0))],
        out_specs=pl.BlockSpec((None, rows, SWA_Q), lambda b, n: (b, n, 0)),
        out_shape=jax.ShapeDtypeStruct((b_, l_, SWA_Q), BF16),
        compiler_params=_params(("arbitrary", "arbitrary")),
        name="swa_prompt",
    )(sinks, sq, skv, skv)


def _swa_step_body(bb, q_ref, kvn_ref, ck_ref, cv_ref, sink_ref, o_ref, nk_ref, nv_ref):
    w = WINDOW
    first = pl.program_id(0) * bb
    row = lax.broadcasted_iota(jnp.int32, (SWA_Q_HEADS, SWA_KV), 0)
    lane = lax.broadcasted_iota(jnp.int32, (SWA_Q_HEADS, SWA_KV), 1)
    own = (lane // SWA_HD) == (row // SWA_GROUP)
    newest = lax.broadcasted_iota(jnp.int32, (SWA_KV, w), 1) == w - 1
    sink = sink_ref[...]
    scale = SWA_HD ** -0.5
    seqs = range(bb)
    kn_all = kvn_ref[0:SWA_KV, :]
    vn_all = kvn_ref[SWA_KV:2 * SWA_KV, :]
    nk = [jnp.where(newest, pltpu.roll(kn_all, w - 1 - (first + i), axis=1), pltpu.roll(ck_ref[i], w - 1, axis=1))
          for i in seqs]
    nv = [jnp.where(newest, pltpu.roll(vn_all, w - 1 - (first + i), axis=1), pltpu.roll(cv_ref[i], w - 1, axis=1))
          for i in seqs]
    q_bd = [jnp.where(own, jnp.concatenate([q_ref[i]] * SWA_KV_HEADS, axis=1), 0.0) for i in seqs]
    s = [_mm(q_bd[i], nk[i]) * scale for i in seqs]
    m = [jnp.maximum(jnp.max(s[i], axis=1, keepdims=True), sink) for i in seqs]
    e = [jnp.exp(s[i] - m[i]) for i in seqs]
    den = [jnp.sum(e[i], axis=1, keepdims=True) + jnp.exp(sink - m[i]) for i in seqs]
    pv = [jnp.where(own, _mm_nt(e[i] / den[i], nv[i]), 0.0) for i in seqs]
    for i in seqs:
        o = pv[i][:, 0:SWA_HD]
        for g in range(1, SWA_KV_HEADS):
            o = o + pv[i][:, g * SWA_HD:(g + 1) * SWA_HD]
        o_ref[i] = o
        nk_ref[i] = nk[i]
        nv_ref[i] = nv[i]


def _swa_step(q3, kvn_t, ck_t, cv_t, sink_col, bb):
    n_ = q3.shape[0]
    assert n_ <= WINDOW
    cache = pl.BlockSpec((bb, SWA_KV, WINDOW), lambda i: (i, 0, 0))
    return pl.pallas_call(
        functools.partial(_swa_step_body, bb),
        grid=(n_ // bb,),
        in_specs=[pl.BlockSpec((bb, SWA_Q_HEADS, SWA_HD), lambda i: (i, 0, 0)),
                  _const_spec(kvn_t.shape),
                  cache, cache,
                  _const_spec(sink_col.shape)],
        out_specs=(pl.BlockSpec((bb, SWA_Q_HEADS, SWA_HD), lambda i: (i, 0, 0)), cache, cache),
        out_shape=(jax.ShapeDtypeStruct(q3.shape, F32),
                   jax.ShapeDtypeStruct(ck_t.shape, F32),
                   jax.ShapeDtypeStruct(cv_t.shape, F32)),
        compiler_params=_params(("arbitrary",)),
        name="swa_step",
    )(q3, kvn_t, ck_t, cv_t, sink_col)


def _dense_body(stateful, tm, og_ref, ob_ref, gab_ref, x_ref, gt1_ref, sh2_ref, sc2_ref, gt2_ref,
                n2w_ref, fnw_ref, wa_ref, wb_ref, wo_ref, wg_ref, wu_ref, cw_ref, cb_ref, wd_ref, *rest):
    if stateful:
        st_ref, y_ref, gout_ref, act_ref = rest
    else:
        y_ref, gout_ref, act_ref, gbuf_ref, carry_ref = rest

        @pl.when(pl.program_id(1) == 0)
        def _():
            carry_ref[...] = jnp.zeros_like(carry_ref)

    y_a = jnp.dot(og_ref[...], wa_ref[...], preferred_element_type=F32)
    y_b = jnp.dot(ob_ref[...], wb_ref[...], preferred_element_type=F32)
    merged = (jax.nn.sigmoid(gab_ref[:, 0:D_MODEL]) * y_a
              + jax.nn.sigmoid(gab_ref[:, D_MODEL:2 * D_MODEL]) * y_b)
    x1 = x_ref[...] + gt1_ref[...] * _mm(merged, wo_ref[...])
    h2 = (_rms(x1, n2w_ref[...]) * (1.0 + sc2_ref[...]) + sh2_ref[...]).astype(BF16)

    for c in range(D_FF // FFN_COLS):
        cols = slice(c * FFN_COLS, (c + 1) * FFN_COLS)
        gate = jnp.dot(h2, wg_ref[:, cols], preferred_element_type=F32)
        up = jnp.dot(h2, wu_ref[:, cols], preferred_element_type=F32)
        if stateful:
            g2 = st_ref[0, :, cols]
            g1 = st_ref[1, :, cols]
            gout_ref[:, cols] = gate
        else:
            gbuf_ref[0:SUBLANES, :] = carry_ref[:, cols]
            gbuf_ref[SUBLANES:SUBLANES + tm, :] = gate
            g2 = gbuf_ref[SUBLANES - 2:SUBLANES - 2 + tm, :]
            g1 = gbuf_ref[SUBLANES - 1:SUBLANES - 1 + tm, :]
            carry_ref[:, cols] = gbuf_ref[tm:tm + SUBLANES, :]
        gc = (cw_ref[0:1, cols] * g2 + cw_ref[1:2, cols] * g1 + cw_ref[2:3, cols] * gate) + cb_ref[:, cols]
        act_ref[:, cols] = (_silu(gc) * up).astype(BF16)
    if not stateful:
        gout_ref[...] = carry_ref[...]

    x2 = x1 + gt2_ref[...] * jnp.dot(act_ref[...], wd_ref[...], preferred_element_type=F32)
    y_ref[...] = _rms(x2, fnw_ref[...])


def _dense(og, ob, gab, x, mods, vecs, ws, st, tm):
    b_, l_, _ = x.shape
    r_ = mods[0].shape[1]
    rt = 1 if r_ == 1 else tm
    mod_map = (lambda b, t: (b, 0, 0)) if r_ == 1 else (lambda b, t: (b, t, 0))
    row_map = lambda b, t: (b, t, 0)
    stateful = st is not None
    in_specs = ([pl.BlockSpec((None, tm, D_MODEL), row_map),
                 pl.BlockSpec((None, tm, D_MODEL), row_map),
                 pl.BlockSpec((None, tm, 2 * D_MODEL), row_map),
                 pl.BlockSpec((None, tm, D_MODEL), row_map)]
                + [pl.BlockSpec((None, rt, D_MODEL), mod_map)] * 4
                + [_const_spec(a.shape) for a in vecs[:2]]
                + [_const_spec(ws[0].shape), _const_spec(ws[1].shape), _const_spec(ws[2].shape),
                   _const_spec(ws[3].shape), _const_spec(ws[4].shape),
                   _const_spec(vecs[2].shape), _const_spec(vecs[3].shape), _const_spec(ws[5].shape)])
    args = [og, ob, gab, x, *mods, vecs[0], vecs[1], ws[0], ws[1], ws[2], ws[3], ws[4], vecs[2], vecs[3], ws[5]]
    scratch = [pltpu.VMEM((tm, D_FF), BF16)]
    if stateful:
        in_specs.append(pl.BlockSpec((FFN_CONV - 1, None, tm, D_FF), lambda b, t: (0, b, t, 0)))
        args.append(st)
        gout_shape = jax.ShapeDtypeStruct((b_, l_, D_FF), F32)
        gout_spec = pl.BlockSpec((None, tm, D_FF), row_map)
    else:
        scratch += [pltpu.VMEM((tm + SUBLANES, FFN_COLS), F32), pltpu.VMEM((SUBLANES, D_FF), F32)]
        gout_shape = jax.ShapeDtypeStruct((b_, SUBLANES, D_FF), F32)
        gout_spec = pl.BlockSpec((None, SUBLANES, D_FF), lambda b, t: (b, 0, 0))
    return pl.pallas_call(
        functools.partial(_dense_body, stateful, tm),
        grid=(b_, l_ // tm),
        in_specs=in_specs,
        out_specs=(pl.BlockSpec((None, tm, D_MODEL), row_map), gout_spec),
        out_shape=(jax.ShapeDtypeStruct((b_, l_, D_MODEL), F32), gout_shape),
        scratch_shapes=scratch,
        compiler_params=_params(("arbitrary", "arbitrary")),
        name="dense_step" if stateful else "dense_prompt",
    )(*args)


def _lane_row(values, offset):
    return jnp.zeros((1, LANES), F32).at[0, offset:offset + values.shape[0]].set(values)


def kernel(x_prompt, x_sample, c_prompt, c_sample, state_gdn_S, state_gdn_conv, cache_swa_k, cache_swa_v,
           state_ffn_conv, w_mod, b_mod, norm1_w, norm2_w, w_in, gdn_conv_w, gdn_a_log, gdn_dt_bias,
           gdn_onorm_w, w_gdn_out, swa_sinks, w_swa_out, w_o, w_ffn_gate, w_ffn_up, ffn_conv_w, ffn_conv_b,
           w_ffn_down, final_norm_w):
    assert w_mod.shape[0] == 1, "single-layer trunk"
    nb, seq, _ = x_prompt.shape
    ns = x_sample.shape[0]
    assert x_sample.shape[1] == 1

    in_ws = _in_weight(jnp.transpose(w_in[0]))
    dense_ws = (w_gdn_out[0].astype(BF16), w_swa_out[0].astype(BF16), w_o[0].astype(BF16),
                w_ffn_gate[0].astype(BF16), w_ffn_up[0].astype(BF16), w_ffn_down[0].astype(BF16))
    dense_vecs = (norm2_w, final_norm_w[None, :], ffn_conv_w[0], ffn_conv_b)
    cw = jnp.transpose(gdn_conv_w[0].reshape(GDN_CONV, GDN_SECTIONS, LANES), (1, 0, 2))
    alog_row = _lane_row(gdn_a_log[0], GDN_HEADS)
    dtb_row = _lane_row(gdn_dt_bias[0], GDN_HEADS)

    mod = _modulation(jnp.concatenate([c_prompt, c_sample], axis=0), w_mod[0], b_mod)
    mod_p = [mod[:nb, i * D_MODEL:(i + 1) * D_MODEL][:, None, :] for i in range(6)]
    mod_s = [mod[nb:, i * D_MODEL:(i + 1) * D_MODEL][None, :, :] for i in range(6)]

    qkvf, gact, ba, sq, skv, gab, qkv_tail = _inproj(x_prompt, mod_p[0], mod_p[1], norm1_w, in_ws, cw, tm=256)
    og, gdn_s_p = _gdn_prompt(qkvf, gact, ba, alog_row, dtb_row, gdn_onorm_w, _level_masks(), lt=4 * CHUNK)
    ob = _swa_prompt(sq, skv, swa_sinks[0], nq=4)
    y_p, gate_tail = _dense(og, ob, gab, x_prompt, (mod_p[2], mod_p[3], mod_p[4], mod_p[5]),
                            dense_vecs, dense_ws, None, tm=512)
    gdn_conv_p = jnp.transpose(qkv_tail[:, :, SUBLANES - (GDN_CONV - 1):, :], (0, 2, 1, 3)).reshape(
        nb, GDN_CONV - 1, GDN_CONV_CH)
    k_p = skv[:, seq - WINDOW:, :SWA_KV].reshape(nb, WINDOW, SWA_KV_HEADS, SWA_HD)
    v_p = skv[:, seq - WINDOW:, SWA_KV:].reshape(nb, WINDOW, SWA_KV_HEADS, SWA_HD)
    ffn_conv_p = gate_tail[:, SUBLANES - (FFN_CONV - 1):, :]

    xs = x_sample.reshape(1, ns, D_MODEL)
    qkvs, gates, bas, sqs, skvs, gabs = _inproj(xs, mod_s[0], mod_s[1], norm1_w, in_ws, None, tm=ns)
    st_gdn = jnp.transpose(state_gdn_conv[0], (1, 0, 2))
    og_s, gdn_s_s = _gdn_step(qkvs[0], st_gdn, gates[0], bas[0], cw, alog_row, dtb_row, gdn_onorm_w,
                              state_gdn_S[0], bb=8)
    to_channel_major = lambda c: jnp.transpose(c, (0, 2, 3, 1)).reshape(ns, SWA_KV, WINDOW)
    from_channel_major = lambda c: jnp.transpose(c.reshape(ns, SWA_KV_HEADS, SWA_HD, WINDOW), (0, 3, 1, 2))
    kvn_t = jnp.pad(jnp.transpose(skvs[0]), ((0, 0), (0, WINDOW - ns)))
    o3, k_s, v_s = _swa_step(sqs[0].reshape(ns, SWA_Q_HEADS, SWA_HD), kvn_t,
                             to_channel_major(cache_swa_k[0]), to_channel_major(cache_swa_v[0]),
                             swa_sinks[0][:, None], bb=8)
    ob_s = o3.reshape(1, ns, SWA_Q).astype(BF16)
    st_ffn = jnp.transpose(state_ffn_conv[0], (1, 0, 2))[:, None]
    y_s, gate_new = _dense(og_s[None], ob_s, gabs, xs, (mod_s[2], mod_s[3], mod_s[4], mod_s[5]),
                           dense_vecs, dense_ws, st_ffn, tm=ns)
    gdn_conv_s = jnp.concatenate([state_gdn_conv[0][:, 1:], qkvs[0][:, None, :]], axis=1)
    ffn_conv_s = jnp.concatenate([state_ffn_conv[0][:, 1:], gate_new[0][:, None, :]], axis=1)

    return (y_p, y_s.reshape(ns, 1, D_MODEL),
            gdn_s_p[None], gdn_s_s[None],
            gdn_conv_p[None], gdn_conv_s[None],
            k_p[None], from_channel_major(k_s)[None],
            v_p[None], from_channel_major(v_s)[None],
            ffn_conv_p[None], ffn_conv_s[None])
```

```python
import functools
import math

import numpy as np
import jax
import jax.numpy as jnp
from jax import lax
from jax.experimental import pallas as pl
from jax.experimental.pallas import tpu as pltpu

F32 = jnp.float32
BF16 = jnp.bfloat16

D_MODEL = 1024
GDN_HEADS = 8
GDN_DK = 128
GDN_DV = 128
GDN_QK = GDN_HEADS * GDN_DK
GDN_V = GDN_HEADS * GDN_DV
GDN_CONV = 4
GDN_CONV_CH = 2 * GDN_QK + GDN_V
GDN_SECTIONS = GDN_CONV_CH // 128
SWA_Q_HEADS = 16
SWA_KV_HEADS = 4
SWA_GROUP = SWA_Q_HEADS // SWA_KV_HEADS
SWA_HD = 64
SWA_Q = SWA_Q_HEADS * SWA_HD
SWA_KV = SWA_KV_HEADS * SWA_HD
WINDOW = 128
D_FF = 2816
FFN_CONV = 3
EPS = 1e-6

LANES = 128
SUBLANES = 8
VMEM_LIMIT = 56 * 1024 * 1024

COL_QKV = 0
COL_GATE = COL_QKV + GDN_CONV_CH
COL_SQ = COL_GATE + GDN_V
COL_SKV = COL_SQ + SWA_Q
COL_GAB = COL_SKV + 2 * SWA_KV
COL_BA = COL_GAB + 2 * D_MODEL
IN_COLS = COL_BA + LANES

SWA_Q_SCALE = SWA_HD ** -0.5 * math.log2(math.e)

CONV_ROWS = 64
CHUNK = 128
GDN_GROUP = 16
FFN_COLS = 256


def _mm(a, b):
    return jnp.dot(a.astype(BF16), b.astype(BF16), preferred_element_type=F32)


def _mm_nt(a, b):
    return lax.dot_general(a.astype(BF16), b.astype(BF16), (((1,), (1,)), ((), ())),
                           preferred_element_type=F32)


def _silu(x):
    return x * jax.nn.sigmoid(x)


def _softplus(x):
    return jnp.maximum(x, 0.0) + jnp.log1p(jnp.exp(-jnp.abs(x)))


def _rms(x, w):
    return x * lax.rsqrt(jnp.mean(x * x, axis=-1, keepdims=True) + EPS) * w


def _const_spec(shape):
    n = len(shape)
    return pl.BlockSpec(shape, lambda *_: (0,) * n, pipeline_mode=pl.Buffered(1))


def _params(sem):
    return pltpu.CompilerParams(dimension_semantics=sem, vmem_limit_bytes=VMEM_LIMIT)


def _mod_body(c_ref, w_ref, b_ref, o_ref):
    o_ref[...] = _mm(_silu(c_ref[...]), w_ref[...]) + b_ref[...]


def _modulation(c_all, w_mod, b_mod):
    rows = c_all.shape[0]
    n_out = w_mod.shape[1]
    tn = D_MODEL
    return pl.pallas_call(
        _mod_body,
        grid=(n_out // tn,),
        in_specs=[pl.BlockSpec((rows, D_MODEL), lambda j: (0, 0)),
                  pl.BlockSpec((D_MODEL, tn), lambda j: (0, j)),
                  pl.BlockSpec((1, tn), lambda j: (0, j))],
        out_specs=pl.BlockSpec((rows, tn), lambda j: (0, j)),
        out_shape=jax.ShapeDtypeStruct((rows, n_out), F32),
        compiler_params=_params(("arbitrary",)),
        name="modulation",
    )(c_all, w_mod, b_mod)


IN_WEIGHT_COLS = 512
IN_WEIGHT_PAD = 256


def _in_weight_body(n_main, wt_ref, ba_ref, o_ref):
    j = pl.program_id(0)

    @pl.when(j < n_main)
    def _():
        o_ref[...] = wt_ref[...].T.astype(BF16)

    @pl.when(j == n_main)
    def _():
        n_ba = ba_ref.shape[0]
        ba = jnp.concatenate([ba_ref[...].T, jnp.zeros((D_MODEL, IN_WEIGHT_COLS - n_ba), F32)], axis=1)
        o_ref[...] = ba.astype(BF16)


def _in_weight(w_t):
    n_ba = 2 * GDN_HEADS
    split = GDN_CONV_CH + GDN_V
    tc = IN_WEIGHT_COLS
    n_main = COL_BA // tc

    def src_row(j):
        jj = jnp.minimum(j, n_main - 1)
        return pl.multiple_of(jnp.where(jj * tc < split, jj * tc, jj * tc + n_ba), n_ba)

    return pl.pallas_call(
        functools.partial(_in_weight_body, n_main),
        grid=(n_main + 1,),
        in_specs=[pl.BlockSpec((pl.Element(tc), pl.Element(D_MODEL)), lambda j: (src_row(j), 0)),
                  pl.BlockSpec((pl.Element(n_ba), pl.Element(D_MODEL)), lambda j: (split, 0))],
        out_specs=pl.BlockSpec((D_MODEL, tc), lambda j: (0, j)),
        out_shape=jax.ShapeDtypeStruct((D_MODEL, COL_BA + IN_WEIGHT_PAD), BF16),
        compiler_params=_params(("arbitrary",)),
        name="in_weight",
    )(w_t, w_t)


def _l2norm(x):
    return x * lax.rsqrt(jnp.sum(x * x, axis=-1, keepdims=True) + EPS)


def _inproj_body(seq_rows, tm, x_ref, sh_ref, sc_ref, nw_ref, w_ref, *rest):
    if seq_rows:
        cw_ref, qkv_ref, gg_ref, ba_ref, sq_ref, skv_ref, gab_ref, tail_ref, xe_ref = rest

        @pl.when(pl.program_id(1) == 0)
        def _():
            xe_ref[:, 0:SUBLANES, :] = jnp.zeros((GDN_SECTIONS, SUBLANES, LANES), F32)
    else:
        qkv_ref, gg_ref, ba_ref, sq_ref, skv_ref, gab_ref = rest

    h = _rms(x_ref[...], nw_ref[...]) * (1.0 + sc_ref[...]) + sh_ref[...]
    hb = h.astype(BF16)

    def proj(lo, width):
        return jnp.dot(hb, w_ref[:, lo:lo + width], preferred_element_type=F32)

    step = 512
    per = step // LANES
    for c in range(GDN_CONV_CH // step):
        z = proj(COL_QKV + c * step, step)
        for k in range(per):
            s = c * per + k
            zs = z[:, k * LANES:(k + 1) * LANES]
            if not seq_rows:
                qkv_ref[:, s * LANES:(s + 1) * LANES] = zs
                continue
            xe_ref[s, SUBLANES:SUBLANES + tm, :] = zs
            w = cw_ref[s]
            for r0 in range(0, tm, CONV_ROWS):
                y = w[0:1] * xe_ref[s, r0 + SUBLANES - 3:r0 + SUBLANES - 3 + CONV_ROWS, :]
                for tap in range(1, GDN_CONV):
                    lo = r0 + SUBLANES - 3 + tap
                    y = y + w[tap:tap + 1] * xe_ref[s, lo:lo + CONV_ROWS, :]
                f = _silu(y)
                if s < GDN_HEADS:
                    f = _l2norm(f) * (GDN_DK ** -0.5)
                elif s < 2 * GDN_HEADS:
                    f = _l2norm(f)
                qkv_ref[s, r0:r0 + CONV_ROWS, :] = f
            xe_ref[s, 0:SUBLANES, :] = xe_ref[s, tm:tm + SUBLANES, :]
    if seq_rows:
        tail_ref[...] = xe_ref[:, 0:SUBLANES, :]
    for c in range(GDN_V // step):
        z = proj(COL_GATE + c * step, step)
        for k in range(per):
            zs = z[:, k * LANES:(k + 1) * LANES]
            if seq_rows:
                gg_ref[c * per + k] = _silu(zs)
            else:
                gg_ref[:, (c * per + k) * LANES:(c * per + k + 1) * LANES] = zs
    ba_ref[...] = proj(COL_BA, LANES)
    for c in range(SWA_Q // step):
        sq_ref[:, c * step:(c + 1) * step] = proj(COL_SQ + c * step, step)
    skv_ref[...] = proj(COL_SKV, 2 * SWA_KV)
    for c in range(2 * D_MODEL // step):
        gab_ref[:, c * step:(c + 1) * step] = proj(COL_GAB + c * step, step)


def _inproj(x, sh, sc, nw, w_all, cw, tm):
    b_, l_, _ = x.shape
    r_ = sh.shape[1]
    rt = 1 if r_ == 1 else tm
    mod_map = (lambda b, t: (b, 0, 0)) if r_ == 1 else (lambda b, t: (b, t, 0))
    row_map = lambda b, t: (b, t, 0)
    head_map = lambda b, t: (b, 0, t, 0)
    seq_rows = cw is not None
    if seq_rows:
        gdn_shapes = (jax.ShapeDtypeStruct((b_, GDN_SECTIONS, l_, LANES), F32),
                      jax.ShapeDtypeStruct((b_, GDN_HEADS, l_, LANES), F32))
        gdn_specs = (pl.BlockSpec((None, GDN_SECTIONS, tm, LANES), head_map),
                     pl.BlockSpec((None, GDN_HEADS, tm, LANES), head_map))
    else:
        gdn_shapes = (jax.ShapeDtypeStruct((b_, l_, GDN_CONV_CH), F32),
                      jax.ShapeDtypeStruct((b_, l_, GDN_V), F32))
        gdn_specs = (pl.BlockSpec((None, tm, GDN_CONV_CH), row_map),
                     pl.BlockSpec((None, tm, GDN_V), row_map))
    out_shape = gdn_shapes + (
        jax.ShapeDtypeStruct((b_, l_, LANES), F32),
        jax.ShapeDtypeStruct((b_, l_, SWA_Q), F32),
        jax.ShapeDtypeStruct((b_, l_, 2 * SWA_KV), F32),
        jax.ShapeDtypeStruct((b_, l_, 2 * D_MODEL), F32),
    )
    out_specs = gdn_specs + (
        pl.BlockSpec((None, tm, LANES), row_map),
        pl.BlockSpec((None, tm, SWA_Q), row_map),
        pl.BlockSpec((None, tm, 2 * SWA_KV), row_map),
        pl.BlockSpec((None, tm, 2 * D_MODEL), row_map),
    )
    in_specs = [
        pl.BlockSpec((None, tm, D_MODEL), row_map),
        pl.BlockSpec((None, rt, D_MODEL), mod_map),
        pl.BlockSpec((None, rt, D_MODEL), mod_map),
        _const_spec(nw.shape),
        _const_spec(w_all.shape),
    ]
    args = [x, sh, sc, nw, w_all]
    scratch = []
    if seq_rows:
        in_specs.append(_const_spec(cw.shape))
        args.append(cw)
        out_shape += (jax.ShapeDtypeStruct((b_, GDN_SECTIONS, SUBLANES, LANES), F32),)
        out_specs += (pl.BlockSpec((None, GDN_SECTIONS, SUBLANES, LANES), lambda b, t: (b, 0, 0, 0)),)
        scratch.append(pltpu.VMEM((GDN_SECTIONS, tm + SUBLANES, LANES), F32))
    return pl.pallas_call(
        functools.partial(_inproj_body, seq_rows, tm),
        grid=(b_, l_ // tm),
        in_specs=in_specs,
        out_specs=out_specs,
        out_shape=out_shape,
        scratch_shapes=scratch,
        compiler_params=_params(("arbitrary", "arbitrary")),
        name="inproj_seq" if seq_rows else "inproj_rows",
    )(*args)


def _delta_gates(ba, alog_row, dtb_row):
    beta_all = jax.nn.sigmoid(ba)
    g_all = -jnp.exp(alog_row) * _softplus(ba + dtb_row)
    return beta_all, g_all


def _lane_column(x, lane_idx, lane):
    return jnp.sum(jnp.where(lane_idx == lane, x, 0.0), axis=1, keepdims=True)


def _level_masks():
    r = np.arange(CHUNK)[:, None]
    c = np.arange(CHUNK)[None, :]
    masks = [(r == c + 1) & (r % 2 == 1)]
    half = 2
    while half < CHUNK:
        full = 2 * half
        masks.append((r // full == c // full) & (r % full >= half) & (c % full < half))
        half = full
    return jnp.asarray(np.stack(masks), dtype=BF16)


def _unit_lower_inverses(ms, masks_ref, eye, between_levels=()):
    ts = [eye - m * masks_ref[0] for m in ms]
    pending = list(between_levels)
    for lvl in range(1, masks_ref.shape[0]):
        off = masks_ref[lvl]
        xs = [jnp.dot(m * off, t, preferred_element_type=F32).astype(BF16) for m, t in zip(ms, ts)]
        ys = [jnp.dot(t, x, preferred_element_type=F32).astype(BF16) for t, x in zip(ts, xs)]
        ts = [t - y for t, y in zip(ts, ys)]
        if pending:
            pending.pop(0)()
    for piece in pending:
        piece()
    return ts


def _cumsum_rows(g, ltri):
    hi = g.astype(BF16)
    r1 = g - hi.astype(F32)
    mid = r1.astype(BF16)
    lo = (r1 - mid.astype(F32)).astype(BF16)
    return (jnp.dot(ltri, hi, preferred_element_type=F32) + jnp.dot(ltri, mid, preferred_element_type=F32)
            + jnp.dot(ltri, lo, preferred_element_type=F32))


def _gated_out_norm(o, gate_act, onw):
    on = o * lax.rsqrt(jnp.mean(o * o, axis=-1, keepdims=True) + EPS) * onw
    return on * gate_act


def _gdn_prompt_body(lt, q_ref, k_ref, v_ref, ba_ref, alog_ref, dtb_ref, gate_ref, onw_ref, masks_ref,
                     og_ref, s_ref):
    @pl.when(pl.program_id(1) == 0)
    def _():
        s_ref[...] = jnp.zeros_like(s_ref)

    beta_all, g_all = _delta_gates(ba_ref[...], alog_ref[...], dtb_ref[...])
    lane_idx = lax.broadcasted_iota(jnp.int32, (CHUNK, LANES), 1)
    row = lax.broadcasted_iota(jnp.int32, (CHUNK, CHUNK), 0)
    col = lax.broadcasted_iota(jnp.int32, (CHUNK, CHUNK), 1)
    tril = row >= col
    strict = row > col
    ltri = jnp.where(tril, 1.0, 0.0).astype(BF16)
    eye = jnp.where(row == col, 1.0, 0.0).astype(BF16)
    onw = onw_ref[...]
    heads = range(GDN_HEADS)
    chunks = range(lt // CHUNK)

    blocks = [(c, j) for c in chunks for j in heads]
    pre = {}
    for c in chunks:
        rows = slice(c * CHUNK, (c + 1) * CHUNK)
        dec = _cumsum_rows(g_all[rows], ltri)
        dec_t = dec.T
        for j in heads:
            q, k, v = q_ref[j, rows, :], k_ref[j, rows, :], v_ref[j, rows, :]
            beta_col = _lane_column(beta_all[rows], lane_idx, j)
            dec_col = _lane_column(dec, lane_idx, GDN_HEADS + j)
            dec_row = dec_t[GDN_HEADS + j:GDN_HEADS + j + 1, :]
            dec_last = dec_row[:, CHUNK - 1:CHUNK]
            gam = jnp.exp(jnp.minimum(dec_col - dec_row, 0.0))
            e_col = jnp.exp(dec_col)
            kb = k * beta_col
            pre[c, j] = dict(q=q, k=k, gam=gam, kb=kb, qe=q * e_col, e_last=jnp.exp(dec_last),
                             kd=k * jnp.exp(dec_last - dec_col),
                             rhs=jnp.concatenate([v * beta_col, kb * e_col], axis=1).astype(BF16))
    a_intra, uw = {}, {}

    def recurrence(c):
        rows = slice(c * CHUNK, (c + 1) * CHUNK)
        mid = {}

        def read_out():
            mid["s"] = [s_ref[j] for j in heads]
            mid["ws_qs"] = [_mm(jnp.concatenate([uw[c, j][:, GDN_DV:], pre[c, j]["qe"]], axis=0), mid["s"][j])
                            for j in heads]

        def update():
            s_prev, ws_qs = mid["s"], mid["ws_qs"]
            v_new = [uw[c, j][:, :GDN_DV] - ws_qs[j][:CHUNK] for j in heads]
            o = [ws_qs[j][CHUNK:] + _mm(a_intra[c, j], v_new[j]) for j in heads]
            s_new = [s_prev[j] * pre[c, j]["e_last"] + _mm(pre[c, j]["kd"].T, v_new[j]) for j in heads]
            for j in heads:
                s_ref[j] = s_new[j]
                og = _gated_out_norm(o[j], gate_ref[j, rows, :], onw)
                og_ref[rows, j * GDN_DV:(j + 1) * GDN_DV] = og.astype(og_ref.dtype)

        return [read_out, update]

    carried = []
    for g0 in range(0, len(blocks), GDN_GROUP):
        grp = blocks[g0:g0 + GDN_GROUP]
        grams = [_mm_nt(jnp.concatenate([pre[b]["kb"], pre[b]["q"]], axis=0), pre[b]["k"]) for b in grp]
        ms = [jnp.where(strict, g[:CHUNK] * pre[b]["gam"], 0.0).astype(BF16) for g, b in zip(grams, grp)]
        a_intra.update({b: jnp.where(tril, g[CHUNK:] * pre[b]["gam"], 0.0) for g, b in zip(grams, grp)})
        t_inv = _unit_lower_inverses(ms, masks_ref, eye, carried)
        uw.update({b: jnp.dot(t, pre[b]["rhs"], preferred_element_type=F32) for t, b in zip(t_inv, grp)})
        carried = [piece for c in sorted({c for c, _ in grp}) for piece in recurrence(c)]
    for piece in carried:
        piece()


def _gdn_prompt(qkvf, gact, ba, alog_row, dtb_row, onw, masks, lt):
    b_, _, l_, _ = qkvf.shape
    sec = lambda s: pl.BlockSpec((None, GDN_HEADS, lt, LANES), lambda b, t, s=s: (b, s, t, 0))
    return pl.pallas_call(
        functools.partial(_gdn_prompt_body, lt),
        grid=(b_, l_ // lt),
        in_specs=[sec(0), sec(1), sec(2),
                  pl.BlockSpec((None, lt, LANES), lambda b, t: (b, t, 0)),
                  _const_spec(alog_row.shape), _const_spec(dtb_row.shape),
                  pl.BlockSpec((None, GDN_HEADS, lt, LANES), lambda b, t: (b, 0, t, 0)),
                  _const_spec(onw.shape), _const_spec(masks.shape)],
        out_specs=(pl.BlockSpec((None, lt, GDN_V), lambda b, t: (b, t, 0)),
                   pl.BlockSpec((None, GDN_HEADS, GDN_DK, GDN_DV), lambda b, t: (b, 0, 0, 0))),
        out_shape=(jax.ShapeDtypeStruct((b_, l_, GDN_V), BF16),
                   jax.ShapeDtypeStruct((b_, GDN_HEADS, GDN_DK, GDN_DV), F32)),
        compiler_params=_params(("arbitrary", "arbitrary")),
        name="gdn_prompt",
    )(qkvf, qkvf, qkvf, ba, alog_row, dtb_row, gact, onw, masks)


def _gdn_step_body(bb, x_ref, st_ref, cw_ref, ba_ref, alog_ref, dtb_ref, gate_ref, onw_ref, s0_ref,
                   og_ref, sn_ref, q_s, k_s, v_s, b_s, e_s, o_s):
    beta_all, g_all = _delta_gates(ba_ref[...], alog_ref[...], dtb_ref[...])
    lane_idx = lax.broadcasted_iota(jnp.int32, (bb, LANES), 1)
    for h in range(GDN_HEADS):
        feats = []
        for s in range(3):
            idx = s * GDN_HEADS + h
            cols = slice(idx * LANES, (idx + 1) * LANES)
            w = cw_ref[idx]
            y = w[0:1] * st_ref[0, :, cols]
            for tap in range(1, GDN_CONV - 1):
                y = y + w[tap:tap + 1] * st_ref[tap, :, cols]
            y = y + w[GDN_CONV - 1:GDN_CONV] * x_ref[:, cols]
            feats.append(_silu(y))
        q, k, v = feats
        q_s[h] = q * lax.rsqrt(jnp.sum(q * q, axis=-1, keepdims=True) + EPS) * (GDN_DK ** -0.5)
        k_s[h] = k * lax.rsqrt(jnp.sum(k * k, axis=-1, keepdims=True) + EPS)
        v_s[h] = v
        b_s[h] = jnp.broadcast_to(_lane_column(beta_all, lane_idx, h), (bb, LANES))
        e_s[h] = jnp.broadcast_to(jnp.exp(_lane_column(g_all, lane_idx, h + GDN_HEADS)), (bb, LANES))

    eye = (lax.broadcasted_iota(jnp.int32, (GDN_DK, GDN_DK), 0)
           == lax.broadcasted_iota(jnp.int32, (GDN_DK, GDN_DK), 1))

    def to_col(r):
        return jnp.sum(jnp.where(eye, jnp.broadcast_to(r, (GDN_DK, GDN_DK)), 0.0), axis=1, keepdims=True)

    sub = lax.broadcasted_iota(jnp.int32, (SUBLANES, GDN_DK), 0)

    def seq_body(i, carry):
        for h in range(GDN_HEADS):
            one = pl.ds(i, 1)
            k_row = k_s[h, one, :]
            q_row = q_s[h, one, :]
            s1 = s0_ref[i, h] * e_s[h, one, :]
            kq = jnp.where(sub == 0, k_row, jnp.where(sub == 1, q_row, 0.0))
            kq_s1 = _mm(kq, s1)
            delta = (v_s[h, one, :] - kq_s1[0:1, :]) * b_s[h, one, :]
            sn_ref[i, h] = s1 + to_col(k_row) * delta
            qk = jnp.sum(q_row * k_row, axis=1, keepdims=True)
            o_s[h, one, :] = kq_s1[1:2, :] + qk * delta
        return carry

    lax.fori_loop(0, bb, seq_body, 0)
    onw = onw_ref[...]
    for h in range(GDN_HEADS):
        cols = slice(h * GDN_DV, (h + 1) * GDN_DV)
        og = _gated_out_norm(o_s[h], _silu(gate_ref[:, cols]), onw)
        og_ref[:, cols] = og.astype(og_ref.dtype)


def _gdn_step(qkv, st, gate, ba, cw, alog_row, dtb_row, onw, s0, bb):
    n_ = ba.shape[0]
    vec = pltpu.VMEM((GDN_HEADS, bb, LANES), F32)
    return pl.pallas_call(
        functools.partial(_gdn_step_body, bb),
        grid=(n_ // bb,),
        in_specs=[pl.BlockSpec((bb, GDN_CONV_CH), lambda i: (i, 0)),
                  pl.BlockSpec((GDN_CONV - 1, bb, GDN_CONV_CH), lambda i: (0, i, 0)),
                  _const_spec(cw.shape),
                  pl.BlockSpec((bb, LANES), lambda i: (i, 0)),
                  _const_spec(alog_row.shape), _const_spec(dtb_row.shape),
                  pl.BlockSpec((bb, GDN_V), lambda i: (i, 0)),
                  _const_spec(onw.shape),
                  pl.BlockSpec((bb, GDN_HEADS, GDN_DK, GDN_DV), lambda i: (i, 0, 0, 0))],
        out_specs=(pl.BlockSpec((bb, GDN_V), lambda i: (i, 0)),
                   pl.BlockSpec((bb, GDN_HEADS, GDN_DK, GDN_DV), lambda i: (i, 0, 0, 0))),
        out_shape=(jax.ShapeDtypeStruct((n_, GDN_V), BF16),
                   jax.ShapeDtypeStruct(s0.shape, F32)),
        scratch_shapes=[vec, vec, vec, vec, vec, vec],
        compiler_params=_params(("arbitrary",)),
        name="gdn_step",
    )(qkv, st, cw, ba, alog_row, dtb_row, gate, onw, s0)


def _swa_prompt_body(nq, sinks_ref, q_ref, kvp_ref, kvc_ref, o_ref):
    n = pl.program_id(1)
    w = WINDOW
    tiles = SWA_KV // LANES
    pairs = 2
    lo_lane = lax.broadcasted_iota(jnp.int32, (w, LANES), 1) < SWA_HD
    lo_row = lax.broadcasted_iota(jnp.int32, (LANES, w), 0) < SWA_HD
    c = lax.broadcasted_iota(jnp.int32, (w, pairs * w), 0)
    i = lax.broadcasted_iota(jnp.int32, (w, pairs * w), 1) & (w - 1)
    from_prev = c > i
    k_blk, vt_blk = [], []
    for j in range(nq + 1):
        src, rows = (kvp_ref, slice(0, w)) if j == 0 else (kvc_ref, slice((j - 1) * w, j * w))
        k_tiles, vt_tiles = [], []
        for t in range(tiles):
            kx = src[rows, t * LANES:(t + 1) * LANES]
            vt = src[rows, SWA_KV + t * LANES:SWA_KV + (t + 1) * LANES].T
            k_tiles.append((kx.astype(BF16), pltpu.roll(kx, SWA_HD, axis=1).astype(BF16)))
            vt_tiles.append((vt.astype(BF16),
                             jnp.concatenate([vt[SWA_HD:], vt[:SWA_HD]], axis=0).astype(BF16)))
        k_blk.append(k_tiles)
        vt_blk.append(vt_tiles)
    items = [(qb, g, p) for qb in range(nq) for g in range(SWA_KV_HEADS) for p in range(2)]
    log2e = math.log2(math.e)
    qm, kz, vzt, sink = {}, {}, {}, {}
    for qb, g, p in items:
        keep = lo_lane if p == 0 else jnp.logical_not(lo_lane)
        q_tiles = [q_ref[qb * w:(qb + 1) * w, (2 * g + r) * LANES:(2 * g + r + 1) * LANES] for r in range(pairs)]
        qm[qb, g, p] = jnp.concatenate([jnp.where(keep, x * SWA_Q_SCALE, 0.0) for x in q_tiles],
                                       axis=0).astype(BF16)
        variant = 0 if p == g % 2 else 1
        kz[qb, g, p] = jnp.concatenate([k_blk[qb + d][g // 2][variant] for d in range(2)], axis=0)
        vzt[qb, g, p] = jnp.concatenate([vt_blk[qb + d][g // 2][variant] for d in range(2)], axis=1)
        sink[qb, g, p] = jnp.concatenate([jnp.full((1, w), sinks_ref[SWA_GROUP * g + 2 * r + p] * log2e, F32)
                                          for r in range(pairs)], axis=1)
    st = {b: lax.dot_general(kz[b], qm[b], (((1,), (1,)), ((), ())), preferred_element_type=F32) for b in items}
    prev = {b: jnp.where(n > 0, st[b][:w], -jnp.inf) if b[0] == 0 else st[b][:w] for b in items}
    u = {b: jnp.where(from_prev, prev[b], st[b][w:]) for b in items}
    m = {b: jnp.maximum(jnp.max(u[b], axis=0, keepdims=True), sink[b]) for b in items}
    eu = {b: jnp.exp2(u[b] - m[b]) for b in items}
    den = {b: jnp.sum(eu[b], axis=0, keepdims=True) + jnp.exp2(sink[b] - m[b]) for b in items}
    et = {b: jnp.concatenate([jnp.where(from_prev, eu[b], 0.0), jnp.where(from_prev, 0.0, eu[b])],
                             axis=0).astype(BF16) for b in items}
    ot = {b: jnp.dot(vzt[b], et[b], preferred_element_type=F32) / den[b] for b in items}
    for qb in range(nq):
        for g in range(SWA_KV_HEADS):
            for r in range(pairs):
                cols = slice(r * w, (r + 1) * w)
                tile_t = jnp.where(lo_row, ot[qb, g, 0][:, cols], ot[qb, g, 1][:, cols])
                o_ref[qb * w:(qb + 1) * w, (2 * g + r) * LANES:(2 * g + r + 1) * LANES] = (
                    tile_t.T.astype(o_ref.dtype))


def _swa_prompt(sq, skv, sinks, nq):
    b_, l_, _ = sq.shape
    rows = nq * WINDOW
    return pl.pallas_call(
        functools.partial(_swa_prompt_body, nq),
        grid=(b_, l_ // rows),
        in_specs=[pl.BlockSpec(memory_space=pltpu.SMEM),
                  pl.BlockSpec((None, rows, SWA_Q), lambda b, n: (b, n, 0)),
                  pl.BlockSpec((None, WINDOW, 2 * SWA_KV), lambda b, n: (b, jnp.maximum(n * nq - 1, 0), 0)),
                  pl.BlockSpec((None, rows, 2 * SWA_KV), lambda b, n: (b, n, 0))],
        out_specs=pl.BlockSpec((None, rows, SWA_Q), lambda b, n: (b, n, 0)),
        out_shape=jax.ShapeDtypeStruct((b_, l_, SWA_Q), BF16),
        compiler_params=_params(("arbitrary", "arbitrary")),
        name="swa_prompt",
    )(sinks, sq, skv, skv)


def _swa_step_body(bb, q_ref, kvn_ref, ck_ref, cv_ref, sink_ref, o_ref, nk_ref, nv_ref):
    w = WINDOW
    first = pl.program_id(0) * bb
    row = lax.broadcasted_iota(jnp.int32, (SWA_Q_HEADS, SWA_KV), 0)
    lane = lax.broadcasted_iota(jnp.int32, (SWA_Q_HEADS, SWA_KV), 1)
    own = (lane // SWA_HD) == (row // SWA_GROUP)
    newest = lax.broadcasted_iota(jnp.int32, (SWA_KV, w), 1) == w - 1
    sink = sink_ref[...]
    scale = SWA_HD ** -0.5
    seqs = range(bb)
    kn_all = kvn_ref[0:SWA_KV, :]
    vn_all = kvn_ref[SWA_KV:2 * SWA_KV, :]
    nk = [jnp.where(newest, pltpu.roll(kn_all, w - 1 - (first + i), axis=1), pltpu.roll(ck_ref[i], w - 1, axis=1))
          for i in seqs]
    nv = [jnp.where(newest, pltpu.roll(vn_all, w - 1 - (first + i), axis=1), pltpu.roll(cv_ref[i], w - 1, axis=1))
          for i in seqs]
    q_bd = [jnp.where(own, jnp.concatenate([q_ref[i]] * SWA_KV_HEADS, axis=1), 0.0) for i in seqs]
    s = [_mm(q_bd[i], nk[i]) * scale for i in seqs]
    m = [jnp.maximum(jnp.max(s[i], axis=1, keepdims=True), sink) for i in seqs]
    e = [jnp.exp(s[i] - m[i]) for i in seqs]
    den = [jnp.sum(e[i], axis=1, keepdims=True) + jnp.exp(sink - m[i]) for i in seqs]
    pv = [jnp.where(own, _mm_nt(e[i] / den[i], nv[i]), 0.0) for i in seqs]
    for i in seqs:
        o = pv[i][:, 0:SWA_HD]
        for g in range(1, SWA_KV_HEADS):
            o = o + pv[i][:, g * SWA_HD:(g + 1) * SWA_HD]
        o_ref[i] = o
        nk_ref[i] = nk[i]
        nv_ref[i] = nv[i]


def _swa_step(q3, kvn_t, ck_t, cv_t, sink_col, bb):
    n_ = q3.shape[0]
    assert n_ <= WINDOW
    cache = pl.BlockSpec((bb, SWA_KV, WINDOW), lambda i: (i, 0, 0))
    return pl.pallas_call(
        functools.partial(_swa_step_body, bb),
        grid=(n_ // bb,),
        in_specs=[pl.BlockSpec((bb, SWA_Q_HEADS, SWA_HD), lambda i: (i, 0, 0)),
                  _const_spec(kvn_t.shape),
                  cache, cache,
                  _const_spec(sink_col.shape)],
        out_specs=(pl.BlockSpec((bb, SWA_Q_HEADS, SWA_HD), lambda i: (i, 0, 0)), cache, cache),
        out_shape=(jax.ShapeDtypeStruct(q3.shape, F32),
                   jax.ShapeDtypeStruct(ck_t.shape, F32),
                   jax.ShapeDtypeStruct(cv_t.shape, F32)),
        compiler_params=_params(("arbitrary",)),
        name="swa_step",
    )(q3, kvn_t, ck_t, cv_t, sink_col)


def _dense_body(stateful, tm, og_ref, ob_ref, gab_ref, x_ref, gt1_ref, sh2_ref, sc2_ref, gt2_ref,
                n2w_ref, fnw_ref, wa_ref, wb_ref, wo_ref, wg_ref, wu_ref, cw_ref, cb_ref, wd_ref, *rest):
    if stateful:
        st_ref, y_ref, gout_ref, act_ref = rest
    else:
        y_ref, gout_ref, act_ref, gbuf_ref, carry_ref = rest

        @pl.when(pl.program_id(1) == 0)
        def _():
            carry_ref[...] = jnp.zeros_like(carry_ref)

    y_a = jnp.dot(og_ref[...], wa_ref[...], preferred_element_type=F32)
    y_b = jnp.dot(ob_ref[...], wb_ref[...], preferred_element_type=F32)
    merged = (jax.nn.sigmoid(gab_ref[:, 0:D_MODEL]) * y_a
              + jax.nn.sigmoid(gab_ref[:, D_MODEL:2 * D_MODEL]) * y_b)
    x1 = x_ref[...] + gt1_ref[...] * _mm(merged, wo_ref[...])
    h2 = (_rms(x1, n2w_ref[...]) * (1.0 + sc2_ref[...]) + sh2_ref[...]).astype(BF16)

    for c in range(D_FF // FFN_COLS):
        cols = slice(c * FFN_COLS, (c + 1) * FFN_COLS)
        gate = jnp.dot(h2, wg_ref[:, cols], preferred_element_type=F32)
        up = jnp.dot(h2, wu_ref[:, cols], preferred_element_type=F32)
        if stateful:
            g2 = st_ref[0, :, cols]
            g1 = st_ref[1, :, cols]
            gout_ref[:, cols] = gate
        else:
            gbuf_ref[0:SUBLANES, :] = carry_ref[:, cols]
            gbuf_ref[SUBLANES:SUBLANES + tm, :] = gate
            g2 = gbuf_ref[SUBLANES - 2:SUBLANES - 2 + tm, :]
            g1 = gbuf_ref[SUBLANES - 1:SUBLANES - 1 + tm, :]
            carry_ref[:, cols] = gbuf_ref[tm:tm + SUBLANES, :]
        gc = (cw_ref[0:1, cols] * g2 + cw_ref[1:2, cols] * g1 + cw_ref[2:3, cols] * gate) + cb_ref[:, cols]
        act_ref[:, cols] = (_silu(gc) * up).astype(BF16)
    if not stateful:
        gout_ref[...] = carry_ref[...]

    x2 = x1 + gt2_ref[...] * jnp.dot(act_ref[...], wd_ref[...], preferred_element_type=F32)
    y_ref[...] = _rms(x2, fnw_ref[...])


def _dense(og, ob, gab, x, mods, vecs, ws, st, tm):
    b_, l_, _ = x.shape
    r_ = mods[0].shape[1]
    rt = 1 if r_ == 1 else tm
    mod_map = (lambda b, t: (b, 0, 0)) if r_ == 1 else (lambda b, t: (b, t, 0))
    row_map = lambda b, t: (b, t, 0)
    stateful = st is not None
    in_specs = ([pl.BlockSpec((None, tm, D_MODEL), row_map),
                 pl.BlockSpec((None, tm, D_MODEL), row_map),
                 pl.BlockSpec((None, tm, 2 * D_MODEL), row_map),
                 pl.BlockSpec((None, tm, D_MODEL), row_map)]
                + [pl.BlockSpec((None, rt, D_MODEL), mod_map)] * 4
                + [_const_spec(a.shape) for a in vecs[:2]]
                + [_const_spec(ws[0].shape), _const_spec(ws[1].shape), _const_spec(ws[2].shape),
                   _const_spec(ws[3].shape), _const_spec(ws[4].shape),
                   _const_spec(vecs[2].shape), _const_spec(vecs[3].shape), _const_spec(ws[5].shape)])
    args = [og, ob, gab, x, *mods, vecs[0], vecs[1], ws[0], ws[1], ws[2], ws[3], ws[4], vecs[2], vecs[3], ws[5]]
    scratch = [pltpu.VMEM((tm, D_FF), BF16)]
    if stateful:
        in_specs.append(pl.BlockSpec((FFN_CONV - 1, None, tm, D_FF), lambda b, t: (0, b, t, 0)))
        args.append(st)
        gout_shape = jax.ShapeDtypeStruct((b_, l_, D_FF), F32)
        gout_spec = pl.BlockSpec((None, tm, D_FF), row_map)
    else:
        scratch += [pltpu.VMEM((tm + SUBLANES, FFN_COLS), F32), pltpu.VMEM((SUBLANES, D_FF), F32)]
        gout_shape = jax.ShapeDtypeStruct((b_, SUBLANES, D_FF), F32)
        gout_spec = pl.BlockSpec((None, SUBLANES, D_FF), lambda b, t: (b, 0, 0))
    return pl.pallas_call(
        functools.partial(_dense_body, stateful, tm),
        grid=(b_, l_ // tm),
        in_specs=in_specs,
        out_specs=(pl.BlockSpec((None, tm, D_MODEL), row_map), gout_spec),
        out_shape=(jax.ShapeDtypeStruct((b_, l_, D_MODEL), F32), gout_shape),
        scratch_shapes=scratch,
        compiler_params=_params(("arbitrary", "arbitrary")),
        name="dense_step" if stateful else "dense_prompt",
    )(*args)


def _lane_row(values, offset):
    return jnp.zeros((1, LANES), F32).at[0, offset:offset + values.shape[0]].set(values)


def kernel(x_prompt, x_sample, c_prompt, c_sample, state_gdn_S, state_gdn_conv, cache_swa_k, cache_swa_v,
           state_ffn_conv, w_mod, b_mod, norm1_w, norm2_w, w_in, gdn_conv_w, gdn_a_log, gdn_dt_bias,
           gdn_onorm_w, w_gdn_out, swa_sinks, w_swa_out, w_o, w_ffn_gate, w_ffn_up, ffn_conv_w, ffn_conv_b,
           w_ffn_down, final_norm_w):
    assert w_mod.shape[0] == 1, "single-layer trunk"
    nb, seq, _ = x_prompt.shape
    ns = x_sample.shape[0]
    assert x_sample.shape[1] == 1

    in_ws = _in_weight(jnp.transpose(w_in[0]))
    dense_ws = (w_gdn_out[0].astype(BF16), w_swa_out[0].astype(BF16), w_o[0].astype(BF16),
                w_ffn_gate[0].astype(BF16), w_ffn_up[0].astype(BF16), w_ffn_down[0].astype(BF16))
    dense_vecs = (norm2_w, final_norm_w[None, :], ffn_conv_w[0], ffn_conv_b)
    cw = jnp.transpose(gdn_conv_w[0].reshape(GDN_CONV, GDN_SECTIONS, LANES), (1, 0, 2))
    alog_row = _lane_row(gdn_a_log[0], GDN_HEADS)
    dtb_row = _lane_row(gdn_dt_bias[0], GDN_HEADS)

    mod = _modulation(jnp.concatenate([c_prompt, c_sample], axis=0), w_mod[0], b_mod)
    mod_p = [mod[:nb, i * D_MODEL:(i + 1) * D_MODEL][:, None, :] for i in range(6)]
    mod_s = [mod[nb:, i * D_MODEL:(i + 1) * D_MODEL][None, :, :] for i in range(6)]

    qkvf, gact, ba, sq, skv, gab, qkv_tail = _inproj(x_prompt, mod_p[0], mod_p[1], norm1_w, in_ws, cw, tm=256)
    og, gdn_s_p = _gdn_prompt(qkvf, gact, ba, alog_row, dtb_row, gdn_onorm_w, _level_masks(), lt=4 * CHUNK)
    ob = _swa_prompt(sq, skv, swa_sinks[0], nq=4)
    y_p, gate_tail = _dense(og, ob, gab, x_prompt, (mod_p[2], mod_p[3], mod_p[4], mod_p[5]),
                            dense_vecs, dense_ws, None, tm=512)
    gdn_conv_p = jnp.transpose(qkv_tail[:, :, SUBLANES - (GDN_CONV - 1):, :], (0, 2, 1, 3)).reshape(
        nb, GDN_CONV - 1, GDN_CONV_CH)
    k_p = skv[:, seq - WINDOW:, :SWA_KV].reshape(nb, WINDOW, SWA_KV_HEADS, SWA_HD)
    v_p = skv[:, seq - WINDOW:, SWA_KV:].reshape(nb, WINDOW, SWA_KV_HEADS, SWA_HD)
    ffn_conv_p = gate_tail[:, SUBLANES - (FFN_CONV - 1):, :]

    xs = x_sample.reshape(1, ns, D_MODEL)
    qkvs, gates, bas, sqs, skvs, gabs = _inproj(xs, mod_s[0], mod_s[1], norm1_w, in_ws, None, tm=ns)
    st_gdn = jnp.transpose(state_gdn_conv[0], (1, 0, 2))
    og_s, gdn_s_s = _gdn_step(qkvs[0], st_gdn, gates[0], bas[0], cw, alog_row, dtb_row, gdn_onorm_w,
                              state_gdn_S[0], bb=16)
    to_channel_major = lambda c: jnp.transpose(c, (0, 2, 3, 1)).reshape(ns, SWA_KV, WINDOW)
    from_channel_major = lambda c: jnp.transpose(c.reshape(ns, SWA_KV_HEADS, SWA_HD, WINDOW), (0, 3, 1, 2))
    kvn_t = jnp.pad(jnp.transpose(skvs[0]), ((0, 0), (0, WINDOW - ns)))
    o3, k_s, v_s = _swa_step(sqs[0].reshape(ns, SWA_Q_HEADS, SWA_HD), kvn_t,
                             to_channel_major(cache_swa_k[0]), to_channel_major(cache_swa_v[0]),
                             swa_sinks[0][:, None], bb=16)
    ob_s = o3.reshape(1, ns, SWA_Q).astype(BF16)
    st_ffn = jnp.transpose(state_ffn_conv[0], (1, 0, 2))[:, None]
    y_s, gate_new = _dense(og_s[None], ob_s, gabs, xs, (mod_s[2], mod_s[3], mod_s[4], mod_s[5]),
                           dense_vecs, dense_ws, st_ffn, tm=ns)
    gdn_conv_s = jnp.concatenate([state_gdn_conv[0][:, 1:], qkvs[0][:, None, :]], axis=1)
    ffn_conv_s = jnp.concatenate([state_ffn_conv[0][:, 1:], gate_new[0][:, None, :]], axis=1)

    return (y_p, y_s.reshape(ns, 1, D_MODEL),
            gdn_s_p[None], gdn_s_s[None],
            gdn_conv_p[None], gdn_conv_s[None],
            k_p[None], from_channel_major(k_s)[None],
            v_p[None], from_channel_major(v_s)[None],
            ffn_conv_p[None], ffn_conv_s[None])
```

```python
import functools
import math

import numpy as np
import jax
import jax.numpy as jnp
from jax import lax
from jax.experimental import pallas as pl
from jax.experimental.pallas import tpu as pltpu

F32 = jnp.float32
BF16 = jnp.bfloat16

D_MODEL = 1024
GDN_HEADS = 8
GDN_DK = 128
GDN_DV = 128
GDN_QK = GDN_HEADS * GDN_DK
GDN_V = GDN_HEADS * GDN_DV
GDN_CONV = 4
GDN_CONV_CH = 2 * GDN_QK + GDN_V
GDN_SECTIONS = GDN_CONV_CH // 128
SWA_Q_HEADS = 16
SWA_KV_HEADS = 4
SWA_GROUP = SWA_Q_HEADS // SWA_KV_HEADS
SWA_HD = 64
SWA_Q = SWA_Q_HEADS * SWA_HD
SWA_KV = SWA_KV_HEADS * SWA_HD
WINDOW = 128
D_FF = 2816
FFN_CONV = 3
EPS = 1e-6

LANES = 128
SUBLANES = 8
VMEM_LIMIT = 56 * 1024 * 1024

COL_QKV = 0
COL_GATE = COL_QKV + GDN_CONV_CH
COL_SQ = COL_GATE + GDN_V
COL_SKV = COL_SQ + SWA_Q
COL_GAB = COL_SKV + 2 * SWA_KV
COL_BA = COL_GAB + 2 * D_MODEL
IN_COLS = COL_BA + LANES

SWA_Q_SCALE = SWA_HD ** -0.5 * math.log2(math.e)

CONV_ROWS = 64
CHUNK = 128
GDN_GROUP = 16
FFN_COLS = 256


def _mm(a, b):
    return jnp.dot(a.astype(BF16), b.astype(BF16), preferred_element_type=F32)


def _mm_nt(a, b):
    return lax.dot_general(a.astype(BF16), b.astype(BF16), (((1,), (1,)), ((), ())),
                           preferred_element_type=F32)


def _silu(x):
    return x * jax.nn.sigmoid(x)


def _softplus(x):
    return jnp.maximum(x, 0.0) + jnp.log1p(jnp.exp(-jnp.abs(x)))


def _rms(x, w):
    return x * lax.rsqrt(jnp.mean(x * x, axis=-1, keepdims=True) + EPS) * w


def _const_spec(shape):
    n = len(shape)
    return pl.BlockSpec(shape, lambda *_: (0,) * n, pipeline_mode=pl.Buffered(1))


def _params(sem):
    return pltpu.CompilerParams(dimension_semantics=sem, vmem_limit_bytes=VMEM_LIMIT)


def _mod_body(c_ref, w_ref, b_ref, o_ref):
    o_ref[...] = _mm(_silu(c_ref[...]), w_ref[...]) + b_ref[...]


def _modulation(c_all, w_mod, b_mod):
    rows = c_all.shape[0]
    n_out = w_mod.shape[1]
    tn = D_MODEL
    return pl.pallas_call(
        _mod_body,
        grid=(n_out // tn,),
        in_specs=[pl.BlockSpec((rows, D_MODEL), lambda j: (0, 0)),
                  pl.BlockSpec((D_MODEL, tn), lambda j: (0, j)),
                  pl.BlockSpec((1, tn), lambda j: (0, j))],
        out_specs=pl.BlockSpec((rows, tn), lambda j: (0, j)),
        out_shape=jax.ShapeDtypeStruct((rows, n_out), F32),
        compiler_params=_params(("arbitrary",)),
        name="modulation",
    )(c_all, w_mod, b_mod)


IN_WEIGHT_COLS = 512
IN_WEIGHT_PAD = 256


def _in_weight_body(n_main, wt_ref, ba_ref, o_ref):
    j = pl.program_id(0)

    @pl.when(j < n_main)
    def _():
        o_ref[...] = wt_ref[...].T.astype(BF16)

    @pl.when(j == n_main)
    def _():
        n_ba = ba_ref.shape[0]
        ba = jnp.concatenate([ba_ref[...].T, jnp.zeros((D_MODEL, IN_WEIGHT_COLS - n_ba), F32)], axis=1)
        o_ref[...] = ba.astype(BF16)


def _in_weight(w_t):
    n_ba = 2 * GDN_HEADS
    split = GDN_CONV_CH + GDN_V
    tc = IN_WEIGHT_COLS
    n_main = COL_BA // tc

    def src_row(j):
        jj = jnp.minimum(j, n_main - 1)
        return pl.multiple_of(jnp.where(jj * tc < split, jj * tc, jj * tc + n_ba), n_ba)

    return pl.pallas_call(
        functools.partial(_in_weight_body, n_main),
        grid=(n_main + 1,),
        in_specs=[pl.BlockSpec((pl.Element(tc), pl.Element(D_MODEL)), lambda j: (src_row(j), 0)),
                  pl.BlockSpec((pl.Element(n_ba), pl.Element(D_MODEL)), lambda j: (split, 0))],
        out_specs=pl.BlockSpec((D_MODEL, tc), lambda j: (0, j)),
        out_shape=jax.ShapeDtypeStruct((D_MODEL, COL_BA + IN_WEIGHT_PAD), BF16),
        compiler_params=_params(("arbitrary",)),
        name="in_weight",
    )(w_t, w_t)


def _l2norm(x):
    return x * lax.rsqrt(jnp.sum(x * x, axis=-1, keepdims=True) + EPS)


def _inproj_body(seq_rows, tm, x_ref, sh_ref, sc_ref, nw_ref, w_ref, *rest):
    if seq_rows:
        cw_ref, qkv_ref, gg_ref, ba_ref, sq_ref, skv_ref, gab_ref, tail_ref, xe_ref = rest

        @pl.when(pl.program_id(1) == 0)
        def _():
            xe_ref[:, 0:SUBLANES, :] = jnp.zeros((GDN_SECTIONS, SUBLANES, LANES), F32)
    else:
        qkv_ref, gg_ref, ba_ref, sq_ref, skv_ref, gab_ref = rest

    h = _rms(x_ref[...], nw_ref[...]) * (1.0 + sc_ref[...]) + sh_ref[...]
    hb = h.astype(BF16)

    def proj(lo, width):
        return jnp.dot(hb, w_ref[:, lo:lo + width], preferred_element_type=F32)

    step = 512
    per = step // LANES
    for c in range(GDN_CONV_CH // step):
        z = proj(COL_QKV + c * step, step)
        for k in range(per):
            s = c * per + k
            zs = z[:, k * LANES:(k + 1) * LANES]
            if not seq_rows:
                qkv_ref[:, s * LANES:(s + 1) * LANES] = zs
                continue
            xe_ref[s, SUBLANES:SUBLANES + tm, :] = zs
            w = cw_ref[s]
            for r0 in range(0, tm, CONV_ROWS):
                y = w[0:1] * xe_ref[s, r0 + SUBLANES - 3:r0 + SUBLANES - 3 + CONV_ROWS, :]
                for tap in range(1, GDN_CONV):
                    lo = r0 + SUBLANES - 3 + tap
                    y = y + w[tap:tap + 1] * xe_ref[s, lo:lo + CONV_ROWS, :]
                f = _silu(y)
                if s < GDN_HEADS:
                    f = _l2norm(f) * (GDN_DK ** -0.5)
                elif s < 2 * GDN_HEADS:
                    f = _l2norm(f)
                qkv_ref[s, r0:r0 + CONV_ROWS, :] = f
            xe_ref[s, 0:SUBLANES, :] = xe_ref[s, tm:tm + SUBLANES, :]
    if seq_rows:
        tail_ref[...] = xe_ref[:, 0:SUBLANES, :]
    for c in range(GDN_V // step):
        z = proj(COL_GATE + c * step, step)
        for k in range(per):
            zs = z[:, k * LANES:(k + 1) * LANES]
            if seq_rows:
                gg_ref[c * per + k] = _silu(zs)
            else:
                gg_ref[:, (c * per + k) * LANES:(c * per + k + 1) * LANES] = zs
    ba_ref[...] = proj(COL_BA, LANES)
    for c in range(SWA_Q // step):
        sq_ref[:, c * step:(c + 1) * step] = proj(COL_SQ + c * step, step)
    skv_ref[...] = proj(COL_SKV, 2 * SWA_KV)
    for c in range(2 * D_MODEL // step):
        gab_ref[:, c * step:(c + 1) * step] = proj(COL_GAB + c * step, step)


def _inproj(x, sh, sc, nw, w_all, cw, tm):
    b_, l_, _ = x.shape
    r_ = sh.shape[1]
    rt = 1 if r_ == 1 else tm
    mod_map = (lambda b, t: (b, 0, 0)) if r_ == 1 else (lambda b, t: (b, t, 0))
    row_map = lambda b, t: (b, t, 0)
    head_map = lambda b, t: (b, 0, t, 0)
    seq_rows = cw is not None
    if seq_rows:
        gdn_shapes = (jax.ShapeDtypeStruct((b_, GDN_SECTIONS, l_, LANES), F32),
                      jax.ShapeDtypeStruct((b_, GDN_HEADS, l_, LANES), F32))
        gdn_specs = (pl.BlockSpec((None, GDN_SECTIONS, tm, LANES), head_map),
                     pl.BlockSpec((None, GDN_HEADS, tm, LANES), head_map))
    else:
        gdn_shapes = (jax.ShapeDtypeStruct((b_, l_, GDN_CONV_CH), F32),
                      jax.ShapeDtypeStruct((b_, l_, GDN_V), F32))
        gdn_specs = (pl.BlockSpec((None, tm, GDN_CONV_CH), row_map),
                     pl.BlockSpec((None, tm, GDN_V), row_map))
    out_shape = gdn_shapes + (
        jax.ShapeDtypeStruct((b_, l_, LANES), F32),
        jax.ShapeDtypeStruct((b_, l_, SWA_Q), F32),
        jax.ShapeDtypeStruct((b_, l_, 2 * SWA_KV), F32),
        jax.ShapeDtypeStruct((b_, l_, 2 * D_MODEL), F32),
    )
    out_specs = gdn_specs + (
        pl.BlockSpec((None, tm, LANES), row_map),
        pl.BlockSpec((None, tm, SWA_Q), row_map),
        pl.BlockSpec((None, tm, 2 * SWA_KV), row_map),
        pl.BlockSpec((None, tm, 2 * D_MODEL), row_map),
    )
    in_specs = [
        pl.BlockSpec((None, tm, D_MODEL), row_map),
        pl.BlockSpec((None, rt, D_MODEL), mod_map),
        pl.BlockSpec((None, rt, D_MODEL), mod_map),
        _const_spec(nw.shape),
        _const_spec(w_all.shape),
    ]
    args = [x, sh, sc, nw, w_all]
    scratch = []
    if seq_rows:
        in_specs.append(_const_spec(cw.shape))
        args.append(cw)
        out_shape += (jax.ShapeDtypeStruct((b_, GDN_SECTIONS, SUBLANES, LANES), F32),)
        out_specs += (pl.BlockSpec((None, GDN_SECTIONS, SUBLANES, LANES), lambda b, t: (b, 0, 0, 0)),)
        scratch.append(pltpu.VMEM((GDN_SECTIONS, tm + SUBLANES, LANES), F32))
    return pl.pallas_call(
        functools.partial(_inproj_body, seq_rows, tm),
        grid=(b_, l_ // tm),
        in_specs=in_specs,
        out_specs=out_specs,
        out_shape=out_shape,
        scratch_shapes=scratch,
        compiler_params=_params(("arbitrary", "arbitrary")),
        name="inproj_seq" if seq_rows else "inproj_rows",
    )(*args)


def _delta_gates(ba, alog_row, dtb_row):
    beta_all = jax.nn.sigmoid(ba)
    g_all = -jnp.exp(alog_row) * _softplus(ba + dtb_row)
    return beta_all, g_all


def _lane_column(x, lane_idx, lane):
    return jnp.sum(jnp.where(lane_idx == lane, x, 0.0), axis=1, keepdims=True)


def _level_masks():
    r = np.arange(CHUNK)[:, None]
    c = np.arange(CHUNK)[None, :]
    masks = [(r == c + 1) & (r % 2 == 1)]
    half = 2
    while half < CHUNK:
        full = 2 * half
        masks.append((r // full == c // full) & (r % full >= half) & (c % full < half))
        half = full
    return jnp.asarray(np.stack(masks), dtype=BF16)


def _unit_lower_inverses(ms, masks_ref, eye, between_levels=()):
    ts = [eye - m * masks_ref[0] for m in ms]
    pending = list(between_levels)
    for lvl in range(1, masks_ref.shape[0]):
        off = masks_ref[lvl]
        xs = [jnp.dot(m * off, t, preferred_element_type=F32).astype(BF16) for m, t in zip(ms, ts)]
        ys = [jnp.dot(t, x, preferred_element_type=F32).astype(BF16) for t, x in zip(ts, xs)]
        ts = [t - y for t, y in zip(ts, ys)]
        if pending:
            pending.pop(0)()
    for piece in pending:
        piece()
    return ts


def _cumsum_rows(g, ltri):
    hi = g.astype(BF16)
    r1 = g - hi.astype(F32)
    mid = r1.astype(BF16)
    lo = (r1 - mid.astype(F32)).astype(BF16)
    return (jnp.dot(ltri, hi, preferred_element_type=F32) + jnp.dot(ltri, mid, preferred_element_type=F32)
            + jnp.dot(ltri, lo, preferred_element_type=F32))


def _gated_out_norm(o, gate_act, onw):
    on = o * lax.rsqrt(jnp.mean(o * o, axis=-1, keepdims=True) + EPS) * onw
    return on * gate_act


def _run_all(pieces):
    for piece in pieces:
        piece()


def _gdn_prompt_body(lt, q_ref, k_ref, v_ref, ba_ref, alog_ref, dtb_ref, gate_ref, onw_ref, masks_ref,
                     f_ref, s0_ref, og_ref, s_ref, sn_ref, read_ref):
    side = [functools.partial(_gdn_state_update, f_ref, s0_ref, sn_ref, read_ref, i, h)
            for i in range(s0_ref.shape[0]) for h in range(GDN_HEADS)]

    @pl.when(pl.program_id(1) == 0)
    def _():
        s_ref[...] = jnp.zeros_like(s_ref)

    beta_all, g_all = _delta_gates(ba_ref[...], alog_ref[...], dtb_ref[...])
    lane_idx = lax.broadcasted_iota(jnp.int32, (CHUNK, LANES), 1)
    row = lax.broadcasted_iota(jnp.int32, (CHUNK, CHUNK), 0)
    col = lax.broadcasted_iota(jnp.int32, (CHUNK, CHUNK), 1)
    tril = row >= col
    strict = row > col
    ltri = jnp.where(tril, 1.0, 0.0).astype(BF16)
    eye = jnp.where(row == col, 1.0, 0.0).astype(BF16)
    onw = onw_ref[...]
    heads = range(GDN_HEADS)
    chunks = range(lt // CHUNK)

    blocks = [(c, j) for c in chunks for j in heads]
    pre = {}
    for c in chunks:
        rows = slice(c * CHUNK, (c + 1) * CHUNK)
        dec = _cumsum_rows(g_all[rows], ltri)
        dec_t = dec.T
        for j in heads:
            q, k, v = q_ref[j, rows, :], k_ref[j, rows, :], v_ref[j, rows, :]
            beta_col = _lane_column(beta_all[rows], lane_idx, j)
            dec_col = _lane_column(dec, lane_idx, GDN_HEADS + j)
            dec_row = dec_t[GDN_HEADS + j:GDN_HEADS + j + 1, :]
            dec_last = dec_row[:, CHUNK - 1:CHUNK]
            gam = jnp.exp(jnp.minimum(dec_col - dec_row, 0.0))
            e_col = jnp.exp(dec_col)
            kb = k * beta_col
            pre[c, j] = dict(q=q, k=k, gam=gam, kb=kb, qe=q * e_col, e_last=jnp.exp(dec_last),
                             kd=k * jnp.exp(dec_last - dec_col),
                             rhs=jnp.concatenate([v * beta_col, kb * e_col], axis=1).astype(BF16))
    a_intra, uw = {}, {}

    def recurrence(c):
        rows = slice(c * CHUNK, (c + 1) * CHUNK)
        mid = {}

        def read_out():
            mid["s"] = [s_ref[j] for j in heads]
            mid["ws_qs"] = [_mm(jnp.concatenate([uw[c, j][:, GDN_DV:], pre[c, j]["qe"]], axis=0), mid["s"][j])
                            for j in heads]

        def update():
            s_prev, ws_qs = mid["s"], mid["ws_qs"]
            v_new = [uw[c, j][:, :GDN_DV] - ws_qs[j][:CHUNK] for j in heads]
            o = [ws_qs[j][CHUNK:] + _mm(a_intra[c, j], v_new[j]) for j in heads]
            s_new = [s_prev[j] * pre[c, j]["e_last"] + _mm(pre[c, j]["kd"].T, v_new[j]) for j in heads]
            for j in heads:
                s_ref[j] = s_new[j]
                og = _gated_out_norm(o[j], gate_ref[j, rows, :], onw)
                og_ref[rows, j * GDN_DV:(j + 1) * GDN_DV] = og.astype(og_ref.dtype)

        return [read_out, update]

    n_groups = -(-len(blocks) // GDN_GROUP)
    n_levels = masks_ref.shape[0] - 1
    per_slot = -(-len(side) // (n_groups * n_levels))
    carried = []
    for g0 in range(0, len(blocks), GDN_GROUP):
        grp = blocks[g0:g0 + GDN_GROUP]
        grams = [_mm_nt(jnp.concatenate([pre[b]["kb"], pre[b]["q"]], axis=0), pre[b]["k"]) for b in grp]
        ms = [jnp.where(strict, g[:CHUNK] * pre[b]["gam"], 0.0).astype(BF16) for g, b in zip(grams, grp)]
        a_intra.update({b: jnp.where(tril, g[CHUNK:] * pre[b]["gam"], 0.0) for g, b in zip(grams, grp)})
        slots = []
        for _ in range(n_levels):
            work = carried[:1] + side[:per_slot]
            carried, side = carried[1:], side[per_slot:]
            slots.append(functools.partial(_run_all, work))
        t_inv = _unit_lower_inverses(ms, masks_ref, eye, slots)
        _run_all(carried)
        uw.update({b: jnp.dot(t, pre[b]["rhs"], preferred_element_type=F32) for t, b in zip(t_inv, grp)})
        carried = [piece for c in sorted({c for c, _ in grp}) for piece in recurrence(c)]
    _run_all(carried + side)


def _gdn_prompt(qkvf, gact, ba, alog_row, dtb_row, onw, masks, step_feats, step_state, lt):
    b_, _, l_, _ = qkvf.shape
    nt = l_ // lt
    per_step = step_state.shape[0] // (b_ * nt)
    assert per_step * b_ * nt == step_state.shape[0]
    sec = lambda s: pl.BlockSpec((None, GDN_HEADS, lt, LANES), lambda b, t, s=s: (b, s, t, 0))
    seq_map = lambda b, t: (b * nt + t, 0, 0)
    state_spec = pl.BlockSpec((per_step, GDN_HEADS, GDN_DK, GDN_DV), lambda b, t: (b * nt + t, 0, 0, 0))
    return pl.pallas_call(
        functools.partial(_gdn_prompt_body, lt),
        grid=(b_, nt),
        in_specs=[sec(0), sec(1), sec(2),
                  pl.BlockSpec((None, lt, LANES), lambda b, t: (b, t, 0)),
                  _const_spec(alog_row.shape), _const_spec(dtb_row.shape),
                  pl.BlockSpec((None, GDN_HEADS, lt, LANES), lambda b, t: (b, 0, t, 0)),
                  _const_spec(onw.shape), _const_spec(masks.shape),
                  pl.BlockSpec((per_step, STEP_FEATURES * GDN_HEADS, LANES), seq_map), state_spec],
        out_specs=(pl.BlockSpec((None, lt, GDN_V), lambda b, t: (b, t, 0)),
                   pl.BlockSpec((None, GDN_HEADS, GDN_DK, GDN_DV), lambda b, t: (b, 0, 0, 0)),
                   state_spec,
                   pl.BlockSpec((per_step, GDN_HEADS, LANES), seq_map)),
        out_shape=(jax.ShapeDtypeStruct((b_, l_, GDN_V), BF16),
                   jax.ShapeDtypeStruct((b_, GDN_HEADS, GDN_DK, GDN_DV), F32),
                   jax.ShapeDtypeStruct(step_state.shape, F32),
                   jax.ShapeDtypeStruct((step_state.shape[0], GDN_HEADS, LANES), F32)),
        compiler_params=_params(("arbitrary", "arbitrary")),
        name="gdn_prompt",
    )(qkvf, qkvf, qkvf, ba, alog_row, dtb_row, gact, onw, masks, step_feats, step_state)


def _gdn_step_body(bb, x_ref, st_ref, cw_ref, ba_ref, alog_ref, dtb_ref, gate_ref, onw_ref, s0_ref,
                   og_ref, sn_ref, q_s, k_s, v_s, b_s, e_s, o_s):
    beta_all, g_all = _delta_gates(ba_ref[...], alog_ref[...], dtb_ref[...])
    lane_idx = lax.broadcasted_iota(jnp.int32, (bb, LANES), 1)
    for h in range(GDN_HEADS):
        feats = []
        for s in range(3):
            idx = s * GDN_HEADS + h
            cols = slice(idx * LANES, (idx + 1) * LANES)
            w = cw_ref[idx]
            y = w[0:1] * st_ref[0, :, cols]
            for tap in range(1, GDN_CONV - 1):
                y = y + w[tap:tap + 1] * st_ref[tap, :, cols]
            y = y + w[GDN_CONV - 1:GDN_CONV] * x_ref[:, cols]
            feats.append(_silu(y))
        q, k, v = feats
        q_s[h] = q * lax.rsqrt(jnp.sum(q * q, axis=-1, keepdims=True) + EPS) * (GDN_DK ** -0.5)
        k_s[h] = k * lax.rsqrt(jnp.sum(k * k, axis=-1, keepdims=True) + EPS)
        v_s[h] = v
        b_s[h] = jnp.broadcast_to(_lane_column(beta_all, lane_idx, h), (bb, LANES))
        e_s[h] = jnp.broadcast_to(jnp.exp(_lane_column(g_all, lane_idx, h + GDN_HEADS)), (bb, LANES))

    eye = (lax.broadcasted_iota(jnp.int32, (GDN_DK, GDN_DK), 0)
           == lax.broadcasted_iota(jnp.int32, (GDN_DK, GDN_DK), 1))

    def to_col(r):
        return jnp.sum(jnp.where(eye, jnp.broadcast_to(r, (GDN_DK, GDN_DK)), 0.0), axis=1, keepdims=True)

    sub = lax.broadcasted_iota(jnp.int32, (SUBLANES, GDN_DK), 0)

    def seq_body(i, carry):
        for h in range(GDN_HEADS):
            one = pl.ds(i, 1)
            k_row = k_s[h, one, :]
            q_row = q_s[h, one, :]
            s1 = s0_ref[i, h] * e_s[h, one, :]
            kq = jnp.where(sub == 0, k_row, jnp.where(sub == 1, q_row, 0.0))
            kq_s1 = _mm(kq, s1)
            delta = (v_s[h, one, :] - kq_s1[0:1, :]) * b_s[h, one, :]
            sn_ref[i, h] = s1 + to_col(k_row) * delta
            qk = jnp.sum(q_row * k_row, axis=1, keepdims=True)
            o_s[h, one, :] = kq_s1[1:2, :] + qk * delta
        return carry

    lax.fori_loop(0, bb, seq_body, 0)
    onw = onw_ref[...]
    for h in range(GDN_HEADS):
        cols = slice(h * GDN_DV, (h + 1) * GDN_DV)
        og = _gated_out_norm(o_s[h], _silu(gate_ref[:, cols]), onw)
        og_ref[:, cols] = og.astype(og_ref.dtype)


def _gdn_step(qkv, st, gate, ba, cw, alog_row, dtb_row, onw, s0, bb):
    n_ = ba.shape[0]
    vec = pltpu.VMEM((GDN_HEADS, bb, LANES), F32)
    return pl.pallas_call(
        functools.partial(_gdn_step_body, bb),
        grid=(n_ // bb,),
        in_specs=[pl.BlockSpec((bb, GDN_CONV_CH), lambda i: (i, 0)),
                  pl.BlockSpec((GDN_CONV - 1, bb, GDN_CONV_CH), lambda i: (0, i, 0)),
                  _const_spec(cw.shape),
                  pl.BlockSpec((bb, LANES), lambda i: (i, 0)),
                  _const_spec(alog_row.shape), _const_spec(dtb_row.shape),
                  pl.BlockSpec((bb, GDN_V), lambda i: (i, 0)),
                  _const_spec(onw.shape),
                  pl.BlockSpec((bb, GDN_HEADS, GDN_DK, GDN_DV), lambda i: (i, 0, 0, 0))],
        out_specs=(pl.BlockSpec((bb, GDN_V), lambda i: (i, 0)),
                   pl.BlockSpec((bb, GDN_HEADS, GDN_DK, GDN_DV), lambda i: (i, 0, 0, 0))),
        out_shape=(jax.ShapeDtypeStruct((n_, GDN_V), BF16),
                   jax.ShapeDtypeStruct(s0.shape, F32)),
        scratch_shapes=[vec, vec, vec, vec, vec, vec],
        compiler_params=_params(("arbitrary",)),
        name="gdn_step",
    )(qkv, st, cw, ba, alog_row, dtb_row, gate, onw, s0)


STEP_FEATURES = 5


def _gdn_step_features_body(bb, x_ref, st_ref, cw_ref, ba_ref, alog_ref, dtb_ref, f_ref):
    beta_all, g_all = _delta_gates(ba_ref[...], alog_ref[...], dtb_ref[...])
    lane_idx = lax.broadcasted_iota(jnp.int32, (bb, LANES), 1)
    for h in range(GDN_HEADS):
        feats = []
        for s in range(3):
            idx = s * GDN_HEADS + h
            cols = slice(idx * LANES, (idx + 1) * LANES)
            w = cw_ref[idx]
            y = w[0:1] * st_ref[0, :, cols]
            for tap in range(1, GDN_CONV - 1):
                y = y + w[tap:tap + 1] * st_ref[tap, :, cols]
            y = y + w[GDN_CONV - 1:GDN_CONV] * x_ref[:, cols]
            feats.append(_silu(y))
        q, k, v = feats
        f_ref[h] = _l2norm(q) * (GDN_DK ** -0.5)
        f_ref[GDN_HEADS + h] = _l2norm(k)
        f_ref[2 * GDN_HEADS + h] = v
        f_ref[3 * GDN_HEADS + h] = jnp.broadcast_to(_lane_column(beta_all, lane_idx, h), (bb, LANES))
        f_ref[4 * GDN_HEADS + h] = jnp.broadcast_to(
            jnp.exp(_lane_column(g_all, lane_idx, h + GDN_HEADS)), (bb, LANES))


def _gdn_step_features(qkv, st, ba, cw, alog_row, dtb_row, bb):
    n_ = ba.shape[0]
    return pl.pallas_call(
        functools.partial(_gdn_step_features_body, bb),
        grid=(n_ // bb,),
        in_specs=[pl.BlockSpec((bb, GDN_CONV_CH), lambda i: (i, 0)),
                  pl.BlockSpec((GDN_CONV - 1, bb, GDN_CONV_CH), lambda i: (0, i, 0)),
                  _const_spec(cw.shape),
                  pl.BlockSpec((bb, LANES), lambda i: (i, 0)),
                  _const_spec(alog_row.shape), _const_spec(dtb_row.shape)],
        out_specs=pl.BlockSpec((STEP_FEATURES * GDN_HEADS, bb, LANES), lambda i: (0, i, 0)),
        out_shape=jax.ShapeDtypeStruct((STEP_FEATURES * GDN_HEADS, n_, LANES), F32),
        compiler_params=_params(("arbitrary",)),
        name="gdn_step_features",
    )(qkv, st, cw, ba, alog_row, dtb_row)


def _gdn_state_update(f_ref, s0_ref, sn_ref, o_ref, i, h):
    eye = (lax.broadcasted_iota(jnp.int32, (GDN_DK, GDN_DK), 0)
           == lax.broadcasted_iota(jnp.int32, (GDN_DK, GDN_DK), 1))

    def row(kind):
        return f_ref[i, kind * GDN_HEADS + h:kind * GDN_HEADS + h + 1, :]

    def to_col(r):
        return jnp.sum(jnp.where(eye, jnp.broadcast_to(r, (GDN_DK, GDN_DK)), 0.0), axis=1, keepdims=True)

    q_row, k_row, v_row, beta, decay = (row(kind) for kind in range(STEP_FEATURES))
    k_col = to_col(k_row)
    s1 = s0_ref[i, h] * decay
    delta = (v_row - jnp.sum(s1 * k_col, axis=0, keepdims=True)) * beta
    s2 = s1 + k_col * delta
    sn_ref[i, h] = s2
    o_ref[i, h:h + 1, :] = jnp.sum(s2 * to_col(q_row), axis=0, keepdims=True)


def _swa_prompt_body(nq, sinks_ref, q_ref, kvp_ref, kvc_ref, o_ref):
    n = pl.program_id(1)
    w = WINDOW
    tiles = SWA_KV // LANES
    pairs = 2
    lo_lane = lax.broadcasted_iota(jnp.int32, (w, LANES), 1) < SWA_HD
    lo_row = lax.broadcasted_iota(jnp.int32, (LANES, w), 0) < SWA_HD
    c = lax.broadcasted_iota(jnp.int32, (w, pairs * w), 0)
    i = lax.broadcasted_iota(jnp.int32, (w, pairs * w), 1) & (w - 1)
    from_prev = c > i
    k_blk, vt_blk = [], []
    for j in range(nq + 1):
        src, rows = (kvp_ref, slice(0, w)) if j == 0 else (kvc_ref, slice((j - 1) * w, j * w))
        k_tiles, vt_tiles = [], []
        for t in range(tiles):
            kx = src[rows, t * LANES:(t + 1) * LANES]
            vt = src[rows, SWA_KV + t * LANES:SWA_KV + (t + 1) * LANES].T
            k_tiles.append((kx.astype(BF16), pltpu.roll(kx, SWA_HD, axis=1).astype(BF16)))
            vt_tiles.append((vt.astype(BF16),
                             jnp.concatenate([vt[SWA_HD:], vt[:SWA_HD]], axis=0).astype(BF16)))
        k_blk.append(k_tiles)
        vt_blk.append(vt_tiles)
    items = [(qb, g, p) for qb in range(nq) for g in range(SWA_KV_HEADS) for p in range(2)]
    log2e = math.log2(math.e)
    qm, kz, vzt, sink = {}, {}, {}, {}
    for qb, g, p in items:
        keep = lo_lane if p == 0 else jnp.logical_not(lo_lane)
        q_tiles = [q_ref[qb * w:(qb + 1) * w, (2 * g + r) * LANES:(2 * g + r + 1) * LANES] for r in range(pairs)]
        qm[qb, g, p] = jnp.concatenate([jnp.where(keep, x * SWA_Q_SCALE, 0.0) for x in q_tiles],
                                       axis=0).astype(BF16)
        variant = 0 if p == g % 2 else 1
        kz[qb, g, p] = jnp.concatenate([k_blk[qb + d][g // 2][variant] for d in range(2)], axis=0)
        vzt[qb, g, p] = jnp.concatenate([vt_blk[qb + d][g // 2][variant] for d in range(2)], axis=1)
        sink[qb, g, p] = jnp.concatenate([jnp.full((1, w), sinks_ref[SWA_GROUP * g + 2 * r + p] * log2e, F32)
                                          for r in range(pairs)], axis=1)
    st = {b: lax.dot_general(kz[b], qm[b], (((1,), (1,)), ((), ())), preferred_element_type=F32) for b in items}
    prev = {b: jnp.where(n > 0, st[b][:w], -jnp.inf) if b[0] == 0 else st[b][:w] for b in items}
    u = {b: jnp.where(from_prev, prev[b], st[b][w:]) for b in items}
    m = {b: jnp.maximum(jnp.max(u[b], axis=0, keepdims=True), sink[b]) for b in items}
    eu = {b: jnp.exp2(u[b] - m[b]) for b in items}
    den = {b: jnp.sum(eu[b], axis=0, keepdims=True) + jnp.exp2(sink[b] - m[b]) for b in items}
    et = {b: jnp.concatenate([jnp.where(from_prev, eu[b], 0.0), jnp.where(from_prev, 0.0, eu[b])],
                             axis=0).astype(BF16) for b in items}
    ot = {b: jnp.dot(vzt[b], et[b], preferred_element_type=F32) / den[b] for b in items}
    for qb in range(nq):
        for g in range(SWA_KV_HEADS):
            for r in range(pairs):
                cols = slice(r * w, (r + 1) * w)
                tile_t = jnp.where(lo_row, ot[qb, g, 0][:, cols], ot[qb, g, 1][:, cols])
                o_ref[qb * w:(qb + 1) * w, (2 * g + r) * LANES:(2 * g + r + 1) * LANES] = (
                    tile_t.T.astype(o_ref.dtype))


def _swa_prompt(sq, skv, sinks, nq):
    b_, l_, _ = sq.shape
    rows = nq * WINDOW
    return pl.pallas_call(
        functools.partial(_swa_prompt_body, nq),
        grid=(b_, l_ // rows),
        in_specs=[pl.BlockSpec(memory_space=pltpu.SMEM),
                  pl.BlockSpec((None, rows, SWA_Q), lambda b, n: (b, n, 0)),
                  pl.BlockSpec((None, WINDOW, 2 * SWA_KV), lambda b, n: (b, jnp.maximum(n * nq - 1, 0), 0)),
                  pl.BlockSpec((None, rows, 2 * SWA_KV), lambda b, n: (b, n, 0))],
        out_specs=pl.BlockSpec((None, rows, SWA_Q), lambda b, n: (b, n, 0)),
        out_shape=jax.ShapeDtypeStruct((b_, l_, SWA_Q), BF16),
        compiler_params=_params(("arbitrary", "arbitrary")),
        name="swa_prompt",
    )(sinks, sq, skv, skv)


def _swa_step_body(bb, q_ref, kvn_ref, ck_ref, cv_ref, sink_ref, o_ref, nk_ref, nv_ref):
    w = WINDOW
    first = pl.program_id(0) * bb
    row = lax.broadcasted_iota(jnp.int32, (SWA_Q_HEADS, SWA_KV), 0)
    lane = lax.broadcasted_iota(jnp.int32, (SWA_Q_HEADS, SWA_KV), 1)
    own = (lane // SWA_HD) == (row // SWA_GROUP)
    newest = lax.broadcasted_iota(jnp.int32, (SWA_KV, w), 1) == w - 1
    sink = sink_ref[...]
    scale = SWA_HD ** -0.5
    seqs = range(bb)
    kn_all = kvn_ref[0:SWA_KV, :]
    vn_all = kvn_ref[SWA_KV:2 * SWA_KV, :]
    nk = [jnp.where(newest, pltpu.roll(kn_all, w - 1 - (first + i), axis=1), pltpu.roll(ck_ref[i], w - 1, axis=1))
          for i in seqs]
    nv = [jnp.where(newest, pltpu.roll(vn_all, w - 1 - (first + i), axis=1), pltpu.roll(cv_ref[i], w - 1, axis=1))
          for i in seqs]
    q_bd = [jnp.where(own, jnp.concatenate([q_ref[i]] * SWA_KV_HEADS, axis=1), 0.0) for i in seqs]
    s = [_mm(q_bd[i], nk[i]) * scale for i in seqs]
    m = [jnp.maximum(jnp.max(s[i], axis=1, keepdims=True), sink) for i in seqs]
    e = [jnp.exp(s[i] - m[i]) for i in seqs]
    den = [jnp.sum(e[i], axis=1, keepdims=True) + jnp.exp(sink - m[i]) for i in seqs]
    pv = [jnp.where(own, _mm_nt(e[i] / den[i], nv[i]), 0.0) for i in seqs]
    for i in seqs:
        o = pv[i][:, 0:SWA_HD]
        for g in range(1, SWA_KV_HEADS):
            o = o + pv[i][:, g * SWA_HD:(g + 1) * SWA_HD]
        o_ref[i] = o
        nk_ref[i] = nk[i]
        nv_ref[i] = nv[i]


def _swa_step(q3, kvn_t, ck_t, cv_t, sink_col, bb):
    n_ = q3.shape[0]
    assert n_ <= WINDOW
    cache = pl.BlockSpec((bb, SWA_KV, WINDOW), lambda i: (i, 0, 0))
    return pl.pallas_call(
        functools.partial(_swa_step_body, bb),
        grid=(n_ // bb,),
        in_specs=[pl.BlockSpec((bb, SWA_Q_HEADS, SWA_HD), lambda i: (i, 0, 0)),
                  _const_spec(kvn_t.shape),
                  cache, cache,
                  _const_spec(sink_col.shape)],
        out_specs=(pl.BlockSpec((bb, SWA_Q_HEADS, SWA_HD), lambda i: (i, 0, 0)), cache, cache),
        out_shape=(jax.ShapeDtypeStruct(q3.shape, F32),
                   jax.ShapeDtypeStruct(ck_t.shape, F32),
                   jax.ShapeDtypeStruct(cv_t.shape, F32)),
        compiler_params=_params(("arbitrary",)),
        name="swa_step",
    )(q3, kvn_t, ck_t, cv_t, sink_col)


def _dense_body(stateful, tm, og_ref, ob_ref, gab_ref, x_ref, gt1_ref, sh2_ref, sc2_ref, gt2_ref,
                n2w_ref, fnw_ref, wa_ref, wb_ref, wo_ref, wg_ref, wu_ref, cw_ref, cb_ref, wd_ref, *rest):
    if stateful:
        st_ref, gate_ref, onw_ref, y_ref, gout_ref, act_ref = rest
        onw = onw_ref[...]
        og = jnp.concatenate(
            [_gated_out_norm(og_ref[h], _silu(gate_ref[:, h * GDN_DV:(h + 1) * GDN_DV]), onw).astype(BF16)
             for h in range(GDN_HEADS)], axis=1)
    else:
        y_ref, gout_ref, act_ref, gbuf_ref, carry_ref = rest
        og = og_ref[...]

        @pl.when(pl.program_id(1) == 0)
        def _():
            carry_ref[...] = jnp.zeros_like(carry_ref)

    y_a = jnp.dot(og, wa_ref[...], preferred_element_type=F32)
    y_b = jnp.dot(ob_ref[...], wb_ref[...], preferred_element_type=F32)
    merged = (jax.nn.sigmoid(gab_ref[:, 0:D_MODEL]) * y_a
              + jax.nn.sigmoid(gab_ref[:, D_MODEL:2 * D_MODEL]) * y_b)
    x1 = x_ref[...] + gt1_ref[...] * _mm(merged, wo_ref[...])
    h2 = (_rms(x1, n2w_ref[...]) * (1.0 + sc2_ref[...]) + sh2_ref[...]).astype(BF16)

    for c in range(D_FF // FFN_COLS):
        cols = slice(c * FFN_COLS, (c + 1) * FFN_COLS)
        gate = jnp.dot(h2, wg_ref[:, cols], preferred_element_type=F32)
        up = jnp.dot(h2, wu_ref[:, cols], preferred_element_type=F32)
        if stateful:
            g2 = st_ref[0, :, cols]
            g1 = st_ref[1, :, cols]
            gout_ref[:, cols] = gate
        else:
            gbuf_ref[0:SUBLANES, :] = carry_ref[:, cols]
            gbuf_ref[SUBLANES:SUBLANES + tm, :] = gate
            g2 = gbuf_ref[SUBLANES - 2:SUBLANES - 2 + tm, :]
            g1 = gbuf_ref[SUBLANES - 1:SUBLANES - 1 + tm, :]
            carry_ref[:, cols] = gbuf_ref[tm:tm + SUBLANES, :]
        gc = (cw_ref[0:1, cols] * g2 + cw_ref[1:2, cols] * g1 + cw_ref[2:3, cols] * gate) + cb_ref[:, cols]
        act_ref[:, cols] = (_silu(gc) * up).astype(BF16)
    if not stateful:
        gout_ref[...] = carry_ref[...]

    x2 = x1 + gt2_ref[...] * jnp.dot(act_ref[...], wd_ref[...], preferred_element_type=F32)
    y_ref[...] = _rms(x2, fnw_ref[...])


def _dense(og, ob, gab, x, mods, vecs, ws, st, tm, step=None):
    b_, l_, _ = x.shape
    r_ = mods[0].shape[1]
    rt = 1 if r_ == 1 else tm
    nt = l_ // tm
    mod_map = (lambda b, t: (b, 0, 0)) if r_ == 1 else (lambda b, t: (b, t, 0))
    row_map = lambda b, t: (b, t, 0)
    stateful = st is not None
    og_spec = (pl.BlockSpec((GDN_HEADS, tm, LANES), lambda b, t: (0, t, 0)) if stateful
               else pl.BlockSpec((None, tm, D_MODEL), row_map))
    in_specs = ([og_spec,
                 pl.BlockSpec((None, tm, D_MODEL), row_map),
                 pl.BlockSpec((None, tm, 2 * D_MODEL), row_map),
                 pl.BlockSpec((None, tm, D_MODEL), row_map)]
                + [pl.BlockSpec((None, rt, D_MODEL), mod_map)] * 4
                + [_const_spec(a.shape) for a in vecs[:2]]
                + [_const_spec(ws[0].shape), _const_spec(ws[1].shape), _const_spec(ws[2].shape),
                   _const_spec(ws[3].shape), _const_spec(ws[4].shape),
                   _const_spec(vecs[2].shape), _const_spec(vecs[3].shape), _const_spec(ws[5].shape)])
    args = [og, ob, gab, x, *mods, vecs[0], vecs[1], ws[0], ws[1], ws[2], ws[3], ws[4], vecs[2], vecs[3], ws[5]]
    scratch = [pltpu.VMEM((tm, D_FF), BF16)]
    out_specs = [pl.BlockSpec((None, tm, D_MODEL), row_map)]
    out_shape = [jax.ShapeDtypeStruct((b_, l_, D_MODEL), F32)]
    if stateful:
        gate, onw = step
        in_specs += [pl.BlockSpec((FFN_CONV - 1, None, tm, D_FF), lambda b, t: (0, b, t, 0)),
                     pl.BlockSpec((None, tm, GDN_V), row_map), _const_spec(onw.shape)]
        args += [st, gate, onw]
        out_shape.append(jax.ShapeDtypeStruct((b_, l_, D_FF), F32))
        out_specs.append(pl.BlockSpec((None, tm, D_FF), row_map))
    else:
        scratch += [pltpu.VMEM((tm + SUBLANES, FFN_COLS), F32), pltpu.VMEM((SUBLANES, D_FF), F32)]
        out_shape.append(jax.ShapeDtypeStruct((b_, SUBLANES, D_FF), F32))
        out_specs.append(pl.BlockSpec((None, SUBLANES, D_FF), lambda b, t: (b, 0, 0)))
    return pl.pallas_call(
        functools.partial(_dense_body, stateful, tm),
        grid=(b_, nt),
        in_specs=in_specs,
        out_specs=tuple(out_specs),
        out_shape=tuple(out_shape),
        scratch_shapes=scratch,
        compiler_params=_params(("arbitrary", "arbitrary")),
        name="dense_step" if stateful else "dense_prompt",
    )(*args)


def _lane_row(values, offset):
    return jnp.zeros((1, LANES), F32).at[0, offset:offset + values.shape[0]].set(values)


def kernel(x_prompt, x_sample, c_prompt, c_sample, state_gdn_S, state_gdn_conv, cache_swa_k, cache_swa_v,
           state_ffn_conv, w_mod, b_mod, norm1_w, norm2_w, w_in, gdn_conv_w, gdn_a_log, gdn_dt_bias,
           gdn_onorm_w, w_gdn_out, swa_sinks, w_swa_out, w_o, w_ffn_gate, w_ffn_up, ffn_conv_w, ffn_conv_b,
           w_ffn_down, final_norm_w):
    assert w_mod.shape[0] == 1, "single-layer trunk"
    nb, seq, _ = x_prompt.shape
    ns = x_sample.shape[0]
    assert x_sample.shape[1] == 1

    in_ws = _in_weight(jnp.transpose(w_in[0]))
    dense_ws = (w_gdn_out[0].astype(BF16), w_swa_out[0].astype(BF16), w_o[0].astype(BF16),
                w_ffn_gate[0].astype(BF16), w_ffn_up[0].astype(BF16), w_ffn_down[0].astype(BF16))
    dense_vecs = (norm2_w, final_norm_w[None, :], ffn_conv_w[0], ffn_conv_b)
    cw = jnp.transpose(gdn_conv_w[0].reshape(GDN_CONV, GDN_SECTIONS, LANES), (1, 0, 2))
    alog_row = _lane_row(gdn_a_log[0], GDN_HEADS)
    dtb_row = _lane_row(gdn_dt_bias[0], GDN_HEADS)

    mod = _modulation(jnp.concatenate([c_prompt, c_sample], axis=0), w_mod[0], b_mod)
    mod_p = [mod[:nb, i * D_MODEL:(i + 1) * D_MODEL][:, None, :] for i in range(6)]
    mod_s = [mod[nb:, i * D_MODEL:(i + 1) * D_MODEL][None, :, :] for i in range(6)]

    xs = x_sample.reshape(1, ns, D_MODEL)
    qkvs, gates, bas, sqs, skvs, gabs = _inproj(xs, mod_s[0], mod_s[1], norm1_w, in_ws, None, tm=ns)
    st_gdn = jnp.transpose(state_gdn_conv[0], (1, 0, 2))
    step_feats = jnp.transpose(_gdn_step_features(qkvs[0], st_gdn, bas[0], cw, alog_row, dtb_row, bb=16), (1, 0, 2))

    qkvf, gact, ba, sq, skv, gab, qkv_tail = _inproj(x_prompt, mod_p[0], mod_p[1], norm1_w, in_ws, cw, tm=256)
    og, gdn_s_p, gdn_s_s, step_read = _gdn_prompt(qkvf, gact, ba, alog_row, dtb_row, gdn_onorm_w, _level_masks(),
                                                  step_feats, state_gdn_S[0], lt=4 * CHUNK)
    ob = _swa_prompt(sq, skv, swa_sinks[0], nq=4)
    y_p, gate_tail = _dense(og, ob, gab, x_prompt, (mod_p[2], mod_p[3], mod_p[4], mod_p[5]),
                            dense_vecs, dense_ws, None, tm=512)
    gdn_conv_p = jnp.transpose(qkv_tail[:, :, SUBLANES - (GDN_CONV - 1):, :], (0, 2, 1, 3)).reshape(
        nb, GDN_CONV - 1, GDN_CONV_CH)
    k_p = skv[:, seq - WINDOW:, :SWA_KV].reshape(nb, WINDOW, SWA_KV_HEADS, SWA_HD)
    v_p = skv[:, seq - WINDOW:, SWA_KV:].reshape(nb, WINDOW, SWA_KV_HEADS, SWA_HD)
    ffn_conv_p = gate_tail[:, SUBLANES - (FFN_CONV - 1):, :]

    to_channel_major = lambda c: jnp.transpose(c, (0, 2, 3, 1)).reshape(ns, SWA_KV, WINDOW)
    from_channel_major = lambda c: jnp.transpose(c.reshape(ns, SWA_KV_HEADS, SWA_HD, WINDOW), (0, 3, 1, 2))
    kvn_t = jnp.pad(jnp.transpose(skvs[0]), ((0, 0), (0, WINDOW - ns)))
    o3, k_s, v_s = _swa_step(sqs[0].reshape(ns, SWA_Q_HEADS, SWA_HD), kvn_t,
                             to_channel_major(cache_swa_k[0]), to_channel_major(cache_swa_v[0]),
                             swa_sinks[0][:, None], bb=16)
    ob_s = o3.reshape(1, ns, SWA_Q).astype(BF16)
    st_ffn = jnp.transpose(state_ffn_conv[0], (1, 0, 2))[:, None]
    y_s, gate_new = _dense(jnp.transpose(step_read, (1, 0, 2)), ob_s, gabs, xs,
                           (mod_s[2], mod_s[3], mod_s[4], mod_s[5]), dense_vecs, dense_ws, st_ffn, tm=ns,
                           step=(gates, gdn_onorm_w))
    gdn_conv_s = jnp.concatenate([state_gdn_conv[0][:, 1:], qkvs[0][:, None, :]], axis=1)
    ffn_conv_s = jnp.concatenate([state_ffn_conv[0][:, 1:], gate_new[0][:, None, :]], axis=1)

    return (y_p, y_s.reshape(ns, 1, D_MODEL),
            gdn_s_p[None], gdn_s_s[None],
            gdn_conv_p[None], gdn_conv_s[None],
            k_p[None], from_channel_major(k_s)[None],
            v_p[None], from_channel_major(v_s)[None],
            ffn_conv_p[None], ffn_conv_s[None])
```

```python
import functools
import math

import numpy as np
import jax
import jax.numpy as jnp
from jax import lax
from jax.experimental import pallas as pl
from jax.experimental.pallas import tpu as pltpu

F32 = jnp.float32
BF16 = jnp.bfloat16

D_MODEL = 1024
GDN_HEADS = 8
GDN_DK = 128
GDN_DV = 128
GDN_QK = GDN_HEADS * GDN_DK
GDN_V = GDN_HEADS * GDN_DV
GDN_CONV = 4
GDN_CONV_CH = 2 * GDN_QK + GDN_V
GDN_SECTIONS = GDN_CONV_CH // 128
SWA_Q_HEADS = 16
SWA_KV_HEADS = 4
SWA_GROUP = SWA_Q_HEADS // SWA_KV_HEADS
SWA_HD = 64
SWA_Q = SWA_Q_HEADS * SWA_HD
SWA_KV = SWA_KV_HEADS * SWA_HD
WINDOW = 128
D_FF = 2816
FFN_CONV = 3
EPS = 1e-6

LANES = 128
SUBLANES = 8
VMEM_LIMIT = 56 * 1024 * 1024

COL_QKV = 0
COL_GATE = COL_QKV + GDN_CONV_CH
COL_SQ = COL_GATE + GDN_V
COL_SKV = COL_SQ + SWA_Q
COL_GAB = COL_SKV + 2 * SWA_KV
COL_BA = COL_GAB + 2 * D_MODEL
IN_COLS = COL_BA + LANES

SWA_Q_SCALE = SWA_HD ** -0.5 * math.log2(math.e)

CONV_ROWS = 64
CHUNK = 128
GDN_GROUP = 16
FFN_COLS = 256


def _mm(a, b):
    return jnp.dot(a.astype(BF16), b.astype(BF16), preferred_element_type=F32)


def _mm_nt(a, b):
    return lax.dot_general(a.astype(BF16), b.astype(BF16), (((1,), (1,)), ((), ())),
                           preferred_element_type=F32)


def _silu(x):
    return x * jax.nn.sigmoid(x)


def _softplus(x):
    return jnp.maximum(x, 0.0) + jnp.log1p(jnp.exp(-jnp.abs(x)))


def _rms(x, w):
    return x * lax.rsqrt(jnp.mean(x * x, axis=-1, keepdims=True) + EPS) * w


def _const_spec(shape):
    n = len(shape)
    return pl.BlockSpec(shape, lambda *_: (0,) * n, pipeline_mode=pl.Buffered(1))


def _params(sem):
    return pltpu.CompilerParams(dimension_semantics=sem, vmem_limit_bytes=VMEM_LIMIT)


def _mod_body(c_ref, w_ref, b_ref, o_ref):
    o_ref[...] = _mm(_silu(c_ref[...]), w_ref[...]) + b_ref[...]


def _modulation(c_all, w_mod, b_mod):
    rows = c_all.shape[0]
    n_out = w_mod.shape[1]
    tn = D_MODEL
    return pl.pallas_call(
        _mod_body,
        grid=(n_out // tn,),
        in_specs=[pl.BlockSpec((rows, D_MODEL), lambda j: (0, 0)),
                  pl.BlockSpec((D_MODEL, tn), lambda j: (0, j)),
                  pl.BlockSpec((1, tn), lambda j: (0, j))],
        out_specs=pl.BlockSpec((rows, tn), lambda j: (0, j)),
        out_shape=jax.ShapeDtypeStruct((rows, n_out), F32),
        compiler_params=_params(("arbitrary",)),
        name="modulation",
    )(c_all, w_mod, b_mod)


IN_WEIGHT_COLS = 512
IN_WEIGHT_PAD = 256


def _in_weight_body(n_main, wt_ref, ba_ref, o_ref):
    j = pl.program_id(0)

    @pl.when(j < n_main)
    def _():
        o_ref[...] = wt_ref[...].T.astype(BF16)

    @pl.when(j == n_main)
    def _():
        n_ba = ba_ref.shape[0]
        ba = jnp.concatenate([ba_ref[...].T, jnp.zeros((D_MODEL, IN_WEIGHT_COLS - n_ba), F32)], axis=1)
        o_ref[...] = ba.astype(BF16)


def _in_weight(w_t):
    n_ba = 2 * GDN_HEADS
    split = GDN_CONV_CH + GDN_V
    tc = IN_WEIGHT_COLS
    n_main = COL_BA // tc

    def src_row(j):
        jj = jnp.minimum(j, n_main - 1)
        return pl.multiple_of(jnp.where(jj * tc < split, jj * tc, jj * tc + n_ba), n_ba)

    return pl.pallas_call(
        functools.partial(_in_weight_body, n_main),
        grid=(n_main + 1,),
        in_specs=[pl.BlockSpec((pl.Element(tc), pl.Element(D_MODEL)), lambda j: (src_row(j), 0)),
                  pl.BlockSpec((pl.Element(n_ba), pl.Element(D_MODEL)), lambda j: (split, 0))],
        out_specs=pl.BlockSpec((D_MODEL, tc), lambda j: (0, j)),
        out_shape=jax.ShapeDtypeStruct((D_MODEL, COL_BA + IN_WEIGHT_PAD), BF16),
        compiler_params=_params(("arbitrary",)),
        name="in_weight",
    )(w_t, w_t)


def _l2norm(x):
    return x * lax.rsqrt(jnp.sum(x * x, axis=-1, keepdims=True) + EPS)


def _inproj_body(seq_rows, tm, x_ref, sh_ref, sc_ref, nw_ref, w_ref, *rest):
    if seq_rows:
        cw_ref, qkv_ref, gg_ref, ba_ref, sq_ref, skv_ref, gab_ref, tail_ref, xe_ref = rest

        @pl.when(pl.program_id(1) == 0)
        def _():
            xe_ref[:, 0:SUBLANES, :] = jnp.zeros((GDN_SECTIONS, SUBLANES, LANES), F32)
    else:
        qkv_ref, gg_ref, ba_ref, sq_ref, skv_ref, gab_ref = rest

    h = _rms(x_ref[...], nw_ref[...]) * (1.0 + sc_ref[...]) + sh_ref[...]
    hb = h.astype(BF16)

    def proj(lo, width):
        return jnp.dot(hb, w_ref[:, lo:lo + width], preferred_element_type=F32)

    step = 512
    per = step // LANES
    for c in range(GDN_CONV_CH // step):
        z = proj(COL_QKV + c * step, step)
        for k in range(per):
            s = c * per + k
            zs = z[:, k * LANES:(k + 1) * LANES]
            if not seq_rows:
                qkv_ref[:, s * LANES:(s + 1) * LANES] = zs
                continue
            xe_ref[s, SUBLANES:SUBLANES + tm, :] = zs
            w = cw_ref[s]
            for r0 in range(0, tm, CONV_ROWS):
                y = w[0:1] * xe_ref[s, r0 + SUBLANES - 3:r0 + SUBLANES - 3 + CONV_ROWS, :]
                for tap in range(1, GDN_CONV):
                    lo = r0 + SUBLANES - 3 + tap
                    y = y + w[tap:tap + 1] * xe_ref[s, lo:lo + CONV_ROWS, :]
                f = _silu(y)
                if s < GDN_HEADS:
                    f = _l2norm(f) * (GDN_DK ** -0.5)
                elif s < 2 * GDN_HEADS:
                    f = _l2norm(f)
                qkv_ref[s, r0:r0 + CONV_ROWS, :] = f
            xe_ref[s, 0:SUBLANES, :] = xe_ref[s, tm:tm + SUBLANES, :]
    if seq_rows:
        tail_ref[...] = xe_ref[:, 0:SUBLANES, :]
    for c in range(GDN_V // step):
        z = proj(COL_GATE + c * step, step)
        for k in range(per):
            zs = z[:, k * LANES:(k + 1) * LANES]
            if seq_rows:
                gg_ref[c * per + k] = _silu(zs)
            else:
                gg_ref[:, (c * per + k) * LANES:(c * per + k + 1) * LANES] = zs
    ba_ref[...] = proj(COL_BA, LANES)
    for c in range(SWA_Q // step):
        sq_ref[:, c * step:(c + 1) * step] = proj(COL_SQ + c * step, step)
    skv_ref[...] = proj(COL_SKV, 2 * SWA_KV)
    for c in range(2 * D_MODEL // step):
        gab_ref[:, c * step:(c + 1) * step] = proj(COL_GAB + c * step, step)


def _inproj(x, sh, sc, nw, w_all, cw, tm):
    b_, l_, _ = x.shape
    r_ = sh.shape[1]
    rt = 1 if r_ == 1 else tm
    mod_map = (lambda b, t: (b, 0, 0)) if r_ == 1 else (lambda b, t: (b, t, 0))
    row_map = lambda b, t: (b, t, 0)
    head_map = lambda b, t: (b, 0, t, 0)
    seq_rows = cw is not None
    if seq_rows:
        gdn_shapes = (jax.ShapeDtypeStruct((b_, GDN_SECTIONS, l_, LANES), F32),
                      jax.ShapeDtypeStruct((b_, GDN_HEADS, l_, LANES), F32))
        gdn_specs = (pl.BlockSpec((None, GDN_SECTIONS, tm, LANES), head_map),
                     pl.BlockSpec((None, GDN_HEADS, tm, LANES), head_map))
    else:
        gdn_shapes = (jax.ShapeDtypeStruct((b_, l_, GDN_CONV_CH), F32),
                      jax.ShapeDtypeStruct((b_, l_, GDN_V), F32))
        gdn_specs = (pl.BlockSpec((None, tm, GDN_CONV_CH), row_map),
                     pl.BlockSpec((None, tm, GDN_V), row_map))
    out_shape = gdn_shapes + (
        jax.ShapeDtypeStruct((b_, l_, LANES), F32),
        jax.ShapeDtypeStruct((b_, l_, SWA_Q), F32),
        jax.ShapeDtypeStruct((b_, l_, 2 * SWA_KV), F32),
        jax.ShapeDtypeStruct((b_, l_, 2 * D_MODEL), F32),
    )
    out_specs = gdn_specs + (
        pl.BlockSpec((None, tm, LANES), row_map),
        pl.BlockSpec((None, tm, SWA_Q), row_map),
        pl.BlockSpec((None, tm, 2 * SWA_KV), row_map),
        pl.BlockSpec((None, tm, 2 * D_MODEL), row_map),
    )
    in_specs = [
        pl.BlockSpec((None, tm, D_MODEL), row_map),
        pl.BlockSpec((None, rt, D_MODEL), mod_map),
        pl.BlockSpec((None, rt, D_MODEL), mod_map),
        _const_spec(nw.shape),
        _const_spec(w_all.shape),
    ]
    args = [x, sh, sc, nw, w_all]
    scratch = []
    if seq_rows:
        in_specs.append(_const_spec(cw.shape))
        args.append(cw)
        out_shape += (jax.ShapeDtypeStruct((b_, GDN_SECTIONS, SUBLANES, LANES), F32),)
        out_specs += (pl.BlockSpec((None, GDN_SECTIONS, SUBLANES, LANES), lambda b, t: (b, 0, 0, 0)),)
        scratch.append(pltpu.VMEM((GDN_SECTIONS, tm + SUBLANES, LANES), F32))
    return pl.pallas_call(
        functools.partial(_inproj_body, seq_rows, tm),
        grid=(b_, l_ // tm),
        in_specs=in_specs,
        out_specs=out_specs,
        out_shape=out_shape,
        scratch_shapes=scratch,
        compiler_params=_params(("arbitrary", "arbitrary")),
        name="inproj_seq" if seq_rows else "inproj_rows",
    )(*args)


def _delta_gates(ba, alog_row, dtb_row):
    beta_all = jax.nn.sigmoid(ba)
    g_all = -jnp.exp(alog_row) * _softplus(ba + dtb_row)
    return beta_all, g_all


def _lane_column(x, lane_idx, lane):
    return jnp.sum(jnp.where(lane_idx == lane, x, 0.0), axis=1, keepdims=True)


def _level_masks():
    r = np.arange(CHUNK)[:, None]
    c = np.arange(CHUNK)[None, :]
    masks = [(r == c + 1) & (r % 2 == 1)]
    half = 2
    while half < CHUNK:
        full = 2 * half
        masks.append((r // full == c // full) & (r % full >= half) & (c % full < half))
        half = full
    return jnp.asarray(np.stack(masks), dtype=BF16)


def _unit_lower_inverses(ms, masks_ref, eye, between_levels=()):
    ts = [eye - m * masks_ref[0] for m in ms]
    pending = list(between_levels)
    for lvl in range(1, masks_ref.shape[0]):
        off = masks_ref[lvl]
        xs = [jnp.dot(m * off, t, preferred_element_type=F32).astype(BF16) for m, t in zip(ms, ts)]
        ys = [jnp.dot(t, x, preferred_element_type=F32).astype(BF16) for t, x in zip(ts, xs)]
        ts = [t - y for t, y in zip(ts, ys)]
        if pending:
            pending.pop(0)()
    for piece in pending:
        piece()
    return ts


def _cumsum_rows(g, ltri):
    hi = g.astype(BF16)
    r1 = g - hi.astype(F32)
    mid = r1.astype(BF16)
    lo = (r1 - mid.astype(F32)).astype(BF16)
    return (jnp.dot(ltri, hi, preferred_element_type=F32) + jnp.dot(ltri, mid, preferred_element_type=F32)
            + jnp.dot(ltri, lo, preferred_element_type=F32))


def _gated_out_norm(o, gate_act, onw):
    on = o * lax.rsqrt(jnp.mean(o * o, axis=-1, keepdims=True) + EPS) * onw
    return on * gate_act


def _run_all(pieces):
    for piece in pieces:
        piece()


def _gdn_prompt_body(lt, q_ref, k_ref, v_ref, ba_ref, alog_ref, dtb_ref, gate_ref, onw_ref, masks_ref,
                     f_ref, s0_ref, sq_ref, kvn_ref, ck_ref, cv_ref, sink_ref,
                     og_ref, s_ref, sn_ref, read_ref, so_ref, nk_ref, nv_ref):
    per_step = s0_ref.shape[0]
    first = (pl.program_id(0) * pl.num_programs(1) + pl.program_id(1)) * per_step
    state_work = [functools.partial(_gdn_state_update, f_ref, s0_ref, sn_ref, read_ref, i, h)
                  for i in range(per_step) for h in range(GDN_HEADS)]
    attn_work = [piece for i in range(per_step)
                 for piece in _swa_step_pieces(first + i, i, sq_ref, kvn_ref, ck_ref, cv_ref, sink_ref,
                                               so_ref, nk_ref, nv_ref)]

    @pl.when(pl.program_id(1) == 0)
    def _():
        s_ref[...] = jnp.zeros_like(s_ref)

    beta_all, g_all = _delta_gates(ba_ref[...], alog_ref[...], dtb_ref[...])
    lane_idx = lax.broadcasted_iota(jnp.int32, (CHUNK, LANES), 1)
    row = lax.broadcasted_iota(jnp.int32, (CHUNK, CHUNK), 0)
    col = lax.broadcasted_iota(jnp.int32, (CHUNK, CHUNK), 1)
    tril = row >= col
    strict = row > col
    ltri = jnp.where(tril, 1.0, 0.0).astype(BF16)
    eye = jnp.where(row == col, 1.0, 0.0).astype(BF16)
    onw = onw_ref[...]
    heads = range(GDN_HEADS)
    chunks = range(lt // CHUNK)

    blocks = [(c, j) for c in chunks for j in heads]
    pre = {}
    for c in chunks:
        rows = slice(c * CHUNK, (c + 1) * CHUNK)
        dec = _cumsum_rows(g_all[rows], ltri)
        dec_t = dec.T
        for j in heads:
            q, k, v = q_ref[j, rows, :], k_ref[j, rows, :], v_ref[j, rows, :]
            beta_col = _lane_column(beta_all[rows], lane_idx, j)
            dec_col = _lane_column(dec, lane_idx, GDN_HEADS + j)
            dec_row = dec_t[GDN_HEADS + j:GDN_HEADS + j + 1, :]
            dec_last = dec_row[:, CHUNK - 1:CHUNK]
            gam = jnp.exp(jnp.minimum(dec_col - dec_row, 0.0))
            e_col = jnp.exp(dec_col)
            kb = k * beta_col
            pre[c, j] = dict(q=q, k=k, gam=gam, kb=kb, qe=q * e_col, e_last=jnp.exp(dec_last),
                             kd=k * jnp.exp(dec_last - dec_col),
                             rhs=jnp.concatenate([v * beta_col, kb * e_col], axis=1).astype(BF16))
    a_intra, uw = {}, {}

    def recurrence(c):
        rows = slice(c * CHUNK, (c + 1) * CHUNK)
        mid = {}

        def read_out():
            mid["s"] = [s_ref[j] for j in heads]
            mid["ws_qs"] = [_mm(jnp.concatenate([uw[c, j][:, GDN_DV:], pre[c, j]["qe"]], axis=0), mid["s"][j])
                            for j in heads]

        def update():
            s_prev, ws_qs = mid["s"], mid["ws_qs"]
            v_new = [uw[c, j][:, :GDN_DV] - ws_qs[j][:CHUNK] for j in heads]
            o = [ws_qs[j][CHUNK:] + _mm(a_intra[c, j], v_new[j]) for j in heads]
            s_new = [s_prev[j] * pre[c, j]["e_last"] + _mm(pre[c, j]["kd"].T, v_new[j]) for j in heads]
            for j in heads:
                s_ref[j] = s_new[j]
                og = _gated_out_norm(o[j], gate_ref[j, rows, :], onw)
                og_ref[rows, j * GDN_DV:(j + 1) * GDN_DV] = og.astype(og_ref.dtype)

        return [read_out, update]

    n_slots = -(-len(blocks) // GDN_GROUP) * (masks_ref.shape[0] - 1)
    state_per_slot = -(-len(state_work) // n_slots)
    attn_per_slot = -(-len(attn_work) // n_slots)
    carried = []
    for g0 in range(0, len(blocks), GDN_GROUP):
        grp = blocks[g0:g0 + GDN_GROUP]
        grams = [_mm_nt(jnp.concatenate([pre[b]["kb"], pre[b]["q"]], axis=0), pre[b]["k"]) for b in grp]
        ms = [jnp.where(strict, g[:CHUNK] * pre[b]["gam"], 0.0).astype(BF16) for g, b in zip(grams, grp)]
        a_intra.update({b: jnp.where(tril, g[CHUNK:] * pre[b]["gam"], 0.0) for g, b in zip(grams, grp)})
        slots = []
        for _ in range(masks_ref.shape[0] - 1):
            work = carried[:1] + state_work[:state_per_slot] + attn_work[:attn_per_slot]
            carried = carried[1:]
            state_work, attn_work = state_work[state_per_slot:], attn_work[attn_per_slot:]
            slots.append(functools.partial(_run_all, work))
        t_inv = _unit_lower_inverses(ms, masks_ref, eye, slots)
        _run_all(carried)
        uw.update({b: jnp.dot(t, pre[b]["rhs"], preferred_element_type=F32) for t, b in zip(t_inv, grp)})
        carried = [piece for c in sorted({c for c, _ in grp}) for piece in recurrence(c)]
    _run_all(carried + state_work + attn_work)


def _gdn_prompt(qkvf, gact, ba, alog_row, dtb_row, onw, masks, step_feats, step_state, step_attn, lt):
    b_, _, l_, _ = qkvf.shape
    nt = l_ // lt
    n_ = step_state.shape[0]
    per_step = n_ // (b_ * nt)
    assert per_step * b_ * nt == n_ and n_ <= WINDOW
    q3, kvn_t, ck_t, cv_t, sink_col = step_attn
    sec = lambda s: pl.BlockSpec((None, GDN_HEADS, lt, LANES), lambda b, t, s=s: (b, s, t, 0))
    seq_map = lambda b, t: (b * nt + t, 0, 0)
    seq_block = lambda *dims: pl.BlockSpec((per_step,) + dims, lambda b, t: (b * nt + t,) + (0,) * len(dims))
    state_spec = seq_block(GDN_HEADS, GDN_DK, GDN_DV)
    cache_spec = seq_block(SWA_KV, WINDOW)
    return pl.pallas_call(
        functools.partial(_gdn_prompt_body, lt),
        grid=(b_, nt),
        in_specs=[sec(0), sec(1), sec(2),
                  pl.BlockSpec((None, lt, LANES), lambda b, t: (b, t, 0)),
                  _const_spec(alog_row.shape), _const_spec(dtb_row.shape),
                  pl.BlockSpec((None, GDN_HEADS, lt, LANES), lambda b, t: (b, 0, t, 0)),
                  _const_spec(onw.shape), _const_spec(masks.shape),
                  seq_block(STEP_FEATURES * GDN_HEADS, LANES), state_spec,
                  seq_block(SWA_Q_HEADS, SWA_HD), _const_spec(kvn_t.shape), cache_spec, cache_spec,
                  _const_spec(sink_col.shape)],
        out_specs=(pl.BlockSpec((None, lt, GDN_V), lambda b, t: (b, t, 0)),
                   pl.BlockSpec((None, GDN_HEADS, GDN_DK, GDN_DV), lambda b, t: (b, 0, 0, 0)),
                   state_spec, seq_block(GDN_HEADS, LANES),
                   seq_block(SWA_Q_HEADS, SWA_HD), cache_spec, cache_spec),
        out_shape=(jax.ShapeDtypeStruct((b_, l_, GDN_V), BF16),
                   jax.ShapeDtypeStruct((b_, GDN_HEADS, GDN_DK, GDN_DV), F32),
                   jax.ShapeDtypeStruct(step_state.shape, F32),
                   jax.ShapeDtypeStruct((n_, GDN_HEADS, LANES), F32),
                   jax.ShapeDtypeStruct(q3.shape, F32),
                   jax.ShapeDtypeStruct(ck_t.shape, F32),
                   jax.ShapeDtypeStruct(cv_t.shape, F32)),
        compiler_params=_params(("arbitrary", "arbitrary")),
        name="gdn_prompt",
    )(qkvf, qkvf, qkvf, ba, alog_row, dtb_row, gact, onw, masks, step_feats, step_state,
      q3, kvn_t, ck_t, cv_t, sink_col)


def _gdn_step_body(bb, x_ref, st_ref, cw_ref, ba_ref, alog_ref, dtb_ref, gate_ref, onw_ref, s0_ref,
                   og_ref, sn_ref, q_s, k_s, v_s, b_s, e_s, o_s):
    beta_all, g_all = _delta_gates(ba_ref[...], alog_ref[...], dtb_ref[...])
    lane_idx = lax.broadcasted_iota(jnp.int32, (bb, LANES), 1)
    for h in range(GDN_HEADS):
        feats = []
        for s in range(3):
            idx = s * GDN_HEADS + h
            cols = slice(idx * LANES, (idx + 1) * LANES)
            w = cw_ref[idx]
            y = w[0:1] * st_ref[0, :, cols]
            for tap in range(1, GDN_CONV - 1):
                y = y + w[tap:tap + 1] * st_ref[tap, :, cols]
            y = y + w[GDN_CONV - 1:GDN_CONV] * x_ref[:, cols]
            feats.append(_silu(y))
        q, k, v = feats
        q_s[h] = q * lax.rsqrt(jnp.sum(q * q, axis=-1, keepdims=True) + EPS) * (GDN_DK ** -0.5)
        k_s[h] = k * lax.rsqrt(jnp.sum(k * k, axis=-1, keepdims=True) + EPS)
        v_s[h] = v
        b_s[h] = jnp.broadcast_to(_lane_column(beta_all, lane_idx, h), (bb, LANES))
        e_s[h] = jnp.broadcast_to(jnp.exp(_lane_column(g_all, lane_idx, h + GDN_HEADS)), (bb, LANES))

    eye = (lax.broadcasted_iota(jnp.int32, (GDN_DK, GDN_DK), 0)
           == lax.broadcasted_iota(jnp.int32, (GDN_DK, GDN_DK), 1))

    def to_col(r):
        return jnp.sum(jnp.where(eye, jnp.broadcast_to(r, (GDN_DK, GDN_DK)), 0.0), axis=1, keepdims=True)

    sub = lax.broadcasted_iota(jnp.int32, (SUBLANES, GDN_DK), 0)

    def seq_body(i, carry):
        for h in range(GDN_HEADS):
            one = pl.ds(i, 1)
            k_row = k_s[h, one, :]
            q_row = q_s[h, one, :]
            s1 = s0_ref[i, h] * e_s[h, one, :]
            kq = jnp.where(sub == 0, k_row, jnp.where(sub == 1, q_row, 0.0))
            kq_s1 = _mm(kq, s1)
            delta = (v_s[h, one, :] - kq_s1[0:1, :]) * b_s[h, one, :]
            sn_ref[i, h] = s1 + to_col(k_row) * delta
            qk = jnp.sum(q_row * k_row, axis=1, keepdims=True)
            o_s[h, one, :] = kq_s1[1:2, :] + qk * delta
        return carry

    lax.fori_loop(0, bb, seq_body, 0)
    onw = onw_ref[...]
    for h in range(GDN_HEADS):
        cols = slice(h * GDN_DV, (h + 1) * GDN_DV)
        og = _gated_out_norm(o_s[h], _silu(gate_ref[:, cols]), onw)
        og_ref[:, cols] = og.astype(og_ref.dtype)


def _gdn_step(qkv, st, gate, ba, cw, alog_row, dtb_row, onw, s0, bb):
    n_ = ba.shape[0]
    vec = pltpu.VMEM((GDN_HEADS, bb, LANES), F32)
    return pl.pallas_call(
        functools.partial(_gdn_step_body, bb),
        grid=(n_ // bb,),
        in_specs=[pl.BlockSpec((bb, GDN_CONV_CH), lambda i: (i, 0)),
                  pl.BlockSpec((GDN_CONV - 1, bb, GDN_CONV_CH), lambda i: (0, i, 0)),
                  _const_spec(cw.shape),
                  pl.BlockSpec((bb, LANES), lambda i: (i, 0)),
                  _const_spec(alog_row.shape), _const_spec(dtb_row.shape),
                  pl.BlockSpec((bb, GDN_V), lambda i: (i, 0)),
                  _const_spec(onw.shape),
                  pl.BlockSpec((bb, GDN_HEADS, GDN_DK, GDN_DV), lambda i: (i, 0, 0, 0))],
        out_specs=(pl.BlockSpec((bb, GDN_V), lambda i: (i, 0)),
                   pl.BlockSpec((bb, GDN_HEADS, GDN_DK, GDN_DV), lambda i: (i, 0, 0, 0))),
        out_shape=(jax.ShapeDtypeStruct((n_, GDN_V), BF16),
                   jax.ShapeDtypeStruct(s0.shape, F32)),
        scratch_shapes=[vec, vec, vec, vec, vec, vec],
        compiler_params=_params(("arbitrary",)),
        name="gdn_step",
    )(qkv, st, cw, ba, alog_row, dtb_row, gate, onw, s0)


STEP_FEATURES = 5


def _gdn_step_features_body(bb, x_ref, st_ref, cw_ref, ba_ref, alog_ref, dtb_ref, f_ref):
    beta_all, g_all = _delta_gates(ba_ref[...], alog_ref[...], dtb_ref[...])
    lane_idx = lax.broadcasted_iota(jnp.int32, (bb, LANES), 1)
    for h in range(GDN_HEADS):
        feats = []
        for s in range(3):
            idx = s * GDN_HEADS + h
            cols = slice(idx * LANES, (idx + 1) * LANES)
            w = cw_ref[idx]
            y = w[0:1] * st_ref[0, :, cols]
            for tap in range(1, GDN_CONV - 1):
                y = y + w[tap:tap + 1] * st_ref[tap, :, cols]
            y = y + w[GDN_CONV - 1:GDN_CONV] * x_ref[:, cols]
            feats.append(_silu(y))
        q, k, v = feats
        f_ref[h] = _l2norm(q) * (GDN_DK ** -0.5)
        f_ref[GDN_HEADS + h] = _l2norm(k)
        f_ref[2 * GDN_HEADS + h] = v
        f_ref[3 * GDN_HEADS + h] = jnp.broadcast_to(_lane_column(beta_all, lane_idx, h), (bb, LANES))
        f_ref[4 * GDN_HEADS + h] = jnp.broadcast_to(
            jnp.exp(_lane_column(g_all, lane_idx, h + GDN_HEADS)), (bb, LANES))


def _gdn_step_features(qkv, st, ba, cw, alog_row, dtb_row, bb):
    n_ = ba.shape[0]
    return pl.pallas_call(
        functools.partial(_gdn_step_features_body, bb),
        grid=(n_ // bb,),
        in_specs=[pl.BlockSpec((bb, GDN_CONV_CH), lambda i: (i, 0)),
                  pl.BlockSpec((GDN_CONV - 1, bb, GDN_CONV_CH), lambda i: (0, i, 0)),
                  _const_spec(cw.shape),
                  pl.BlockSpec((bb, LANES), lambda i: (i, 0)),
                  _const_spec(alog_row.shape), _const_spec(dtb_row.shape)],
        out_specs=pl.BlockSpec((STEP_FEATURES * GDN_HEADS, bb, LANES), lambda i: (0, i, 0)),
        out_shape=jax.ShapeDtypeStruct((STEP_FEATURES * GDN_HEADS, n_, LANES), F32),
        compiler_params=_params(("arbitrary",)),
        name="gdn_step_features",
    )(qkv, st, cw, ba, alog_row, dtb_row)


def _gdn_state_update(f_ref, s0_ref, sn_ref, o_ref, i, h):
    eye = (lax.broadcasted_iota(jnp.int32, (GDN_DK, GDN_DK), 0)
           == lax.broadcasted_iota(jnp.int32, (GDN_DK, GDN_DK), 1))

    def row(kind):
        return f_ref[i, kind * GDN_HEADS + h:kind * GDN_HEADS + h + 1, :]

    def to_col(r):
        return jnp.sum(jnp.where(eye, jnp.broadcast_to(r, (GDN_DK, GDN_DK)), 0.0), axis=1, keepdims=True)

    q_row, k_row, v_row, beta, decay = (row(kind) for kind in range(STEP_FEATURES))
    k_col = to_col(k_row)
    s1 = s0_ref[i, h] * decay
    delta = (v_row - jnp.sum(s1 * k_col, axis=0, keepdims=True)) * beta
    s2 = s1 + k_col * delta
    sn_ref[i, h] = s2
    o_ref[i, h:h + 1, :] = jnp.sum(s2 * to_col(q_row), axis=0, keepdims=True)


def _swa_prompt_body(nq, sinks_ref, q_ref, kvp_ref, kvc_ref, o_ref):
    n = pl.program_id(1)
    w = WINDOW
    tiles = SWA_KV // LANES
    pairs = 2
    lo_lane = lax.broadcasted_iota(jnp.int32, (w, LANES), 1) < SWA_HD
    lo_row = lax.broadcasted_iota(jnp.int32, (LANES, w), 0) < SWA_HD
    c = lax.broadcasted_iota(jnp.int32, (w, pairs * w), 0)
    i = lax.broadcasted_iota(jnp.int32, (w, pairs * w), 1) & (w - 1)
    from_prev = c > i
    k_blk, vt_blk = [], []
    for j in range(nq + 1):
        src, rows = (kvp_ref, slice(0, w)) if j == 0 else (kvc_ref, slice((j - 1) * w, j * w))
        k_tiles, vt_tiles = [], []
        for t in range(tiles):
            kx = src[rows, t * LANES:(t + 1) * LANES]
            vt = src[rows, SWA_KV + t * LANES:SWA_KV + (t + 1) * LANES].T
            k_tiles.append((kx.astype(BF16), pltpu.roll(kx, SWA_HD, axis=1).astype(BF16)))
            vt_tiles.append((vt.astype(BF16),
                             jnp.concatenate([vt[SWA_HD:], vt[:SWA_HD]], axis=0).astype(BF16)))
        k_blk.append(k_tiles)
        vt_blk.append(vt_tiles)
    items = [(qb, g, p) for qb in range(nq) for g in range(SWA_KV_HEADS) for p in range(2)]
    log2e = math.log2(math.e)
    qm, kz, vzt, sink = {}, {}, {}, {}
    for qb, g, p in items:
        keep = lo_lane if p == 0 else jnp.logical_not(lo_lane)
        q_tiles = [q_ref[qb * w:(qb + 1) * w, (2 * g + r) * LANES:(2 * g + r + 1) * LANES] for r in range(pairs)]
        qm[qb, g, p] = jnp.concatenate([jnp.where(keep, x * SWA_Q_SCALE, 0.0) for x in q_tiles],
                                       axis=0).astype(BF16)
        variant = 0 if p == g % 2 else 1
        kz[qb, g, p] = jnp.concatenate([k_blk[qb + d][g // 2][variant] for d in range(2)], axis=0)
        vzt[qb, g, p] = jnp.concatenate([vt_blk[qb + d][g // 2][variant] for d in range(2)], axis=1)
        sink[qb, g, p] = jnp.concatenate([jnp.full((1, w), sinks_ref[SWA_GROUP * g + 2 * r + p] * log2e, F32)
                                          for r in range(pairs)], axis=1)
    st = {b: lax.dot_general(kz[b], qm[b], (((1,), (1,)), ((), ())), preferred_element_type=F32) for b in items}
    prev = {b: jnp.where(n > 0, st[b][:w], -jnp.inf) if b[0] == 0 else st[b][:w] for b in items}
    u = {b: jnp.where(from_prev, prev[b], st[b][w:]) for b in items}
    m = {b: jnp.maximum(jnp.max(u[b], axis=0, keepdims=True), sink[b]) for b in items}
    eu = {b: jnp.exp2(u[b] - m[b]) for b in items}
    den = {b: jnp.sum(eu[b], axis=0, keepdims=True) + jnp.exp2(sink[b] - m[b]) for b in items}
    et = {b: jnp.concatenate([jnp.where(from_prev, eu[b], 0.0), jnp.where(from_prev, 0.0, eu[b])],
                             axis=0).astype(BF16) for b in items}
    ot = {b: jnp.dot(vzt[b], et[b], preferred_element_type=F32) / den[b] for b in items}
    for qb in range(nq):
        for g in range(SWA_KV_HEADS):
            for r in range(pairs):
                cols = slice(r * w, (r + 1) * w)
                tile_t = jnp.where(lo_row, ot[qb, g, 0][:, cols], ot[qb, g, 1][:, cols])
                o_ref[qb * w:(qb + 1) * w, (2 * g + r) * LANES:(2 * g + r + 1) * LANES] = (
                    tile_t.T.astype(o_ref.dtype))


def _swa_prompt(sq, skv, sinks, nq):
    b_, l_, _ = sq.shape
    rows = nq * WINDOW
    return pl.pallas_call(
        functools.partial(_swa_prompt_body, nq),
        grid=(b_, l_ // rows),
        in_specs=[pl.BlockSpec(memory_space=pltpu.SMEM),
                  pl.BlockSpec((None, rows, SWA_Q), lambda b, n: (b, n, 0)),
                  pl.BlockSpec((None, WINDOW, 2 * SWA_KV), lambda b, n: (b, jnp.maximum(n * nq - 1, 0), 0)),
                  pl.BlockSpec((None, rows, 2 * SWA_KV), lambda b, n: (b, n, 0))],
        out_specs=pl.BlockSpec((None, rows, SWA_Q), lambda b, n: (b, n, 0)),
        out_shape=jax.ShapeDtypeStruct((b_, l_, SWA_Q), BF16),
        compiler_params=_params(("arbitrary", "arbitrary")),
        name="swa_prompt",
    )(sinks, sq, skv, skv)


def _swa_step_pieces(seq, i, q_ref, kvn_ref, ck_ref, cv_ref, sink_ref, o_ref, nk_ref, nv_ref):
    w = WINDOW
    mid = {}

    def own():
        row = lax.broadcasted_iota(jnp.int32, (SWA_Q_HEADS, SWA_KV), 0)
        lane = lax.broadcasted_iota(jnp.int32, (SWA_Q_HEADS, SWA_KV), 1)
        return (lane // SWA_HD) == (row // SWA_GROUP)

    def append_and_score():
        newest = lax.broadcasted_iota(jnp.int32, (SWA_KV, w), 1) == w - 1
        mid["nk"] = jnp.where(newest, pltpu.roll(kvn_ref[0:SWA_KV, :], w - 1 - seq, axis=1),
                              pltpu.roll(ck_ref[i], w - 1, axis=1))
        mid["nv"] = jnp.where(newest, pltpu.roll(kvn_ref[SWA_KV:2 * SWA_KV, :], w - 1 - seq, axis=1),
                              pltpu.roll(cv_ref[i], w - 1, axis=1))
        q_bd = jnp.where(own(), jnp.concatenate([q_ref[i]] * SWA_KV_HEADS, axis=1), 0.0)
        mid["s"] = _mm(q_bd, mid["nk"]) * (SWA_HD ** -0.5)

    def values():
        sink = sink_ref[...]
        m = jnp.maximum(jnp.max(mid["s"], axis=1, keepdims=True), sink)
        e = jnp.exp(mid["s"] - m)
        den = jnp.sum(e, axis=1, keepdims=True) + jnp.exp(sink - m)
        mid["pv"] = _mm_nt(e / den, mid["nv"])

    def write_out():
        pv = jnp.where(own(), mid["pv"], 0.0)
        o = pv[:, 0:SWA_HD]
        for g in range(1, SWA_KV_HEADS):
            o = o + pv[:, g * SWA_HD:(g + 1) * SWA_HD]
        o_ref[i] = o
        nk_ref[i] = mid["nk"]
        nv_ref[i] = mid["nv"]

    return [append_and_score, values, write_out]


def _dense_body(stateful, tm, og_ref, ob_ref, gab_ref, x_ref, gt1_ref, sh2_ref, sc2_ref, gt2_ref,
                n2w_ref, fnw_ref, wa_ref, wb_ref, wo_ref, wg_ref, wu_ref, cw_ref, cb_ref, wd_ref, *rest):
    if stateful:
        st_ref, gate_ref, onw_ref, y_ref, gout_ref, act_ref = rest
        onw = onw_ref[...]
        og = jnp.concatenate(
            [_gated_out_norm(og_ref[h], _silu(gate_ref[:, h * GDN_DV:(h + 1) * GDN_DV]), onw).astype(BF16)
             for h in range(GDN_HEADS)], axis=1)
    else:
        y_ref, gout_ref, act_ref, gbuf_ref, carry_ref = rest
        og = og_ref[...]

        @pl.when(pl.program_id(1) == 0)
        def _():
            carry_ref[...] = jnp.zeros_like(carry_ref)

    y_a = jnp.dot(og, wa_ref[...], preferred_element_type=F32)
    y_b = jnp.dot(ob_ref[...], wb_ref[...], preferred_element_type=F32)
    merged = (jax.nn.sigmoid(gab_ref[:, 0:D_MODEL]) * y_a
              + jax.nn.sigmoid(gab_ref[:, D_MODEL:2 * D_MODEL]) * y_b)
    x1 = x_ref[...] + gt1_ref[...] * _mm(merged, wo_ref[...])
    h2 = (_rms(x1, n2w_ref[...]) * (1.0 + sc2_ref[...]) + sh2_ref[...]).astype(BF16)

    for c in range(D_FF // FFN_COLS):
        cols = slice(c * FFN_COLS, (c + 1) * FFN_COLS)
        gate = jnp.dot(h2, wg_ref[:, cols], preferred_element_type=F32)
        up = jnp.dot(h2, wu_ref[:, cols], preferred_element_type=F32)
        if stateful:
            g2 = st_ref[0, :, cols]
            g1 = st_ref[1, :, cols]
            gout_ref[:, cols] = gate
        else:
            gbuf_ref[0:SUBLANES, :] = carry_ref[:, cols]
            gbuf_ref[SUBLANES:SUBLANES + tm, :] = gate
            g2 = gbuf_ref[SUBLANES - 2:SUBLANES - 2 + tm, :]
            g1 = gbuf_ref[SUBLANES - 1:SUBLANES - 1 + tm, :]
            carry_ref[:, cols] = gbuf_ref[tm:tm + SUBLANES, :]
        gc = (cw_ref[0:1, cols] * g2 + cw_ref[1:2, cols] * g1 + cw_ref[2:3, cols] * gate) + cb_ref[:, cols]
        act_ref[:, cols] = (_silu(gc) * up).astype(BF16)
    if not stateful:
        gout_ref[...] = carry_ref[...]

    x2 = x1 + gt2_ref[...] * jnp.dot(act_ref[...], wd_ref[...], preferred_element_type=F32)
    y_ref[...] = _rms(x2, fnw_ref[...])


def _dense(og, ob, gab, x, mods, vecs, ws, st, tm, step=None):
    b_, l_, _ = x.shape
    r_ = mods[0].shape[1]
    rt = 1 if r_ == 1 else tm
    nt = l_ // tm
    mod_map = (lambda b, t: (b, 0, 0)) if r_ == 1 else (lambda b, t: (b, t, 0))
    row_map = lambda b, t: (b, t, 0)
    stateful = st is not None
    og_spec = (pl.BlockSpec((GDN_HEADS, tm, LANES), lambda b, t: (0, t, 0)) if stateful
               else pl.BlockSpec((None, tm, D_MODEL), row_map))
    in_specs = ([og_spec,
                 pl.BlockSpec((None, tm, D_MODEL), row_map),
                 pl.BlockSpec((None, tm, 2 * D_MODEL), row_map),
                 pl.BlockSpec((None, tm, D_MODEL), row_map)]
                + [pl.BlockSpec((None, rt, D_MODEL), mod_map)] * 4
                + [_const_spec(a.shape) for a in vecs[:2]]
                + [_const_spec(ws[0].shape), _const_spec(ws[1].shape), _const_spec(ws[2].shape),
                   _const_spec(ws[3].shape), _const_spec(ws[4].shape),
                   _const_spec(vecs[2].shape), _const_spec(vecs[3].shape), _const_spec(ws[5].shape)])
    args = [og, ob, gab, x, *mods, vecs[0], vecs[1], ws[0], ws[1], ws[2], ws[3], ws[4], vecs[2], vecs[3], ws[5]]
    scratch = [pltpu.VMEM((tm, D_FF), BF16)]
    out_specs = [pl.BlockSpec((None, tm, D_MODEL), row_map)]
    out_shape = [jax.ShapeDtypeStruct((b_, l_, D_MODEL), F32)]
    if stateful:
        gate, onw = step
        in_specs += [pl.BlockSpec((FFN_CONV - 1, None, tm, D_FF), lambda b, t: (0, b, t, 0)),
                     pl.BlockSpec((None, tm, GDN_V), row_map), _const_spec(onw.shape)]
        args += [st, gate, onw]
        out_shape.append(jax.ShapeDtypeStruct((b_, l_, D_FF), F32))
        out_specs.append(pl.BlockSpec((None, tm, D_FF), row_map))
    else:
        scratch += [pltpu.VMEM((tm + SUBLANES, FFN_COLS), F32), pltpu.VMEM((SUBLANES, D_FF), F32)]
        out_shape.append(jax.ShapeDtypeStruct((b_, SUBLANES, D_FF), F32))
        out_specs.append(pl.BlockSpec((None, SUBLANES, D_FF), lambda b, t: (b, 0, 0)))
    return pl.pallas_call(
        functools.partial(_dense_body, stateful, tm),
        grid=(b_, nt),
        in_specs=in_specs,
        out_specs=tuple(out_specs),
        out_shape=tuple(out_shape),
        scratch_shapes=scratch,
        compiler_params=_params(("arbitrary", "arbitrary")),
        name="dense_step" if stateful else "dense_prompt",
    )(*args)


def _lane_row(values, offset):
    return jnp.zeros((1, LANES), F32).at[0, offset:offset + values.shape[0]].set(values)


def kernel(x_prompt, x_sample, c_prompt, c_sample, state_gdn_S, state_gdn_conv, cache_swa_k, cache_swa_v,
           state_ffn_conv, w_mod, b_mod, norm1_w, norm2_w, w_in, gdn_conv_w, gdn_a_log, gdn_dt_bias,
           gdn_onorm_w, w_gdn_out, swa_sinks, w_swa_out, w_o, w_ffn_gate, w_ffn_up, ffn_conv_w, ffn_conv_b,
           w_ffn_down, final_norm_w):
    assert w_mod.shape[0] == 1, "single-layer trunk"
    nb, seq, _ = x_prompt.shape
    ns = x_sample.shape[0]
    assert x_sample.shape[1] == 1

    in_ws = _in_weight(jnp.transpose(w_in[0]))
    dense_ws = (w_gdn_out[0].astype(BF16), w_swa_out[0].astype(BF16), w_o[0].astype(BF16),
                w_ffn_gate[0].astype(BF16), w_ffn_up[0].astype(BF16), w_ffn_down[0].astype(BF16))
    dense_vecs = (norm2_w, final_norm_w[None, :], ffn_conv_w[0], ffn_conv_b)
    cw = jnp.transpose(gdn_conv_w[0].reshape(GDN_CONV, GDN_SECTIONS, LANES), (1, 0, 2))
    alog_row = _lane_row(gdn_a_log[0], GDN_HEADS)
    dtb_row = _lane_row(gdn_dt_bias[0], GDN_HEADS)

    mod = _modulation(jnp.concatenate([c_prompt, c_sample], axis=0), w_mod[0], b_mod)
    mod_p = [mod[:nb, i * D_MODEL:(i + 1) * D_MODEL][:, None, :] for i in range(6)]
    mod_s = [mod[nb:, i * D_MODEL:(i + 1) * D_MODEL][None, :, :] for i in range(6)]

    xs = x_sample.reshape(1, ns, D_MODEL)
    qkvs, gates, bas, sqs, skvs, gabs = _inproj(xs, mod_s[0], mod_s[1], norm1_w, in_ws, None, tm=ns)
    st_gdn = jnp.transpose(state_gdn_conv[0], (1, 0, 2))
    step_feats = jnp.transpose(_gdn_step_features(qkvs[0], st_gdn, bas[0], cw, alog_row, dtb_row, bb=ns), (1, 0, 2))
    to_channel_major = lambda c: jnp.transpose(c, (0, 2, 3, 1)).reshape(ns, SWA_KV, WINDOW)
    from_channel_major = lambda c: jnp.transpose(c.reshape(ns, SWA_KV_HEADS, SWA_HD, WINDOW), (0, 3, 1, 2))
    kvn_t = jnp.pad(jnp.transpose(skvs[0]), ((0, 0), (0, WINDOW - ns)))
    step_attn = (sqs[0].reshape(ns, SWA_Q_HEADS, SWA_HD), kvn_t, to_channel_major(cache_swa_k[0]),
                 to_channel_major(cache_swa_v[0]), swa_sinks[0][:, None])

    qkvf, gact, ba, sq, skv, gab, qkv_tail = _inproj(x_prompt, mod_p[0], mod_p[1], norm1_w, in_ws, cw, tm=256)
    og, gdn_s_p, gdn_s_s, step_read, o3, k_s, v_s = _gdn_prompt(
        qkvf, gact, ba, alog_row, dtb_row, gdn_onorm_w, _level_masks(), step_feats, state_gdn_S[0], step_attn,
        lt=4 * CHUNK)
    ob = _swa_prompt(sq, skv, swa_sinks[0], nq=4)
    y_p, gate_tail = _dense(og, ob, gab, x_prompt, (mod_p[2], mod_p[3], mod_p[4], mod_p[5]),
                            dense_vecs, dense_ws, None, tm=512)
    gdn_conv_p = jnp.transpose(qkv_tail[:, :, SUBLANES - (GDN_CONV - 1):, :], (0, 2, 1, 3)).reshape(
        nb, GDN_CONV - 1, GDN_CONV_CH)
    k_p = skv[:, seq - WINDOW:, :SWA_KV].reshape(nb, WINDOW, SWA_KV_HEADS, SWA_HD)
    v_p = skv[:, seq - WINDOW:, SWA_KV:].reshape(nb, WINDOW, SWA_KV_HEADS, SWA_HD)
    ffn_conv_p = gate_tail[:, SUBLANES - (FFN_CONV - 1):, :]

    ob_s = o3.reshape(1, ns, SWA_Q).astype(BF16)
    st_ffn = jnp.transpose(state_ffn_conv[0], (1, 0, 2))[:, None]
    y_s, gate_new = _dense(jnp.transpose(step_read, (1, 0, 2)), ob_s, gabs, xs,
                           (mod_s[2], mod_s[3], mod_s[4], mod_s[5]), dense_vecs, dense_ws, st_ffn, tm=ns,
                           step=(gates, gdn_onorm_w))
    gdn_conv_s = jnp.concatenate([state_gdn_conv[0][:, 1:], qkvs[0][:, None, :]], axis=1)
    ffn_conv_s = jnp.concatenate([state_ffn_conv[0][:, 1:], gate_new[0][:, None, :]], axis=1)

    return (y_p, y_s.reshape(ns, 1, D_MODEL),
            gdn_s_p[None], gdn_s_s[None],
            gdn_conv_p[None], gdn_conv_s[None],
            k_p[None], from_channel_major(k_s)[None],
            v_p[None], from_channel_major(v_s)[None],
            ffn_conv_p[None], ffn_conv_s[None])
```

```python
import functools
import math

import numpy as np
import jax
import jax.numpy as jnp
from jax import lax
from jax.experimental import pallas as pl
from jax.experimental.pallas import tpu as pltpu

F32 = jnp.float32
BF16 = jnp.bfloat16

D_MODEL = 1024
GDN_HEADS = 8
GDN_DK = 128
GDN_DV = 128
GDN_QK = GDN_HEADS * GDN_DK
GDN_V = GDN_HEADS * GDN_DV
GDN_CONV = 4
GDN_CONV_CH = 2 * GDN_QK + GDN_V
GDN_SECTIONS = GDN_CONV_CH // 128
SWA_Q_HEADS = 16
SWA_KV_HEADS = 4
SWA_GROUP = SWA_Q_HEADS // SWA_KV_HEADS
SWA_HD = 64
SWA_Q = SWA_Q_HEADS * SWA_HD
SWA_KV = SWA_KV_HEADS * SWA_HD
WINDOW = 128
D_FF = 2816
FFN_CONV = 3
EPS = 1e-6

LANES = 128
SUBLANES = 8
VMEM_LIMIT = 56 * 1024 * 1024

COL_QKV = 0
COL_GATE = COL_QKV + GDN_CONV_CH
COL_SQ = COL_GATE + GDN_V
COL_SKV = COL_SQ + SWA_Q
COL_GAB = COL_SKV + 2 * SWA_KV
COL_BA = COL_GAB + 2 * D_MODEL
IN_COLS = COL_BA + LANES

SWA_Q_SCALE = SWA_HD ** -0.5 * math.log2(math.e)

CONV_ROWS = 64
CHUNK = 128
GDN_GROUP = 16
FFN_COLS = 256


def _mm(a, b):
    return jnp.dot(a.astype(BF16), b.astype(BF16), preferred_element_type=F32)


def _mm_nt(a, b):
    return lax.dot_general(a.astype(BF16), b.astype(BF16), (((1,), (1,)), ((), ())),
                           preferred_element_type=F32)


def _silu(x):
    return x * jax.nn.sigmoid(x)


def _softplus(x):
    return jnp.maximum(x, 0.0) + jnp.log1p(jnp.exp(-jnp.abs(x)))


def _rms(x, w):
    return x * lax.rsqrt(jnp.mean(x * x, axis=-1, keepdims=True) + EPS) * w


def _const_spec(shape):
    n = len(shape)
    return pl.BlockSpec(shape, lambda *_: (0,) * n, pipeline_mode=pl.Buffered(1))


def _params(sem):
    return pltpu.CompilerParams(dimension_semantics=sem, vmem_limit_bytes=VMEM_LIMIT)


def _mod_body(c_ref, w_ref, b_ref, o_ref):
    o_ref[...] = _mm(_silu(c_ref[...]), w_ref[...]) + b_ref[...]


def _modulation(c_all, w_mod, b_mod):
    rows = c_all.shape[0]
    n_out = w_mod.shape[1]
    tn = D_MODEL
    return pl.pallas_call(
        _mod_body,
        grid=(n_out // tn,),
        in_specs=[pl.BlockSpec((rows, D_MODEL), lambda j: (0, 0)),
                  pl.BlockSpec((D_MODEL, tn), lambda j: (0, j)),
                  pl.BlockSpec((1, tn), lambda j: (0, j))],
        out_specs=pl.BlockSpec((rows, tn), lambda j: (0, j)),
        out_shape=jax.ShapeDtypeStruct((rows, n_out), F32),
        compiler_params=_params(("arbitrary",)),
        name="modulation",
    )(c_all, w_mod, b_mod)


IN_WEIGHT_COLS = 512
IN_WEIGHT_PAD = 256


def _in_weight_body(n_main, wt_ref, ba_ref, o_ref):
    j = pl.program_id(0)

    @pl.when(j < n_main)
    def _():
        o_ref[...] = wt_ref[...].T.astype(BF16)

    @pl.when(j == n_main)
    def _():
        n_ba = ba_ref.shape[0]
        ba = jnp.concatenate([ba_ref[...].T, jnp.zeros((D_MODEL, IN_WEIGHT_COLS - n_ba), F32)], axis=1)
        o_ref[...] = ba.astype(BF16)


def _in_weight(w_t):
    n_ba = 2 * GDN_HEADS
    split = GDN_CONV_CH + GDN_V
    tc = IN_WEIGHT_COLS
    n_main = COL_BA // tc

    def src_row(j):
        jj = jnp.minimum(j, n_main - 1)
        return pl.multiple_of(jnp.where(jj * tc < split, jj * tc, jj * tc + n_ba), n_ba)

    return pl.pallas_call(
        functools.partial(_in_weight_body, n_main),
        grid=(n_main + 1,),
        in_specs=[pl.BlockSpec((pl.Element(tc), pl.Element(D_MODEL)), lambda j: (src_row(j), 0)),
                  pl.BlockSpec((pl.Element(n_ba), pl.Element(D_MODEL)), lambda j: (split, 0))],
        out_specs=pl.BlockSpec((D_MODEL, tc), lambda j: (0, j)),
        out_shape=jax.ShapeDtypeStruct((D_MODEL, COL_BA + IN_WEIGHT_PAD), BF16),
        compiler_params=_params(("arbitrary",)),
        name="in_weight",
    )(w_t, w_t)


def _l2norm(x):
    return x * lax.rsqrt(jnp.sum(x * x, axis=-1, keepdims=True) + EPS)


def _inproj_body(seq_rows, tm, x_ref, sh_ref, sc_ref, nw_ref, w_ref, *rest):
    if seq_rows:
        n_cast = (len(rest) - 9) // 2
        cw_ref, cast_in = rest[0], rest[1:1 + n_cast]
        qkv_ref, gg_ref, ba_ref, sq_ref, skv_ref, gab_ref, tail_ref = rest[1 + n_cast:8 + n_cast]
        cast_out, xe_ref = rest[8 + n_cast:8 + 2 * n_cast], rest[-1]
        for src, dst in zip(cast_in, cast_out):
            dst[...] = src[...].astype(dst.dtype)

        @pl.when(pl.program_id(1) == 0)
        def _():
            xe_ref[:, 0:SUBLANES, :] = jnp.zeros((GDN_SECTIONS, SUBLANES, LANES), F32)
    else:
        qkv_ref, gg_ref, ba_ref, sq_ref, skv_ref, gab_ref = rest

    h = _rms(x_ref[...], nw_ref[...]) * (1.0 + sc_ref[...]) + sh_ref[...]
    hb = h.astype(BF16)

    def proj(lo, width):
        return jnp.dot(hb, w_ref[:, lo:lo + width], preferred_element_type=F32)

    step = 512
    per = step // LANES
    for c in range(GDN_CONV_CH // step):
        z = proj(COL_QKV + c * step, step)
        for k in range(per):
            s = c * per + k
            zs = z[:, k * LANES:(k + 1) * LANES]
            if not seq_rows:
                qkv_ref[:, s * LANES:(s + 1) * LANES] = zs
                continue
            xe_ref[s, SUBLANES:SUBLANES + tm, :] = zs
            w = cw_ref[s]
            for r0 in range(0, tm, CONV_ROWS):
                y = w[0:1] * xe_ref[s, r0 + SUBLANES - 3:r0 + SUBLANES - 3 + CONV_ROWS, :]
                for tap in range(1, GDN_CONV):
                    lo = r0 + SUBLANES - 3 + tap
                    y = y + w[tap:tap + 1] * xe_ref[s, lo:lo + CONV_ROWS, :]
                f = _silu(y)
                if s < GDN_HEADS:
                    f = _l2norm(f) * (GDN_DK ** -0.5)
                elif s < 2 * GDN_HEADS:
                    f = _l2norm(f)
                qkv_ref[s, r0:r0 + CONV_ROWS, :] = f
            xe_ref[s, 0:SUBLANES, :] = xe_ref[s, tm:tm + SUBLANES, :]
    if seq_rows:
        tail_ref[...] = xe_ref[:, 0:SUBLANES, :]
    for c in range(GDN_V // step):
        z = proj(COL_GATE + c * step, step)
        for k in range(per):
            zs = z[:, k * LANES:(k + 1) * LANES]
            if seq_rows:
                gg_ref[c * per + k] = _silu(zs)
            else:
                gg_ref[:, (c * per + k) * LANES:(c * per + k + 1) * LANES] = zs
    ba_ref[...] = proj(COL_BA, LANES)
    for c in range(SWA_Q // step):
        sq_ref[:, c * step:(c + 1) * step] = proj(COL_SQ + c * step, step)
    skv_ref[...] = proj(COL_SKV, 2 * SWA_KV)
    for c in range(2 * D_MODEL // step):
        gab_ref[:, c * step:(c + 1) * step] = proj(COL_GAB + c * step, step)


def _inproj(x, sh, sc, nw, w_all, cw, tm, cast_ws=()):
    b_, l_, _ = x.shape
    nt = l_ // tm
    r_ = sh.shape[1]
    rt = 1 if r_ == 1 else tm
    mod_map = (lambda b, t: (b, 0, 0)) if r_ == 1 else (lambda b, t: (b, t, 0))
    row_map = lambda b, t: (b, t, 0)
    head_map = lambda b, t: (b, 0, t, 0)
    seq_rows = cw is not None
    if seq_rows:
        gdn_shapes = (jax.ShapeDtypeStruct((b_, GDN_SECTIONS, l_, LANES), F32),
                      jax.ShapeDtypeStruct((b_, GDN_HEADS, l_, LANES), F32))
        gdn_specs = (pl.BlockSpec((None, GDN_SECTIONS, tm, LANES), head_map),
                     pl.BlockSpec((None, GDN_HEADS, tm, LANES), head_map))
    else:
        gdn_shapes = (jax.ShapeDtypeStruct((b_, l_, GDN_CONV_CH), F32),
                      jax.ShapeDtypeStruct((b_, l_, GDN_V), F32))
        gdn_specs = (pl.BlockSpec((None, tm, GDN_CONV_CH), row_map),
                     pl.BlockSpec((None, tm, GDN_V), row_map))
    out_shape = gdn_shapes + (
        jax.ShapeDtypeStruct((b_, l_, LANES), F32),
        jax.ShapeDtypeStruct((b_, l_, SWA_Q), F32),
        jax.ShapeDtypeStruct((b_, l_, 2 * SWA_KV), F32),
        jax.ShapeDtypeStruct((b_, l_, 2 * D_MODEL), F32),
    )
    out_specs = gdn_specs + (
        pl.BlockSpec((None, tm, LANES), row_map),
        pl.BlockSpec((None, tm, SWA_Q), row_map),
        pl.BlockSpec((None, tm, 2 * SWA_KV), row_map),
        pl.BlockSpec((None, tm, 2 * D_MODEL), row_map),
    )
    in_specs = [
        pl.BlockSpec((None, tm, D_MODEL), row_map),
        pl.BlockSpec((None, rt, D_MODEL), mod_map),
        pl.BlockSpec((None, rt, D_MODEL), mod_map),
        _const_spec(nw.shape),
        _const_spec(w_all.shape),
    ]
    args = [x, sh, sc, nw, w_all]
    scratch = []
    if seq_rows:
        in_specs.append(_const_spec(cw.shape))
        args.append(cw)
        out_shape += (jax.ShapeDtypeStruct((b_, GDN_SECTIONS, SUBLANES, LANES), F32),)
        out_specs += (pl.BlockSpec((None, GDN_SECTIONS, SUBLANES, LANES), lambda b, t: (b, 0, 0, 0)),)
        scratch.append(pltpu.VMEM((GDN_SECTIONS, tm + SUBLANES, LANES), F32))
        steps = b_ * nt
        for w in cast_ws:
            slab = pl.BlockSpec((w.shape[0] // steps, w.shape[1]), lambda b, t: (b * nt + t, 0))
            assert w.shape[0] % (steps * 2 * SUBLANES) == 0
            in_specs.append(slab)
            args.append(w)
            out_shape += (jax.ShapeDtypeStruct(w.shape, BF16),)
            out_specs += (slab,)
    return pl.pallas_call(
        functools.partial(_inproj_body, seq_rows, tm),
        grid=(b_, l_ // tm),
        in_specs=in_specs,
        out_specs=out_specs,
        out_shape=out_shape,
        scratch_shapes=scratch,
        compiler_params=_params(("arbitrary", "arbitrary")),
        name="inproj_seq" if seq_rows else "inproj_rows",
    )(*args)


def _delta_gates(ba, alog_row, dtb_row):
    beta_all = jax.nn.sigmoid(ba)
    g_all = -jnp.exp(alog_row) * _softplus(ba + dtb_row)
    return beta_all, g_all


def _lane_column(x, lane_idx, lane):
    return jnp.sum(jnp.where(lane_idx == lane, x, 0.0), axis=1, keepdims=True)


def _level_masks():
    r = np.arange(CHUNK)[:, None]
    c = np.arange(CHUNK)[None, :]
    masks = [(r == c + 1) & (r % 2 == 1)]
    half = 2
    while half < CHUNK:
        full = 2 * half
        masks.append((r // full == c // full) & (r % full >= half) & (c % full < half))
        half = full
    return jnp.asarray(np.stack(masks), dtype=BF16)


def _unit_lower_inverses(ms, masks_ref, eye, between_levels=()):
    ts = [eye - m * masks_ref[0] for m in ms]
    pending = list(between_levels)
    for lvl in range(1, masks_ref.shape[0]):
        off = masks_ref[lvl]
        xs = [jnp.dot(m * off, t, preferred_element_type=F32).astype(BF16) for m, t in zip(ms, ts)]
        ys = [jnp.dot(t, x, preferred_element_type=F32).astype(BF16) for t, x in zip(ts, xs)]
        ts = [t - y for t, y in zip(ts, ys)]
        if pending:
            pending.pop(0)()
    for piece in pending:
        piece()
    return ts


def _cumsum_rows(g, ltri):
    hi = g.astype(BF16)
    r1 = g - hi.astype(F32)
    mid = r1.astype(BF16)
    lo = (r1 - mid.astype(F32)).astype(BF16)
    return (jnp.dot(ltri, hi, preferred_element_type=F32) + jnp.dot(ltri, mid, preferred_element_type=F32)
            + jnp.dot(ltri, lo, preferred_element_type=F32))


def _gated_out_norm(o, gate_act, onw):
    on = o * lax.rsqrt(jnp.mean(o * o, axis=-1, keepdims=True) + EPS) * onw
    return on * gate_act


def _run_all(pieces):
    for piece in pieces:
        piece()


def _gdn_prompt_body(lt, q_ref, k_ref, v_ref, ba_ref, alog_ref, dtb_ref, gate_ref, onw_ref, masks_ref,
                     f_ref, s0_ref, sq_ref, kvn_ref, ck_ref, cv_ref, sink_ref,
                     og_ref, s_ref, sn_ref, read_ref, so_ref, nk_ref, nv_ref):
    per_step = s0_ref.shape[0]
    first = (pl.program_id(0) * pl.num_programs(1) + pl.program_id(1)) * per_step
    state_work = [functools.partial(_gdn_state_update, f_ref, s0_ref, sn_ref, read_ref, i, h)
                  for i in range(per_step) for h in range(GDN_HEADS)]
    attn_work = [piece for i in range(per_step)
                 for piece in _swa_step_pieces(first + i, i, sq_ref, kvn_ref, ck_ref, cv_ref, sink_ref,
                                               so_ref, nk_ref, nv_ref)]

    @pl.when(pl.program_id(1) == 0)
    def _():
        s_ref[...] = jnp.zeros_like(s_ref)

    beta_all, g_all = _delta_gates(ba_ref[...], alog_ref[...], dtb_ref[...])
    lane_idx = lax.broadcasted_iota(jnp.int32, (CHUNK, LANES), 1)
    row = lax.broadcasted_iota(jnp.int32, (CHUNK, CHUNK), 0)
    col = lax.broadcasted_iota(jnp.int32, (CHUNK, CHUNK), 1)
    tril = row >= col
    strict = row > col
    ltri = jnp.where(tril, 1.0, 0.0).astype(BF16)
    eye = jnp.where(row == col, 1.0, 0.0).astype(BF16)
    onw = onw_ref[...]
    heads = range(GDN_HEADS)
    chunks = range(lt // CHUNK)

    blocks = [(c, j) for c in chunks for j in heads]
    pre = {}
    for c in chunks:
        rows = slice(c * CHUNK, (c + 1) * CHUNK)
        dec = _cumsum_rows(g_all[rows], ltri)
        dec_t = dec.T
        for j in heads:
            q, k, v = q_ref[j, rows, :], k_ref[j, rows, :], v_ref[j, rows, :]
            beta_col = _lane_column(beta_all[rows], lane_idx, j)
            dec_col = _lane_column(dec, lane_idx, GDN_HEADS + j)
            dec_row = dec_t[GDN_HEADS + j:GDN_HEADS + j + 1, :]
            dec_last = dec_row[:, CHUNK - 1:CHUNK]
            gam = jnp.exp(jnp.minimum(dec_col - dec_row, 0.0))
            e_col = jnp.exp(dec_col)
            kb = k * beta_col
            pre[c, j] = dict(q=q, k=k, gam=gam, kb=kb, qe=q * e_col, e_last=jnp.exp(dec_last),
                             kd=k * jnp.exp(dec_last - dec_col),
                             rhs=jnp.concatenate([v * beta_col, kb * e_col], axis=1).astype(BF16))
    a_intra, uw = {}, {}

    def recurrence(c):
        rows = slice(c * CHUNK, (c + 1) * CHUNK)
        mid = {}

        def read_out():
            mid["s"] = [s_ref[j] for j in heads]
            mid["ws_qs"] = [_mm(jnp.concatenate([uw[c, j][:, GDN_DV:], pre[c, j]["qe"]], axis=0), mid["s"][j])
                            for j in heads]

        def update():
            s_prev, ws_qs = mid["s"], mid["ws_qs"]
            v_new = [uw[c, j][:, :GDN_DV] - ws_qs[j][:CHUNK] for j in heads]
            o = [ws_qs[j][CHUNK:] + _mm(a_intra[c, j], v_new[j]) for j in heads]
            s_new = [s_prev[j] * pre[c, j]["e_last"] + _mm(pre[c, j]["kd"].T, v_new[j]) for j in heads]
            for j in heads:
                s_ref[j] = s_new[j]
                og = _gated_out_norm(o[j], gate_ref[j, rows, :], onw)
                og_ref[rows, j * GDN_DV:(j + 1) * GDN_DV] = og.astype(og_ref.dtype)

        return [read_out, update]

    n_slots = -(-len(blocks) // GDN_GROUP) * (masks_ref.shape[0] - 1)
    state_per_slot = -(-len(state_work) // n_slots)
    attn_per_slot = -(-len(attn_work) // n_slots)
    carried = []
    for g0 in range(0, len(blocks), GDN_GROUP):
        grp = blocks[g0:g0 + GDN_GROUP]
        grams = [_mm_nt(jnp.concatenate([pre[b]["kb"], pre[b]["q"]], axis=0), pre[b]["k"]) for b in grp]
        ms = [jnp.where(strict, g[:CHUNK] * pre[b]["gam"], 0.0).astype(BF16) for g, b in zip(grams, grp)]
        a_intra.update({b: jnp.where(tril, g[CHUNK:] * pre[b]["gam"], 0.0) for g, b in zip(grams, grp)})
        slots = []
        for _ in range(masks_ref.shape[0] - 1):
            work = carried[:1] + state_work[:state_per_slot] + attn_work[:attn_per_slot]
            carried = carried[1:]
            state_work, attn_work = state_work[state_per_slot:], attn_work[attn_per_slot:]
            slots.append(functools.partial(_run_all, work))
        t_inv = _unit_lower_inverses(ms, masks_ref, eye, slots)
        _run_all(carried)
        uw.update({b: jnp.dot(t, pre[b]["rhs"], preferred_element_type=F32) for t, b in zip(t_inv, grp)})
        carried = [piece for c in sorted({c for c, _ in grp}) for piece in recurrence(c)]
    _run_all(carried + state_work + attn_work)


def _gdn_prompt(qkvf, gact, ba, alog_row, dtb_row, onw, masks, step_feats, step_state, step_attn, lt):
    b_, _, l_, _ = qkvf.shape
    nt = l_ // lt
    n_ = step_state.shape[0]
    per_step = n_ // (b_ * nt)
    assert per_step * b_ * nt == n_ and n_ <= WINDOW
    q3, kvn_t, ck_t, cv_t, sink_col = step_attn
    sec = lambda s: pl.BlockSpec((None, GDN_HEADS, lt, LANES), lambda b, t, s=s: (b, s, t, 0))
    seq_map = lambda b, t: (b * nt + t, 0, 0)
    seq_block = lambda *dims: pl.BlockSpec((per_step,) + dims, lambda b, t: (b * nt + t,) + (0,) * len(dims))
    state_spec = seq_block(GDN_HEADS, GDN_DK, GDN_DV)
    cache_spec = seq_block(SWA_KV, WINDOW)
    return pl.pallas_call(
        functools.partial(_gdn_prompt_body, lt),
        grid=(b_, nt),
        in_specs=[sec(0), sec(1), sec(2),
                  pl.BlockSpec((None, lt, LANES), lambda b, t: (b, t, 0)),
                  _const_spec(alog_row.shape), _const_spec(dtb_row.shape),
                  pl.BlockSpec((None, GDN_HEADS, lt, LANES), lambda b, t: (b, 0, t, 0)),
                  _const_spec(onw.shape), _const_spec(masks.shape),
                  seq_block(STEP_FEATURES * GDN_HEADS, LANES), state_spec,
                  seq_block(SWA_Q_HEADS, SWA_HD), _const_spec(kvn_t.shape), cache_spec, cache_spec,
                  _const_spec(sink_col.shape)],
        out_specs=(pl.BlockSpec((None, lt, GDN_V), lambda b, t: (b, t, 0)),
                   pl.BlockSpec((None, GDN_HEADS, GDN_DK, GDN_DV), lambda b, t: (b, 0, 0, 0)),
                   state_spec, seq_block(GDN_HEADS, LANES),
                   seq_block(SWA_Q_HEADS, SWA_HD), cache_spec, cache_spec),
        out_shape=(jax.ShapeDtypeStruct((b_, l_, GDN_V), BF16),
                   jax.ShapeDtypeStruct((b_, GDN_HEADS, GDN_DK, GDN_DV), F32),
                   jax.ShapeDtypeStruct(step_state.shape, F32),
                   jax.ShapeDtypeStruct((n_, GDN_HEADS, LANES), F32),
                   jax.ShapeDtypeStruct(q3.shape, F32),
                   jax.ShapeDtypeStruct(ck_t.shape, F32),
                   jax.ShapeDtypeStruct(cv_t.shape, F32)),
        compiler_params=_params(("arbitrary", "arbitrary")),
        name="gdn_prompt",
    )(qkvf, qkvf, qkvf, ba, alog_row, dtb_row, gact, onw, masks, step_feats, step_state,
      q3, kvn_t, ck_t, cv_t, sink_col)


def _gdn_step_body(bb, x_ref, st_ref, cw_ref, ba_ref, alog_ref, dtb_ref, gate_ref, onw_ref, s0_ref,
                   og_ref, sn_ref, q_s, k_s, v_s, b_s, e_s, o_s):
    beta_all, g_all = _delta_gates(ba_ref[...], alog_ref[...], dtb_ref[...])
    lane_idx = lax.broadcasted_iota(jnp.int32, (bb, LANES), 1)
    for h in range(GDN_HEADS):
        feats = []
        for s in range(3):
            idx = s * GDN_HEADS + h
            cols = slice(idx * LANES, (idx + 1) * LANES)
            w = cw_ref[idx]
            y = w[0:1] * st_ref[0, :, cols]
            for tap in range(1, GDN_CONV - 1):
                y = y + w[tap:tap + 1] * st_ref[tap, :, cols]
            y = y + w[GDN_CONV - 1:GDN_CONV] * x_ref[:, cols]
            feats.append(_silu(y))
        q, k, v = feats
        q_s[h] = q * lax.rsqrt(jnp.sum(q * q, axis=-1, keepdims=True) + EPS) * (GDN_DK ** -0.5)
        k_s[h] = k * lax.rsqrt(jnp.sum(k * k, axis=-1, keepdims=True) + EPS)
        v_s[h] = v
        b_s[h] = jnp.broadcast_to(_lane_column(beta_all, lane_idx, h), (bb, LANES))
        e_s[h] = jnp.broadcast_to(jnp.exp(_lane_column(g_all, lane_idx, h + GDN_HEADS)), (bb, LANES))

    eye = (lax.broadcasted_iota(jnp.int32, (GDN_DK, GDN_DK), 0)
           == lax.broadcasted_iota(jnp.int32, (GDN_DK, GDN_DK), 1))

    def to_col(r):
        return jnp.sum(jnp.where(eye, jnp.broadcast_to(r, (GDN_DK, GDN_DK)), 0.0), axis=1, keepdims=True)

    sub = lax.broadcasted_iota(jnp.int32, (SUBLANES, GDN_DK), 0)

    def seq_body(i, carry):
        for h in range(GDN_HEADS):
            one = pl.ds(i, 1)
            k_row = k_s[h, one, :]
            q_row = q_s[h, one, :]
            s1 = s0_ref[i, h] * e_s[h, one, :]
            kq = jnp.where(sub == 0, k_row, jnp.where(sub == 1, q_row, 0.0))
            kq_s1 = _mm(kq, s1)
            delta = (v_s[h, one, :] - kq_s1[0:1, :]) * b_s[h, one, :]
            sn_ref[i, h] = s1 + to_col(k_row) * delta
            qk = jnp.sum(q_row * k_row, axis=1, keepdims=True)
            o_s[h, one, :] = kq_s1[1:2, :] + qk * delta
        return carry

    lax.fori_loop(0, bb, seq_body, 0)
    onw = onw_ref[...]
    for h in range(GDN_HEADS):
        cols = slice(h * GDN_DV, (h + 1) * GDN_DV)
        og = _gated_out_norm(o_s[h], _silu(gate_ref[:, cols]), onw)
        og_ref[:, cols] = og.astype(og_ref.dtype)


def _gdn_step(qkv, st, gate, ba, cw, alog_row, dtb_row, onw, s0, bb):
    n_ = ba.shape[0]
    vec = pltpu.VMEM((GDN_HEADS, bb, LANES), F32)
    return pl.pallas_call(
        functools.partial(_gdn_step_body, bb),
        grid=(n_ // bb,),
        in_specs=[pl.BlockSpec((bb, GDN_CONV_CH), lambda i: (i, 0)),
                  pl.BlockSpec((GDN_CONV - 1, bb, GDN_CONV_CH), lambda i: (0, i, 0)),
                  _const_spec(cw.shape),
                  pl.BlockSpec((bb, LANES), lambda i: (i, 0)),
                  _const_spec(alog_row.shape), _const_spec(dtb_row.shape),
                  pl.BlockSpec((bb, GDN_V), lambda i: (i, 0)),
                  _const_spec(onw.shape),
                  pl.BlockSpec((bb, GDN_HEADS, GDN_DK, GDN_DV), lambda i: (i, 0, 0, 0))],
        out_specs=(pl.BlockSpec((bb, GDN_V), lambda i: (i, 0)),
                   pl.BlockSpec((bb, GDN_HEADS, GDN_DK, GDN_DV), lambda i: (i, 0, 0, 0))),
        out_shape=(jax.ShapeDtypeStruct((n_, GDN_V), BF16),
                   jax.ShapeDtypeStruct(s0.shape, F32)),
        scratch_shapes=[vec, vec, vec, vec, vec, vec],
        compiler_params=_params(("arbitrary",)),
        name="gdn_step",
    )(qkv, st, cw, ba, alog_row, dtb_row, gate, onw, s0)


STEP_FEATURES = 5


def _gdn_step_features_body(bb, x_ref, st_ref, cw_ref, ba_ref, alog_ref, dtb_ref, f_ref):
    beta_all, g_all = _delta_gates(ba_ref[...], alog_ref[...], dtb_ref[...])
    lane_idx = lax.broadcasted_iota(jnp.int32, (bb, LANES), 1)
    for h in range(GDN_HEADS):
        feats = []
        for s in range(3):
            idx = s * GDN_HEADS + h
            cols = slice(idx * LANES, (idx + 1) * LANES)
            w = cw_ref[idx]
            y = w[0:1] * st_ref[0, :, cols]
            for tap in range(1, GDN_CONV - 1):
                y = y + w[tap:tap + 1] * st_ref[tap, :, cols]
            y = y + w[GDN_CONV - 1:GDN_CONV] * x_ref[:, cols]
            feats.append(_silu(y))
        q, k, v = feats
        f_ref[h] = _l2norm(q) * (GDN_DK ** -0.5)
        f_ref[GDN_HEADS + h] = _l2norm(k)
        f_ref[2 * GDN_HEADS + h] = v
        f_ref[3 * GDN_HEADS + h] = jnp.broadcast_to(_lane_column(beta_all, lane_idx, h), (bb, LANES))
        f_ref[4 * GDN_HEADS + h] = jnp.broadcast_to(
            jnp.exp(_lane_column(g_all, lane_idx, h + GDN_HEADS)), (bb, LANES))


def _gdn_step_features(qkv, st, ba, cw, alog_row, dtb_row, bb):
    n_ = ba.shape[0]
    return pl.pallas_call(
        functools.partial(_gdn_step_features_body, bb),
        grid=(n_ // bb,),
        in_specs=[pl.BlockSpec((bb, GDN_CONV_CH), lambda i: (i, 0)),
                  pl.BlockSpec((GDN_CONV - 1, bb, GDN_CONV_CH), lambda i: (0, i, 0)),
                  _const_spec(cw.shape),
                  pl.BlockSpec((bb, LANES), lambda i: (i, 0)),
                  _const_spec(alog_row.shape), _const_spec(dtb_row.shape)],
        out_specs=pl.BlockSpec((STEP_FEATURES * GDN_HEADS, bb, LANES), lambda i: (0, i, 0)),
        out_shape=jax.ShapeDtypeStruct((STEP_FEATURES * GDN_HEADS, n_, LANES), F32),
        compiler_params=_params(("arbitrary",)),
        name="gdn_step_features",
    )(qkv, st, cw, ba, alog_row, dtb_row)


def _gdn_state_update(f_ref, s0_ref, sn_ref, o_ref, i, h):
    eye = (lax.broadcasted_iota(jnp.int32, (GDN_DK, GDN_DK), 0)
           == lax.broadcasted_iota(jnp.int32, (GDN_DK, GDN_DK), 1))

    def row(kind):
        return f_ref[i, kind * GDN_HEADS + h:kind * GDN_HEADS + h + 1, :]

    def to_col(r):
        return jnp.sum(jnp.where(eye, jnp.broadcast_to(r, (GDN_DK, GDN_DK)), 0.0), axis=1, keepdims=True)

    q_row, k_row, v_row, beta, decay = (row(kind) for kind in range(STEP_FEATURES))
    k_col = to_col(k_row)
    s1 = s0_ref[i, h] * decay
    delta = (v_row - jnp.sum(s1 * k_col, axis=0, keepdims=True)) * beta
    s2 = s1 + k_col * delta
    sn_ref[i, h] = s2
    o_ref[i, h:h + 1, :] = jnp.sum(s2 * to_col(q_row), axis=0, keepdims=True)


def _swa_prompt_body(nq, sinks_ref, q_ref, kvp_ref, kvc_ref, o_ref):
    n = pl.program_id(1)
    w = WINDOW
    tiles = SWA_KV // LANES
    pairs = 2
    lo_lane = lax.broadcasted_iota(jnp.int32, (w, LANES), 1) < SWA_HD
    lo_row = lax.broadcasted_iota(jnp.int32, (LANES, w), 0) < SWA_HD
    c = lax.broadcasted_iota(jnp.int32, (w, pairs * w), 0)
    i = lax.broadcasted_iota(jnp.int32, (w, pairs * w), 1) & (w - 1)
    from_prev = c > i
    k_blk, vt_blk = [], []
    for j in range(nq + 1):
        src, rows = (kvp_ref, slice(0, w)) if j == 0 else (kvc_ref, slice((j - 1) * w, j * w))
        k_tiles, vt_tiles = [], []
        for t in range(tiles):
            kx = src[rows, t * LANES:(t + 1) * LANES]
            vt = src[rows, SWA_KV + t * LANES:SWA_KV + (t + 1) * LANES].T
            k_tiles.append((kx.astype(BF16), pltpu.roll(kx, SWA_HD, axis=1).astype(BF16)))
            vt_tiles.append((vt.astype(BF16),
                             jnp.concatenate([vt[SWA_HD:], vt[:SWA_HD]], axis=0).astype(BF16)))
        k_blk.append(k_tiles)
        vt_blk.append(vt_tiles)
    items = [(qb, g, p) for qb in range(nq) for g in range(SWA_KV_HEADS) for p in range(2)]
    log2e = math.log2(math.e)
    qm, kz, vzt, sink = {}, {}, {}, {}
    for qb, g, p in items:
        keep = lo_lane if p == 0 else jnp.logical_not(lo_lane)
        q_tiles = [q_ref[qb * w:(qb + 1) * w, (2 * g + r) * LANES:(2 * g + r + 1) * LANES] for r in range(pairs)]
        qm[qb, g, p] = jnp.concatenate([jnp.where(keep, x * SWA_Q_SCALE, 0.0) for x in q_tiles],
                                       axis=0).astype(BF16)
        variant = 0 if p == g % 2 else 1
        kz[qb, g, p] = jnp.concatenate([k_blk[qb + d][g // 2][variant] for d in range(2)], axis=0)
        vzt[qb, g, p] = jnp.concatenate([vt_blk[qb + d][g // 2][variant] for d in range(2)], axis=1)
        sink[qb, g, p] = jnp.concatenate([jnp.full((1, w), sinks_ref[SWA_GROUP * g + 2 * r + p] * log2e, F32)
                                          for r in range(pairs)], axis=1)
    st = {b: lax.dot_general(kz[b], qm[b], (((1,), (1,)), ((), ())), preferred_element_type=F32) for b in items}
    prev = {b: jnp.where(n > 0, st[b][:w], -jnp.inf) if b[0] == 0 else st[b][:w] for b in items}
    u = {b: jnp.where(from_prev, prev[b], st[b][w:]) for b in items}
    m = {b: jnp.maximum(jnp.max(u[b], axis=0, keepdims=True), sink[b]) for b in items}
    eu = {b: jnp.exp2(u[b] - m[b]) for b in items}
    den = {b: jnp.sum(eu[b], axis=0, keepdims=True) + jnp.exp2(sink[b] - m[b]) for b in items}
    et = {b: jnp.concatenate([jnp.where(from_prev, eu[b], 0.0), jnp.where(from_prev, 0.0, eu[b])],
                             axis=0).astype(BF16) for b in items}
    ot = {b: jnp.dot(vzt[b], et[b], preferred_element_type=F32) / den[b] for b in items}
    for qb in range(nq):
        for g in range(SWA_KV_HEADS):
            for r in range(pairs):
                cols = slice(r * w, (r + 1) * w)
                tile_t = jnp.where(lo_row, ot[qb, g, 0][:, cols], ot[qb, g, 1][:, cols])
                o_ref[qb * w:(qb + 1) * w, (2 * g + r) * LANES:(2 * g + r + 1) * LANES] = (
                    tile_t.T.astype(o_ref.dtype))


def _swa_prompt(sq, skv, sinks, nq):
    b_, l_, _ = sq.shape
    rows = nq * WINDOW
    return pl.pallas_call(
        functools.partial(_swa_prompt_body, nq),
        grid=(b_, l_ // rows),
        in_specs=[pl.BlockSpec(memory_space=pltpu.SMEM),
                  pl.BlockSpec((None, rows, SWA_Q), lambda b, n: (b, n, 0)),
                  pl.BlockSpec((None, WINDOW, 2 * SWA_KV), lambda b, n: (b, jnp.maximum(n * nq - 1, 0), 0)),
                  pl.BlockSpec((None, rows, 2 * SWA_KV), lambda b, n: (b, n, 0))],
        out_specs=pl.BlockSpec((None, rows, SWA_Q), lambda b, n: (b, n, 0)),
        out_shape=jax.ShapeDtypeStruct((b_, l_, SWA_Q), BF16),
        compiler_params=_params(("arbitrary", "arbitrary")),
        name="swa_prompt",
    )(sinks, sq, skv, skv)


def _swa_step_pieces(seq, i, q_ref, kvn_ref, ck_ref, cv_ref, sink_ref, o_ref, nk_ref, nv_ref):
    w = WINDOW
    mid = {}

    def own():
        row = lax.broadcasted_iota(jnp.int32, (SWA_Q_HEADS, SWA_KV), 0)
        lane = lax.broadcasted_iota(jnp.int32, (SWA_Q_HEADS, SWA_KV), 1)
        return (lane // SWA_HD) == (row // SWA_GROUP)

    def append_and_score():
        newest = lax.broadcasted_iota(jnp.int32, (SWA_KV, w), 1) == w - 1
        mid["nk"] = jnp.where(newest, pltpu.roll(kvn_ref[0:SWA_KV, :], w - 1 - seq, axis=1),
                              pltpu.roll(ck_ref[i], w - 1, axis=1))
        mid["nv"] = jnp.where(newest, pltpu.roll(kvn_ref[SWA_KV:2 * SWA_KV, :], w - 1 - seq, axis=1),
                              pltpu.roll(cv_ref[i], w - 1, axis=1))
        q_bd = jnp.where(own(), jnp.concatenate([q_ref[i]] * SWA_KV_HEADS, axis=1), 0.0)
        mid["s"] = _mm(q_bd, mid["nk"]) * (SWA_HD ** -0.5)

    def values():
        sink = sink_ref[...]
        m = jnp.maximum(jnp.max(mid["s"], axis=1, keepdims=True), sink)
        e = jnp.exp(mid["s"] - m)
        den = jnp.sum(e, axis=1, keepdims=True) + jnp.exp(sink - m)
        mid["pv"] = _mm_nt(e / den, mid["nv"])

    def write_out():
        pv = jnp.where(own(), mid["pv"], 0.0)
        o = pv[:, 0:SWA_HD]
        for g in range(1, SWA_KV_HEADS):
            o = o + pv[:, g * SWA_HD:(g + 1) * SWA_HD]
        o_ref[i] = o
        nk_ref[i] = mid["nk"]
        nv_ref[i] = mid["nv"]

    return [append_and_score, values, write_out]


def _dense_body(stateful, tm, og_ref, ob_ref, gab_ref, x_ref, gt1_ref, sh2_ref, sc2_ref, gt2_ref,
                n2w_ref, fnw_ref, wa_ref, wb_ref, wo_ref, wg_ref, wu_ref, cw_ref, cb_ref, wd_ref, *rest):
    if stateful:
        st_ref, gate_ref, onw_ref, y_ref, gout_ref, act_ref = rest
        onw = onw_ref[...]
        og = jnp.concatenate(
            [_gated_out_norm(og_ref[h], _silu(gate_ref[:, h * GDN_DV:(h + 1) * GDN_DV]), onw).astype(BF16)
             for h in range(GDN_HEADS)], axis=1)
    else:
        y_ref, gout_ref, act_ref, gbuf_ref, carry_ref = rest
        og = og_ref[...]

        @pl.when(pl.program_id(1) == 0)
        def _():
            carry_ref[...] = jnp.zeros_like(carry_ref)

    y_a = jnp.dot(og, wa_ref[...], preferred_element_type=F32)
    y_b = jnp.dot(ob_ref[...], wb_ref[...], preferred_element_type=F32)
    merged = (jax.nn.sigmoid(gab_ref[:, 0:D_MODEL]) * y_a
              + jax.nn.sigmoid(gab_ref[:, D_MODEL:2 * D_MODEL]) * y_b)
    x1 = x_ref[...] + gt1_ref[...] * _mm(merged, wo_ref[...])
    h2 = (_rms(x1, n2w_ref[...]) * (1.0 + sc2_ref[...]) + sh2_ref[...]).astype(BF16)

    for c in range(D_FF // FFN_COLS):
        cols = slice(c * FFN_COLS, (c + 1) * FFN_COLS)
        gate = jnp.dot(h2, wg_ref[:, cols], preferred_element_type=F32)
        up = jnp.dot(h2, wu_ref[:, cols], preferred_element_type=F32)
        if stateful:
            g2 = st_ref[0, :, cols]
            g1 = st_ref[1, :, cols]
            gout_ref[:, cols] = gate
        else:
            gbuf_ref[0:SUBLANES, :] = carry_ref[:, cols]
            gbuf_ref[SUBLANES:SUBLANES + tm, :] = gate
            g2 = gbuf_ref[SUBLANES - 2:SUBLANES - 2 + tm, :]
            g1 = gbuf_ref[SUBLANES - 1:SUBLANES - 1 + tm, :]
            carry_ref[:, cols] = gbuf_ref[tm:tm + SUBLANES, :]
        gc = (cw_ref[0:1, cols] * g2 + cw_ref[1:2, cols] * g1 + cw_ref[2:3, cols] * gate) + cb_ref[:, cols]
        act_ref[:, cols] = (_silu(gc) * up).astype(BF16)
    if not stateful:
        gout_ref[...] = carry_ref[...]

    x2 = x1 + gt2_ref[...] * jnp.dot(act_ref[...], wd_ref[...], preferred_element_type=F32)
    y_ref[...] = _rms(x2, fnw_ref[...])


def _dense(og, ob, gab, x, mods, vecs, ws, st, tm, step=None):
    b_, l_, _ = x.shape
    r_ = mods[0].shape[1]
    rt = 1 if r_ == 1 else tm
    nt = l_ // tm
    mod_map = (lambda b, t: (b, 0, 0)) if r_ == 1 else (lambda b, t: (b, t, 0))
    row_map = lambda b, t: (b, t, 0)
    stateful = st is not None
    og_spec = (pl.BlockSpec((GDN_HEADS, tm, LANES), lambda b, t: (0, t, 0)) if stateful
               else pl.BlockSpec((None, tm, D_MODEL), row_map))
    in_specs = ([og_spec,
                 pl.BlockSpec((None, tm, D_MODEL), row_map),
                 pl.BlockSpec((None, tm, 2 * D_MODEL), row_map),
                 pl.BlockSpec((None, tm, D_MODEL), row_map)]
                + [pl.BlockSpec((None, rt, D_MODEL), mod_map)] * 4
                + [_const_spec(a.shape) for a in vecs[:2]]
                + [_const_spec(ws[0].shape), _const_spec(ws[1].shape), _const_spec(ws[2].shape),
                   _const_spec(ws[3].shape), _const_spec(ws[4].shape),
                   _const_spec(vecs[2].shape), _const_spec(vecs[3].shape), _const_spec(ws[5].shape)])
    args = [og, ob, gab, x, *mods, vecs[0], vecs[1], ws[0], ws[1], ws[2], ws[3], ws[4], vecs[2], vecs[3], ws[5]]
    scratch = [pltpu.VMEM((tm, D_FF), BF16)]
    out_specs = [pl.BlockSpec((None, tm, D_MODEL), row_map)]
    out_shape = [jax.ShapeDtypeStruct((b_, l_, D_MODEL), F32)]
    if stateful:
        gate, onw = step
        in_specs += [pl.BlockSpec((FFN_CONV - 1, None, tm, D_FF), lambda b, t: (0, b, t, 0)),
                     pl.BlockSpec((None, tm, GDN_V), row_map), _const_spec(onw.shape)]
        args += [st, gate, onw]
        out_shape.append(jax.ShapeDtypeStruct((b_, l_, D_FF), F32))
        out_specs.append(pl.BlockSpec((None, tm, D_FF), row_map))
    else:
        scratch += [pltpu.VMEM((tm + SUBLANES, FFN_COLS), F32), pltpu.VMEM((SUBLANES, D_FF), F32)]
        out_shape.append(jax.ShapeDtypeStruct((b_, SUBLANES, D_FF), F32))
        out_specs.append(pl.BlockSpec((None, SUBLANES, D_FF), lambda b, t: (b, 0, 0)))
    return pl.pallas_call(
        functools.partial(_dense_body, stateful, tm),
        grid=(b_, nt),
        in_specs=in_specs,
        out_specs=tuple(out_specs),
        out_shape=tuple(out_shape),
        scratch_shapes=scratch,
        compiler_params=_params(("arbitrary", "arbitrary")),
        name="dense_step" if stateful else "dense_prompt",
    )(*args)


def _lane_row(values, offset):
    return jnp.zeros((1, LANES), F32).at[0, offset:offset + values.shape[0]].set(values)


def kernel(x_prompt, x_sample, c_prompt, c_sample, state_gdn_S, state_gdn_conv, cache_swa_k, cache_swa_v,
           state_ffn_conv, w_mod, b_mod, norm1_w, norm2_w, w_in, gdn_conv_w, gdn_a_log, gdn_dt_bias,
           gdn_onorm_w, w_gdn_out, swa_sinks, w_swa_out, w_o, w_ffn_gate, w_ffn_up, ffn_conv_w, ffn_conv_b,
           w_ffn_down, final_norm_w):
    assert w_mod.shape[0] == 1, "single-layer trunk"
    nb, seq, _ = x_prompt.shape
    ns = x_sample.shape[0]
    assert x_sample.shape[1] == 1

    in_ws = _in_weight(jnp.transpose(w_in[0]))
    dense_vecs = (norm2_w, final_norm_w[None, :], ffn_conv_w[0], ffn_conv_b)
    cw = jnp.transpose(gdn_conv_w[0].reshape(GDN_CONV, GDN_SECTIONS, LANES), (1, 0, 2))
    alog_row = _lane_row(gdn_a_log[0], GDN_HEADS)
    dtb_row = _lane_row(gdn_dt_bias[0], GDN_HEADS)

    mod = _modulation(jnp.concatenate([c_prompt, c_sample], axis=0), w_mod[0], b_mod)
    mod_p = [mod[:nb, i * D_MODEL:(i + 1) * D_MODEL][:, None, :] for i in range(6)]
    mod_s = [mod[nb:, i * D_MODEL:(i + 1) * D_MODEL][None, :, :] for i in range(6)]

    xs = x_sample.reshape(1, ns, D_MODEL)
    qkvs, gates, bas, sqs, skvs, gabs = _inproj(xs, mod_s[0], mod_s[1], norm1_w, in_ws, None, tm=ns)
    st_gdn = jnp.transpose(state_gdn_conv[0], (1, 0, 2))
    step_feats = jnp.transpose(_gdn_step_features(qkvs[0], st_gdn, bas[0], cw, alog_row, dtb_row, bb=ns), (1, 0, 2))
    to_channel_major = lambda c: jnp.transpose(c, (0, 2, 3, 1)).reshape(ns, SWA_KV, WINDOW)
    from_channel_major = lambda c: jnp.transpose(c.reshape(ns, SWA_KV_HEADS, SWA_HD, WINDOW), (0, 3, 1, 2))
    kvn_t = jnp.pad(jnp.transpose(skvs[0]), ((0, 0), (0, WINDOW - ns)))
    step_attn = (sqs[0].reshape(ns, SWA_Q_HEADS, SWA_HD), kvn_t, to_channel_major(cache_swa_k[0]),
                 to_channel_major(cache_swa_v[0]), swa_sinks[0][:, None])

    later_ws = (w_gdn_out[0], w_swa_out[0], w_o[0], w_ffn_gate[0], w_ffn_up[0])
    qkvf, gact, ba, sq, skv, gab, qkv_tail, *later_bf16 = _inproj(
        x_prompt, mod_p[0], mod_p[1], norm1_w, in_ws, cw, tm=256, cast_ws=later_ws)
    dense_ws = (*later_bf16, w_ffn_down[0].astype(BF16))
    og, gdn_s_p, gdn_s_s, step_read, o3, k_s, v_s = _gdn_prompt(
        qkvf, gact, ba, alog_row, dtb_row, gdn_onorm_w, _level_masks(), step_feats, state_gdn_S[0], step_attn,
        lt=4 * CHUNK)
    ob = _swa_prompt(sq, skv, swa_sinks[0], nq=4)
    y_p, gate_tail = _dense(og, ob, gab, x_prompt, (mod_p[2], mod_p[3], mod_p[4], mod_p[5]),
                            dense_vecs, dense_ws, None, tm=512)
    gdn_conv_p = jnp.transpose(qkv_tail[:, :, SUBLANES - (GDN_CONV - 1):, :], (0, 2, 1, 3)).reshape(
        nb, GDN_CONV - 1, GDN_CONV_CH)
    k_p = skv[:, seq - WINDOW:, :SWA_KV].reshape(nb, WINDOW, SWA_KV_HEADS, SWA_HD)
    v_p = skv[:, seq - WINDOW:, SWA_KV:].reshape(nb, WINDOW, SWA_KV_HEADS, SWA_HD)
    ffn_conv_p = gate_tail[:, SUBLANES - (FFN_CONV - 1):, :]

    ob_s = o3.reshape(1, ns, SWA_Q).astype(BF16)
    st_ffn = jnp.transpose(state_ffn_conv[0], (1, 0, 2))[:, None]
    y_s, gate_new = _dense(jnp.transpose(step_read, (1, 0, 2)), ob_s, gabs, xs,
                           (mod_s[2], mod_s[3], mod_s[4], mod_s[5]), dense_vecs, dense_ws, st_ffn, tm=ns,
                           step=(gates, gdn_onorm_w))
    gdn_conv_s = jnp.concatenate([state_gdn_conv[0][:, 1:], qkvs[0][:, None, :]], axis=1)
    ffn_conv_s = jnp.concatenate([state_ffn_conv[0][:, 1:], gate_new[0][:, None, :]], axis=1)

    return (y_p, y_s.reshape(ns, 1, D_MODEL),
            gdn_s_p[None], gdn_s_s[None],
            gdn_conv_p[None], gdn_conv_s[None],
            k_p[None], from_channel_major(k_s)[None],
            v_p[None], from_channel_major(v_s)[None],
            ffn_conv_p[None], ffn_conv_s[None])
```

```python
import functools
import math

import numpy as np
import jax
import jax.numpy as jnp
from jax import lax
from jax.experimental import pallas as pl
from jax.experimental.pallas import tpu as pltpu

F32 = jnp.float32
BF16 = jnp.bfloat16

D_MODEL = 1024
GDN_HEADS = 8
GDN_DK = 128
GDN_DV = 128
GDN_QK = GDN_HEADS * GDN_DK
GDN_V = GDN_HEADS * GDN_DV
GDN_CONV = 4
GDN_CONV_CH = 2 * GDN_QK + GDN_V
GDN_SECTIONS = GDN_CONV_CH // 128
SWA_Q_HEADS = 16
SWA_KV_HEADS = 4
SWA_GROUP = SWA_Q_HEADS // SWA_KV_HEADS
SWA_HD = 64
SWA_Q = SWA_Q_HEADS * SWA_HD
SWA_KV = SWA_KV_HEADS * SWA_HD
WINDOW = 128
D_FF = 2816
FFN_CONV = 3
EPS = 1e-6

LANES = 128
SUBLANES = 8
VMEM_LIMIT = 56 * 1024 * 1024

COL_QKV = 0
COL_GATE = COL_QKV + GDN_CONV_CH
COL_SQ = COL_GATE + GDN_V
COL_SKV = COL_SQ + SWA_Q
COL_GAB = COL_SKV + 2 * SWA_KV
COL_BA = COL_GAB + 2 * D_MODEL
IN_COLS = COL_BA + LANES

SWA_Q_SCALE = SWA_HD ** -0.5 * math.log2(math.e)

CONV_ROWS = 64
CHUNK = 128
GDN_GROUP = 16
FFN_COLS = 256


def _mm(a, b):
    return jnp.dot(a.astype(BF16), b.astype(BF16), preferred_element_type=F32)


def _mm_nt(a, b):
    return lax.dot_general(a.astype(BF16), b.astype(BF16), (((1,), (1,)), ((), ())),
                           preferred_element_type=F32)


def _silu(x):
    return x * jax.nn.sigmoid(x)


def _softplus(x):
    return jnp.maximum(x, 0.0) + jnp.log1p(jnp.exp(-jnp.abs(x)))


def _rms(x, w):
    return x * lax.rsqrt(jnp.mean(x * x, axis=-1, keepdims=True) + EPS) * w


def _const_spec(shape):
    n = len(shape)
    return pl.BlockSpec(shape, lambda *_: (0,) * n, pipeline_mode=pl.Buffered(1))


def _params(sem):
    return pltpu.CompilerParams(dimension_semantics=sem, vmem_limit_bytes=VMEM_LIMIT)


def _mod_body(c_ref, w_ref, b_ref, o_ref):
    o_ref[...] = _mm(_silu(c_ref[...]), w_ref[...]) + b_ref[...]


def _modulation(c_all, w_mod, b_mod):
    rows = c_all.shape[0]
    n_out = w_mod.shape[1]
    tn = D_MODEL
    return pl.pallas_call(
        _mod_body,
        grid=(n_out // tn,),
        in_specs=[pl.BlockSpec((rows, D_MODEL), lambda j: (0, 0)),
                  pl.BlockSpec((D_MODEL, tn), lambda j: (0, j)),
                  pl.BlockSpec((1, tn), lambda j: (0, j))],
        out_specs=pl.BlockSpec((rows, tn), lambda j: (0, j)),
        out_shape=jax.ShapeDtypeStruct((rows, n_out), F32),
        compiler_params=_params(("arbitrary",)),
        name="modulation",
    )(c_all, w_mod, b_mod)


IN_WEIGHT_COLS = 512
IN_WEIGHT_PAD = 256


def _in_weight_body(n_main, wt_ref, ba_ref, o_ref):
    j = pl.program_id(0)

    @pl.when(j < n_main)
    def _():
        o_ref[...] = wt_ref[...].T.astype(BF16)

    @pl.when(j == n_main)
    def _():
        n_ba = ba_ref.shape[0]
        ba = jnp.concatenate([ba_ref[...].T, jnp.zeros((D_MODEL, IN_WEIGHT_COLS - n_ba), F32)], axis=1)
        o_ref[...] = ba.astype(BF16)


def _in_weight(w_t):
    n_ba = 2 * GDN_HEADS
    split = GDN_CONV_CH + GDN_V
    tc = IN_WEIGHT_COLS
    n_main = COL_BA // tc

    def src_row(j):
        jj = jnp.minimum(j, n_main - 1)
        return pl.multiple_of(jnp.where(jj * tc < split, jj * tc, jj * tc + n_ba), n_ba)

    return pl.pallas_call(
        functools.partial(_in_weight_body, n_main),
        grid=(n_main + 1,),
        in_specs=[pl.BlockSpec((pl.Element(tc), pl.Element(D_MODEL)), lambda j: (src_row(j), 0)),
                  pl.BlockSpec((pl.Element(n_ba), pl.Element(D_MODEL)), lambda j: (split, 0))],
        out_specs=pl.BlockSpec((D_MODEL, tc), lambda j: (0, j)),
        out_shape=jax.ShapeDtypeStruct((D_MODEL, COL_BA + IN_WEIGHT_PAD), BF16),
        compiler_params=_params(("arbitrary",)),
        name="in_weight",
    )(w_t, w_t)


def _l2norm(x):
    return x * lax.rsqrt(jnp.sum(x * x, axis=-1, keepdims=True) + EPS)


def _inproj_body(seq_rows, tm, x_ref, sh_ref, sc_ref, nw_ref, w_ref, *rest):
    if seq_rows:
        n_cast = (len(rest) - 9) // 2
        cw_ref, cast_in = rest[0], rest[1:1 + n_cast]
        qkv_ref, gg_ref, ba_ref, sq_ref, skv_ref, gab_ref, tail_ref = rest[1 + n_cast:8 + n_cast]
        cast_out, xe_ref = rest[8 + n_cast:8 + 2 * n_cast], rest[-1]
        for src, dst in zip(cast_in, cast_out):
            dst[...] = src[...].astype(dst.dtype)

        @pl.when(pl.program_id(1) == 0)
        def _():
            xe_ref[:, 0:SUBLANES, :] = jnp.zeros((GDN_SECTIONS, SUBLANES, LANES), F32)
    else:
        qkv_ref, gg_ref, ba_ref, sq_ref, skv_ref, gab_ref = rest

    h = _rms(x_ref[...], nw_ref[...]) * (1.0 + sc_ref[...]) + sh_ref[...]
    hb = h.astype(BF16)

    def proj(lo, width):
        return jnp.dot(hb, w_ref[:, lo:lo + width], preferred_element_type=F32)

    step = 512
    per = step // LANES
    for c in range(GDN_CONV_CH // step):
        z = proj(COL_QKV + c * step, step)
        for k in range(per):
            s = c * per + k
            zs = z[:, k * LANES:(k + 1) * LANES]
            if not seq_rows:
                qkv_ref[:, s * LANES:(s + 1) * LANES] = zs
                continue
            xe_ref[s, SUBLANES:SUBLANES + tm, :] = zs
            w = cw_ref[s]
            for r0 in range(0, tm, CONV_ROWS):
                y = w[0:1] * xe_ref[s, r0 + SUBLANES - 3:r0 + SUBLANES - 3 + CONV_ROWS, :]
                for tap in range(1, GDN_CONV):
                    lo = r0 + SUBLANES - 3 + tap
                    y = y + w[tap:tap + 1] * xe_ref[s, lo:lo + CONV_ROWS, :]
                f = _silu(y)
                if s < GDN_HEADS:
                    f = _l2norm(f) * (GDN_DK ** -0.5)
                elif s < 2 * GDN_HEADS:
                    f = _l2norm(f)
                qkv_ref[s, r0:r0 + CONV_ROWS, :] = f
            xe_ref[s, 0:SUBLANES, :] = xe_ref[s, tm:tm + SUBLANES, :]
    if seq_rows:
        tail_ref[...] = xe_ref[:, 0:SUBLANES, :]
    for c in range(GDN_V // step):
        z = proj(COL_GATE + c * step, step)
        for k in range(per):
            zs = z[:, k * LANES:(k + 1) * LANES]
            if seq_rows:
                gg_ref[c * per + k] = _silu(zs)
            else:
                gg_ref[:, (c * per + k) * LANES:(c * per + k + 1) * LANES] = zs
    ba_ref[...] = proj(COL_BA, LANES)
    for c in range(SWA_Q // step):
        sq_ref[:, c * step:(c + 1) * step] = proj(COL_SQ + c * step, step)
    skv_ref[...] = proj(COL_SKV, 2 * SWA_KV)
    for c in range(2 * D_MODEL // step):
        gab_ref[:, c * step:(c + 1) * step] = proj(COL_GAB + c * step, step)


def _inproj(x, sh, sc, nw, w_all, cw, tm, cast_ws=()):
    b_, l_, _ = x.shape
    nt = l_ // tm
    r_ = sh.shape[1]
    rt = 1 if r_ == 1 else tm
    mod_map = (lambda b, t: (b, 0, 0)) if r_ == 1 else (lambda b, t: (b, t, 0))
    row_map = lambda b, t: (b, t, 0)
    head_map = lambda b, t: (b, 0, t, 0)
    seq_rows = cw is not None
    if seq_rows:
        gdn_shapes = (jax.ShapeDtypeStruct((b_, GDN_SECTIONS, l_, LANES), F32),
                      jax.ShapeDtypeStruct((b_, GDN_HEADS, l_, LANES), F32))
        gdn_specs = (pl.BlockSpec((None, GDN_SECTIONS, tm, LANES), head_map),
                     pl.BlockSpec((None, GDN_HEADS, tm, LANES), head_map))
    else:
        gdn_shapes = (jax.ShapeDtypeStruct((b_, l_, GDN_CONV_CH), F32),
                      jax.ShapeDtypeStruct((b_, l_, GDN_V), F32))
        gdn_specs = (pl.BlockSpec((None, tm, GDN_CONV_CH), row_map),
                     pl.BlockSpec((None, tm, GDN_V), row_map))
    out_shape = gdn_shapes + (
        jax.ShapeDtypeStruct((b_, l_, LANES), F32),
        jax.ShapeDtypeStruct((b_, l_, SWA_Q), F32),
        jax.ShapeDtypeStruct((b_, l_, 2 * SWA_KV), F32),
        jax.ShapeDtypeStruct((b_, l_, 2 * D_MODEL), F32),
    )
    out_specs = gdn_specs + (
        pl.BlockSpec((None, tm, LANES), row_map),
        pl.BlockSpec((None, tm, SWA_Q), row_map),
        pl.BlockSpec((None, tm, 2 * SWA_KV), row_map),
        pl.BlockSpec((None, tm, 2 * D_MODEL), row_map),
    )
    in_specs = [
        pl.BlockSpec((None, tm, D_MODEL), row_map),
        pl.BlockSpec((None, rt, D_MODEL), mod_map),
        pl.BlockSpec((None, rt, D_MODEL), mod_map),
        _const_spec(nw.shape),
        _const_spec(w_all.shape),
    ]
    args = [x, sh, sc, nw, w_all]
    scratch = []
    if seq_rows:
        in_specs.append(_const_spec(cw.shape))
        args.append(cw)
        out_shape += (jax.ShapeDtypeStruct((b_, GDN_SECTIONS, SUBLANES, LANES), F32),)
        out_specs += (pl.BlockSpec((None, GDN_SECTIONS, SUBLANES, LANES), lambda b, t: (b, 0, 0, 0)),)
        scratch.append(pltpu.VMEM((GDN_SECTIONS, tm + SUBLANES, LANES), F32))
        steps = b_ * nt
        for w in cast_ws:
            slab = pl.BlockSpec((w.shape[0] // steps, w.shape[1]), lambda b, t: (b * nt + t, 0))
            assert w.shape[0] % (steps * 2 * SUBLANES) == 0
            in_specs.append(slab)
            args.append(w)
            out_shape += (jax.ShapeDtypeStruct(w.shape, BF16),)
            out_specs += (slab,)
    return pl.pallas_call(
        functools.partial(_inproj_body, seq_rows, tm),
        grid=(b_, l_ // tm),
        in_specs=in_specs,
        out_specs=out_specs,
        out_shape=out_shape,
        scratch_shapes=scratch,
        compiler_params=_params(("arbitrary", "arbitrary")),
        name="inproj_seq" if seq_rows else "inproj_rows",
    )(*args)


def _delta_gates(ba, alog_row, dtb_row):
    beta_all = jax.nn.sigmoid(ba)
    g_all = -jnp.exp(alog_row) * _softplus(ba + dtb_row)
    return beta_all, g_all


def _lane_column(x, lane_idx, lane):
    return jnp.sum(jnp.where(lane_idx == lane, x, 0.0), axis=1, keepdims=True)


def _level_masks():
    r = np.arange(CHUNK)[:, None]
    c = np.arange(CHUNK)[None, :]
    masks = [(r == c + 1) & (r % 2 == 1)]
    half = 2
    while half < CHUNK:
        full = 2 * half
        masks.append((r // full == c // full) & (r % full >= half) & (c % full < half))
        half = full
    return jnp.asarray(np.stack(masks), dtype=BF16)


def _unit_lower_inverses(ms, masks_ref, eye, between_levels=()):
    ts = [eye - m * masks_ref[0] for m in ms]
    pending = list(between_levels)
    for lvl in range(1, masks_ref.shape[0]):
        off = masks_ref[lvl]
        xs = [jnp.dot(m * off, t, preferred_element_type=F32).astype(BF16) for m, t in zip(ms, ts)]
        ys = [jnp.dot(t, x, preferred_element_type=F32).astype(BF16) for t, x in zip(ts, xs)]
        ts = [t - y for t, y in zip(ts, ys)]
        if pending:
            pending.pop(0)()
    for piece in pending:
        piece()
    return ts


def _cumsum_rows(g, ltri):
    hi = g.astype(BF16)
    r1 = g - hi.astype(F32)
    mid = r1.astype(BF16)
    lo = (r1 - mid.astype(F32)).astype(BF16)
    return (jnp.dot(ltri, hi, preferred_element_type=F32) + jnp.dot(ltri, mid, preferred_element_type=F32)
            + jnp.dot(ltri, lo, preferred_element_type=F32))


def _gated_out_norm(o, gate_act, onw):
    on = o * lax.rsqrt(jnp.mean(o * o, axis=-1, keepdims=True) + EPS) * onw
    return on * gate_act


def _run_all(pieces):
    for piece in pieces:
        piece()


def _gdn_prompt_body(lt, q_ref, k_ref, v_ref, ba_ref, alog_ref, dtb_ref, gate_ref, onw_ref, masks_ref,
                     f_ref, s0_ref, sq_ref, kvn_ref, ck_ref, cv_ref, sink_ref,
                     og_ref, s_ref, sn_ref, read_ref, so_ref, nk_ref, nv_ref):
    per_step = s0_ref.shape[0]
    first = (pl.program_id(0) * pl.num_programs(1) + pl.program_id(1)) * per_step
    state_work = [functools.partial(_gdn_state_update, f_ref, s0_ref, sn_ref, read_ref, i, h)
                  for i in range(per_step) for h in range(GDN_HEADS)]
    attn_work = [piece for i in range(per_step)
                 for piece in _swa_step_pieces(first + i, i, sq_ref, kvn_ref, ck_ref, cv_ref, sink_ref,
                                               so_ref, nk_ref, nv_ref)]

    @pl.when(pl.program_id(1) == 0)
    def _():
        s_ref[...] = jnp.zeros_like(s_ref)

    beta_all, g_all = _delta_gates(ba_ref[...], alog_ref[...], dtb_ref[...])
    lane_idx = lax.broadcasted_iota(jnp.int32, (CHUNK, LANES), 1)
    row = lax.broadcasted_iota(jnp.int32, (CHUNK, CHUNK), 0)
    col = lax.broadcasted_iota(jnp.int32, (CHUNK, CHUNK), 1)
    tril = row >= col
    strict = row > col
    ltri = jnp.where(tril, 1.0, 0.0).astype(BF16)
    eye = jnp.where(row == col, 1.0, 0.0).astype(BF16)
    onw = onw_ref[...]
    heads = range(GDN_HEADS)
    chunks = range(lt // CHUNK)

    blocks = [(c, j) for c in chunks for j in heads]
    pre = {}
    for c in chunks:
        rows = slice(c * CHUNK, (c + 1) * CHUNK)
        dec = _cumsum_rows(g_all[rows], ltri)
        dec_t = dec.T
        for j in heads:
            q, k, v = q_ref[j, rows, :], k_ref[j, rows, :], v_ref[j, rows, :]
            beta_col = _lane_column(beta_all[rows], lane_idx, j)
            dec_col = _lane_column(dec, lane_idx, GDN_HEADS + j)
            dec_row = dec_t[GDN_HEADS + j:GDN_HEADS + j + 1, :]
            dec_last = dec_row[:, CHUNK - 1:CHUNK]
            gam = jnp.exp(jnp.minimum(dec_col - dec_row, 0.0))
            e_col = jnp.exp(dec_col)
            kb = k * beta_col
            pre[c, j] = dict(q=q, k=k, gam=gam, kb=kb, qe=q * e_col, e_last=jnp.exp(dec_last),
                             kd=k * jnp.exp(dec_last - dec_col),
                             rhs=jnp.concatenate([v * beta_col, kb * e_col], axis=1).astype(BF16))
    a_intra, uw = {}, {}

    def recurrence(c):
        rows = slice(c * CHUNK, (c + 1) * CHUNK)
        mid = {}

        def read_out():
            mid["s"] = [s_ref[j] for j in heads]
            mid["ws_qs"] = [_mm(jnp.concatenate([uw[c, j][:, GDN_DV:], pre[c, j]["qe"]], axis=0), mid["s"][j])
                            for j in heads]

        def update():
            s_prev, ws_qs = mid["s"], mid["ws_qs"]
            v_new = [uw[c, j][:, :GDN_DV] - ws_qs[j][:CHUNK] for j in heads]
            o = [ws_qs[j][CHUNK:] + _mm(a_intra[c, j], v_new[j]) for j in heads]
            s_new = [s_prev[j] * pre[c, j]["e_last"] + _mm(pre[c, j]["kd"].T, v_new[j]) for j in heads]
            for j in heads:
                s_ref[j] = s_new[j]
                og = _gated_out_norm(o[j], gate_ref[j, rows, :], onw)
                og_ref[rows, j * GDN_DV:(j + 1) * GDN_DV] = og.astype(og_ref.dtype)

        return [read_out, update]

    n_slots = -(-len(blocks) // GDN_GROUP) * (masks_ref.shape[0] - 1)
    state_per_slot = -(-len(state_work) // n_slots)
    attn_per_slot = -(-len(attn_work) // n_slots)
    carried = []
    for g0 in range(0, len(blocks), GDN_GROUP):
        grp = blocks[g0:g0 + GDN_GROUP]
        grams = [_mm_nt(jnp.concatenate([pre[b]["kb"], pre[b]["q"]], axis=0), pre[b]["k"]) for b in grp]
        ms = [jnp.where(strict, g[:CHUNK] * pre[b]["gam"], 0.0).astype(BF16) for g, b in zip(grams, grp)]
        a_intra.update({b: jnp.where(tril, g[CHUNK:] * pre[b]["gam"], 0.0) for g, b in zip(grams, grp)})
        slots = []
        for _ in range(masks_ref.shape[0] - 1):
            work = carried[:1] + state_work[:state_per_slot] + attn_work[:attn_per_slot]
            carried = carried[1:]
            state_work, attn_work = state_work[state_per_slot:], attn_work[attn_per_slot:]
            slots.append(functools.partial(_run_all, work))
        t_inv = _unit_lower_inverses(ms, masks_ref, eye, slots)
        _run_all(carried)
        uw.update({b: jnp.dot(t, pre[b]["rhs"], preferred_element_type=F32) for t, b in zip(t_inv, grp)})
        carried = [piece for c in sorted({c for c, _ in grp}) for piece in recurrence(c)]
    _run_all(carried + state_work + attn_work)


def _gdn_prompt(qkvf, gact, ba, alog_row, dtb_row, onw, masks, step_feats, step_state, step_attn, lt):
    b_, _, l_, _ = qkvf.shape
    nt = l_ // lt
    n_ = step_state.shape[0]
    per_step = n_ // (b_ * nt)
    assert per_step * b_ * nt == n_ and n_ <= WINDOW
    q3, kvn_t, ck_t, cv_t, sink_col = step_attn
    sec = lambda s: pl.BlockSpec((None, GDN_HEADS, lt, LANES), lambda b, t, s=s: (b, s, t, 0))
    seq_map = lambda b, t: (b * nt + t, 0, 0)
    seq_block = lambda *dims: pl.BlockSpec((per_step,) + dims, lambda b, t: (b * nt + t,) + (0,) * len(dims))
    state_spec = seq_block(GDN_HEADS, GDN_DK, GDN_DV)
    cache_spec = seq_block(SWA_KV, WINDOW)
    return pl.pallas_call(
        functools.partial(_gdn_prompt_body, lt),
        grid=(b_, nt),
        in_specs=[sec(0), sec(1), sec(2),
                  pl.BlockSpec((None, lt, LANES), lambda b, t: (b, t, 0)),
                  _const_spec(alog_row.shape), _const_spec(dtb_row.shape),
                  pl.BlockSpec((None, GDN_HEADS, lt, LANES), lambda b, t: (b, 0, t, 0)),
                  _const_spec(onw.shape), _const_spec(masks.shape),
                  seq_block(STEP_FEATURES * GDN_HEADS, LANES), state_spec,
                  seq_block(SWA_Q_HEADS, SWA_HD), _const_spec(kvn_t.shape), cache_spec, cache_spec,
                  _const_spec(sink_col.shape)],
        out_specs=(pl.BlockSpec((None, lt, GDN_V), lambda b, t: (b, t, 0)),
                   pl.BlockSpec((None, GDN_HEADS, GDN_DK, GDN_DV), lambda b, t: (b, 0, 0, 0)),
                   state_spec, seq_block(GDN_HEADS, LANES),
                   seq_block(SWA_Q_HEADS, SWA_HD), cache_spec, cache_spec),
        out_shape=(jax.ShapeDtypeStruct((b_, l_, GDN_V), BF16),
                   jax.ShapeDtypeStruct((b_, GDN_HEADS, GDN_DK, GDN_DV), F32),
                   jax.ShapeDtypeStruct(step_state.shape, F32),
                   jax.ShapeDtypeStruct((n_, GDN_HEADS, LANES), F32),
                   jax.ShapeDtypeStruct(q3.shape, F32),
                   jax.ShapeDtypeStruct(ck_t.shape, F32),
                   jax.ShapeDtypeStruct(cv_t.shape, F32)),
        compiler_params=_params(("arbitrary", "arbitrary")),
        name="gdn_prompt",
    )(qkvf, qkvf, qkvf, ba, alog_row, dtb_row, gact, onw, masks, step_feats, step_state,
      q3, kvn_t, ck_t, cv_t, sink_col)


STEP_FEATURES = 5


def _gdn_step_features_body(bb, x_ref, st_ref, cw_ref, ba_ref, alog_ref, dtb_ref, f_ref):
    beta_all, g_all = _delta_gates(ba_ref[...], alog_ref[...], dtb_ref[...])
    lane_idx = lax.broadcasted_iota(jnp.int32, (bb, LANES), 1)
    for h in range(GDN_HEADS):
        feats = []
        for s in range(3):
            idx = s * GDN_HEADS + h
            cols = slice(idx * LANES, (idx + 1) * LANES)
            w = cw_ref[idx]
            y = w[0:1] * st_ref[0, :, cols]
            for tap in range(1, GDN_CONV - 1):
                y = y + w[tap:tap + 1] * st_ref[tap, :, cols]
            y = y + w[GDN_CONV - 1:GDN_CONV] * x_ref[:, cols]
            feats.append(_silu(y))
        q, k, v = feats
        f_ref[h] = _l2norm(q) * (GDN_DK ** -0.5)
        f_ref[GDN_HEADS + h] = _l2norm(k)
        f_ref[2 * GDN_HEADS + h] = v
        f_ref[3 * GDN_HEADS + h] = jnp.broadcast_to(_lane_column(beta_all, lane_idx, h), (bb, LANES))
        f_ref[4 * GDN_HEADS + h] = jnp.broadcast_to(
            jnp.exp(_lane_column(g_all, lane_idx, h + GDN_HEADS)), (bb, LANES))


def _gdn_step_features(qkv, st, ba, cw, alog_row, dtb_row, bb):
    n_ = ba.shape[0]
    return pl.pallas_call(
        functools.partial(_gdn_step_features_body, bb),
        grid=(n_ // bb,),
        in_specs=[pl.BlockSpec((bb, GDN_CONV_CH), lambda i: (i, 0)),
                  pl.BlockSpec((GDN_CONV - 1, bb, GDN_CONV_CH), lambda i: (0, i, 0)),
                  _const_spec(cw.shape),
                  pl.BlockSpec((bb, LANES), lambda i: (i, 0)),
                  _const_spec(alog_row.shape), _const_spec(dtb_row.shape)],
        out_specs=pl.BlockSpec((STEP_FEATURES * GDN_HEADS, bb, LANES), lambda i: (0, i, 0)),
        out_shape=jax.ShapeDtypeStruct((STEP_FEATURES * GDN_HEADS, n_, LANES), F32),
        compiler_params=_params(("arbitrary",)),
        name="gdn_step_features",
    )(qkv, st, cw, ba, alog_row, dtb_row)


def _gdn_state_update(f_ref, s0_ref, sn_ref, o_ref, i, h):
    eye = (lax.broadcasted_iota(jnp.int32, (GDN_DK, GDN_DK), 0)
           == lax.broadcasted_iota(jnp.int32, (GDN_DK, GDN_DK), 1))

    def row(kind):
        return f_ref[i, kind * GDN_HEADS + h:kind * GDN_HEADS + h + 1, :]

    def to_col(r):
        return jnp.sum(jnp.where(eye, jnp.broadcast_to(r, (GDN_DK, GDN_DK)), 0.0), axis=1, keepdims=True)

    q_row, k_row, v_row, beta, decay = (row(kind) for kind in range(STEP_FEATURES))
    k_col = to_col(k_row)
    s1 = s0_ref[i, h] * decay
    delta = (v_row - jnp.sum(s1 * k_col, axis=0, keepdims=True)) * beta
    s2 = s1 + k_col * delta
    sn_ref[i, h] = s2
    o_ref[i, h:h + 1, :] = jnp.sum(s2 * to_col(q_row), axis=0, keepdims=True)


def _swa_prompt_body(nq, sinks_ref, q_ref, kvp_ref, kvc_ref, o_ref):
    n = pl.program_id(1)
    w = WINDOW
    tiles = SWA_KV // LANES
    pairs = 2
    lo_lane = lax.broadcasted_iota(jnp.int32, (w, LANES), 1) < SWA_HD
    lo_row = lax.broadcasted_iota(jnp.int32, (LANES, w), 0) < SWA_HD
    c = lax.broadcasted_iota(jnp.int32, (w, pairs * w), 0)
    i = lax.broadcasted_iota(jnp.int32, (w, pairs * w), 1) & (w - 1)
    from_prev = c > i
    k_blk, vt_blk = [], []
    for j in range(nq + 1):
        src, rows = (kvp_ref, slice(0, w)) if j == 0 else (kvc_ref, slice((j - 1) * w, j * w))
        k_tiles, vt_tiles = [], []
        for t in range(tiles):
            kx = src[rows, t * LANES:(t + 1) * LANES]
            vt = src[rows, SWA_KV + t * LANES:SWA_KV + (t + 1) * LANES].T
            k_tiles.append((kx.astype(BF16), pltpu.roll(kx, SWA_HD, axis=1).astype(BF16)))
            vt_tiles.append((vt.astype(BF16),
                             jnp.concatenate([vt[SWA_HD:], vt[:SWA_HD]], axis=0).astype(BF16)))
        k_blk.append(k_tiles)
        vt_blk.append(vt_tiles)
    items = [(qb, g, p) for qb in range(nq) for g in range(SWA_KV_HEADS) for p in range(2)]
    log2e = math.log2(math.e)
    qm, kz, vzt, sink = {}, {}, {}, {}
    for qb, g, p in items:
        keep = lo_lane if p == 0 else jnp.logical_not(lo_lane)
        q_tiles = [q_ref[qb * w:(qb + 1) * w, (2 * g + r) * LANES:(2 * g + r + 1) * LANES] for r in range(pairs)]
        qm[qb, g, p] = jnp.concatenate([jnp.where(keep, x * SWA_Q_SCALE, 0.0) for x in q_tiles],
                                       axis=0).astype(BF16)
        variant = 0 if p == g % 2 else 1
        kz[qb, g, p] = jnp.concatenate([k_blk[qb + d][g // 2][variant] for d in range(2)], axis=0)
        vzt[qb, g, p] = jnp.concatenate([vt_blk[qb + d][g // 2][variant] for d in range(2)], axis=1)
        sink[qb, g, p] = jnp.concatenate([jnp.full((1, w), sinks_ref[SWA_GROUP * g + 2 * r + p] * log2e, F32)
                                          for r in range(pairs)], axis=1)
    st = {b: lax.dot_general(kz[b], qm[b], (((1,), (1,)), ((), ())), preferred_element_type=F32) for b in items}
    prev = {b: jnp.where(n > 0, st[b][:w], -jnp.inf) if b[0] == 0 else st[b][:w] for b in items}
    u = {b: jnp.where(from_prev, prev[b], st[b][w:]) for b in items}
    m = {b: jnp.maximum(jnp.max(u[b], axis=0, keepdims=True), sink[b]) for b in items}
    eu = {b: jnp.exp2(u[b] - m[b]) for b in items}
    den = {b: jnp.sum(eu[b], axis=0, keepdims=True) + jnp.exp2(sink[b] - m[b]) for b in items}
    et = {b: jnp.concatenate([jnp.where(from_prev, eu[b], 0.0), jnp.where(from_prev, 0.0, eu[b])],
                             axis=0).astype(BF16) for b in items}
    ot = {b: jnp.dot(vzt[b], et[b], preferred_element_type=F32) / den[b] for b in items}
    for qb in range(nq):
        for g in range(SWA_KV_HEADS):
            for r in range(pairs):
                cols = slice(r * w, (r + 1) * w)
                tile_t = jnp.where(lo_row, ot[qb, g, 0][:, cols], ot[qb, g, 1][:, cols])
                o_ref[qb * w:(qb + 1) * w, (2 * g + r) * LANES:(2 * g + r + 1) * LANES] = (
                    tile_t.T.astype(o_ref.dtype))


def _swa_prompt(sq, skv, sinks, nq):
    b_, l_, _ = sq.shape
    rows = nq * WINDOW
    return pl.pallas_call(
        functools.partial(_swa_prompt_body, nq),
        grid=(b_, l_ // rows),
        in_specs=[pl.BlockSpec(memory_space=pltpu.SMEM),
                  pl.BlockSpec((None, rows, SWA_Q), lambda b, n: (b, n, 0)),
                  pl.BlockSpec((None, WINDOW, 2 * SWA_KV), lambda b, n: (b, jnp.maximum(n * nq - 1, 0), 0)),
                  pl.BlockSpec((None, rows, 2 * SWA_KV), lambda b, n: (b, n, 0))],
        out_specs=pl.BlockSpec((None, rows, SWA_Q), lambda b, n: (b, n, 0)),
        out_shape=jax.ShapeDtypeStruct((b_, l_, SWA_Q), BF16),
        compiler_params=_params(("arbitrary", "arbitrary")),
        name="swa_prompt",
    )(sinks, sq, skv, skv)


def _swa_step_pieces(seq, i, q_ref, kvn_ref, ck_ref, cv_ref, sink_ref, o_ref, nk_ref, nv_ref):
    w = WINDOW
    mid = {}

    def own():
        row = lax.broadcasted_iota(jnp.int32, (SWA_Q_HEADS, SWA_KV), 0)
        lane = lax.broadcasted_iota(jnp.int32, (SWA_Q_HEADS, SWA_KV), 1)
        return (lane // SWA_HD) == (row // SWA_GROUP)

    def append_and_score():
        newest = lax.broadcasted_iota(jnp.int32, (SWA_KV, w), 1) == w - 1
        mid["nk"] = jnp.where(newest, pltpu.roll(kvn_ref[0:SWA_KV, :], w - 1 - seq, axis=1),
                              pltpu.roll(ck_ref[i], w - 1, axis=1))
        mid["nv"] = jnp.where(newest, pltpu.roll(kvn_ref[SWA_KV:2 * SWA_KV, :], w - 1 - seq, axis=1),
                              pltpu.roll(cv_ref[i], w - 1, axis=1))
        q_bd = jnp.where(own(), jnp.concatenate([q_ref[i]] * SWA_KV_HEADS, axis=1), 0.0)
        mid["s"] = _mm(q_bd, mid["nk"]) * (SWA_HD ** -0.5)

    def values():
        sink = sink_ref[...]
        m = jnp.maximum(jnp.max(mid["s"], axis=1, keepdims=True), sink)
        e = jnp.exp(mid["s"] - m)
        den = jnp.sum(e, axis=1, keepdims=True) + jnp.exp(sink - m)
        mid["pv"] = _mm_nt(e / den, mid["nv"])

    def write_out():
        pv = jnp.where(own(), mid["pv"], 0.0)
        o = pv[:, 0:SWA_HD]
        for g in range(1, SWA_KV_HEADS):
            o = o + pv[:, g * SWA_HD:(g + 1) * SWA_HD]
        o_ref[i] = o
        nk_ref[i] = mid["nk"]
        nv_ref[i] = mid["nv"]

    return [append_and_score, values, write_out]


def _dense_body(stateful, tm, og_ref, ob_ref, gab_ref, x_ref, gt1_ref, sh2_ref, sc2_ref, gt2_ref,
                n2w_ref, fnw_ref, wa_ref, wb_ref, wo_ref, wg_ref, wu_ref, cw_ref, cb_ref, wd_ref, *rest):
    if stateful:
        st_ref, gate_ref, onw_ref, y_ref, gout_ref, act_ref = rest
        onw = onw_ref[...]
        og = jnp.concatenate(
            [_gated_out_norm(og_ref[h], _silu(gate_ref[:, h * GDN_DV:(h + 1) * GDN_DV]), onw).astype(BF16)
             for h in range(GDN_HEADS)], axis=1)
    else:
        y_ref, gout_ref, act_ref, gbuf_ref, carry_ref = rest
        og = og_ref[...]

        @pl.when(pl.program_id(1) == 0)
        def _():
            carry_ref[...] = jnp.zeros_like(carry_ref)

    y_a = jnp.dot(og, wa_ref[...], preferred_element_type=F32)
    y_b = jnp.dot(ob_ref[...], wb_ref[...], preferred_element_type=F32)
    merged = (jax.nn.sigmoid(gab_ref[:, 0:D_MODEL]) * y_a
              + jax.nn.sigmoid(gab_ref[:, D_MODEL:2 * D_MODEL]) * y_b)
    x1 = x_ref[...] + gt1_ref[...] * _mm(merged, wo_ref[...])
    h2 = (_rms(x1, n2w_ref[...]) * (1.0 + sc2_ref[...]) + sh2_ref[...]).astype(BF16)

    for c in range(D_FF // FFN_COLS):
        cols = slice(c * FFN_COLS, (c + 1) * FFN_COLS)
        gate = jnp.dot(h2, wg_ref[:, cols], preferred_element_type=F32)
        up = jnp.dot(h2, wu_ref[:, cols], preferred_element_type=F32)
        if stateful:
            g2 = st_ref[0, :, cols]
            g1 = st_ref[1, :, cols]
            gout_ref[:, cols] = gate
        else:
            gbuf_ref[0:SUBLANES, :] = carry_ref[:, cols]
            gbuf_ref[SUBLANES:SUBLANES + tm, :] = gate
            g2 = gbuf_ref[SUBLANES - 2:SUBLANES - 2 + tm, :]
            g1 = gbuf_ref[SUBLANES - 1:SUBLANES - 1 + tm, :]
            carry_ref[:, cols] = gbuf_ref[tm:tm + SUBLANES, :]
        gc = (cw_ref[0:1, cols] * g2 + cw_ref[1:2, cols] * g1 + cw_ref[2:3, cols] * gate) + cb_ref[:, cols]
        act_ref[:, cols] = (_silu(gc) * up).astype(BF16)
    if not stateful:
        gout_ref[...] = carry_ref[...]

    x2 = x1 + gt2_ref[...] * jnp.dot(act_ref[...], wd_ref[...], preferred_element_type=F32)
    y_ref[...] = _rms(x2, fnw_ref[...])


def _dense(og, ob, gab, x, mods, vecs, ws, st, tm, step=None):
    b_, l_, _ = x.shape
    r_ = mods[0].shape[1]
    rt = 1 if r_ == 1 else tm
    nt = l_ // tm
    mod_map = (lambda b, t: (b, 0, 0)) if r_ == 1 else (lambda b, t: (b, t, 0))
    row_map = lambda b, t: (b, t, 0)
    stateful = st is not None
    og_spec = (pl.BlockSpec((GDN_HEADS, tm, LANES), lambda b, t: (0, t, 0)) if stateful
               else pl.BlockSpec((None, tm, D_MODEL), row_map))
    in_specs = ([og_spec,
                 pl.BlockSpec((None, tm, D_MODEL), row_map),
                 pl.BlockSpec((None, tm, 2 * D_MODEL), row_map),
                 pl.BlockSpec((None, tm, D_MODEL), row_map)]
                + [pl.BlockSpec((None, rt, D_MODEL), mod_map)] * 4
                + [_const_spec(a.shape) for a in vecs[:2]]
                + [_const_spec(ws[0].shape), _const_spec(ws[1].shape), _const_spec(ws[2].shape),
                   _const_spec(ws[3].shape), _const_spec(ws[4].shape),
                   _const_spec(vecs[2].shape), _const_spec(vecs[3].shape), _const_spec(ws[5].shape)])
    args = [og, ob, gab, x, *mods, vecs[0], vecs[1], ws[0], ws[1], ws[2], ws[3], ws[4], vecs[2], vecs[3], ws[5]]
    scratch = [pltpu.VMEM((tm, D_FF), BF16)]
    out_specs = [pl.BlockSpec((None, tm, D_MODEL), row_map)]
    out_shape = [jax.ShapeDtypeStruct((b_, l_, D_MODEL), F32)]
    if stateful:
        gate, onw = step
        in_specs += [pl.BlockSpec((FFN_CONV - 1, None, tm, D_FF), lambda b, t: (0, b, t, 0)),
                     pl.BlockSpec((None, tm, GDN_V), row_map), _const_spec(onw.shape)]
        args += [st, gate, onw]
        out_shape.append(jax.ShapeDtypeStruct((b_, l_, D_FF), F32))
        out_specs.append(pl.BlockSpec((None, tm, D_FF), row_map))
    else:
        scratch += [pltpu.VMEM((tm + SUBLANES, FFN_COLS), F32), pltpu.VMEM((SUBLANES, D_FF), F32)]
        out_shape.append(jax.ShapeDtypeStruct((b_, SUBLANES, D_FF), F32))
        out_specs.append(pl.BlockSpec((None, SUBLANES, D_FF), lambda b, t: (b, 0, 0)))
    return pl.pallas_call(
        functools.partial(_dense_body, stateful, tm),
        grid=(b_, nt),
        in_specs=in_specs,
        out_specs=tuple(out_specs),
        out_shape=tuple(out_shape),
        scratch_shapes=scratch,
        compiler_params=_params(("arbitrary", "arbitrary")),
        name="dense_step" if stateful else "dense_prompt",
    )(*args)


def _lane_row(values, offset):
    return jnp.zeros((1, LANES), F32).at[0, offset:offset + values.shape[0]].set(values)


def kernel(x_prompt, x_sample, c_prompt, c_sample, state_gdn_S, state_gdn_conv, cache_swa_k, cache_swa_v,
           state_ffn_conv, w_mod, b_mod, norm1_w, norm2_w, w_in, gdn_conv_w, gdn_a_log, gdn_dt_bias,
           gdn_onorm_w, w_gdn_out, swa_sinks, w_swa_out, w_o, w_ffn_gate, w_ffn_up, ffn_conv_w, ffn_conv_b,
           w_ffn_down, final_norm_w):
    assert w_mod.shape[0] == 1, "single-layer trunk"
    nb, seq, _ = x_prompt.shape
    ns = x_sample.shape[0]
    assert x_sample.shape[1] == 1

    in_ws = _in_weight(jnp.transpose(w_in[0]))
    dense_vecs = (norm2_w, final_norm_w[None, :], ffn_conv_w[0], ffn_conv_b)
    cw = jnp.transpose(gdn_conv_w[0].reshape(GDN_CONV, GDN_SECTIONS, LANES), (1, 0, 2))
    alog_row = _lane_row(gdn_a_log[0], GDN_HEADS)
    dtb_row = _lane_row(gdn_dt_bias[0], GDN_HEADS)

    mod = _modulation(jnp.concatenate([c_prompt, c_sample], axis=0), w_mod[0], b_mod)
    mod_p = [mod[:nb, i * D_MODEL:(i + 1) * D_MODEL][:, None, :] for i in range(6)]
    mod_s = [mod[nb:, i * D_MODEL:(i + 1) * D_MODEL][None, :, :] for i in range(6)]

    xs = x_sample.reshape(1, ns, D_MODEL)
    qkvs, gates, bas, sqs, skvs, gabs = _inproj(xs, mod_s[0], mod_s[1], norm1_w, in_ws, None, tm=ns)
    st_gdn = jnp.transpose(state_gdn_conv[0], (1, 0, 2))
    step_feats = jnp.transpose(_gdn_step_features(qkvs[0], st_gdn, bas[0], cw, alog_row, dtb_row, bb=ns), (1, 0, 2))
    to_channel_major = lambda c: jnp.transpose(c, (0, 2, 3, 1)).reshape(ns, SWA_KV, WINDOW)
    from_channel_major = lambda c: jnp.transpose(c.reshape(ns, SWA_KV_HEADS, SWA_HD, WINDOW), (0, 3, 1, 2))
    kvn_t = jnp.pad(jnp.transpose(skvs[0]), ((0, 0), (0, WINDOW - ns)))
    step_attn = (sqs[0].reshape(ns, SWA_Q_HEADS, SWA_HD), kvn_t, to_channel_major(cache_swa_k[0]),
                 to_channel_major(cache_swa_v[0]), swa_sinks[0][:, None])

    later_ws = (w_gdn_out[0], w_swa_out[0], w_o[0], w_ffn_gate[0], w_ffn_up[0])
    qkvf, gact, ba, sq, skv, gab, qkv_tail, *later_bf16 = _inproj(
        x_prompt, mod_p[0], mod_p[1], norm1_w, in_ws, cw, tm=256, cast_ws=later_ws)
    dense_ws = (*later_bf16, w_ffn_down[0].astype(BF16))
    og, gdn_s_p, gdn_s_s, step_read, o3, k_s, v_s = _gdn_prompt(
        qkvf, gact, ba, alog_row, dtb_row, gdn_onorm_w, _level_masks(), step_feats, state_gdn_S[0], step_attn,
        lt=4 * CHUNK)
    ob = _swa_prompt(sq, skv, swa_sinks[0], nq=4)
    y_p, gate_tail = _dense(og, ob, gab, x_prompt, (mod_p[2], mod_p[3], mod_p[4], mod_p[5]),
                            dense_vecs, dense_ws, None, tm=512)
    gdn_conv_p = jnp.transpose(qkv_tail[:, :, SUBLANES - (GDN_CONV - 1):, :], (0, 2, 1, 3)).reshape(
        nb, GDN_CONV - 1, GDN_CONV_CH)
    k_p = skv[:, seq - WINDOW:, :SWA_KV].reshape(nb, WINDOW, SWA_KV_HEADS, SWA_HD)
    v_p = skv[:, seq - WINDOW:, SWA_KV:].reshape(nb, WINDOW, SWA_KV_HEADS, SWA_HD)
    ffn_conv_p = gate_tail[:, SUBLANES - (FFN_CONV - 1):, :]

    ob_s = o3.reshape(1, ns, SWA_Q).astype(BF16)
    st_ffn = jnp.transpose(state_ffn_conv[0], (1, 0, 2))[:, None]
    y_s, gate_new = _dense(jnp.transpose(step_read, (1, 0, 2)), ob_s, gabs, xs,
                           (mod_s[2], mod_s[3], mod_s[4], mod_s[5]), dense_vecs, dense_ws, st_ffn, tm=ns,
                           step=(gates, gdn_onorm_w))
    gdn_conv_s = jnp.concatenate([state_gdn_conv[0][:, 1:], qkvs[0][:, None, :]], axis=1)
    ffn_conv_s = jnp.concatenate([state_ffn_conv[0][:, 1:], gate_new[0][:, None, :]], axis=1)

    return (y_p, y_s.reshape(ns, 1, D_MODEL),
            gdn_s_p[None], gdn_s_s[None],
            gdn_conv_p[None], gdn_conv_s[None],
            k_p[None], from_channel_major(k_s)[None],
            v_p[None], from_channel_major(v_s)[None],
            ffn_conv_p[None], ffn_conv_s[None])
```

```python
import functools
import math

import numpy as np
import jax
import jax.numpy as jnp
from jax import lax
from jax.experimental import pallas as pl
from jax.experimental.pallas import tpu as pltpu

F32 = jnp.float32
BF16 = jnp.bfloat16

D_MODEL = 1024
GDN_HEADS = 8
GDN_DK = 128
GDN_DV = 128
GDN_QK = GDN_HEADS * GDN_DK
GDN_V = GDN_HEADS * GDN_DV
GDN_CONV = 4
GDN_CONV_CH = 2 * GDN_QK + GDN_V
GDN_SECTIONS = GDN_CONV_CH // 128
SWA_Q_HEADS = 16
SWA_KV_HEADS = 4
SWA_GROUP = SWA_Q_HEADS // SWA_KV_HEADS
SWA_HD = 64
SWA_Q = SWA_Q_HEADS * SWA_HD
SWA_KV = SWA_KV_HEADS * SWA_HD
WINDOW = 128
D_FF = 2816
FFN_CONV = 3
EPS = 1e-6

LANES = 128
SUBLANES = 8
VMEM_LIMIT = 56 * 1024 * 1024

COL_QKV = 0
COL_GATE = COL_QKV + GDN_CONV_CH
COL_SQ = COL_GATE + GDN_V
COL_SKV = COL_SQ + SWA_Q
COL_GAB = COL_SKV + 2 * SWA_KV
COL_BA = COL_GAB + 2 * D_MODEL
IN_COLS = COL_BA + LANES

SWA_Q_SCALE = SWA_HD ** -0.5 * math.log2(math.e)

CONV_ROWS = 64
CHUNK = 128
GDN_GROUP = 16
FFN_COLS = 256


def _mm(a, b):
    return jnp.dot(a.astype(BF16), b.astype(BF16), preferred_element_type=F32)


def _mm_nt(a, b):
    return lax.dot_general(a.astype(BF16), b.astype(BF16), (((1,), (1,)), ((), ())),
                           preferred_element_type=F32)


def _silu(x):
    return x * jax.nn.sigmoid(x)


def _softplus(x):
    return jnp.maximum(x, 0.0) + jnp.log1p(jnp.exp(-jnp.abs(x)))


def _rms(x, w):
    return x * lax.rsqrt(jnp.mean(x * x, axis=-1, keepdims=True) + EPS) * w


def _const_spec(shape):
    n = len(shape)
    return pl.BlockSpec(shape, lambda *_: (0,) * n, pipeline_mode=pl.Buffered(1))


def _params(sem):
    return pltpu.CompilerParams(dimension_semantics=sem, vmem_limit_bytes=VMEM_LIMIT)


def _mod_body(c_ref, w_ref, b_ref, o_ref):
    o_ref[...] = _mm(_silu(c_ref[...]), w_ref[...]) + b_ref[...]


def _modulation(c_all, w_mod, b_mod):
    rows = c_all.shape[0]
    n_out = w_mod.shape[1]
    tn = D_MODEL
    return pl.pallas_call(
        _mod_body,
        grid=(n_out // tn,),
        in_specs=[pl.BlockSpec((rows, D_MODEL), lambda j: (0, 0)),
                  pl.BlockSpec((D_MODEL, tn), lambda j: (0, j)),
                  pl.BlockSpec((1, tn), lambda j: (0, j))],
        out_specs=pl.BlockSpec((rows, tn), lambda j: (0, j)),
        out_shape=jax.ShapeDtypeStruct((rows, n_out), F32),
        compiler_params=_params(("arbitrary",)),
        name="modulation",
    )(c_all, w_mod, b_mod)


IN_WEIGHT_COLS = 512
IN_WEIGHT_PAD = 256


def _in_weight_body(n_main, wt_ref, ba_ref, o_ref):
    j = pl.program_id(0)

    @pl.when(j < n_main)
    def _():
        o_ref[...] = wt_ref[...].T.astype(BF16)

    @pl.when(j == n_main)
    def _():
        n_ba = ba_ref.shape[0]
        ba = jnp.concatenate([ba_ref[...].T, jnp.zeros((D_MODEL, IN_WEIGHT_COLS - n_ba), F32)], axis=1)
        o_ref[...] = ba.astype(BF16)


def _in_weight(w_t):
    n_ba = 2 * GDN_HEADS
    split = GDN_CONV_CH + GDN_V
    tc = IN_WEIGHT_COLS
    n_main = COL_BA // tc

    def src_row(j):
        jj = jnp.minimum(j, n_main - 1)
        return pl.multiple_of(jnp.where(jj * tc < split, jj * tc, jj * tc + n_ba), n_ba)

    return pl.pallas_call(
        functools.partial(_in_weight_body, n_main),
        grid=(n_main + 1,),
        in_specs=[pl.BlockSpec((pl.Element(tc), pl.Element(D_MODEL)), lambda j: (src_row(j), 0)),
                  pl.BlockSpec((pl.Element(n_ba), pl.Element(D_MODEL)), lambda j: (split, 0))],
        out_specs=pl.BlockSpec((D_MODEL, tc), lambda j: (0, j)),
        out_shape=jax.ShapeDtypeStruct((D_MODEL, COL_BA + IN_WEIGHT_PAD), BF16),
        compiler_params=_params(("arbitrary",)),
        name="in_weight",
    )(w_t, w_t)


def _l2norm(x):
    return x * lax.rsqrt(jnp.sum(x * x, axis=-1, keepdims=True) + EPS)


def _inproj_body(seq_rows, tm, x_ref, sh_ref, sc_ref, nw_ref, w_ref, *rest):
    filler = []
    if seq_rows:
        n_cast = (len(rest) - 9 - 8) // 2
        cw_ref, cast_in = rest[0], rest[1:1 + n_cast]
        attn_in = rest[1 + n_cast:6 + n_cast]
        qkv_ref, gg_ref, ba_ref, sq_ref, skv_ref, gab_ref, tail_ref = rest[6 + n_cast:13 + n_cast]
        cast_out = rest[13 + n_cast:13 + 2 * n_cast]
        attn_out, xe_ref = rest[13 + 2 * n_cast:16 + 2 * n_cast], rest[-1]
        for src, dst in zip(cast_in, cast_out):
            dst[...] = src[...].astype(dst.dtype)
        per_step = attn_in[2].shape[0]
        first = (pl.program_id(0) * pl.num_programs(1) + pl.program_id(1)) * per_step
        filler = [piece for i in range(per_step) for piece in _swa_step_pieces(first + i, i, *attn_in, *attn_out)]

        @pl.when(pl.program_id(1) == 0)
        def _():
            xe_ref[:, 0:SUBLANES, :] = jnp.zeros((GDN_SECTIONS, SUBLANES, LANES), F32)
    else:
        qkv_ref, gg_ref, ba_ref, sq_ref, skv_ref, gab_ref = rest

    h = _rms(x_ref[...], nw_ref[...]) * (1.0 + sc_ref[...]) + sh_ref[...]
    hb = h.astype(BF16)

    def proj(lo, width):
        return jnp.dot(hb, w_ref[:, lo:lo + width], preferred_element_type=F32)

    step = 512
    per = step // LANES
    for c in range(GDN_CONV_CH // step):
        z = proj(COL_QKV + c * step, step)
        for k in range(per):
            s = c * per + k
            zs = z[:, k * LANES:(k + 1) * LANES]
            if not seq_rows:
                qkv_ref[:, s * LANES:(s + 1) * LANES] = zs
                continue
            xe_ref[s, SUBLANES:SUBLANES + tm, :] = zs
            w = cw_ref[s]
            for r0 in range(0, tm, CONV_ROWS):
                y = w[0:1] * xe_ref[s, r0 + SUBLANES - 3:r0 + SUBLANES - 3 + CONV_ROWS, :]
                for tap in range(1, GDN_CONV):
                    lo = r0 + SUBLANES - 3 + tap
                    y = y + w[tap:tap + 1] * xe_ref[s, lo:lo + CONV_ROWS, :]
                f = _silu(y)
                if s < GDN_HEADS:
                    f = _l2norm(f) * (GDN_DK ** -0.5)
                elif s < 2 * GDN_HEADS:
                    f = _l2norm(f)
                qkv_ref[s, r0:r0 + CONV_ROWS, :] = f
            xe_ref[s, 0:SUBLANES, :] = xe_ref[s, tm:tm + SUBLANES, :]
    if seq_rows:
        tail_ref[...] = xe_ref[:, 0:SUBLANES, :]
    for c in range(GDN_V // step):
        z = proj(COL_GATE + c * step, step)
        for k in range(per):
            zs = z[:, k * LANES:(k + 1) * LANES]
            if seq_rows:
                gg_ref[c * per + k] = _silu(zs)
            else:
                gg_ref[:, (c * per + k) * LANES:(c * per + k + 1) * LANES] = zs
    ba_ref[...] = proj(COL_BA, LANES)
    for c in range(SWA_Q // step):
        sq_ref[:, c * step:(c + 1) * step] = proj(COL_SQ + c * step, step)
        if filler:
            filler.pop(0)()
    skv_ref[...] = proj(COL_SKV, 2 * SWA_KV)
    for c in range(2 * D_MODEL // step):
        gab_ref[:, c * step:(c + 1) * step] = proj(COL_GAB + c * step, step)
        if filler:
            filler.pop(0)()
    for piece in filler:
        piece()


def _inproj(x, sh, sc, nw, w_all, cw, tm, cast_ws=(), step_attn=None):
    b_, l_, _ = x.shape
    nt = l_ // tm
    r_ = sh.shape[1]
    rt = 1 if r_ == 1 else tm
    mod_map = (lambda b, t: (b, 0, 0)) if r_ == 1 else (lambda b, t: (b, t, 0))
    row_map = lambda b, t: (b, t, 0)
    head_map = lambda b, t: (b, 0, t, 0)
    seq_rows = cw is not None
    if seq_rows:
        gdn_shapes = (jax.ShapeDtypeStruct((b_, GDN_SECTIONS, l_, LANES), F32),
                      jax.ShapeDtypeStruct((b_, GDN_HEADS, l_, LANES), F32))
        gdn_specs = (pl.BlockSpec((None, GDN_SECTIONS, tm, LANES), head_map),
                     pl.BlockSpec((None, GDN_HEADS, tm, LANES), head_map))
    else:
        gdn_shapes = (jax.ShapeDtypeStruct((b_, l_, GDN_CONV_CH), F32),
                      jax.ShapeDtypeStruct((b_, l_, GDN_V), F32))
        gdn_specs = (pl.BlockSpec((None, tm, GDN_CONV_CH), row_map),
                     pl.BlockSpec((None, tm, GDN_V), row_map))
    out_shape = gdn_shapes + (
        jax.ShapeDtypeStruct((b_, l_, LANES), F32),
        jax.ShapeDtypeStruct((b_, l_, SWA_Q), F32),
        jax.ShapeDtypeStruct((b_, l_, 2 * SWA_KV), F32),
        jax.ShapeDtypeStruct((b_, l_, 2 * D_MODEL), F32),
    )
    out_specs = gdn_specs + (
        pl.BlockSpec((None, tm, LANES), row_map),
        pl.BlockSpec((None, tm, SWA_Q), row_map),
        pl.BlockSpec((None, tm, 2 * SWA_KV), row_map),
        pl.BlockSpec((None, tm, 2 * D_MODEL), row_map),
    )
    in_specs = [
        pl.BlockSpec((None, tm, D_MODEL), row_map),
        pl.BlockSpec((None, rt, D_MODEL), mod_map),
        pl.BlockSpec((None, rt, D_MODEL), mod_map),
        _const_spec(nw.shape),
        _const_spec(w_all.shape),
    ]
    args = [x, sh, sc, nw, w_all]
    scratch = []
    if seq_rows:
        in_specs.append(_const_spec(cw.shape))
        args.append(cw)
        out_shape += (jax.ShapeDtypeStruct((b_, GDN_SECTIONS, SUBLANES, LANES), F32),)
        out_specs += (pl.BlockSpec((None, GDN_SECTIONS, SUBLANES, LANES), lambda b, t: (b, 0, 0, 0)),)
        scratch.append(pltpu.VMEM((GDN_SECTIONS, tm + SUBLANES, LANES), F32))
        steps = b_ * nt
        for w in cast_ws:
            slab = pl.BlockSpec((w.shape[0] // steps, w.shape[1]), lambda b, t: (b * nt + t, 0))
            assert w.shape[0] % (steps * 2 * SUBLANES) == 0
            in_specs.append(slab)
            args.append(w)
            out_shape += (jax.ShapeDtypeStruct(w.shape, BF16),)
            out_specs += (slab,)
        q3, kvn_t, ck_t, cv_t, sink_col = step_attn
        n_ = q3.shape[0]
        per_step = n_ // steps
        assert per_step * steps == n_ and n_ <= WINDOW
        seq_block = lambda *dims: pl.BlockSpec((per_step,) + dims, lambda b, t: (b * nt + t,) + (0,) * len(dims))
        cache_spec = seq_block(SWA_KV, WINDOW)
        in_specs += [seq_block(SWA_Q_HEADS, SWA_HD), _const_spec(kvn_t.shape), cache_spec, cache_spec,
                     _const_spec(sink_col.shape)]
        args += [q3, kvn_t, ck_t, cv_t, sink_col]
        out_shape += (jax.ShapeDtypeStruct(q3.shape, F32), jax.ShapeDtypeStruct(ck_t.shape, F32),
                      jax.ShapeDtypeStruct(cv_t.shape, F32))
        out_specs += (seq_block(SWA_Q_HEADS, SWA_HD), cache_spec, cache_spec)
    return pl.pallas_call(
        functools.partial(_inproj_body, seq_rows, tm),
        grid=(b_, l_ // tm),
        in_specs=in_specs,
        out_specs=out_specs,
        out_shape=out_shape,
        scratch_shapes=scratch,
        compiler_params=_params(("arbitrary", "arbitrary")),
        name="inproj_seq" if seq_rows else "inproj_rows",
    )(*args)


def _delta_gates(ba, alog_row, dtb_row):
    beta_all = jax.nn.sigmoid(ba)
    g_all = -jnp.exp(alog_row) * _softplus(ba + dtb_row)
    return beta_all, g_all


def _lane_column(x, lane_idx, lane):
    return jnp.sum(jnp.where(lane_idx == lane, x, 0.0), axis=1, keepdims=True)


def _level_masks():
    r = np.arange(CHUNK)[:, None]
    c = np.arange(CHUNK)[None, :]
    masks = [(r == c + 1) & (r % 2 == 1)]
    half = 2
    while half < CHUNK:
        full = 2 * half
        masks.append((r // full == c // full) & (r % full >= half) & (c % full < half))
        half = full
    return jnp.asarray(np.stack(masks), dtype=BF16)


def _unit_lower_inverses(ms, masks_ref, eye, between_levels=()):
    ts = [eye - m * masks_ref[0] for m in ms]
    pending = list(between_levels)
    for lvl in range(1, masks_ref.shape[0]):
        off = masks_ref[lvl]
        xs = [jnp.dot(m * off, t, preferred_element_type=F32).astype(BF16) for m, t in zip(ms, ts)]
        ys = [jnp.dot(t, x, preferred_element_type=F32).astype(BF16) for t, x in zip(ts, xs)]
        ts = [t - y for t, y in zip(ts, ys)]
        if pending:
            pending.pop(0)()
    for piece in pending:
        piece()
    return ts


def _cumsum_rows(g, ltri):
    hi = g.astype(BF16)
    r1 = g - hi.astype(F32)
    mid = r1.astype(BF16)
    lo = (r1 - mid.astype(F32)).astype(BF16)
    return (jnp.dot(ltri, hi, preferred_element_type=F32) + jnp.dot(ltri, mid, preferred_element_type=F32)
            + jnp.dot(ltri, lo, preferred_element_type=F32))


def _gated_out_norm(o, gate_act, onw):
    on = o * lax.rsqrt(jnp.mean(o * o, axis=-1, keepdims=True) + EPS) * onw
    return on * gate_act


def _run_all(pieces):
    for piece in pieces:
        piece()


def _gdn_prompt_body(lt, q_ref, k_ref, v_ref, ba_ref, alog_ref, dtb_ref, gate_ref, onw_ref, masks_ref,
                     f_ref, s0_ref, og_ref, s_ref, sn_ref, read_ref):
    per_step = s0_ref.shape[0]
    state_work = [functools.partial(_gdn_state_update, f_ref, s0_ref, sn_ref, read_ref, i, h)
                  for i in range(per_step) for h in range(GDN_HEADS)]
    attn_work = []

    @pl.when(pl.program_id(1) == 0)
    def _():
        s_ref[...] = jnp.zeros_like(s_ref)

    beta_all, g_all = _delta_gates(ba_ref[...], alog_ref[...], dtb_ref[...])
    lane_idx = lax.broadcasted_iota(jnp.int32, (CHUNK, LANES), 1)
    row = lax.broadcasted_iota(jnp.int32, (CHUNK, CHUNK), 0)
    col = lax.broadcasted_iota(jnp.int32, (CHUNK, CHUNK), 1)
    tril = row >= col
    strict = row > col
    ltri = jnp.where(tril, 1.0, 0.0).astype(BF16)
    eye = jnp.where(row == col, 1.0, 0.0).astype(BF16)
    onw = onw_ref[...]
    heads = range(GDN_HEADS)
    chunks = range(lt // CHUNK)

    blocks = [(c, j) for c in chunks for j in heads]
    pre = {}
    for c in chunks:
        rows = slice(c * CHUNK, (c + 1) * CHUNK)
        dec = _cumsum_rows(g_all[rows], ltri)
        dec_t = dec.T
        for j in heads:
            q, k, v = q_ref[j, rows, :], k_ref[j, rows, :], v_ref[j, rows, :]
            beta_col = _lane_column(beta_all[rows], lane_idx, j)
            dec_col = _lane_column(dec, lane_idx, GDN_HEADS + j)
            dec_row = dec_t[GDN_HEADS + j:GDN_HEADS + j + 1, :]
            dec_last = dec_row[:, CHUNK - 1:CHUNK]
            gam = jnp.exp(jnp.minimum(dec_col - dec_row, 0.0))
            e_col = jnp.exp(dec_col)
            kb = k * beta_col
            pre[c, j] = dict(q=q, k=k, gam=gam, kb=kb, qe=q * e_col, e_last=jnp.exp(dec_last),
                             kd=k * jnp.exp(dec_last - dec_col),
                             rhs=jnp.concatenate([v * beta_col, kb * e_col], axis=1).astype(BF16))
    a_intra, uw = {}, {}

    def recurrence(c):
        rows = slice(c * CHUNK, (c + 1) * CHUNK)
        mid = {}

        def read_out():
            mid["s"] = [s_ref[j] for j in heads]
            mid["ws_qs"] = [_mm(jnp.concatenate([uw[c, j][:, GDN_DV:], pre[c, j]["qe"]], axis=0), mid["s"][j])
                            for j in heads]

        def update():
            s_prev, ws_qs = mid["s"], mid["ws_qs"]
            v_new = [uw[c, j][:, :GDN_DV] - ws_qs[j][:CHUNK] for j in heads]
            o = [ws_qs[j][CHUNK:] + _mm(a_intra[c, j], v_new[j]) for j in heads]
            s_new = [s_prev[j] * pre[c, j]["e_last"] + _mm(pre[c, j]["kd"].T, v_new[j]) for j in heads]
            for j in heads:
                s_ref[j] = s_new[j]
                og = _gated_out_norm(o[j], gate_ref[j, rows, :], onw)
                og_ref[rows, j * GDN_DV:(j + 1) * GDN_DV] = og.astype(og_ref.dtype)

        return [read_out, update]

    n_slots = -(-len(blocks) // GDN_GROUP) * (masks_ref.shape[0] - 1)
    state_per_slot = -(-len(state_work) // n_slots)
    attn_per_slot = -(-len(attn_work) // n_slots)
    carried = []
    for g0 in range(0, len(blocks), GDN_GROUP):
        grp = blocks[g0:g0 + GDN_GROUP]
        grams = [_mm_nt(jnp.concatenate([pre[b]["kb"], pre[b]["q"]], axis=0), pre[b]["k"]) for b in grp]
        ms = [jnp.where(strict, g[:CHUNK] * pre[b]["gam"], 0.0).astype(BF16) for g, b in zip(grams, grp)]
        a_intra.update({b: jnp.where(tril, g[CHUNK:] * pre[b]["gam"], 0.0) for g, b in zip(grams, grp)})
        slots = []
        for _ in range(masks_ref.shape[0] - 1):
            work = carried[:1] + state_work[:state_per_slot] + attn_work[:attn_per_slot]
            carried = carried[1:]
            state_work, attn_work = state_work[state_per_slot:], attn_work[attn_per_slot:]
            slots.append(functools.partial(_run_all, work))
        t_inv = _unit_lower_inverses(ms, masks_ref, eye, slots)
        _run_all(carried)
        uw.update({b: jnp.dot(t, pre[b]["rhs"], preferred_element_type=F32) for t, b in zip(t_inv, grp)})
        carried = [piece for c in sorted({c for c, _ in grp}) for piece in recurrence(c)]
    _run_all(carried + state_work + attn_work)


def _gdn_prompt(qkvf, gact, ba, alog_row, dtb_row, onw, masks, step_feats, step_state, lt):
    b_, _, l_, _ = qkvf.shape
    nt = l_ // lt
    n_ = step_state.shape[0]
    per_step = n_ // (b_ * nt)
    assert per_step * b_ * nt == n_
    sec = lambda s: pl.BlockSpec((None, GDN_HEADS, lt, LANES), lambda b, t, s=s: (b, s, t, 0))
    seq_map = lambda b, t: (b * nt + t, 0, 0)
    seq_block = lambda *dims: pl.BlockSpec((per_step,) + dims, lambda b, t: (b * nt + t,) + (0,) * len(dims))
    state_spec = seq_block(GDN_HEADS, GDN_DK, GDN_DV)
    return pl.pallas_call(
        functools.partial(_gdn_prompt_body, lt),
        grid=(b_, nt),
        in_specs=[sec(0), sec(1), sec(2),
                  pl.BlockSpec((None, lt, LANES), lambda b, t: (b, t, 0)),
                  _const_spec(alog_row.shape), _const_spec(dtb_row.shape),
                  pl.BlockSpec((None, GDN_HEADS, lt, LANES), lambda b, t: (b, 0, t, 0)),
                  _const_spec(onw.shape), _const_spec(masks.shape),
                  seq_block(STEP_FEATURES * GDN_HEADS, LANES), state_spec],
        out_specs=(pl.BlockSpec((None, lt, GDN_V), lambda b, t: (b, t, 0)),
                   pl.BlockSpec((None, GDN_HEADS, GDN_DK, GDN_DV), lambda b, t: (b, 0, 0, 0)),
                   state_spec, seq_block(GDN_HEADS, LANES)),
        out_shape=(jax.ShapeDtypeStruct((b_, l_, GDN_V), BF16),
                   jax.ShapeDtypeStruct((b_, GDN_HEADS, GDN_DK, GDN_DV), F32),
                   jax.ShapeDtypeStruct(step_state.shape, F32),
                   jax.ShapeDtypeStruct((n_, GDN_HEADS, LANES), F32)),
        compiler_params=_params(("arbitrary", "arbitrary")),
        name="gdn_prompt",
    )(qkvf, qkvf, qkvf, ba, alog_row, dtb_row, gact, onw, masks, step_feats, step_state)


STEP_FEATURES = 5


def _gdn_step_features_body(bb, x_ref, st_ref, cw_ref, ba_ref, alog_ref, dtb_ref, f_ref):
    beta_all, g_all = _delta_gates(ba_ref[...], alog_ref[...], dtb_ref[...])
    lane_idx = lax.broadcasted_iota(jnp.int32, (bb, LANES), 1)
    for h in range(GDN_HEADS):
        feats = []
        for s in range(3):
            idx = s * GDN_HEADS + h
            cols = slice(idx * LANES, (idx + 1) * LANES)
            w = cw_ref[idx]
            y = w[0:1] * st_ref[0, :, cols]
            for tap in range(1, GDN_CONV - 1):
                y = y + w[tap:tap + 1] * st_ref[tap, :, cols]
            y = y + w[GDN_CONV - 1:GDN_CONV] * x_ref[:, cols]
            feats.append(_silu(y))
        q, k, v = feats
        f_ref[h] = _l2norm(q) * (GDN_DK ** -0.5)
        f_ref[GDN_HEADS + h] = _l2norm(k)
        f_ref[2 * GDN_HEADS + h] = v
        f_ref[3 * GDN_HEADS + h] = jnp.broadcast_to(_lane_column(beta_all, lane_idx, h), (bb, LANES))
        f_ref[4 * GDN_HEADS + h] = jnp.broadcast_to(
            jnp.exp(_lane_column(g_all, lane_idx, h + GDN_HEADS)), (bb, LANES))


def _gdn_step_features(qkv, st, ba, cw, alog_row, dtb_row, bb):
    n_ = ba.shape[0]
    return pl.pallas_call(
        functools.partial(_gdn_step_features_body, bb),
        grid=(n_ // bb,),
        in_specs=[pl.BlockSpec((bb, GDN_CONV_CH), lambda i: (i, 0)),
                  pl.BlockSpec((GDN_CONV - 1, bb, GDN_CONV_CH), lambda i: (0, i, 0)),
                  _const_spec(cw.shape),
                  pl.BlockSpec((bb, LANES), lambda i: (i, 0)),
                  _const_spec(alog_row.shape), _const_spec(dtb_row.shape)],
        out_specs=pl.BlockSpec((STEP_FEATURES * GDN_HEADS, bb, LANES), lambda i: (0, i, 0)),
        out_shape=jax.ShapeDtypeStruct((STEP_FEATURES * GDN_HEADS, n_, LANES), F32),
        compiler_params=_params(("arbitrary",)),
        name="gdn_step_features",
    )(qkv, st, cw, ba, alog_row, dtb_row)


def _gdn_state_update(f_ref, s0_ref, sn_ref, o_ref, i, h):
    eye = (lax.broadcasted_iota(jnp.int32, (GDN_DK, GDN_DK), 0)
           == lax.broadcasted_iota(jnp.int32, (GDN_DK, GDN_DK), 1))

    def row(kind):
        return f_ref[i, kind * GDN_HEADS + h:kind * GDN_HEADS + h + 1, :]

    def to_col(r):
        return jnp.sum(jnp.where(eye, jnp.broadcast_to(r, (GDN_DK, GDN_DK)), 0.0), axis=1, keepdims=True)

    q_row, k_row, v_row, beta, decay = (row(kind) for kind in range(STEP_FEATURES))
    k_col = to_col(k_row)
    s1 = s0_ref[i, h] * decay
    delta = (v_row - jnp.sum(s1 * k_col, axis=0, keepdims=True)) * beta
    s2 = s1 + k_col * delta
    sn_ref[i, h] = s2
    o_ref[i, h:h + 1, :] = jnp.sum(s2 * to_col(q_row), axis=0, keepdims=True)


def _swa_prompt_body(nq, sinks_ref, q_ref, kvp_ref, kvc_ref, o_ref):
    n = pl.program_id(1)
    w = WINDOW
    tiles = SWA_KV // LANES
    pairs = 2
    lo_lane = lax.broadcasted_iota(jnp.int32, (w, LANES), 1) < SWA_HD
    lo_row = lax.broadcasted_iota(jnp.int32, (LANES, w), 0) < SWA_HD
    c = lax.broadcasted_iota(jnp.int32, (w, pairs * w), 0)
    i = lax.broadcasted_iota(jnp.int32, (w, pairs * w), 1) & (w - 1)
    from_prev = c > i
    k_blk, vt_blk = [], []
    for j in range(nq + 1):
        src, rows = (kvp_ref, slice(0, w)) if j == 0 else (kvc_ref, slice((j - 1) * w, j * w))
        k_tiles, vt_tiles = [], []
        for t in range(tiles):
            kx = src[rows, t * LANES:(t + 1) * LANES]
            vt = src[rows, SWA_KV + t * LANES:SWA_KV + (t + 1) * LANES].T
            k_tiles.append((kx.astype(BF16), pltpu.roll(kx, SWA_HD, axis=1).astype(BF16)))
            vt_tiles.append((vt.astype(BF16),
                             jnp.concatenate([vt[SWA_HD:], vt[:SWA_HD]], axis=0).astype(BF16)))
        k_blk.append(k_tiles)
        vt_blk.append(vt_tiles)
    items = [(qb, g, p) for qb in range(nq) for g in range(SWA_KV_HEADS) for p in range(2)]
    log2e = math.log2(math.e)
    qm, kz, vzt, sink = {}, {}, {}, {}
    for qb, g, p in items:
        keep = lo_lane if p == 0 else jnp.logical_not(lo_lane)
        q_tiles = [q_ref[qb * w:(qb + 1) * w, (2 * g + r) * LANES:(2 * g + r + 1) * LANES] for r in range(pairs)]
        qm[qb, g, p] = jnp.concatenate([jnp.where(keep, x * SWA_Q_SCALE, 0.0) for x in q_tiles],
                                       axis=0).astype(BF16)
        variant = 0 if p == g % 2 else 1
        kz[qb, g, p] = jnp.concatenate([k_blk[qb + d][g // 2][variant] for d in range(2)], axis=0)
        vzt[qb, g, p] = jnp.concatenate([vt_blk[qb + d][g // 2][variant] for d in range(2)], axis=1)
        sink[qb, g, p] = jnp.concatenate([jnp.full((1, w), sinks_ref[SWA_GROUP * g + 2 * r + p] * log2e, F32)
                                          for r in range(pairs)], axis=1)
    st = {b: lax.dot_general(kz[b], qm[b], (((1,), (1,)), ((), ())), preferred_element_type=F32) for b in items}
    prev = {b: jnp.where(n > 0, st[b][:w], -jnp.inf) if b[0] == 0 else st[b][:w] for b in items}
    u = {b: jnp.where(from_prev, prev[b], st[b][w:]) for b in items}
    m = {b: jnp.maximum(jnp.max(u[b], axis=0, keepdims=True), sink[b]) for b in items}
    eu = {b: jnp.exp2(u[b] - m[b]) for b in items}
    den = {b: jnp.sum(eu[b], axis=0, keepdims=True) + jnp.exp2(sink[b] - m[b]) for b in items}
    et = {b: jnp.concatenate([jnp.where(from_prev, eu[b], 0.0), jnp.where(from_prev, 0.0, eu[b])],
                             axis=0).astype(BF16) for b in items}
    ot = {b: jnp.dot(vzt[b], et[b], preferred_element_type=F32) / den[b] for b in items}
    for qb in range(nq):
        for g in range(SWA_KV_HEADS):
            for r in range(pairs):
                cols = slice(r * w, (r + 1) * w)
                tile_t = jnp.where(lo_row, ot[qb, g, 0][:, cols], ot[qb, g, 1][:, cols])
                o_ref[qb * w:(qb + 1) * w, (2 * g + r) * LANES:(2 * g + r + 1) * LANES] = (
                    tile_t.T.astype(o_ref.dtype))


def _swa_prompt(sq, skv, sinks, nq):
    b_, l_, _ = sq.shape
    rows = nq * WINDOW
    return pl.pallas_call(
        functools.partial(_swa_prompt_body, nq),
        grid=(b_, l_ // rows),
        in_specs=[pl.BlockSpec(memory_space=pltpu.SMEM),
                  pl.BlockSpec((None, rows, SWA_Q), lambda b, n: (b, n, 0)),
                  pl.BlockSpec((None, WINDOW, 2 * SWA_KV), lambda b, n: (b, jnp.maximum(n * nq - 1, 0), 0)),
                  pl.BlockSpec((None, rows, 2 * SWA_KV), lambda b, n: (b, n, 0))],
        out_specs=pl.BlockSpec((None, rows, SWA_Q), lambda b, n: (b, n, 0)),
        out_shape=jax.ShapeDtypeStruct((b_, l_, SWA_Q), BF16),
        compiler_params=_params(("arbitrary", "arbitrary")),
        name="swa_prompt",
    )(sinks, sq, skv, skv)


def _swa_step_pieces(seq, i, q_ref, kvn_ref, ck_ref, cv_ref, sink_ref, o_ref, nk_ref, nv_ref):
    w = WINDOW
    mid = {}

    def own():
        row = lax.broadcasted_iota(jnp.int32, (SWA_Q_HEADS, SWA_KV), 0)
        lane = lax.broadcasted_iota(jnp.int32, (SWA_Q_HEADS, SWA_KV), 1)
        return (lane // SWA_HD) == (row // SWA_GROUP)

    def append_and_score():
        newest = lax.broadcasted_iota(jnp.int32, (SWA_KV, w), 1) == w - 1
        mid["nk"] = jnp.where(newest, pltpu.roll(kvn_ref[0:SWA_KV, :], w - 1 - seq, axis=1),
                              pltpu.roll(ck_ref[i], w - 1, axis=1))
        mid["nv"] = jnp.where(newest, pltpu.roll(kvn_ref[SWA_KV:2 * SWA_KV, :], w - 1 - seq, axis=1),
                              pltpu.roll(cv_ref[i], w - 1, axis=1))
        q_bd = jnp.where(own(), jnp.concatenate([q_ref[i]] * SWA_KV_HEADS, axis=1), 0.0)
        mid["s"] = _mm(q_bd, mid["nk"]) * (SWA_HD ** -0.5)

    def values():
        sink = sink_ref[...]
        m = jnp.maximum(jnp.max(mid["s"], axis=1, keepdims=True), sink)
        e = jnp.exp(mid["s"] - m)
        den = jnp.sum(e, axis=1, keepdims=True) + jnp.exp(sink - m)
        mid["pv"] = _mm_nt(e / den, mid["nv"])

    def write_out():
        pv = jnp.where(own(), mid["pv"], 0.0)
        o = pv[:, 0:SWA_HD]
        for g in range(1, SWA_KV_HEADS):
            o = o + pv[:, g * SWA_HD:(g + 1) * SWA_HD]
        o_ref[i] = o
        nk_ref[i] = mid["nk"]
        nv_ref[i] = mid["nv"]

    return [append_and_score, values, write_out]


def _dense_body(stateful, tm, og_ref, ob_ref, gab_ref, x_ref, gt1_ref, sh2_ref, sc2_ref, gt2_ref,
                n2w_ref, fnw_ref, wa_ref, wb_ref, wo_ref, wg_ref, wu_ref, cw_ref, cb_ref, wd_ref, *rest):
    if stateful:
        st_ref, gate_ref, onw_ref, y_ref, gout_ref, act_ref = rest
        onw = onw_ref[...]
        og = jnp.concatenate(
            [_gated_out_norm(og_ref[h], _silu(gate_ref[:, h * GDN_DV:(h + 1) * GDN_DV]), onw).astype(BF16)
             for h in range(GDN_HEADS)], axis=1)
    else:
        y_ref, gout_ref, act_ref, gbuf_ref, carry_ref = rest
        og = og_ref[...]

        @pl.when(pl.program_id(1) == 0)
        def _():
            carry_ref[...] = jnp.zeros_like(carry_ref)

    y_a = jnp.dot(og, wa_ref[...], preferred_element_type=F32)
    y_b = jnp.dot(ob_ref[...], wb_ref[...], preferred_element_type=F32)
    merged = (jax.nn.sigmoid(gab_ref[:, 0:D_MODEL]) * y_a
              + jax.nn.sigmoid(gab_ref[:, D_MODEL:2 * D_MODEL]) * y_b)
    x1 = x_ref[...] + gt1_ref[...] * _mm(merged, wo_ref[...])
    h2 = (_rms(x1, n2w_ref[...]) * (1.0 + sc2_ref[...]) + sh2_ref[...]).astype(BF16)

    for c in range(D_FF // FFN_COLS):
        cols = slice(c * FFN_COLS, (c + 1) * FFN_COLS)
        gate = jnp.dot(h2, wg_ref[:, cols], preferred_element_type=F32)
        up = jnp.dot(h2, wu_ref[:, cols], preferred_element_type=F32)
        if stateful:
            g2 = st_ref[0, :, cols]
            g1 = st_ref[1, :, cols]
            gout_ref[:, cols] = gate
        else:
            gbuf_ref[0:SUBLANES, :] = carry_ref[:, cols]
            gbuf_ref[SUBLANES:SUBLANES + tm, :] = gate
            g2 = gbuf_ref[SUBLANES - 2:SUBLANES - 2 + tm, :]
            g1 = gbuf_ref[SUBLANES - 1:SUBLANES - 1 + tm, :]
            carry_ref[:, cols] = gbuf_ref[tm:tm + SUBLANES, :]
        gc = (cw_ref[0:1, cols] * g2 + cw_ref[1:2, cols] * g1 + cw_ref[2:3, cols] * gate) + cb_ref[:, cols]
        act_ref[:, cols] = (_silu(gc) * up).astype(BF16)
    if not stateful:
        gout_ref[...] = carry_ref[...]

    x2 = x1 + gt2_ref[...] * jnp.dot(act_ref[...], wd_ref[...], preferred_element_type=F32)
    y_ref[...] = _rms(x2, fnw_ref[...])


def _dense(og, ob, gab, x, mods, vecs, ws, st, tm, step=None):
    b_, l_, _ = x.shape
    r_ = mods[0].shape[1]
    rt = 1 if r_ == 1 else tm
    nt = l_ // tm
    mod_map = (lambda b, t: (b, 0, 0)) if r_ == 1 else (lambda b, t: (b, t, 0))
    row_map = lambda b, t: (b, t, 0)
    stateful = st is not None
    og_spec = (pl.BlockSpec((GDN_HEADS, tm, LANES), lambda b, t: (0, t, 0)) if stateful
               else pl.BlockSpec((None, tm, D_MODEL), row_map))
    in_specs = ([og_spec,
                 pl.BlockSpec((None, tm, D_MODEL), row_map),
                 pl.BlockSpec((None, tm, 2 * D_MODEL), row_map),
                 pl.BlockSpec((None, tm, D_MODEL), row_map)]
                + [pl.BlockSpec((None, rt, D_MODEL), mod_map)] * 4
                + [_const_spec(a.shape) for a in vecs[:2]]
                + [_const_spec(ws[0].shape), _const_spec(ws[1].shape), _const_spec(ws[2].shape),
                   _const_spec(ws[3].shape), _const_spec(ws[4].shape),
                   _const_spec(vecs[2].shape), _const_spec(vecs[3].shape), _const_spec(ws[5].shape)])
    args = [og, ob, gab, x, *mods, vecs[0], vecs[1], ws[0], ws[1], ws[2], ws[3], ws[4], vecs[2], vecs[3], ws[5]]
    scratch = [pltpu.VMEM((tm, D_FF), BF16)]
    out_specs = [pl.BlockSpec((None, tm, D_MODEL), row_map)]
    out_shape = [jax.ShapeDtypeStruct((b_, l_, D_MODEL), F32)]
    if stateful:
        gate, onw = step
        in_specs += [pl.BlockSpec((FFN_CONV - 1, None, tm, D_FF), lambda b, t: (0, b, t, 0)),
                     pl.BlockSpec((None, tm, GDN_V), row_map), _const_spec(onw.shape)]
        args += [st, gate, onw]
        out_shape.append(jax.ShapeDtypeStruct((b_, l_, D_FF), F32))
        out_specs.append(pl.BlockSpec((None, tm, D_FF), row_map))
    else:
        scratch += [pltpu.VMEM((tm + SUBLANES, FFN_COLS), F32), pltpu.VMEM((SUBLANES, D_FF), F32)]
        out_shape.append(jax.ShapeDtypeStruct((b_, SUBLANES, D_FF), F32))
        out_specs.append(pl.BlockSpec((None, SUBLANES, D_FF), lambda b, t: (b, 0, 0)))
    return pl.pallas_call(
        functools.partial(_dense_body, stateful, tm),
        grid=(b_, nt),
        in_specs=in_specs,
        out_specs=tuple(out_specs),
        out_shape=tuple(out_shape),
        scratch_shapes=scratch,
        compiler_params=_params(("arbitrary", "arbitrary")),
        name="dense_step" if stateful else "dense_prompt",
    )(*args)


def _lane_row(values, offset):
    return jnp.zeros((1, LANES), F32).at[0, offset:offset + values.shape[0]].set(values)


def kernel(x_prompt, x_sample, c_prompt, c_sample, state_gdn_S, state_gdn_conv, cache_swa_k, cache_swa_v,
           state_ffn_conv, w_mod, b_mod, norm1_w, norm2_w, w_in, gdn_conv_w, gdn_a_log, gdn_dt_bias,
           gdn_onorm_w, w_gdn_out, swa_sinks, w_swa_out, w_o, w_ffn_gate, w_ffn_up, ffn_conv_w, ffn_conv_b,
           w_ffn_down, final_norm_w):
    assert w_mod.shape[0] == 1, "single-layer trunk"
    nb, seq, _ = x_prompt.shape
    ns = x_sample.shape[0]
    assert x_sample.shape[1] == 1

    in_ws = _in_weight(jnp.transpose(w_in[0]))
    dense_vecs = (norm2_w, final_norm_w[None, :], ffn_conv_w[0], ffn_conv_b)
    cw = jnp.transpose(gdn_conv_w[0].reshape(GDN_CONV, GDN_SECTIONS, LANES), (1, 0, 2))
    alog_row = _lane_row(gdn_a_log[0], GDN_HEADS)
    dtb_row = _lane_row(gdn_dt_bias[0], GDN_HEADS)

    mod = _modulation(jnp.concatenate([c_prompt, c_sample], axis=0), w_mod[0], b_mod)
    mod_p = [mod[:nb, i * D_MODEL:(i + 1) * D_MODEL][:, None, :] for i in range(6)]
    mod_s = [mod[nb:, i * D_MODEL:(i + 1) * D_MODEL][None, :, :] for i in range(6)]

    xs = x_sample.reshape(1, ns, D_MODEL)
    qkvs, gates, bas, sqs, skvs, gabs = _inproj(xs, mod_s[0], mod_s[1], norm1_w, in_ws, None, tm=ns)
    st_gdn = jnp.transpose(state_gdn_conv[0], (1, 0, 2))
    step_feats = jnp.transpose(_gdn_step_features(qkvs[0], st_gdn, bas[0], cw, alog_row, dtb_row, bb=ns), (1, 0, 2))
    to_channel_major = lambda c: jnp.transpose(c, (0, 2, 3, 1)).reshape(ns, SWA_KV, WINDOW)
    from_channel_major = lambda c: jnp.transpose(c.reshape(ns, SWA_KV_HEADS, SWA_HD, WINDOW), (0, 3, 1, 2))
    kvn_t = jnp.pad(jnp.transpose(skvs[0]), ((0, 0), (0, WINDOW - ns)))
    step_attn = (sqs[0].reshape(ns, SWA_Q_HEADS, SWA_HD), kvn_t, to_channel_major(cache_swa_k[0]),
                 to_channel_major(cache_swa_v[0]), swa_sinks[0][:, None])

    later_ws = (w_gdn_out[0], w_swa_out[0], w_o[0], w_ffn_gate[0], w_ffn_up[0])
    qkvf, gact, ba, sq, skv, gab, qkv_tail, *hosted = _inproj(
        x_prompt, mod_p[0], mod_p[1], norm1_w, in_ws, cw, tm=256, cast_ws=later_ws, step_attn=step_attn)
    later_bf16, (o3, k_s, v_s) = hosted[:len(later_ws)], hosted[len(later_ws):]
    dense_ws = (*later_bf16, w_ffn_down[0].astype(BF16))
    og, gdn_s_p, gdn_s_s, step_read = _gdn_prompt(
        qkvf, gact, ba, alog_row, dtb_row, gdn_onorm_w, _level_masks(), step_feats, state_gdn_S[0], lt=4 * CHUNK)
    ob = _swa_prompt(sq, skv, swa_sinks[0], nq=4)
    y_p, gate_tail = _dense(og, ob, gab, x_prompt, (mod_p[2], mod_p[3], mod_p[4], mod_p[5]),
                            dense_vecs, dense_ws, None, tm=512)
    gdn_conv_p = jnp.transpose(qkv_tail[:, :, SUBLANES - (GDN_CONV - 1):, :], (0, 2, 1, 3)).reshape(
        nb, GDN_CONV - 1, GDN_CONV_CH)
    k_p = skv[:, seq - WINDOW:, :SWA_KV].reshape(nb, WINDOW, SWA_KV_HEADS, SWA_HD)
    v_p = skv[:, seq - WINDOW:, SWA_KV:].reshape(nb, WINDOW, SWA_KV_HEADS, SWA_HD)
    ffn_conv_p = gate_tail[:, SUBLANES - (FFN_CONV - 1):, :]

    ob_s = o3.reshape(1, ns, SWA_Q).astype(BF16)
    st_ffn = jnp.transpose(state_ffn_conv[0], (1, 0, 2))[:, None]
    y_s, gate_new = _dense(jnp.transpose(step_read, (1, 0, 2)), ob_s, gabs, xs,
                           (mod_s[2], mod_s[3], mod_s[4], mod_s[5]), dense_vecs, dense_ws, st_ffn, tm=ns,
                           step=(gates, gdn_onorm_w))
    gdn_conv_s = jnp.concatenate([state_gdn_conv[0][:, 1:], qkvs[0][:, None, :]], axis=1)
    ffn_conv_s = jnp.concatenate([state_ffn_conv[0][:, 1:], gate_new[0][:, None, :]], axis=1)

    return (y_p, y_s.reshape(ns, 1, D_MODEL),
            gdn_s_p[None], gdn_s_s[None],
            gdn_conv_p[None], gdn_conv_s[None],
            k_p[None], from_channel_major(k_s)[None],
            v_p[None], from_channel_major(v_s)[None],
            ffn_conv_p[None], ffn_conv_s[None])
```

```python
import functools
import math

import numpy as np
import jax
import jax.numpy as jnp
from jax import lax
from jax.experimental import pallas as pl
from jax.experimental.pallas import tpu as pltpu

F32 = jnp.float32
BF16 = jnp.bfloat16

D_MODEL = 1024
GDN_HEADS = 8
GDN_DK = 128
GDN_DV = 128
GDN_QK = GDN_HEADS * GDN_DK
GDN_V = GDN_HEADS * GDN_DV
GDN_CONV = 4
GDN_CONV_CH = 2 * GDN_QK + GDN_V
GDN_SECTIONS = GDN_CONV_CH // 128
SWA_Q_HEADS = 16
SWA_KV_HEADS = 4
SWA_GROUP = SWA_Q_HEADS // SWA_KV_HEADS
SWA_HD = 64
SWA_Q = SWA_Q_HEADS * SWA_HD
SWA_KV = SWA_KV_HEADS * SWA_HD
WINDOW = 128
D_FF = 2816
FFN_CONV = 3
EPS = 1e-6

LANES = 128
SUBLANES = 8
VMEM_LIMIT = 56 * 1024 * 1024

COL_QKV = 0
COL_GATE = COL_QKV + GDN_CONV_CH
COL_SQ = COL_GATE + GDN_V
COL_SKV = COL_SQ + SWA_Q
COL_GAB = COL_SKV + 2 * SWA_KV
COL_BA = COL_GAB + 2 * D_MODEL
IN_COLS = COL_BA + LANES

SWA_Q_SCALE = SWA_HD ** -0.5 * math.log2(math.e)

CONV_ROWS = 64
CHUNK = 128
GDN_GROUP = 16
FFN_COLS = 256


def _mm(a, b):
    return jnp.dot(a.astype(BF16), b.astype(BF16), preferred_element_type=F32)


def _mm_nt(a, b):
    return lax.dot_general(a.astype(BF16), b.astype(BF16), (((1,), (1,)), ((), ())),
                           preferred_element_type=F32)


def _silu(x):
    return x * jax.nn.sigmoid(x)


def _softplus(x):
    return jnp.maximum(x, 0.0) + jnp.log1p(jnp.exp(-jnp.abs(x)))


def _rms(x, w):
    return x * lax.rsqrt(jnp.mean(x * x, axis=-1, keepdims=True) + EPS) * w


def _const_spec(shape):
    n = len(shape)
    return pl.BlockSpec(shape, lambda *_: (0,) * n, pipeline_mode=pl.Buffered(1))


def _params(sem):
    return pltpu.CompilerParams(dimension_semantics=sem, vmem_limit_bytes=VMEM_LIMIT)


def _mod_body(c_ref, w_ref, b_ref, o_ref):
    o_ref[...] = _mm(_silu(c_ref[...]), w_ref[...]) + b_ref[...]


def _modulation(c_all, w_mod, b_mod):
    rows = c_all.shape[0]
    n_out = w_mod.shape[1]
    tn = D_MODEL
    return pl.pallas_call(
        _mod_body,
        grid=(n_out // tn,),
        in_specs=[pl.BlockSpec((rows, D_MODEL), lambda j: (0, 0)),
                  pl.BlockSpec((D_MODEL, tn), lambda j: (0, j)),
                  pl.BlockSpec((1, tn), lambda j: (0, j))],
        out_specs=pl.BlockSpec((rows, tn), lambda j: (0, j)),
        out_shape=jax.ShapeDtypeStruct((rows, n_out), F32),
        compiler_params=_params(("arbitrary",)),
        name="modulation",
    )(c_all, w_mod, b_mod)


IN_WEIGHT_COLS = 512
IN_WEIGHT_PAD = 256


def _in_weight_body(n_main, wt_ref, ba_ref, o_ref):
    j = pl.program_id(0)

    @pl.when(j < n_main)
    def _():
        o_ref[...] = wt_ref[...].T.astype(BF16)

    @pl.when(j == n_main)
    def _():
        n_ba = ba_ref.shape[0]
        ba = jnp.concatenate([ba_ref[...].T, jnp.zeros((D_MODEL, IN_WEIGHT_COLS - n_ba), F32)], axis=1)
        o_ref[...] = ba.astype(BF16)


def _in_weight(w_t):
    n_ba = 2 * GDN_HEADS
    split = GDN_CONV_CH + GDN_V
    tc = IN_WEIGHT_COLS
    n_main = COL_BA // tc

    def src_row(j):
        jj = jnp.minimum(j, n_main - 1)
        return pl.multiple_of(jnp.where(jj * tc < split, jj * tc, jj * tc + n_ba), n_ba)

    return pl.pallas_call(
        functools.partial(_in_weight_body, n_main),
        grid=(n_main + 1,),
        in_specs=[pl.BlockSpec((pl.Element(tc), pl.Element(D_MODEL)), lambda j: (src_row(j), 0)),
                  pl.BlockSpec((pl.Element(n_ba), pl.Element(D_MODEL)), lambda j: (split, 0))],
        out_specs=pl.BlockSpec((D_MODEL, tc), lambda j: (0, j)),
        out_shape=jax.ShapeDtypeStruct((D_MODEL, COL_BA + IN_WEIGHT_PAD), BF16),
        compiler_params=_params(("arbitrary",)),
        name="in_weight",
    )(w_t, w_t)


def _l2norm(x):
    return x * lax.rsqrt(jnp.sum(x * x, axis=-1, keepdims=True) + EPS)


def _inproj_body(seq_rows, tm, x_ref, sh_ref, sc_ref, nw_ref, w_ref, *rest):
    filler = []
    if seq_rows:
        n_cast = (len(rest) - 9 - 8 - 4) // 2
        cw_ref, cast_in = rest[0], rest[1:1 + n_cast]
        attn_in, state_in = rest[1 + n_cast:6 + n_cast], rest[6 + n_cast:8 + n_cast]
        qkv_ref, gg_ref, ba_ref, sq_ref, skv_ref, gab_ref, tail_ref = rest[8 + n_cast:15 + n_cast]
        cast_out = rest[15 + n_cast:15 + 2 * n_cast]
        attn_out, state_out = rest[15 + 2 * n_cast:18 + 2 * n_cast], rest[18 + 2 * n_cast:20 + 2 * n_cast]
        xe_ref = rest[-1]
        for src, dst in zip(cast_in, cast_out):
            dst[...] = src[...].astype(dst.dtype)
        per_step = attn_in[2].shape[0]
        first = (pl.program_id(0) * pl.num_programs(1) + pl.program_id(1)) * per_step
        filler = [piece for i in range(per_step) for piece in _swa_step_pieces(first + i, i, *attn_in, *attn_out)]
        state_work = [functools.partial(_gdn_state_update, *state_in, *state_out, i, h)
                      for i in range(per_step) for h in range(GDN_HEADS)]
        group = -(-len(state_work) // len(filler))
        filler = [functools.partial(_run_all, [piece] + state_work[k * group:(k + 1) * group])
                  for k, piece in enumerate(filler)]

        @pl.when(pl.program_id(1) == 0)
        def _():
            xe_ref[:, 0:SUBLANES, :] = jnp.zeros((GDN_SECTIONS, SUBLANES, LANES), F32)
    else:
        qkv_ref, gg_ref, ba_ref, sq_ref, skv_ref, gab_ref = rest

    h = _rms(x_ref[...], nw_ref[...]) * (1.0 + sc_ref[...]) + sh_ref[...]
    hb = h.astype(BF16)

    def proj(lo, width):
        return jnp.dot(hb, w_ref[:, lo:lo + width], preferred_element_type=F32)

    step = 512
    per = step // LANES
    for c in range(GDN_CONV_CH // step):
        z = proj(COL_QKV + c * step, step)
        for k in range(per):
            s = c * per + k
            zs = z[:, k * LANES:(k + 1) * LANES]
            if not seq_rows:
                qkv_ref[:, s * LANES:(s + 1) * LANES] = zs
                continue
            xe_ref[s, SUBLANES:SUBLANES + tm, :] = zs
            w = cw_ref[s]
            for r0 in range(0, tm, CONV_ROWS):
                y = w[0:1] * xe_ref[s, r0 + SUBLANES - 3:r0 + SUBLANES - 3 + CONV_ROWS, :]
                for tap in range(1, GDN_CONV):
                    lo = r0 + SUBLANES - 3 + tap
                    y = y + w[tap:tap + 1] * xe_ref[s, lo:lo + CONV_ROWS, :]
                f = _silu(y)
                if s < GDN_HEADS:
                    f = _l2norm(f) * (GDN_DK ** -0.5)
                elif s < 2 * GDN_HEADS:
                    f = _l2norm(f)
                qkv_ref[s, r0:r0 + CONV_ROWS, :] = f
            xe_ref[s, 0:SUBLANES, :] = xe_ref[s, tm:tm + SUBLANES, :]
    if seq_rows:
        tail_ref[...] = xe_ref[:, 0:SUBLANES, :]
    for c in range(GDN_V // step):
        z = proj(COL_GATE + c * step, step)
        for k in range(per):
            zs = z[:, k * LANES:(k + 1) * LANES]
            if seq_rows:
                gg_ref[c * per + k] = _silu(zs)
            else:
                gg_ref[:, (c * per + k) * LANES:(c * per + k + 1) * LANES] = zs
    ba_ref[...] = proj(COL_BA, LANES)
    for c in range(SWA_Q // step):
        sq_ref[:, c * step:(c + 1) * step] = proj(COL_SQ + c * step, step)
        if filler:
            filler.pop(0)()
    skv_ref[...] = proj(COL_SKV, 2 * SWA_KV)
    for c in range(2 * D_MODEL // step):
        gab_ref[:, c * step:(c + 1) * step] = proj(COL_GAB + c * step, step)
        if filler:
            filler.pop(0)()
    for piece in filler:
        piece()


def _inproj(x, sh, sc, nw, w_all, cw, tm, cast_ws=(), step_attn=None, step_state=None):
    b_, l_, _ = x.shape
    nt = l_ // tm
    r_ = sh.shape[1]
    rt = 1 if r_ == 1 else tm
    mod_map = (lambda b, t: (b, 0, 0)) if r_ == 1 else (lambda b, t: (b, t, 0))
    row_map = lambda b, t: (b, t, 0)
    head_map = lambda b, t: (b, 0, t, 0)
    seq_rows = cw is not None
    if seq_rows:
        gdn_shapes = (jax.ShapeDtypeStruct((b_, GDN_SECTIONS, l_, LANES), F32),
                      jax.ShapeDtypeStruct((b_, GDN_HEADS, l_, LANES), F32))
        gdn_specs = (pl.BlockSpec((None, GDN_SECTIONS, tm, LANES), head_map),
                     pl.BlockSpec((None, GDN_HEADS, tm, LANES), head_map))
    else:
        gdn_shapes = (jax.ShapeDtypeStruct((b_, l_, GDN_CONV_CH), F32),
                      jax.ShapeDtypeStruct((b_, l_, GDN_V), F32))
        gdn_specs = (pl.BlockSpec((None, tm, GDN_CONV_CH), row_map),
                     pl.BlockSpec((None, tm, GDN_V), row_map))
    out_shape = gdn_shapes + (
        jax.ShapeDtypeStruct((b_, l_, LANES), F32),
        jax.ShapeDtypeStruct((b_, l_, SWA_Q), F32),
        jax.ShapeDtypeStruct((b_, l_, 2 * SWA_KV), F32),
        jax.ShapeDtypeStruct((b_, l_, 2 * D_MODEL), F32),
    )
    out_specs = gdn_specs + (
        pl.BlockSpec((None, tm, LANES), row_map),
        pl.BlockSpec((None, tm, SWA_Q), row_map),
        pl.BlockSpec((None, tm, 2 * SWA_KV), row_map),
        pl.BlockSpec((None, tm, 2 * D_MODEL), row_map),
    )
    in_specs = [
        pl.BlockSpec((None, tm, D_MODEL), row_map),
        pl.BlockSpec((None, rt, D_MODEL), mod_map),
        pl.BlockSpec((None, rt, D_MODEL), mod_map),
        _const_spec(nw.shape),
        _const_spec(w_all.shape),
    ]
    args = [x, sh, sc, nw, w_all]
    scratch = []
    if seq_rows:
        in_specs.append(_const_spec(cw.shape))
        args.append(cw)
        out_shape += (jax.ShapeDtypeStruct((b_, GDN_SECTIONS, SUBLANES, LANES), F32),)
        out_specs += (pl.BlockSpec((None, GDN_SECTIONS, SUBLANES, LANES), lambda b, t: (b, 0, 0, 0)),)
        scratch.append(pltpu.VMEM((GDN_SECTIONS, tm + SUBLANES, LANES), F32))
        steps = b_ * nt
        for w in cast_ws:
            slab = pl.BlockSpec((w.shape[0] // steps, w.shape[1]), lambda b, t: (b * nt + t, 0))
            assert w.shape[0] % (steps * 2 * SUBLANES) == 0
            in_specs.append(slab)
            args.append(w)
            out_shape += (jax.ShapeDtypeStruct(w.shape, BF16),)
            out_specs += (slab,)
        q3, kvn_t, ck_t, cv_t, sink_col = step_attn
        n_ = q3.shape[0]
        per_step = n_ // steps
        assert per_step * steps == n_ and n_ <= WINDOW
        seq_block = lambda *dims: pl.BlockSpec((per_step,) + dims, lambda b, t: (b * nt + t,) + (0,) * len(dims))
        cache_spec = seq_block(SWA_KV, WINDOW)
        in_specs += [seq_block(SWA_Q_HEADS, SWA_HD), _const_spec(kvn_t.shape), cache_spec, cache_spec,
                     _const_spec(sink_col.shape)]
        args += [q3, kvn_t, ck_t, cv_t, sink_col]
        out_shape += (jax.ShapeDtypeStruct(q3.shape, F32), jax.ShapeDtypeStruct(ck_t.shape, F32),
                      jax.ShapeDtypeStruct(cv_t.shape, F32))
        out_specs += (seq_block(SWA_Q_HEADS, SWA_HD), cache_spec, cache_spec)
        feats, s0 = step_state
        state_spec = seq_block(GDN_HEADS, GDN_DK, GDN_DV)
        in_specs += [seq_block(STEP_FEATURES * GDN_HEADS, LANES), state_spec]
        args += [feats, s0]
        out_shape += (jax.ShapeDtypeStruct(s0.shape, F32), jax.ShapeDtypeStruct((n_, GDN_HEADS, LANES), F32))
        out_specs += (state_spec, seq_block(GDN_HEADS, LANES))
    return pl.pallas_call(
        functools.partial(_inproj_body, seq_rows, tm),
        grid=(b_, l_ // tm),
        in_specs=in_specs,
        out_specs=out_specs,
        out_shape=out_shape,
        scratch_shapes=scratch,
        compiler_params=_params(("arbitrary", "arbitrary")),
        name="inproj_seq" if seq_rows else "inproj_rows",
    )(*args)


def _delta_gates(ba, alog_row, dtb_row):
    beta_all = jax.nn.sigmoid(ba)
    g_all = -jnp.exp(alog_row) * _softplus(ba + dtb_row)
    return beta_all, g_all


def _lane_column(x, lane_idx, lane):
    return jnp.sum(jnp.where(lane_idx == lane, x, 0.0), axis=1, keepdims=True)


def _level_masks():
    r = np.arange(CHUNK)[:, None]
    c = np.arange(CHUNK)[None, :]
    masks = [(r == c + 1) & (r % 2 == 1)]
    half = 2
    while half < CHUNK:
        full = 2 * half
        masks.append((r // full == c // full) & (r % full >= half) & (c % full < half))
        half = full
    return jnp.asarray(np.stack(masks), dtype=BF16)


def _unit_lower_inverses(ms, masks_ref, eye, between_levels=()):
    ts = [eye - m * masks_ref[0] for m in ms]
    pending = list(between_levels)
    for lvl in range(1, masks_ref.shape[0]):
        off = masks_ref[lvl]
        xs = [jnp.dot(m * off, t, preferred_element_type=F32).astype(BF16) for m, t in zip(ms, ts)]
        ys = [jnp.dot(t, x, preferred_element_type=F32).astype(BF16) for t, x in zip(ts, xs)]
        ts = [t - y for t, y in zip(ts, ys)]
        if pending:
            pending.pop(0)()
    for piece in pending:
        piece()
    return ts


def _cumsum_rows(g, ltri):
    hi = g.astype(BF16)
    r1 = g - hi.astype(F32)
    mid = r1.astype(BF16)
    lo = (r1 - mid.astype(F32)).astype(BF16)
    return (jnp.dot(ltri, hi, preferred_element_type=F32) + jnp.dot(ltri, mid, preferred_element_type=F32)
            + jnp.dot(ltri, lo, preferred_element_type=F32))


def _gated_out_norm(o, gate_act, onw):
    on = o * lax.rsqrt(jnp.mean(o * o, axis=-1, keepdims=True) + EPS) * onw
    return on * gate_act


def _run_all(pieces):
    for piece in pieces:
        piece()


def _gdn_prompt_body(lt, q_ref, k_ref, v_ref, ba_ref, alog_ref, dtb_ref, gate_ref, onw_ref, masks_ref,
                     og_ref, s_ref):
    state_work = []
    attn_work = []

    @pl.when(pl.program_id(1) == 0)
    def _():
        s_ref[...] = jnp.zeros_like(s_ref)

    beta_all, g_all = _delta_gates(ba_ref[...], alog_ref[...], dtb_ref[...])
    lane_idx = lax.broadcasted_iota(jnp.int32, (CHUNK, LANES), 1)
    row = lax.broadcasted_iota(jnp.int32, (CHUNK, CHUNK), 0)
    col = lax.broadcasted_iota(jnp.int32, (CHUNK, CHUNK), 1)
    tril = row >= col
    strict = row > col
    ltri = jnp.where(tril, 1.0, 0.0).astype(BF16)
    eye = jnp.where(row == col, 1.0, 0.0).astype(BF16)
    onw = onw_ref[...]
    heads = range(GDN_HEADS)
    chunks = range(lt // CHUNK)

    blocks = [(c, j) for c in chunks for j in heads]
    pre = {}
    for c in chunks:
        rows = slice(c * CHUNK, (c + 1) * CHUNK)
        dec = _cumsum_rows(g_all[rows], ltri)
        dec_t = dec.T
        for j in heads:
            q, k, v = q_ref[j, rows, :], k_ref[j, rows, :], v_ref[j, rows, :]
            beta_col = _lane_column(beta_all[rows], lane_idx, j)
            dec_col = _lane_column(dec, lane_idx, GDN_HEADS + j)
            dec_row = dec_t[GDN_HEADS + j:GDN_HEADS + j + 1, :]
            dec_last = dec_row[:, CHUNK - 1:CHUNK]
            gam = jnp.exp(jnp.minimum(dec_col - dec_row, 0.0))
            e_col = jnp.exp(dec_col)
            kb = k * beta_col
            pre[c, j] = dict(q=q, k=k, gam=gam, kb=kb, qe=q * e_col, e_last=jnp.exp(dec_last),
                             kd=k * jnp.exp(dec_last - dec_col),
                             rhs=jnp.concatenate([v * beta_col, kb * e_col], axis=1).astype(BF16))
    a_intra, uw = {}, {}

    def recurrence(c):
        rows = slice(c * CHUNK, (c + 1) * CHUNK)
        mid = {}

        def read_out():
            mid["s"] = [s_ref[j] for j in heads]
            mid["ws_qs"] = [_mm(jnp.concatenate([uw[c, j][:, GDN_DV:], pre[c, j]["qe"]], axis=0), mid["s"][j])
                            for j in heads]

        def update():
            s_prev, ws_qs = mid["s"], mid["ws_qs"]
            v_new = [uw[c, j][:, :GDN_DV] - ws_qs[j][:CHUNK] for j in heads]
            o = [ws_qs[j][CHUNK:] + _mm(a_intra[c, j], v_new[j]) for j in heads]
            s_new = [s_prev[j] * pre[c, j]["e_last"] + _mm(pre[c, j]["kd"].T, v_new[j]) for j in heads]
            for j in heads:
                s_ref[j] = s_new[j]
                og = _gated_out_norm(o[j], gate_ref[j, rows, :], onw)
                og_ref[rows, j * GDN_DV:(j + 1) * GDN_DV] = og.astype(og_ref.dtype)

        return [read_out, update]

    n_slots = -(-len(blocks) // GDN_GROUP) * (masks_ref.shape[0] - 1)
    state_per_slot = -(-len(state_work) // n_slots)
    attn_per_slot = -(-len(attn_work) // n_slots)
    carried = []
    for g0 in range(0, len(blocks), GDN_GROUP):
        grp = blocks[g0:g0 + GDN_GROUP]
        grams = [_mm_nt(jnp.concatenate([pre[b]["kb"], pre[b]["q"]], axis=0), pre[b]["k"]) for b in grp]
        ms = [jnp.where(strict, g[:CHUNK] * pre[b]["gam"], 0.0).astype(BF16) for g, b in zip(grams, grp)]
        a_intra.update({b: jnp.where(tril, g[CHUNK:] * pre[b]["gam"], 0.0) for g, b in zip(grams, grp)})
        slots = []
        for _ in range(masks_ref.shape[0] - 1):
            work = carried[:1] + state_work[:state_per_slot] + attn_work[:attn_per_slot]
            carried = carried[1:]
            state_work, attn_work = state_work[state_per_slot:], attn_work[attn_per_slot:]
            slots.append(functools.partial(_run_all, work))
        t_inv = _unit_lower_inverses(ms, masks_ref, eye, slots)
        _run_all(carried)
        uw.update({b: jnp.dot(t, pre[b]["rhs"], preferred_element_type=F32) for t, b in zip(t_inv, grp)})
        carried = [piece for c in sorted({c for c, _ in grp}) for piece in recurrence(c)]
    _run_all(carried + state_work + attn_work)


def _gdn_prompt(qkvf, gact, ba, alog_row, dtb_row, onw, masks, lt):
    b_, _, l_, _ = qkvf.shape
    nt = l_ // lt
    sec = lambda s: pl.BlockSpec((None, GDN_HEADS, lt, LANES), lambda b, t, s=s: (b, s, t, 0))
    return pl.pallas_call(
        functools.partial(_gdn_prompt_body, lt),
        grid=(b_, nt),
        in_specs=[sec(0), sec(1), sec(2),
                  pl.BlockSpec((None, lt, LANES), lambda b, t: (b, t, 0)),
                  _const_spec(alog_row.shape), _const_spec(dtb_row.shape),
                  pl.BlockSpec((None, GDN_HEADS, lt, LANES), lambda b, t: (b, 0, t, 0)),
                  _const_spec(onw.shape), _const_spec(masks.shape)],
        out_specs=(pl.BlockSpec((None, lt, GDN_V), lambda b, t: (b, t, 0)),
                   pl.BlockSpec((None, GDN_HEADS, GDN_DK, GDN_DV), lambda b, t: (b, 0, 0, 0))),
        out_shape=(jax.ShapeDtypeStruct((b_, l_, GDN_V), BF16),
                   jax.ShapeDtypeStruct((b_, GDN_HEADS, GDN_DK, GDN_DV), F32)),
        compiler_params=_params(("arbitrary", "arbitrary")),
        name="gdn_prompt",
    )(qkvf, qkvf, qkvf, ba, alog_row, dtb_row, gact, onw, masks)


STEP_FEATURES = 5


def _gdn_step_features_body(bb, x_ref, st_ref, cw_ref, ba_ref, alog_ref, dtb_ref, f_ref):
    beta_all, g_all = _delta_gates(ba_ref[...], alog_ref[...], dtb_ref[...])
    lane_idx = lax.broadcasted_iota(jnp.int32, (bb, LANES), 1)
    for h in range(GDN_HEADS):
        feats = []
        for s in range(3):
            idx = s * GDN_HEADS + h
            cols = slice(idx * LANES, (idx + 1) * LANES)
            w = cw_ref[idx]
            y = w[0:1] * st_ref[0, :, cols]
            for tap in range(1, GDN_CONV - 1):
                y = y + w[tap:tap + 1] * st_ref[tap, :, cols]
            y = y + w[GDN_CONV - 1:GDN_CONV] * x_ref[:, cols]
            feats.append(_silu(y))
        q, k, v = feats
        f_ref[h] = _l2norm(q) * (GDN_DK ** -0.5)
        f_ref[GDN_HEADS + h] = _l2norm(k)
        f_ref[2 * GDN_HEADS + h] = v
        f_ref[3 * GDN_HEADS + h] = jnp.broadcast_to(_lane_column(beta_all, lane_idx, h), (bb, LANES))
        f_ref[4 * GDN_HEADS + h] = jnp.broadcast_to(
            jnp.exp(_lane_column(g_all, lane_idx, h + GDN_HEADS)), (bb, LANES))


def _gdn_step_features(qkv, st, ba, cw, alog_row, dtb_row, bb):
    n_ = ba.shape[0]
    return pl.pallas_call(
        functools.partial(_gdn_step_features_body, bb),
        grid=(n_ // bb,),
        in_specs=[pl.BlockSpec((bb, GDN_CONV_CH), lambda i: (i, 0)),
                  pl.BlockSpec((GDN_CONV - 1, bb, GDN_CONV_CH), lambda i: (0, i, 0)),
                  _const_spec(cw.shape),
                  pl.BlockSpec((bb, LANES), lambda i: (i, 0)),
                  _const_spec(alog_row.shape), _const_spec(dtb_row.shape)],
        out_specs=pl.BlockSpec((STEP_FEATURES * GDN_HEADS, bb, LANES), lambda i: (0, i, 0)),
        out_shape=jax.ShapeDtypeStruct((STEP_FEATURES * GDN_HEADS, n_, LANES), F32),
        compiler_params=_params(("arbitrary",)),
        name="gdn_step_features",
    )(qkv, st, cw, ba, alog_row, dtb_row)


def _gdn_state_update(f_ref, s0_ref, sn_ref, o_ref, i, h):
    eye = (lax.broadcasted_iota(jnp.int32, (GDN_DK, GDN_DK), 0)
           == lax.broadcasted_iota(jnp.int32, (GDN_DK, GDN_DK), 1))

    def row(kind):
        return f_ref[i, kind * GDN_HEADS + h:kind * GDN_HEADS + h + 1, :]

    def to_col(r):
        return jnp.sum(jnp.where(eye, jnp.broadcast_to(r, (GDN_DK, GDN_DK)), 0.0), axis=1, keepdims=True)

    q_row, k_row, v_row, beta, decay = (row(kind) for kind in range(STEP_FEATURES))
    k_col = to_col(k_row)
    s1 = s0_ref[i, h] * decay
    delta = (v_row - jnp.sum(s1 * k_col, axis=0, keepdims=True)) * beta
    s2 = s1 + k_col * delta
    sn_ref[i, h] = s2
    o_ref[i, h:h + 1, :] = jnp.sum(s2 * to_col(q_row), axis=0, keepdims=True)


def _swa_prompt_body(nq, sinks_ref, q_ref, kvp_ref, kvc_ref, o_ref):
    n = pl.program_id(1)
    w = WINDOW
    tiles = SWA_KV // LANES
    pairs = 2
    lo_lane = lax.broadcasted_iota(jnp.int32, (w, LANES), 1) < SWA_HD
    lo_row = lax.broadcasted_iota(jnp.int32, (LANES, w), 0) < SWA_HD
    c = lax.broadcasted_iota(jnp.int32, (w, pairs * w), 0)
    i = lax.broadcasted_iota(jnp.int32, (w, pairs * w), 1) & (w - 1)
    from_prev = c > i
    k_blk, vt_blk = [], []
    for j in range(nq + 1):
        src, rows = (kvp_ref, slice(0, w)) if j == 0 else (kvc_ref, slice((j - 1) * w, j * w))
        k_tiles, vt_tiles = [], []
        for t in range(tiles):
            kx = src[rows, t * LANES:(t + 1) * LANES]
            vt = src[rows, SWA_KV + t * LANES:SWA_KV + (t + 1) * LANES].T
            k_tiles.append((kx.astype(BF16), pltpu.roll(kx, SWA_HD, axis=1).astype(BF16)))
            vt_tiles.append((vt.astype(BF16),
                             jnp.concatenate([vt[SWA_HD:], vt[:SWA_HD]], axis=0).astype(BF16)))
        k_blk.append(k_tiles)
        vt_blk.append(vt_tiles)
    items = [(qb, g, p) for qb in range(nq) for g in range(SWA_KV_HEADS) for p in range(2)]
    log2e = math.log2(math.e)
    qm, kz, vzt, sink = {}, {}, {}, {}
    for qb, g, p in items:
        keep = lo_lane if p == 0 else jnp.logical_not(lo_lane)
        q_tiles = [q_ref[qb * w:(qb + 1) * w, (2 * g + r) * LANES:(2 * g + r + 1) * LANES] for r in range(pairs)]
        qm[qb, g, p] = jnp.concatenate([jnp.where(keep, x * SWA_Q_SCALE, 0.0) for x in q_tiles],
                                       axis=0).astype(BF16)
        variant = 0 if p == g % 2 else 1
        kz[qb, g, p] = jnp.concatenate([k_blk[qb + d][g // 2][variant] for d in range(2)], axis=0)
        vzt[qb, g, p] = jnp.concatenate([vt_blk[qb + d][g // 2][variant] for d in range(2)], axis=1)
        sink[qb, g, p] = jnp.concatenate([jnp.full((1, w), sinks_ref[SWA_GROUP * g + 2 * r + p] * log2e, F32)
                                          for r in range(pairs)], axis=1)
    st = {b: lax.dot_general(kz[b], qm[b], (((1,), (1,)), ((), ())), preferred_element_type=F32) for b in items}
    prev = {b: jnp.where(n > 0, st[b][:w], -jnp.inf) if b[0] == 0 else st[b][:w] for b in items}
    u = {b: jnp.where(from_prev, prev[b], st[b][w:]) for b in items}
    m = {b: jnp.maximum(jnp.max(u[b], axis=0, keepdims=True), sink[b]) for b in items}
    eu = {b: jnp.exp2(u[b] - m[b]) for b in items}
    den = {b: jnp.sum(eu[b], axis=0, keepdims=True) + jnp.exp2(sink[b] - m[b]) for b in items}
    et = {b: jnp.concatenate([jnp.where(from_prev, eu[b], 0.0), jnp.where(from_prev, 0.0, eu[b])],
                             axis=0).astype(BF16) for b in items}
    ot = {b: jnp.dot(vzt[b], et[b], preferred_element_type=F32) / den[b] for b in items}
    for qb in range(nq):
        for g in range(SWA_KV_HEADS):
            for r in range(pairs):
                cols = slice(r * w, (r + 1) * w)
                tile_t = jnp.where(lo_row, ot[qb, g, 0][:, cols], ot[qb, g, 1][:, cols])
                o_ref[qb * w:(qb + 1) * w, (2 * g + r) * LANES:(2 * g + r + 1) * LANES] = (
                    tile_t.T.astype(o_ref.dtype))


def _swa_prompt(sq, skv, sinks, nq):
    b_, l_, _ = sq.shape
    rows = nq * WINDOW
    return pl.pallas_call(
        functools.partial(_swa_prompt_body, nq),
        grid=(b_, l_ // rows),
        in_specs=[pl.BlockSpec(memory_space=pltpu.SMEM),
                  pl.BlockSpec((None, rows, SWA_Q), lambda b, n: (b, n, 0)),
                  pl.BlockSpec((None, WINDOW, 2 * SWA_KV), lambda b, n: (b, jnp.maximum(n * nq - 1, 0), 0)),
                  pl.BlockSpec((None, rows, 2 * SWA_KV), lambda b, n: (b, n, 0))],
        out_specs=pl.BlockSpec((None, rows, SWA_Q), lambda b, n: (b, n, 0)),
        out_shape=jax.ShapeDtypeStruct((b_, l_, SWA_Q), BF16),
        compiler_params=_params(("arbitrary", "arbitrary")),
        name="swa_prompt",
    )(sinks, sq, skv, skv)


def _swa_step_pieces(seq, i, q_ref, kvn_ref, ck_ref, cv_ref, sink_ref, o_ref, nk_ref, nv_ref):
    w = WINDOW
    mid = {}

    def own():
        row = lax.broadcasted_iota(jnp.int32, (SWA_Q_HEADS, SWA_KV), 0)
        lane = lax.broadcasted_iota(jnp.int32, (SWA_Q_HEADS, SWA_KV), 1)
        return (lane // SWA_HD) == (row // SWA_GROUP)

    def append_and_score():
        newest = lax.broadcasted_iota(jnp.int32, (SWA_KV, w), 1) == w - 1
        mid["nk"] = jnp.where(newest, pltpu.roll(kvn_ref[0:SWA_KV, :], w - 1 - seq, axis=1),
                              pltpu.roll(ck_ref[i], w - 1, axis=1))
        mid["nv"] = jnp.where(newest, pltpu.roll(kvn_ref[SWA_KV:2 * SWA_KV, :], w - 1 - seq, axis=1),
                              pltpu.roll(cv_ref[i], w - 1, axis=1))
        q_bd = jnp.where(own(), jnp.concatenate([q_ref[i]] * SWA_KV_HEADS, axis=1), 0.0)
        mid["s"] = _mm(q_bd, mid["nk"]) * (SWA_HD ** -0.5)

    def values():
        sink = sink_ref[...]
        m = jnp.maximum(jnp.max(mid["s"], axis=1, keepdims=True), sink)
        e = jnp.exp(mid["s"] - m)
        den = jnp.sum(e, axis=1, keepdims=True) + jnp.exp(sink - m)
        mid["pv"] = _mm_nt(e / den, mid["nv"])

    def write_out():
        pv = jnp.where(own(), mid["pv"], 0.0)
        o = pv[:, 0:SWA_HD]
        for g in range(1, SWA_KV_HEADS):
            o = o + pv[:, g * SWA_HD:(g + 1) * SWA_HD]
        o_ref[i] = o
        nk_ref[i] = mid["nk"]
        nv_ref[i] = mid["nv"]

    return [append_and_score, values, write_out]


def _dense_body(stateful, tm, og_ref, ob_ref, gab_ref, x_ref, gt1_ref, sh2_ref, sc2_ref, gt2_ref,
                n2w_ref, fnw_ref, wa_ref, wb_ref, wo_ref, wg_ref, wu_ref, cw_ref, cb_ref, wd_ref, *rest):
    if stateful:
        st_ref, gate_ref, onw_ref, y_ref, gout_ref, act_ref = rest
        onw = onw_ref[...]
        og = jnp.concatenate(
            [_gated_out_norm(og_ref[h], _silu(gate_ref[:, h * GDN_DV:(h + 1) * GDN_DV]), onw).astype(BF16)
             for h in range(GDN_HEADS)], axis=1)
    else:
        y_ref, gout_ref, act_ref, gbuf_ref, carry_ref = rest
        og = og_ref[...]

        @pl.when(pl.program_id(1) == 0)
        def _():
            carry_ref[...] = jnp.zeros_like(carry_ref)

    y_a = jnp.dot(og, wa_ref[...], preferred_element_type=F32)
    y_b = jnp.dot(ob_ref[...], wb_ref[...], preferred_element_type=F32)
    merged = (jax.nn.sigmoid(gab_ref[:, 0:D_MODEL]) * y_a
              + jax.nn.sigmoid(gab_ref[:, D_MODEL:2 * D_MODEL]) * y_b)
    x1 = x_ref[...] + gt1_ref[...] * _mm(merged, wo_ref[...])
    h2 = (_rms(x1, n2w_ref[...]) * (1.0 + sc2_ref[...]) + sh2_ref[...]).astype(BF16)

    for c in range(D_FF // FFN_COLS):
        cols = slice(c * FFN_COLS, (c + 1) * FFN_COLS)
        gate = jnp.dot(h2, wg_ref[:, cols], preferred_element_type=F32)
        up = jnp.dot(h2, wu_ref[:, cols], preferred_element_type=F32)
        if stateful:
            g2 = st_ref[0, :, cols]
            g1 = st_ref[1, :, cols]
            gout_ref[:, cols] = gate
        else:
            gbuf_ref[0:SUBLANES, :] = carry_ref[:, cols]
            gbuf_ref[SUBLANES:SUBLANES + tm, :] = gate
            g2 = gbuf_ref[SUBLANES - 2:SUBLANES - 2 + tm, :]
            g1 = gbuf_ref[SUBLANES - 1:SUBLANES - 1 + tm, :]
            carry_ref[:, cols] = gbuf_ref[tm:tm + SUBLANES, :]
        gc = (cw_ref[0:1, cols] * g2 + cw_ref[1:2, cols] * g1 + cw_ref[2:3, cols] * gate) + cb_ref[:, cols]
        act_ref[:, cols] = (_silu(gc) * up).astype(BF16)
    if not stateful:
        gout_ref[...] = carry_ref[...]

    x2 = x1 + gt2_ref[...] * jnp.dot(act_ref[...], wd_ref[...], preferred_element_type=F32)
    y_ref[...] = _rms(x2, fnw_ref[...])


def _dense(og, ob, gab, x, mods, vecs, ws, st, tm, step=None):
    b_, l_, _ = x.shape
    r_ = mods[0].shape[1]
    rt = 1 if r_ == 1 else tm
    nt = l_ // tm
    mod_map = (lambda b, t: (b, 0, 0)) if r_ == 1 else (lambda b, t: (b, t, 0))
    row_map = lambda b, t: (b, t, 0)
    stateful = st is not None
    og_spec = (pl.BlockSpec((GDN_HEADS, tm, LANES), lambda b, t: (0, t, 0)) if stateful
               else pl.BlockSpec((None, tm, D_MODEL), row_map))
    in_specs = ([og_spec,
                 pl.BlockSpec((None, tm, D_MODEL), row_map),
                 pl.BlockSpec((None, tm, 2 * D_MODEL), row_map),
                 pl.BlockSpec((None, tm, D_MODEL), row_map)]
                + [pl.BlockSpec((None, rt, D_MODEL), mod_map)] * 4
                + [_const_spec(a.shape) for a in vecs[:2]]
                + [_const_spec(ws[0].shape), _const_spec(ws[1].shape), _const_spec(ws[2].shape),
                   _const_spec(ws[3].shape), _const_spec(ws[4].shape),
                   _const_spec(vecs[2].shape), _const_spec(vecs[3].shape), _const_spec(ws[5].shape)])
    args = [og, ob, gab, x, *mods, vecs[0], vecs[1], ws[0], ws[1], ws[2], ws[3], ws[4], vecs[2], vecs[3], ws[5]]
    scratch = [pltpu.VMEM((tm, D_FF), BF16)]
    out_specs = [pl.BlockSpec((None, tm, D_MODEL), row_map)]
    out_shape = [jax.ShapeDtypeStruct((b_, l_, D_MODEL), F32)]
    if stateful:
        gate, onw = step
        in_specs += [pl.BlockSpec((FFN_CONV - 1, None, tm, D_FF), lambda b, t: (0, b, t, 0)),
                     pl.BlockSpec((None, tm, GDN_V), row_map), _const_spec(onw.shape)]
        args += [st, gate, onw]
        out_shape.append(jax.ShapeDtypeStruct((b_, l_, D_FF), F32))
        out_specs.append(pl.BlockSpec((None, tm, D_FF), row_map))
    else:
        scratch += [pltpu.VMEM((tm + SUBLANES, FFN_COLS), F32), pltpu.VMEM((SUBLANES, D_FF), F32)]
        out_shape.append(jax.ShapeDtypeStruct((b_, SUBLANES, D_FF), F32))
        out_specs.append(pl.BlockSpec((None, SUBLANES, D_FF), lambda b, t: (b, 0, 0)))
    return pl.pallas_call(
        functools.partial(_dense_body, stateful, tm),
        grid=(b_, nt),
        in_specs=in_specs,
        out_specs=tuple(out_specs),
        out_shape=tuple(out_shape),
        scratch_shapes=scratch,
        compiler_params=_params(("arbitrary", "arbitrary")),
        name="dense_step" if stateful else "dense_prompt",
    )(*args)


def _lane_row(values, offset):
    return jnp.zeros((1, LANES), F32).at[0, offset:offset + values.shape[0]].set(values)


def kernel(x_prompt, x_sample, c_prompt, c_sample, state_gdn_S, state_gdn_conv, cache_swa_k, cache_swa_v,
           state_ffn_conv, w_mod, b_mod, norm1_w, norm2_w, w_in, gdn_conv_w, gdn_a_log, gdn_dt_bias,
           gdn_onorm_w, w_gdn_out, swa_sinks, w_swa_out, w_o, w_ffn_gate, w_ffn_up, ffn_conv_w, ffn_conv_b,
           w_ffn_down, final_norm_w):
    assert w_mod.shape[0] == 1, "single-layer trunk"
    nb, seq, _ = x_prompt.shape
    ns = x_sample.shape[0]
    assert x_sample.shape[1] == 1

    in_ws = _in_weight(jnp.transpose(w_in[0]))
    dense_vecs = (norm2_w, final_norm_w[None, :], ffn_conv_w[0], ffn_conv_b)
    cw = jnp.transpose(gdn_conv_w[0].reshape(GDN_CONV, GDN_SECTIONS, LANES), (1, 0, 2))
    alog_row = _lane_row(gdn_a_log[0], GDN_HEADS)
    dtb_row = _lane_row(gdn_dt_bias[0], GDN_HEADS)

    mod = _modulation(jnp.concatenate([c_prompt, c_sample], axis=0), w_mod[0], b_mod)
    mod_p = [mod[:nb, i * D_MODEL:(i + 1) * D_MODEL][:, None, :] for i in range(6)]
    mod_s = [mod[nb:, i * D_MODEL:(i + 1) * D_MODEL][None, :, :] for i in range(6)]

    xs = x_sample.reshape(1, ns, D_MODEL)
    qkvs, gates, bas, sqs, skvs, gabs = _inproj(xs, mod_s[0], mod_s[1], norm1_w, in_ws, None, tm=ns)
    st_gdn = jnp.transpose(state_gdn_conv[0], (1, 0, 2))
    step_feats = jnp.transpose(_gdn_step_features(qkvs[0], st_gdn, bas[0], cw, alog_row, dtb_row, bb=ns), (1, 0, 2))
    to_channel_major = lambda c: jnp.transpose(c, (0, 2, 3, 1)).reshape(ns, SWA_KV, WINDOW)
    from_channel_major = lambda c: jnp.transpose(c.reshape(ns, SWA_KV_HEADS, SWA_HD, WINDOW), (0, 3, 1, 2))
    kvn_t = jnp.pad(jnp.transpose(skvs[0]), ((0, 0), (0, WINDOW - ns)))
    step_attn = (sqs[0].reshape(ns, SWA_Q_HEADS, SWA_HD), kvn_t, to_channel_major(cache_swa_k[0]),
                 to_channel_major(cache_swa_v[0]), swa_sinks[0][:, None])

    later_ws = (w_gdn_out[0], w_swa_out[0], w_o[0], w_ffn_gate[0], w_ffn_up[0])
    qkvf, gact, ba, sq, skv, gab, qkv_tail, *hosted = _inproj(
        x_prompt, mod_p[0], mod_p[1], norm1_w, in_ws, cw, tm=256, cast_ws=later_ws, step_attn=step_attn,
        step_state=(step_feats, state_gdn_S[0]))
    later_bf16, (o3, k_s, v_s, gdn_s_s, step_read) = hosted[:len(later_ws)], hosted[len(later_ws):]
    dense_ws = (*later_bf16, w_ffn_down[0].astype(BF16))
    og, gdn_s_p = _gdn_prompt(qkvf, gact, ba, alog_row, dtb_row, gdn_onorm_w, _level_masks(), lt=4 * CHUNK)
    ob = _swa_prompt(sq, skv, swa_sinks[0], nq=4)
    y_p, gate_tail = _dense(og, ob, gab, x_prompt, (mod_p[2], mod_p[3], mod_p[4], mod_p[5]),
                            dense_vecs, dense_ws, None, tm=512)
    gdn_conv_p = jnp.transpose(qkv_tail[:, :, SUBLANES - (GDN_CONV - 1):, :], (0, 2, 1, 3)).reshape(
        nb, GDN_CONV - 1, GDN_CONV_CH)
    k_p = skv[:, seq - WINDOW:, :SWA_KV].reshape(nb, WINDOW, SWA_KV_HEADS, SWA_HD)
    v_p = skv[:, seq - WINDOW:, SWA_KV:].reshape(nb, WINDOW, SWA_KV_HEADS, SWA_HD)
    ffn_conv_p = gate_tail[:, SUBLANES - (FFN_CONV - 1):, :]

    ob_s = o3.reshape(1, ns, SWA_Q).astype(BF16)
    st_ffn = jnp.transpose(state_ffn_conv[0], (1, 0, 2))[:, None]
    y_s, gate_new = _dense(jnp.transpose(step_read, (1, 0, 2)), ob_s, gabs, xs,
                           (mod_s[2], mod_s[3], mod_s[4], mod_s[5]), dense_vecs, dense_ws, st_ffn, tm=ns,
                           step=(gates, gdn_onorm_w))
    gdn_conv_s = jnp.concatenate([state_gdn_conv[0][:, 1:], qkvs[0][:, None, :]], axis=1)
    ffn_conv_s = jnp.concatenate([state_ffn_conv[0][:, 1:], gate_new[0][:, None, :]], axis=1)

    return (y_p, y_s.reshape(ns, 1, D_MODEL),
            gdn_s_p[None], gdn_s_s[None],
            gdn_conv_p[None], gdn_conv_s[None],
            k_p[None], from_channel_major(k_s)[None],
            v_p[None], from_channel_major(v_s)[None],
            ffn_conv_p[None], ffn_conv_s[None])
```
